```python
import math
import jax
import jax.numpy as jnp
from jax import lax
import numpy as np

D_MODEL = 1024
BATCH = 2
SEQ = 8192
DEPTH = 1
DEC_BATCH = 32
DEC_SEQ = 4
PAST_LEN = 16384
PAGE_SIZE = 128

D_PLE = 256
N_HEADS_A = 8
HEAD_DIM_A = 64
WIDTH_A = N_HEADS_A * HEAD_DIM_A
DILATED_BRANCHES = ((128, 1), (512, 4), (2048, 16))
BLK = 128
WINDOW_MAX = 2048
MAX_DIL = 16
NUM_BUCKETS = 32
MAX_DISTANCE = 2048
N_HEADS_B = 4
KEY_DIM_B = 64
VAL_DIM_B = 128
QK_WIDTH_B = N_HEADS_B * KEY_DIM_B
WIDTH_B = N_HEADS_B * VAL_DIM_B
RET_CHUNK = 128
GN_EPS = 1e-6
MIX_WIDTH = WIDTH_A + WIDTH_B
IN_WIDTH = 3 * WIDTH_A + 2 * QK_WIDTH_B + 2 * WIDTH_B
N_EXPERTS = 32
TOP_K = 4
D_EXPERT = 1024
SWIGLU_LIMIT = 7.0
SWIGLU_ALPHA = 1.702
MOE_BLOCK_PROMPT = 256
MOE_BLOCK_SAMPLE = 32
LN_EPS = 1e-5
DEEPNORM_ALPHA = (2.0 * DEPTH) ** 0.25
DEEPNORM_BETA = (8.0 * DEPTH) ** -0.25
NEG_INF = -1e30

kernel_name = 'hymba_dilated_retention_moe_step'


def layer_norm(x, g, b):
    xf = x.astype(jnp.float32)
    mu = jnp.mean(xf, axis=-1, keepdims=True)
    var = jnp.mean(jnp.square(xf - mu), axis=-1, keepdims=True)
    y = (xf - mu) * lax.rsqrt(var + LN_EPS) * g.astype(jnp.float32) + b.astype(jnp.float32)
    return y.astype(x.dtype)


def t5_bucket(dist):
    dist = np.asarray(dist, dtype=np.int32)
    max_exact = NUM_BUCKETS // 2
    d = np.maximum(dist, 1).astype(np.float32)
    large = max_exact + (np.log(d / max_exact) / np.log(MAX_DISTANCE / max_exact)
                         * (NUM_BUCKETS - max_exact)).astype(np.int32)
    large = np.minimum(large, NUM_BUCKETS - 1)
    return np.where(dist < max_exact, dist, large).astype(np.int32)


def rotary(x, pos):
    half = x.shape[-1] // 2
    inv_freq = 1.0 / (10000.0 ** jnp.linspace(0.0, 1.0, half, dtype=jnp.float32))
    ang = pos.astype(jnp.float32)[:, None] * inv_freq[None, :]
    cos = jnp.cos(ang)[None, :, None, :]
    sin = jnp.sin(ang)[None, :, None, :]
    x1, x2 = x[..., :half], x[..., half:]
    return jnp.concatenate([x1 * cos - x2 * sin, x2 * cos + x1 * sin], axis=-1)


def project_in(x, w_in, pos):
    B, S, _ = x.shape
    z = x @ w_in
    splits = np.cumsum([WIDTH_A] * 3 + [QK_WIDTH_B] * 2 + [WIDTH_B] * 2)[:-1].tolist()
    qa, ka, va, qb, kb, vb, gb = jnp.split(z, splits, axis=-1)
    qa = qa.reshape(B, S, N_HEADS_A, HEAD_DIM_A)
    ka = ka.reshape(B, S, N_HEADS_A, HEAD_DIM_A)
    va = va.reshape(B, S, N_HEADS_A, HEAD_DIM_A)
    qb = rotary(qb.reshape(B, S, N_HEADS_B, KEY_DIM_B).astype(jnp.float32), pos)
    kb = rotary(kb.reshape(B, S, N_HEADS_B, KEY_DIM_B).astype(jnp.float32), pos) * (KEY_DIM_B ** -0.5)
    vb = vb.reshape(B, S, N_HEADS_B, VAL_DIM_B).astype(jnp.float32)
    return qa, ka, va, qb, kb, vb, gb


def _dilated_branch_prompt(q, k, v, window, dil, rel_bias):
    B, Sp, H, Dh = q.shape
    n_keys = window // dil
    L = Sp // dil
    nb = L // BLK

    def split(t):
        return t.reshape(B, L, dil, H, Dh).transpose(0, 2, 1, 3, 4).reshape(B, dil, nb, BLK, H, Dh)

    def band(t):
        prev = jnp.pad(t, ((0, 0), (0, 0), (1, 0), (0, 0), (0, 0), (0, 0)))[:, :, :-1]
        return jnp.concatenate([prev, t], axis=3)

    qs = split(q).astype(jnp.float32)
    kb = band(split(k)).astype(jnp.float32)
    vb = band(split(v)).astype(jnp.float32)
    i = np.arange(BLK)[:, None]
    j = np.arange(2 * BLK)[None, :]
    rel = BLK + i - j
    in_band = (rel >= 0) & (rel <= n_keys)
    mask = in_band[None] & ((np.arange(nb) > 0)[:, None, None] | (j >= BLK)[None])
    bias = rel_bias[t5_bucket(np.clip(rel, 0, None) * dil)].astype(jnp.float32)
    s = jnp.einsum('bgnqhd,bgnkhd->bgnhqk', qs, kb) * (Dh ** -0.5) + bias.transpose(2, 0, 1)[None, None, None]
    s = jnp.where(mask[None, None, :, None], s, NEG_INF)
    m = jnp.max(s, axis=-1, keepdims=True)
    e = jnp.exp(s - m)
    den = jnp.sum(e, axis=-1)
    o = jnp.einsum('bgnhqk,bgnkhd->bgnqhd', e, vb) / den.transpose(0, 1, 2, 4, 3)[..., None]
    lse = (m[..., 0] + jnp.log(den)).transpose(0, 1, 2, 4, 3)
    o = o.reshape(B, dil, L, H, Dh).transpose(0, 2, 1, 3, 4).reshape(B, Sp, H, Dh)
    lse = lse.reshape(B, dil, L, H).transpose(0, 2, 1, 3).reshape(B, Sp, H)
    return o, lse


def dilated_attention_prompt(q, k, v, rel_bias):
    B, S, H, Dh = q.shape
    span = BLK * MAX_DIL
    Sp = -(-S // span) * span
    pad = ((0, 0), (0, Sp - S), (0, 0), (0, 0))
    qp, kp, vp = jnp.pad(q, pad), jnp.pad(k, pad), jnp.pad(v, pad)
    outs, lses = [], []
    for window, dil in DILATED_BRANCHES:
        o, lse = _dilated_branch_prompt(qp, kp, vp, window, dil, rel_bias)
        outs.append(o)
        lses.append(lse)
    wts = jax.nn.softmax(jnp.stack(lses), axis=0)
    o = jnp.einsum('nbsh,nbshd->bshd', wts, jnp.stack(outs))
    return o[:, :S]


def dilated_attention_sample(q, k_all, v_all, rel_bias, w_buf):
    DS, Dh = q.shape[1], q.shape[-1]
    qf = q.astype(jnp.float32)
    outs, lses = [], []
    for window, dil in DILATED_BRANCHES:
        n_keys = window // dil
        kk = np.arange(n_keys + 1)
        idx = w_buf + np.arange(DS)[:, None] - kk[None, :] * dil
        valid = idx >= 0
        idxc = np.maximum(idx, 0)
        kg = k_all[:, idxc].astype(jnp.float32)
        vg = v_all[:, idxc].astype(jnp.float32)
        bias = rel_bias[t5_bucket(kk * dil)].astype(jnp.float32)
        s = jnp.einsum('bshd,bskhd->bhsk', qf, kg) * (Dh ** -0.5) + bias.T[None, :, None, :]
        s = jnp.where(valid[None, None], s, NEG_INF)
        m = jnp.max(s, axis=-1, keepdims=True)
        e = jnp.exp(s - m)
        den = jnp.sum(e, axis=-1)
        o = jnp.einsum('bhsk,bskhd->bshd', e, vg) / den.transpose(0, 2, 1)[..., None]
        outs.append(o)
        lses.append((m[..., 0] + jnp.log(den)).transpose(0, 2, 1))
    wts = jax.nn.softmax(jnp.stack(lses), axis=0)
    return jnp.einsum('nbsh,nbshd->bshd', wts, jnp.stack(outs))


def retention(q, k, v, state0, chunk):
    B, S, H, _ = q.shape
    nc = S // chunk

    def to_chunks(t):
        return t.reshape(B, nc, chunk, H, t.shape[-1]).transpose(1, 0, 3, 2, 4)

    log_g = jnp.log(1.0 - 2.0 ** (-5.0 - jnp.arange(H, dtype=jnp.float32)))
    i = jnp.arange(chunk, dtype=jnp.float32)
    diff = i[:, None] - i[None, :]
    causal = diff >= 0
    dmat = jnp.where(causal[None], jnp.exp(jnp.where(causal, diff, 0.0)[None] * log_g[:, None, None]), 0.0)
    q_decay = jnp.exp((i[None, :] + 1.0) * log_g[:, None])
    k_decay = jnp.exp((chunk - 1.0 - i)[None, :] * log_g[:, None])
    c_decay = jnp.exp(chunk * log_g)

    def step(st, inp):
        qc, kc, vc = inp
        a = jnp.einsum('bhqd,bhkd->bhqk', qc, kc) * dmat
        o = jnp.einsum('bhqk,bhkv->bhqv', a, vc) + jnp.einsum('bhqd,bhdv->bhqv', qc, st) * q_decay[:, :, None]
        st = st * c_decay[:, None, None] + jnp.einsum('bhkd,bhkv->bhdv', kc * k_decay[:, :, None], vc)
        return st, o

    st, o = lax.scan(step, state0, (to_chunks(q), to_chunks(k), to_chunks(v)))
    o = o.transpose(1, 0, 3, 2, 4).reshape(B, S, H, v.shape[-1])
    return o, st


def mix_out(oa, ob, gb, w_out):
    B, S = oa.shape[:2]
    mu = jnp.mean(ob, axis=-1, keepdims=True)
    var = jnp.mean(jnp.square(ob - mu), axis=-1, keepdims=True)
    obn = ((ob - mu) * lax.rsqrt(var + GN_EPS)).reshape(B, S, WIDTH_B)
    gated = jax.nn.silu(gb.astype(jnp.float32)) * obn
    cat = jnp.concatenate([oa.reshape(B, S, WIDTH_A).astype(jnp.float32), gated], axis=-1)
    return cat.astype(w_out.dtype) @ w_out


def moe_ffn(h, router_w, router_b, w_gu, b_gu, w_dn, b_dn, block):
    T, D = h.shape
    logits = (h @ router_w + router_b).astype(jnp.float32)
    top_val, top_idx = lax.top_k(logits, TOP_K)
    gates = jax.nn.softmax(top_val, axis=-1)
    n = T * TOP_K
    flat_e = top_idx.reshape(-1).astype(jnp.int32)
    flat_t = jnp.repeat(jnp.arange(T, dtype=jnp.int32), TOP_K)
    flat_g = gates.reshape(-1)
    order = jnp.argsort(flat_e)
    se, st, sg = flat_e[order], flat_t[order], flat_g[order]
    counts = jnp.bincount(flat_e, length=N_EXPERTS).astype(jnp.int32)
    padded = (counts + block - 1) // block * block
    pad_end = jnp.cumsum(padded)
    pad_start = pad_end - padded
    start = jnp.cumsum(counts) - counts
    dest = pad_start[se] + jnp.arange(n, dtype=jnp.int32) - start[se]
    n_blocks = -(-n // block) + N_EXPERTS
    rows = n_blocks * block
    row_tok = jnp.full((rows,), T, jnp.int32).at[dest].set(st)
    block_e = jnp.minimum(jnp.searchsorted(pad_end, jnp.arange(n_blocks, dtype=jnp.int32) * block, side='right'),
                          N_EXPERTS - 1)
    hp = jnp.concatenate([h, jnp.zeros((1, D), h.dtype)], axis=0)[row_tok].reshape(n_blocks, block, D)

    def expert_block(args):
        xb, e = args
        gu = xb @ w_gu[e] + b_gu[e]
        gate = jnp.minimum(gu[:, :D_EXPERT], SWIGLU_LIMIT)
        up = jnp.clip(gu[:, D_EXPERT:], -SWIGLU_LIMIT, SWIGLU_LIMIT)
        act = (up + 1.0) * gate * jax.nn.sigmoid(SWIGLU_ALPHA * gate)
        return act @ w_dn[e] + b_dn[e]

    yb = lax.map(expert_block, (hp, block_e)).reshape(rows, D)
    contrib = yb[dest] * sg[:, None].astype(yb.dtype)
    return jax.ops.segment_sum(contrib, st, num_segments=T)


def residual_ffn_ple(x, mix, p, ln1_g, ln1_b, router_w, router_b, w_gu, b_gu, w_dn, b_dn,
                     ln2_g, ln2_b, w_pg, b_pg, w_pp, moe_block):
    h = layer_norm(DEEPNORM_ALPHA * x + mix.astype(x.dtype), ln1_g, ln1_b)
    B, S, D = h.shape
    f = moe_ffn(h.reshape(B * S, D), router_w, router_b, w_gu, b_gu, w_dn, b_dn, moe_block).reshape(B, S, D)
    h = layer_norm(DEEPNORM_ALPHA * h + f.astype(h.dtype), ln2_g, ln2_b)
    gate = jax.nn.sigmoid(h @ w_pg + b_pg)
    return h + gate * (p @ w_pp)


def setup_inputs(seed: int = 0) -> dict:
    key = jax.random.key(seed)
    ks = jax.random.split(key, 24)
    f32 = jnp.float32
    w_buf = min(WINDOW_MAX, PAST_LEN)

    def nrm(k, shape, scale):
        return jax.random.normal(k, shape, f32) * scale

    beta = DEEPNORM_BETA
    col_scale = np.concatenate([np.ones(2 * WIDTH_A), np.full(WIDTH_A, beta),
                                np.ones(2 * QK_WIDTH_B), np.full(WIDTH_B, beta),
                                np.ones(WIDTH_B)]).astype(np.float32)
    return {
        'x_prompt': nrm(ks[0], (BATCH, SEQ, D_MODEL), 1.0),
        'x_sample': nrm(ks[1], (DEC_BATCH, DEC_SEQ, D_MODEL), 1.0),
        'cache_win_k': nrm(ks[2], (DEPTH, DEC_BATCH, w_buf, N_HEADS_A, HEAD_DIM_A), 1.0),
        'cache_win_v': nrm(ks[3], (DEPTH, DEC_BATCH, w_buf, N_HEADS_A, HEAD_DIM_A), beta),
        'state_ret': nrm(ks[4], (DEPTH, DEC_BATCH, N_HEADS_B, KEY_DIM_B, VAL_DIM_B), 0.5),
        'p_prompt': nrm(ks[5], (DEPTH, BATCH, SEQ, D_PLE), 1.0),
        'p_sample': nrm(ks[6], (DEPTH, DEC_BATCH, DEC_SEQ, D_PLE), 1.0),
        'rel_bias': nrm(ks[7], (NUM_BUCKETS, N_HEADS_A), 0.5),
        'w_in': nrm(ks[8], (DEPTH, D_MODEL, IN_WIDTH), D_MODEL ** -0.5) * jnp.asarray(col_scale),
        'w_out': nrm(ks[9], (DEPTH, MIX_WIDTH, D_MODEL), MIX_WIDTH ** -0.5 * beta),
        'ln1_g': 1.0 + nrm(ks[10], (DEPTH, D_MODEL), 0.01),
        'ln1_b': nrm(ks[11], (DEPTH, D_MODEL), 0.01),
        'router_w': nrm(ks[12], (DEPTH, D_MODEL, N_EXPERTS), D_MODEL ** -0.5),
        'router_b': nrm(ks[13], (DEPTH, N_EXPERTS), 0.01),
        'w_gate_up': nrm(ks[14], (DEPTH, N_EXPERTS, D_MODEL, 2 * D_EXPERT), D_MODEL ** -0.5 * beta),
        'b_gate_up': nrm(ks[15], (DEPTH, N_EXPERTS, 2 * D_EXPERT), 0.01),
        'w_down': nrm(ks[16], (DEPTH, N_EXPERTS, D_EXPERT, D_MODEL), D_EXPERT ** -0.5 * beta),
        'b_down': nrm(ks[17], (DEPTH, N_EXPERTS, D_MODEL), 0.01),
        'ln2_g': 1.0 + nrm(ks[18], (DEPTH, D_MODEL), 0.01),
        'ln2_b': nrm(ks[19], (DEPTH, D_MODEL), 0.01),
        'w_ple_gate': nrm(ks[20], (DEPTH, D_MODEL, D_MODEL), D_MODEL ** -0.5),
        'b_ple_gate': nrm(ks[21], (DEPTH, D_MODEL), 0.01),
        'w_ple_proj': nrm(ks[22], (DEPTH, D_PLE, D_MODEL), D_PLE ** -0.5),
    }


def reference(x_prompt, x_sample, cache_win_k, cache_win_v, state_ret, p_prompt, p_sample,
              rel_bias, w_in, w_out, ln1_g, ln1_b, router_w, router_b, w_gate_up, b_gate_up,
              w_down, b_down, ln2_g, ln2_b, w_ple_gate, b_ple_gate, w_ple_proj):
    B, S, _ = x_prompt.shape
    DB, DS, _ = x_sample.shape
    w_buf = cache_win_k.shape[2]
    w_prompt = min(WINDOW_MAX, S)
    pos_p = jnp.arange(S, dtype=jnp.int32)
    pos_s = PAST_LEN + jnp.arange(DS, dtype=jnp.int32)
    hp, hs = x_prompt, x_sample
    kp_l, vp_l, rp_l, ks_l, vs_l, rs_l = [], [], [], [], [], []
    for i in range(DEPTH):
        ffn_args = (ln1_g[i], ln1_b[i], router_w[i], router_b[i], w_gate_up[i], b_gate_up[i],
                    w_down[i], b_down[i], ln2_g[i], ln2_b[i], w_ple_gate[i], b_ple_gate[i], w_ple_proj[i])
        qa, ka, va, qb, kb, vb, gb = project_in(hp, w_in[i], pos_p)
        oa = dilated_attention_prompt(qa, ka, va, rel_bias)
        ob, rst_p = retention(qb, kb, vb, jnp.zeros((B, N_HEADS_B, KEY_DIM_B, VAL_DIM_B), jnp.float32), RET_CHUNK)
        mix = mix_out(oa, ob, gb, w_out[i])
        kp_l.append(ka[:, S - w_prompt:])
        vp_l.append(va[:, S - w_prompt:])
        rp_l.append(rst_p)
        hp = residual_ffn_ple(hp, mix, p_prompt[i], *ffn_args, MOE_BLOCK_PROMPT)
        qa_s, ka_s, va_s, qb_s, kb_s, vb_s, gb_s = project_in(hs, w_in[i], pos_s)
        k_all = jnp.concatenate([cache_win_k[i], ka_s.astype(cache_win_k.dtype)], axis=1)
        v_all = jnp.concatenate([cache_win_v[i], va_s.astype(cache_win_v.dtype)], axis=1)
        oa_s = dilated_attention_sample(qa_s, k_all, v_all, rel_bias, w_buf)
        ob_s, rst_s = retention(qb_s, kb_s, vb_s, state_ret[i].astype(jnp.float32), DS)
        mix_s = mix_out(oa_s, ob_s, gb_s, w_out[i])
        ks_l.append(k_all[:, DS:])
        vs_l.append(v_all[:, DS:])
        rs_l.append(rst_s)
        hs = residual_ffn_ple(hs, mix_s, p_sample[i], *ffn_args, MOE_BLOCK_SAMPLE)
    return (hp, hs, jnp.stack(kp_l), jnp.stack(vp_l), jnp.stack(rp_l), jnp.stack(ks_l), jnp.stack(vs_l), jnp.stack(rs_l))
```

```python
import functools

import numpy as np
import jax
import jax.numpy as jnp
from jax import lax
from jax.experimental import pallas as pl
from jax.experimental.pallas import tpu as pltpu

F32 = jnp.float32
BF16 = jnp.bfloat16

D_MODEL = 1024
D_PLE = 256
N_HEADS_A = 8
HEAD_DIM_A = 64
WIDTH_A = N_HEADS_A * HEAD_DIM_A
DILATED_BRANCHES = ((128, 1), (512, 4), (2048, 16))
BLK = 128
WINDOW_MAX = 2048
MAX_DIL = 16
NUM_BUCKETS = 32
MAX_DISTANCE = 2048
N_HEADS_B = 4
KEY_DIM_B = 64
VAL_DIM_B = 128
QK_WIDTH_B = N_HEADS_B * KEY_DIM_B
WIDTH_B = N_HEADS_B * VAL_DIM_B
RET_CHUNK = 128
GN_EPS = 1e-6
TOP_K = 4
SWIGLU_LIMIT = 7.0
SWIGLU_ALPHA = 1.702
LN_EPS = 1e-5
NEG_INF = -1e30
PAST_LEN = 16384
MOE_BLOCK = 256
LANES = 128
SUBLANES = 8
VMEM_LIMIT = 52 * 1024 * 1024


def _params(n_axes, vmem=VMEM_LIMIT):
    return pltpu.CompilerParams(dimension_semantics=("arbitrary",) * n_axes, vmem_limit_bytes=vmem)


def _t5_bucket(dist):
    dist = np.asarray(dist, dtype=np.int32)
    max_exact = NUM_BUCKETS // 2
    d = np.maximum(dist, 1).astype(np.float32)
    large = max_exact + (np.log(d / max_exact) / np.log(MAX_DISTANCE / max_exact)
                         * (NUM_BUCKETS - max_exact)).astype(np.int32)
    large = np.minimum(large, NUM_BUCKETS - 1)
    return np.where(dist < max_exact, dist, large).astype(np.int32)


def _in_proj_body(x_ref, w_ref, cos_ref, sin_ref,
                  qa_ref, ka_ref, va_ref, qb_ref, kb_ref, vb_ref, gb_ref, kf_ref, vf_ref):
    x = x_ref[...].astype(BF16)

    def proj(lo, hi):
        return jnp.dot(x, w_ref[:, lo:hi], preferred_element_type=F32)

    o = 0
    qa_ref[...] = (proj(o, o + WIDTH_A) * (HEAD_DIM_A ** -0.5)).astype(BF16)
    o += WIDTH_A
    ka = proj(o, o + WIDTH_A)
    ka_ref[...] = ka.astype(BF16)
    kf_ref[...] = ka
    o += WIDTH_A
    va = proj(o, o + WIDTH_A)
    va_ref[...] = va.astype(BF16)
    vf_ref[...] = va
    o += WIDTH_A

    cos = cos_ref[...]
    sin = sin_ref[...]
    lane = lax.broadcasted_iota(jnp.int32, cos.shape, 1)
    first_half = (lane % KEY_DIM_B) < (KEY_DIM_B // 2)

    def rot(z):
        sw = jnp.where(first_half, pltpu.roll(z, QK_WIDTH_B - KEY_DIM_B // 2, 1), pltpu.roll(z, KEY_DIM_B // 2, 1))
        return z * cos + sw * sin

    qb_ref[...] = rot(proj(o, o + QK_WIDTH_B)).astype(BF16)
    o += QK_WIDTH_B
    kb_ref[...] = (rot(proj(o, o + QK_WIDTH_B)) * (KEY_DIM_B ** -0.5)).astype(BF16)
    o += QK_WIDTH_B
    vb_ref[...] = proj(o, o + WIDTH_B).astype(BF16)
    o += WIDTH_B
    gb_ref[...] = proj(o, o + WIDTH_B).astype(BF16)


def _in_proj(x2d, w_bf, cos_t, sin_t, tm, seq_tiles, win_tiles):
    T = x2d.shape[0]
    nt = T // tm
    n_seq = nt // seq_tiles
    j0 = seq_tiles - win_tiles

    def tok(i):
        return (i, 0)

    def tab(i):
        return (i % seq_tiles, 0)

    def win(i):
        return ((i // seq_tiles) * win_tiles + jnp.maximum(i % seq_tiles - j0, 0), 0)

    def tspec(w):
        return pl.BlockSpec((tm, w), tok)

    out_shape = (
        jax.ShapeDtypeStruct((T, WIDTH_A), BF16), jax.ShapeDtypeStruct((T, WIDTH_A), BF16),
        jax.ShapeDtypeStruct((T, WIDTH_A), BF16),
        jax.ShapeDtypeStruct((T, QK_WIDTH_B), BF16), jax.ShapeDtypeStruct((T, QK_WIDTH_B), BF16),
        jax.ShapeDtypeStruct((T, WIDTH_B), BF16), jax.ShapeDtypeStruct((T, WIDTH_B), BF16),
        jax.ShapeDtypeStruct((n_seq * win_tiles * tm, WIDTH_A), F32),
        jax.ShapeDtypeStruct((n_seq * win_tiles * tm, WIDTH_A), F32),
    )
    return pl.pallas_call(
        _in_proj_body,
        grid=(nt,),
        in_specs=[tspec(D_MODEL), pl.BlockSpec(w_bf.shape, lambda i: (0, 0)),
                  pl.BlockSpec((tm, QK_WIDTH_B), tab), pl.BlockSpec((tm, QK_WIDTH_B), tab)],
        out_specs=(tspec(WIDTH_A), tspec(WIDTH_A), tspec(WIDTH_A), tspec(QK_WIDTH_B), tspec(QK_WIDTH_B),
                   tspec(WIDTH_B), tspec(WIDTH_B),
                   pl.BlockSpec((tm, WIDTH_A), win), pl.BlockSpec((tm, WIDTH_A), win)),
        out_shape=out_shape,
        compiler_params=_params(1),
        name="in_proj",
    )(x2d, w_bf, cos_t, sin_t)


def _rotary_tables(pos):
    half = KEY_DIM_B // 2
    inv_freq = 1.0 / (10000.0 ** jnp.linspace(0.0, 1.0, half, dtype=F32))
    ang = pos.astype(F32)[:, None] * inv_freq[None, :]
    cos = jnp.cos(ang)
    sin = jnp.sin(ang)
    cos_h = jnp.concatenate([cos, cos], axis=-1)
    sin_h = jnp.concatenate([-sin, sin], axis=-1)
    return jnp.tile(cos_h, (1, N_HEADS_B)), jnp.tile(sin_h, (1, N_HEADS_B))


def _attn_body(q_ref, k_ref, v_ref, bias_ref, o_ref, lse_ref, pk_ref, pv_ref):
    n = pl.program_id(2)

    @pl.when(n == 0)
    def _():
        pk_ref[...] = jnp.zeros_like(pk_ref)
        pv_ref[...] = jnp.zeros_like(pv_ref)

    q = q_ref[...]
    kc = k_ref[...]
    vc = v_ref[...]
    kp = pk_ref[...]
    vp = pv_ref[...]
    lane = lax.broadcasted_iota(jnp.int32, (BLK, LANES), 1)
    lse_tile = jnp.zeros((BLK, LANES), F32)
    for h in range(N_HEADS_A):
        sl = slice(h * HEAD_DIM_A, (h + 1) * HEAD_DIM_A)
        kh = jnp.concatenate([kp[:, sl], kc[:, sl]], axis=0)
        vh = jnp.concatenate([vp[:, sl], vc[:, sl]], axis=0)
        s = lax.dot_general(q[:, sl], kh, (((1,), (1,)), ((), ())), preferred_element_type=F32) + bias_ref[h]
        m = jnp.max(s, axis=-1, keepdims=True)
        e = jnp.exp(s - m)
        den = jnp.sum(e, axis=-1, keepdims=True)
        o = jnp.dot(e.astype(BF16), vh, preferred_element_type=F32) / den
        o_ref[:, sl] = o.astype(BF16)
        lse_tile = jnp.where(lane == h, m + jnp.log(den), lse_tile)
    lse_ref[...] = lse_tile
    pk_ref[...] = kc
    pv_ref[...] = vc


def _attn_bias_tables(rel_bias, window, dil):
    n_keys = window // dil
    i = np.arange(BLK)[:, None]
    j = np.arange(2 * BLK)[None, :]
    rel = BLK + i - j
    in_band = (rel >= 0) & (rel <= n_keys)
    bias = rel_bias[_t5_bucket(np.clip(rel, 0, None) * dil)].astype(F32).transpose(2, 0, 1)
    later = jnp.where(jnp.asarray(in_band)[None], bias, NEG_INF)
    first = jnp.where(jnp.asarray(in_band & (j >= BLK))[None], bias, NEG_INF)
    return jnp.stack([first, later])


def _dilated_branch(q, k, v, bias_tab, dil):
    B, S, _ = q.shape
    L = S // dil
    nb = L // BLK

    def view(t):
        return t.reshape(B, L, dil * t.shape[-1])

    def cls(b, r, n):
        return (b, n, r)

    qkv_spec = pl.BlockSpec((None, BLK, WIDTH_A), cls)
    o, lse = pl.pallas_call(
        _attn_body,
        grid=(B, dil, nb),
        in_specs=[qkv_spec, qkv_spec, qkv_spec,
                  pl.BlockSpec((None, N_HEADS_A, BLK, 2 * BLK), lambda b, r, n: (jnp.minimum(n, 1), 0, 0, 0))],
        out_specs=(qkv_spec, pl.BlockSpec((None, BLK, LANES), cls)),
        out_shape=(jax.ShapeDtypeStruct((B, L, dil * WIDTH_A), BF16),
                   jax.ShapeDtypeStruct((B, L, dil * LANES), F32)),
        scratch_shapes=[pltpu.VMEM((BLK, WIDTH_A), BF16), pltpu.VMEM((BLK, WIDTH_A), BF16)],
        compiler_params=_params(3),
        name=f"dil_attn_d{dil}",
    )(view(q), view(k), view(v), bias_tab)
    return o.reshape(B, S, WIDTH_A), lse.reshape(B, S, LANES)


def _ret_body(*refs, n_branch):
    qb_ref, kb_ref, vb_ref, gb_ref = refs[:4]
    p = 4
    if n_branch:
        o_refs = refs[p:p + n_branch]
        l_refs = refs[p + n_branch:p + 2 * n_branch]
        exp_ref = refs[p + 2 * n_branch]
        p += 2 * n_branch + 1
    else:
        oa_ref = refs[p]
        p += 1
    st0_ref, dmat_ref, qdec_ref, kdec_ref, cdec_ref, cat_ref, sto_ref, st_ref = refs[p:]

    @pl.when(pl.program_id(1) == 0)
    def _():
        st_ref[...] = st0_ref[...]

    if n_branch:
        ls = [r[...] for r in l_refs]
        mx = functools.reduce(jnp.maximum, ls)
        ws = [jnp.exp(l - mx) for l in ls]
        tot = functools.reduce(lambda a, b: a + b, ws)
        oa = None
        for w, o_ref in zip(ws, o_refs):
            w = w / tot
            w_hi = w.astype(BF16)
            w_lo = (w - w_hi.astype(F32)).astype(BF16)
            w_full = (jnp.dot(w_hi, exp_ref[...], preferred_element_type=F32)
                      + jnp.dot(w_lo, exp_ref[...], preferred_element_type=F32))
            term = w_full * o_ref[...].astype(F32)
            oa = term if oa is None else oa + term
        cat_ref[:, :WIDTH_A] = oa.astype(BF16)
    else:
        cat_ref[:, :WIDTH_A] = oa_ref[...].astype(BF16)

    for h in range(N_HEADS_B):
        ks = slice(h * KEY_DIM_B, (h + 1) * KEY_DIM_B)
        vs = slice(h * VAL_DIM_B, (h + 1) * VAL_DIM_B)
        q = qb_ref[:, ks]
        k = kb_ref[:, ks]
        v = vb_ref[:, vs]
        st = st_ref[h]
        a = lax.dot_general(q, k, (((1,), (1,)), ((), ())), preferred_element_type=F32) * dmat_ref[h]
        o = (jnp.dot(a.astype(BF16), v, preferred_element_type=F32)
             + jnp.dot(q, st.astype(BF16), preferred_element_type=F32) * qdec_ref[h])
        kd = (k.astype(F32) * kdec_ref[h]).astype(BF16)
        st_new = st * cdec_ref[h] + lax.dot_general(kd, v, (((0,), (0,)), ((), ())), preferred_element_type=F32)
        st_ref[h] = st_new
        sto_ref[h] = st_new
        mu = jnp.mean(o, axis=-1, keepdims=True)
        var = jnp.mean(jnp.square(o - mu), axis=-1, keepdims=True)
        obn = (o - mu) * lax.rsqrt(var + GN_EPS)
        g = gb_ref[:, vs].astype(F32)
        gated = g * (1.0 / (1.0 + jnp.exp(-g))) * obn
        cat_ref[:, WIDTH_A + h * VAL_DIM_B:WIDTH_A + (h + 1) * VAL_DIM_B] = gated.astype(BF16)


def _decay_tables(chunk, rows):
    H = N_HEADS_B
    log_g = jnp.log(1.0 - 2.0 ** (-5.0 - jnp.arange(H, dtype=F32)))
    i = jnp.arange(rows, dtype=F32)
    live = np.arange(rows) < chunk
    diff = i[:, None] - i[None, :]
    causal = (diff >= 0) & jnp.asarray(live[:, None] & live[None, :])
    dmat = jnp.where(causal[None], jnp.exp(jnp.where(causal, diff, 0.0)[None] * log_g[:, None, None]), 0.0)
    q_decay = jnp.where(jnp.asarray(live)[None], jnp.exp((i[None, :] + 1.0) * log_g[:, None]), 0.0)
    k_decay = jnp.where(jnp.asarray(live)[None], jnp.exp((chunk - 1.0 - i)[None, :] * log_g[:, None]), 0.0)
    c_decay = jnp.exp(chunk * log_g)
    qdec = jnp.broadcast_to(q_decay[:, :, None], (H, rows, VAL_DIM_B))
    kdec = jnp.broadcast_to(k_decay[:, :, None], (H, rows, KEY_DIM_B))
    cdec = jnp.broadcast_to(c_decay[:, None, None], (H, KEY_DIM_B, VAL_DIM_B))
    return dmat.astype(F32), qdec.astype(F32), kdec.astype(F32), cdec.astype(F32)


def _ret_mix(qb, kb, vb, gb, attn, state0, chunk):
    B, S, _ = qb.shape
    rows = RET_CHUNK
    nc = S // rows
    tables = _decay_tables(chunk, rows)

    def tok(b, c):
        return (b, c, 0)

    def tspec(w):
        return pl.BlockSpec((None, rows, w), tok)

    def const(shape):
        return pl.BlockSpec(shape, lambda b, c: (0,) * len(shape))

    ins = [qb, kb, vb, gb]
    in_specs = [tspec(QK_WIDTH_B), tspec(QK_WIDTH_B), tspec(WIDTH_B), tspec(WIDTH_B)]
    if isinstance(attn, tuple):
        outs_a, lses = attn
        n_branch = len(outs_a)
        expand = np.zeros((LANES, WIDTH_A), np.float32)
        for h in range(N_HEADS_A):
            expand[h, h * HEAD_DIM_A:(h + 1) * HEAD_DIM_A] = 1.0
        ins += list(outs_a) + list(lses) + [jnp.asarray(expand, BF16)]
        in_specs += [tspec(WIDTH_A)] * n_branch + [tspec(LANES)] * n_branch + [const((LANES, WIDTH_A))]
    else:
        n_branch = 0
        ins.append(attn)
        in_specs.append(tspec(WIDTH_A))
    st_shape = (N_HEADS_B, KEY_DIM_B, VAL_DIM_B)
    st_spec = pl.BlockSpec((None,) + st_shape, lambda b, c: (b, 0, 0, 0))
    ins += [state0] + list(tables)
    in_specs += [st_spec] + [const(t.shape) for t in tables]
    return pl.pallas_call(
        functools.partial(_ret_body, n_branch=n_branch),
        grid=(B, nc),
        in_specs=in_specs,
        out_specs=(tspec(WIDTH_A + WIDTH_B), st_spec),
        out_shape=(jax.ShapeDtypeStruct((B, S, WIDTH_A + WIDTH_B), BF16),
                   jax.ShapeDtypeStruct((B,) + st_shape, F32)),
        scratch_shapes=[pltpu.VMEM(st_shape, F32)],
        compiler_params=_params(2),
        name=f"ret_mix_{n_branch}",
    )(*ins)


SAMP_Q_ROWS = 64
SAMP_NEW_ROWS = 128


def _samp_attn_body(q_ref, kc_ref, vc_ref, kn_ref, vn_ref, bc_ref, bn_ref, hm_ref, o_ref):
    q = q_ref[...]
    kc = kc_ref[...].astype(BF16)
    nt = (((1,), (1,)), ((), ()))
    s_c = lax.dot_general(q, kc, nt, preferred_element_type=F32)
    s_n = lax.dot_general(q, kn_ref[...], nt, preferred_element_type=F32)
    es_c, es_n, dens, lses = [], [], [], []
    for n in range(len(DILATED_BRANCHES)):
        sc = s_c + bc_ref[n]
        sn = s_n + bn_ref[n]
        m = jnp.maximum(jnp.max(sc, axis=-1, keepdims=True), jnp.max(sn, axis=-1, keepdims=True))
        ec = jnp.exp(sc - m)
        en = jnp.exp(sn - m)
        den = jnp.sum(ec, axis=-1, keepdims=True) + jnp.sum(en, axis=-1, keepdims=True)
        es_c.append(ec)
        es_n.append(en)
        dens.append(den)
        lses.append(m + jnp.log(den))
    mx = functools.reduce(jnp.maximum, lses)
    ws = [jnp.exp(l - mx) for l in lses]
    tot = functools.reduce(lambda a, b: a + b, ws)
    p_c = None
    p_n = None
    for w, den, ec, en in zip(ws, dens, es_c, es_n):
        coef = w / (tot * den)
        p_c = coef * ec if p_c is None else p_c + coef * ec
        p_n = coef * en if p_n is None else p_n + coef * en
    o = (jnp.dot(p_c.astype(BF16), vc_ref[...].astype(BF16), preferred_element_type=F32)
         + jnp.dot(p_n.astype(BF16), vn_ref[...], preferred_element_type=F32))
    o = o * hm_ref[...]
    o_ref[...] = jnp.sum(o.reshape(SUBLANES, N_HEADS_A, WIDTH_A), axis=1)


def _samp_bias_tables(rel_bias, w_buf, ds):
    tabs_c, tabs_n = [], []
    s = np.arange(SUBLANES)[:, None]
    live_s = s < ds
    for window, dil in DILATED_BRANCHES:
        n_keys = window // dil
        for keys, live_k, tabs in ((np.arange(w_buf)[None, :], True, tabs_c),
                                   (w_buf + np.arange(SAMP_NEW_ROWS)[None, :],
                                    np.arange(SAMP_NEW_ROWS)[None, :] < ds, tabs_n)):
            dist = w_buf + s - keys
            valid = (dist >= 0) & (dist % dil == 0) & (dist // dil <= n_keys) & live_k
            bias = rel_bias[_t5_bucket(np.clip(dist, 0, None))].astype(F32)
            tab = jnp.where(jnp.asarray(valid)[..., None], bias, NEG_INF)
            pad = jnp.where(jnp.asarray(np.broadcast_to(live_k, dist.shape))[..., None], 0.0, NEG_INF)
            tab = jnp.where(jnp.asarray(live_s)[..., None], tab, pad)
            tabs.append(tab.transpose(0, 2, 1).reshape(SAMP_Q_ROWS, keys.shape[1]))
    return jnp.stack(tabs_c), jnp.stack(tabs_n)


def _samp_attn(qa, cache_k, cache_v, k_new, v_new, rel_bias):
    DB, DS, _ = qa.shape
    W = cache_k.shape[1]
    head_of_lane = np.arange(WIDTH_A) // HEAD_DIM_A
    hmask = (np.arange(SAMP_Q_ROWS)[:, None] % N_HEADS_A == head_of_lane[None, :])
    q8 = jnp.pad(qa, ((0, 0), (0, SUBLANES - DS), (0, 0)))
    q_rows = jnp.where(jnp.asarray(hmask)[None], jnp.repeat(q8, N_HEADS_A, axis=1), jnp.zeros((), BF16))

    def pad_new(t):
        return jnp.pad(t.astype(BF16), ((0, 0), (0, SAMP_NEW_ROWS - DS), (0, 0)))

    bias_c, bias_n = _samp_bias_tables(rel_bias, W, DS)

    def per_b(w, rows):
        return pl.BlockSpec((None, rows, w), lambda b: (b, 0, 0))

    def const(a):
        return pl.BlockSpec(a.shape, lambda b: (0,) * a.ndim)

    hm = jnp.asarray(hmask, F32)
    return pl.pallas_call(
        _samp_attn_body,
        grid=(DB,),
        in_specs=[per_b(WIDTH_A, SAMP_Q_ROWS), per_b(WIDTH_A, W), per_b(WIDTH_A, W),
                  per_b(WIDTH_A, SAMP_NEW_ROWS), per_b(WIDTH_A, SAMP_NEW_ROWS),
                  const(bias_c), const(bias_n), const(hm)],
        out_specs=per_b(WIDTH_A, SUBLANES),
        out_shape=jax.ShapeDtypeStruct((DB, SUBLANES, WIDTH_A), F32),
        compiler_params=_params(1),
        name="samp_attn",
    )(q_rows, cache_k, cache_v, pad_new(k_new), pad_new(v_new), bias_c, bias_n, hm)


def _route_body(cat_ref, x_ref, wout_ref, g_ref, b_ref, rwh_ref, rwl_ref, rb_ref, tril_ref, base_ref,
                h_ref, hb_ref, route_ref, cnt_ref, *, alpha):
    @pl.when(pl.program_id(0) == 0)
    def _():
        cnt_ref[...] = base_ref[...]

    mix = jnp.dot(cat_ref[...], wout_ref[...], preferred_element_type=F32)
    y = alpha * x_ref[...] + mix
    mu = jnp.mean(y, axis=-1, keepdims=True)
    var = jnp.mean(jnp.square(y - mu), axis=-1, keepdims=True)
    h = (y - mu) * lax.rsqrt(var + LN_EPS) * g_ref[...] + b_ref[...]
    h_ref[...] = h
    hb = h.astype(BF16)
    hb_ref[...] = hb
    hl = (h - hb.astype(F32)).astype(BF16)
    logits = (jnp.dot(hb, rwh_ref[...], preferred_element_type=F32)
              + jnp.dot(hb, rwl_ref[...], preferred_element_type=F32)
              + jnp.dot(hl, rwh_ref[...], preferred_element_type=F32)) + rb_ref[...]

    tm = logits.shape[0]
    lane = lax.broadcasted_iota(jnp.int32, (tm, LANES), 1)
    work = logits
    vals, idxs = [], []
    for _ in range(TOP_K):
        m = jnp.max(work, axis=-1, keepdims=True)
        idx = jnp.min(jnp.where(work == m, lane, LANES), axis=-1, keepdims=True)
        vals.append(m)
        idxs.append(idx)
        work = jnp.where(lane == idx, -jnp.inf, work)
    es = [jnp.exp(v - vals[0]) for v in vals]
    tot = functools.reduce(lambda a, b: a + b, es)
    onehot = jnp.zeros((tm, LANES), F32)
    for idx in idxs:
        onehot = onehot + (lane == idx).astype(F32)
    before = jnp.dot(tril_ref[...], onehot.astype(BF16), preferred_element_type=F32) + cnt_ref[0:1, :]
    route = jnp.zeros((tm, LANES), F32)
    for k in range(TOP_K):
        rank = jnp.sum(jnp.where(lane == idxs[k], before, 0.0), axis=-1, keepdims=True)
        route = jnp.where(lane == k, idxs[k].astype(F32), route)
        route = jnp.where(lane == TOP_K + k, es[k] / tot, route)
        route = jnp.where(lane == 2 * TOP_K + k, rank, route)
    route_ref[...] = route
    cnt_ref[...] = cnt_ref[...] + jnp.sum(onehot, axis=0, keepdims=True)


def _out_route(cat, x2d, w_out_bf, ln_g, ln_b, rw_hi, rw_lo, rb, base, tm, alpha):
    T = x2d.shape[0]
    tril = jnp.asarray(np.tril(np.ones((tm, tm), np.float32), -1), BF16)

    def tok(i):
        return (i, 0)

    def const(a):
        return pl.BlockSpec(a.shape, lambda i: (0,) * a.ndim)

    ins = (cat, x2d, w_out_bf, ln_g, ln_b, rw_hi, rw_lo, rb, tril, base)
    in_specs = [pl.BlockSpec((tm, cat.shape[1]), tok), pl.BlockSpec((tm, D_MODEL), tok)] + [const(a) for a in ins[2:]]
    return pl.pallas_call(
        functools.partial(_route_body, alpha=alpha),
        grid=(T // tm,),
        in_specs=in_specs,
        out_specs=(pl.BlockSpec((tm, D_MODEL), tok), pl.BlockSpec((tm, D_MODEL), tok),
                   pl.BlockSpec((tm, LANES), tok), pl.BlockSpec((SUBLANES, LANES), lambda i: (0, 0))),
        out_shape=(jax.ShapeDtypeStruct((T, D_MODEL), F32), jax.ShapeDtypeStruct((T, D_MODEL), BF16),
                   jax.ShapeDtypeStruct((T, LANES), F32), jax.ShapeDtypeStruct((SUBLANES, LANES), F32)),
        compiler_params=_params(1),
        name="out_route",
    )(*ins)


MOE_CAST_ROWS = 128


def _moe_body(be_ref, nused_ref, x_ref, wgu_ref, bgu_ref, wdn_ref, bdn_ref, y_ref, wgu_bf, wdn_bf):
    b = pl.program_id(0)
    e = be_ref[b]
    prev = be_ref[jnp.maximum(b - 1, 0)]
    d_exp = wdn_ref.shape[0]

    @pl.when((b == 0) | (e != prev))
    def _():
        def cast_gu(i, c):
            r = pl.ds(pl.multiple_of(i * MOE_CAST_ROWS, MOE_CAST_ROWS), MOE_CAST_ROWS)
            wgu_bf[r, :] = wgu_ref[r, :].astype(BF16)
            return c

        def cast_dn(i, c):
            r = pl.ds(pl.multiple_of(i * MOE_CAST_ROWS, MOE_CAST_ROWS), MOE_CAST_ROWS)
            wdn_bf[r, :] = wdn_ref[r, :].astype(BF16)
            return c

        lax.fori_loop(0, wgu_ref.shape[0] // MOE_CAST_ROWS, cast_gu, 0)
        lax.fori_loop(0, d_exp // MOE_CAST_ROWS, cast_dn, 0)

    @pl.when(b < nused_ref[0])
    def _():
        x = x_ref[...]
        half = d_exp // 2
        y = None
        for c in range(2):
            lo = c * half
            gate = jnp.dot(x, wgu_bf[:, lo:lo + half], preferred_element_type=F32) + bgu_ref[:, lo:lo + half]
            up = (jnp.dot(x, wgu_bf[:, d_exp + lo:d_exp + lo + half], preferred_element_type=F32)
                  + bgu_ref[:, d_exp + lo:d_exp + lo + half])
            gate = jnp.minimum(gate, SWIGLU_LIMIT)
            up = jnp.clip(up, -SWIGLU_LIMIT, SWIGLU_LIMIT)
            act = (up + 1.0) * gate * (1.0 / (1.0 + jnp.exp(-SWIGLU_ALPHA * gate)))
            part = jnp.dot(act.astype(BF16), wdn_bf[lo:lo + half, :], preferred_element_type=F32)
            y = part if y is None else y + part
        y_ref[...] = y + bdn_ref[...]

    @pl.when(b >= nused_ref[0])
    def _():
        y_ref[...] = jnp.zeros_like(y_ref)


def _moe_ffn(xs, block_e, n_used, w_gu, b_gu, w_dn, b_dn):
    rows, D = xs.shape
    E, _, two_de = w_gu.shape
    d_exp = two_de // 2
    nb = rows // MOE_BLOCK
    grid_spec = pltpu.PrefetchScalarGridSpec(
        num_scalar_prefetch=2,
        grid=(nb,),
        in_specs=[
            pl.BlockSpec((MOE_BLOCK, D), lambda b, be, nu: (b, 0)),
            pl.BlockSpec((None, D, two_de), lambda b, be, nu: (be[b], 0, 0)),
            pl.BlockSpec((None, 1, two_de), lambda b, be, nu: (be[b], 0, 0)),
            pl.BlockSpec((None, d_exp, D), lambda b, be, nu: (be[b], 0, 0)),
            pl.BlockSpec((None, 1, D), lambda b, be, nu: (be[b], 0, 0)),
        ],
        out_specs=pl.BlockSpec((MOE_BLOCK, D), lambda b, be, nu: (b, 0)),
        scratch_shapes=[pltpu.VMEM((D, two_de), BF16), pltpu.VMEM((d_exp, D), BF16)],
    )
    return pl.pallas_call(
        _moe_body,
        grid_spec=grid_spec,
        out_shape=jax.ShapeDtypeStruct((rows, D), F32),
        compiler_params=_params(1),
        name="moe_ffn",
    )(block_e, n_used, xs, w_gu, b_gu.reshape(E, 1, two_de), w_dn, b_dn.reshape(E, 1, D))


def _ple_body(h_ref, f_ref, p_ref, g_ref, b_ref, wpg_ref, bpg_ref, wpp_ref, o_ref, *, alpha):
    y = alpha * h_ref[...] + f_ref[...]
    mu = jnp.mean(y, axis=-1, keepdims=True)
    var = jnp.mean(jnp.square(y - mu), axis=-1, keepdims=True)
    h2 = (y - mu) * lax.rsqrt(var + LN_EPS) * g_ref[...] + b_ref[...]
    z = jnp.dot(h2.astype(BF16), wpg_ref[...], preferred_element_type=F32) + bpg_ref[...]
    gate = 1.0 / (1.0 + jnp.exp(-z))
    proj = jnp.dot(p_ref[...].astype(BF16), wpp_ref[...], preferred_element_type=F32)
    o_ref[...] = h2 + gate * proj


def _ffn_ple(h, f, p, ln_g, ln_b, w_pg_bf, b_pg, w_pp_bf, tm, alpha):
    T = h.shape[0]

    def tok(i):
        return (i, 0)

    def const(a):
        return pl.BlockSpec(a.shape, lambda i: (0,) * a.ndim)

    consts = (ln_g, ln_b, w_pg_bf, b_pg, w_pp_bf)
    return pl.pallas_call(
        functools.partial(_ple_body, alpha=alpha),
        grid=(T // tm,),
        in_specs=[pl.BlockSpec((tm, D_MODEL), tok), pl.BlockSpec((tm, D_MODEL), tok),
                  pl.BlockSpec((tm, p.shape[1]), tok)] + [const(a) for a in consts],
        out_specs=pl.BlockSpec((tm, D_MODEL), tok),
        out_shape=jax.ShapeDtypeStruct((T, D_MODEL), F32),
        compiler_params=_params(1),
        name="ffn_ple",
    )(h, f, p, *consts)


def _row(v):
    return v.reshape(1, -1).astype(F32)


def kernel(x_prompt, x_sample, cache_win_k, cache_win_v, state_ret, p_prompt, p_sample, rel_bias, w_in, w_out,
           ln1_g, ln1_b, router_w, router_b, w_gate_up, b_gate_up, w_down, b_down, ln2_g, ln2_b,
           w_ple_gate, b_ple_gate, w_ple_proj):
    B, S, D = x_prompt.shape
    DB, DS, _ = x_sample.shape
    depth = w_in.shape[0]
    w_buf = cache_win_k.shape[2]
    n_exp = router_w.shape[-1]
    alpha = (2.0 * depth) ** 0.25
    assert depth == 1 and D == D_MODEL
    assert S % (BLK * MAX_DIL) == 0 and S >= WINDOW_MAX and w_buf == WINDOW_MAX and DS <= SUBLANES
    tm_p = 512
    Tp, Ts = B * S, DB * DS
    assert Tp % tm_p == 0 and Ts % SUBLANES == 0

    i = 0
    w_in_bf = w_in[i].astype(BF16)
    w_out_bf = w_out[i].astype(BF16)
    w_pg_bf = w_ple_gate[i].astype(BF16)
    w_pp_bf = w_ple_proj[i].astype(BF16)
    rw = jnp.pad(router_w[i], ((0, 0), (0, LANES - n_exp)))
    rw_hi = rw.astype(BF16)
    rw_lo = (rw - rw_hi.astype(F32)).astype(BF16)
    rb = jnp.pad(router_b[i], (0, LANES - n_exp), constant_values=NEG_INF).reshape(1, LANES)

    cos_p, sin_p = _rotary_tables(jnp.arange(S, dtype=jnp.int32))
    qa, ka, va, qb, kb, vb, gb, kf, vf = _in_proj(x_prompt.reshape(Tp, D), w_in_bf, cos_p, sin_p, tm_p,
                                                   S // tm_p, WINDOW_MAX // tm_p)

    def seq(t):
        return t.reshape(B, S, t.shape[-1])

    outs_a, lses = [], []
    for window, dil in DILATED_BRANCHES:
        o_n, l_n = _dilated_branch(seq(qa), seq(ka), seq(va), _attn_bias_tables(rel_bias, window, dil), dil)
        outs_a.append(o_n)
        lses.append(l_n)
    st_zero = jnp.zeros((B, N_HEADS_B, KEY_DIM_B, VAL_DIM_B), F32)
    cat_p, rst_p = _ret_mix(seq(qb), seq(kb), seq(vb), seq(gb), (outs_a, lses), st_zero, RET_CHUNK)
    base0 = jnp.zeros((SUBLANES, LANES), F32)
    h_p, hb_p, route_p, cnt_p = _out_route(cat_p.reshape(Tp, -1), x_prompt.reshape(Tp, D), w_out_bf,
                                           _row(ln1_g[i]), _row(ln1_b[i]), rw_hi, rw_lo, rb, base0, tm_p, alpha)

    pos_s = jnp.tile(PAST_LEN + jnp.arange(DS, dtype=jnp.int32), DB)
    cos_s, sin_s = _rotary_tables(pos_s)
    qa_s, _, _, qb_s, kb_s, vb_s, gb_s, kf_s, vf_s = _in_proj(x_sample.reshape(Ts, D), w_in_bf, cos_s, sin_s,
                                                             Ts, 1, 1)
    ck = cache_win_k[i].reshape(DB, w_buf, WIDTH_A)
    cv = cache_win_v[i].reshape(DB, w_buf, WIDTH_A)
    k_new = kf_s.reshape(DB, DS, WIDTH_A)
    v_new = vf_s.reshape(DB, DS, WIDTH_A)
    oa_s = _samp_attn(qa_s.reshape(DB, DS, WIDTH_A), ck, cv, k_new, v_new, rel_bias)

    def pad_rows(t, rows):
        t = t.reshape(DB, -1, t.shape[-1])
        return jnp.pad(t, ((0, 0), (0, rows - t.shape[1]), (0, 0)))

    cat_s, rst_s = _ret_mix(pad_rows(qb_s, RET_CHUNK), pad_rows(kb_s, RET_CHUNK), pad_rows(vb_s, RET_CHUNK),
                            pad_rows(gb_s, RET_CHUNK), pad_rows(oa_s, RET_CHUNK), state_ret[i].astype(F32), DS)
    cat_s = cat_s[:, :DS].reshape(Ts, -1)
    h_s, hb_s, route_s, cnt = _out_route(cat_s, x_sample.reshape(Ts, D), w_out_bf, _row(ln1_g[i]), _row(ln1_b[i]),
                                         rw_hi, rw_lo, rb, cnt_p, Ts, alpha)

    T = Tp + Ts
    route = jnp.concatenate([route_p, route_s], axis=0)
    top_idx = route[:, :TOP_K].astype(jnp.int32)
    gates = route[:, TOP_K:2 * TOP_K]
    rank = route[:, 2 * TOP_K:3 * TOP_K].astype(jnp.int32)
    counts = cnt[0, :n_exp].astype(jnp.int32)
    padded = (counts + MOE_BLOCK - 1) // MOE_BLOCK * MOE_BLOCK
    pad_end = jnp.cumsum(padded)
    pad_start = pad_end - padded
    dest = pad_start[top_idx] + rank
    n_blocks = -(-T * TOP_K // MOE_BLOCK) + n_exp
    rows = n_blocks * MOE_BLOCK
    row_tok = jnp.full((rows,), T, jnp.int32).at[dest.reshape(-1)].set(
        jnp.repeat(jnp.arange(T, dtype=jnp.int32), TOP_K))
    block_e = jnp.minimum(jnp.searchsorted(pad_end, jnp.arange(n_blocks, dtype=jnp.int32) * MOE_BLOCK, side='right'),
                          n_exp - 1).astype(jnp.int32)
    n_used = (pad_end[-1:] // MOE_BLOCK).astype(jnp.int32)
    hb_all = jnp.concatenate([hb_p, hb_s, jnp.zeros((1, D), BF16)], axis=0)
    xs = hb_all[row_tok]
    yb = _moe_ffn(xs, block_e, n_used, w_gate_up[i], b_gate_up[i], w_down[i], b_down[i])
    f = jnp.sum(yb[dest] * gates[..., None], axis=1)

    ple_args = (_row(ln2_g[i]), _row(ln2_b[i]), w_pg_bf, _row(b_ple_gate[i]), w_pp_bf)
    y_p = _ffn_ple(h_p, f[:Tp], p_prompt[i].reshape(Tp, D_PLE), *ple_args, tm_p, alpha)
    y_s = _ffn_ple(h_s, f[Tp:], p_sample[i].reshape(Ts, D_PLE), *ple_args, Ts, alpha)

    w_prompt = min(WINDOW_MAX, S)
    kv_shape = (1, B, w_prompt, N_HEADS_A, HEAD_DIM_A)
    ks_out = jnp.concatenate([ck[:, DS:], k_new], axis=1).reshape(1, DB, w_buf, N_HEADS_A, HEAD_DIM_A)
    vs_out = jnp.concatenate([cv[:, DS:], v_new], axis=1).reshape(1, DB, w_buf, N_HEADS_A, HEAD_DIM_A)
    return (y_p.reshape(B, S, D), y_s.reshape(DB, DS, D), kf.reshape(kv_shape), vf.reshape(kv_shape),
            rst_p[None], ks_out, vs_out, rst_s[None])
```

```python
import functools

import numpy as np
import jax
import jax.numpy as jnp
from jax import lax
from jax.experimental import pallas as pl
from jax.experimental.pallas import tpu as pltpu
from jax.experimental.pallas import tpu_sc as plsc

F32 = jnp.float32
BF16 = jnp.bfloat16

D_MODEL = 1024
D_PLE = 256
N_HEADS_A = 8
HEAD_DIM_A = 64
WIDTH_A = N_HEADS_A * HEAD_DIM_A
DILATED_BRANCHES = ((128, 1), (512, 4), (2048, 16))
BLK = 128
WINDOW_MAX = 2048
MAX_DIL = 16
NUM_BUCKETS = 32
MAX_DISTANCE = 2048
N_HEADS_B = 4
KEY_DIM_B = 64
VAL_DIM_B = 128
QK_WIDTH_B = N_HEADS_B * KEY_DIM_B
WIDTH_B = N_HEADS_B * VAL_DIM_B
RET_CHUNK = 128
GN_EPS = 1e-6
TOP_K = 4
SWIGLU_LIMIT = 7.0
SWIGLU_ALPHA = 1.702
LN_EPS = 1e-5
NEG_INF = -1e30
PAST_LEN = 16384
MOE_BLOCK = 256
LANES = 128
SUBLANES = 8
VMEM_LIMIT = 52 * 1024 * 1024


def _params(n_axes, vmem=VMEM_LIMIT):
    return pltpu.CompilerParams(dimension_semantics=("arbitrary",) * n_axes, vmem_limit_bytes=vmem)


def _t5_bucket(dist):
    dist = np.asarray(dist, dtype=np.int32)
    max_exact = NUM_BUCKETS // 2
    d = np.maximum(dist, 1).astype(np.float32)
    large = max_exact + (np.log(d / max_exact) / np.log(MAX_DISTANCE / max_exact)
                         * (NUM_BUCKETS - max_exact)).astype(np.int32)
    large = np.minimum(large, NUM_BUCKETS - 1)
    return np.where(dist < max_exact, dist, large).astype(np.int32)


def _pack_bf16_pairs(v):
    w = v.shape[1] // 2
    lo = lax.bitcast_convert_type(v[:, :w].astype(BF16).astype(F32), jnp.uint32) >> 16
    hi = lax.bitcast_convert_type(v[:, w:].astype(BF16).astype(F32), jnp.uint32) & jnp.uint32(0xFFFF0000)
    return lax.bitcast_convert_type(lo | hi, jnp.int32)


def _unpack_bf16_pairs(p):
    u = lax.bitcast_convert_type(p, jnp.uint32)
    lo = lax.bitcast_convert_type(u << 16, F32)
    hi = lax.bitcast_convert_type(u & jnp.uint32(0xFFFF0000), F32)
    return lo, hi


SC_CORES = 2
SC_SUBCORES = 16
SC_WORKERS = SC_CORES * SC_SUBCORES
SC_ALIGN = 8
SC_CHUNK_ROWS = 80


def _sc_mesh():
    return plsc.VectorSubcoreMesh(core_axis_name="c", subcore_axis_name="s")


def _sc_chunk(per_worker):
    c = max(d for d in range(SC_ALIGN, SC_CHUNK_ROWS + 1, SC_ALIGN) if per_worker % d == 0)
    return c


def _sc_scatter_rows(src, dest_flat, n_out):
    T, W = src.shape
    K = dest_flat.shape[0] // T
    per_w = T // SC_WORKERS
    assert per_w * SC_WORKERS == T and per_w % SC_ALIGN == 0
    chunk = _sc_chunk(per_w)

    @functools.partial(
        pl.kernel, mesh=_sc_mesh(), out_type=jax.ShapeDtypeStruct((n_out, W), src.dtype),
        scratch_types=[pltpu.VMEM((chunk, W), src.dtype)] + [pltpu.VMEM((chunk,), jnp.int32)] * K
        + [pltpu.SemaphoreType.DMA],
        name="sc_scatter_rows")
    def k(src_hbm, dest_hbm, out_hbm, rows_v, *rest):
        idx_vs, sem = rest[:K], rest[K]
        base = (lax.axis_index("s") * SC_CORES + lax.axis_index("c")) * per_w

        @pl.loop(0, per_w // chunk)
        def _(j):
            off = pl.multiple_of(base + j * chunk, SC_ALIGN)
            pltpu.sync_copy(src_hbm.at[pl.ds(off, chunk)], rows_v)
            for kk in range(K):
                pltpu.sync_copy(dest_hbm.at[pl.ds(kk * T + off, chunk)], idx_vs[kk])
            copies = [pltpu.async_copy(rows_v, out_hbm.at[idx_vs[kk]], sem) for kk in range(K)]
            for c in copies:
                c.wait()

    return k(src, dest_flat)


def _sc_gather_rows(table, idx):
    B = idx.shape[0]
    W = table.shape[1]
    per_w = B // SC_WORKERS
    assert per_w * SC_WORKERS == B and per_w % SC_ALIGN == 0
    chunk = _sc_chunk(per_w)

    @functools.partial(
        pl.kernel, mesh=_sc_mesh(), out_type=jax.ShapeDtypeStruct((B, W), table.dtype),
        scratch_types=[pltpu.VMEM((chunk,), jnp.int32), pltpu.VMEM((chunk, W), table.dtype), pltpu.SemaphoreType.DMA],
        name="sc_gather_rows")
    def k(table_hbm, idx_hbm, out_hbm, idx_v, rows_v, sem):
        base = (lax.axis_index("s") * SC_CORES + lax.axis_index("c")) * per_w

        @pl.loop(0, per_w // chunk)
        def _(j):
            off = pl.multiple_of(base + j * chunk, SC_ALIGN)
            pltpu.sync_copy(idx_hbm.at[pl.ds(off, chunk)], idx_v)
            pltpu.async_copy(table_hbm.at[idx_v], rows_v, sem).wait()
            pltpu.sync_copy(rows_v, out_hbm.at[pl.ds(off, chunk)])

    return k(table, idx)


def _in_proj_body(x_ref, w_ref, cos_ref, sin_ref,
                  qa_ref, ka_ref, va_ref, qb_ref, kb_ref, vb_ref, gb_ref, kf_ref, vf_ref):
    x = x_ref[...].astype(BF16)

    def proj(lo, hi):
        return jnp.dot(x, w_ref[:, lo:hi], preferred_element_type=F32)

    o = 0
    qa_ref[...] = (proj(o, o + WIDTH_A) * (HEAD_DIM_A ** -0.5)).astype(BF16)
    o += WIDTH_A
    ka = proj(o, o + WIDTH_A)
    ka_ref[...] = ka.astype(BF16)
    kf_ref[...] = ka
    o += WIDTH_A
    va = proj(o, o + WIDTH_A)
    va_ref[...] = va.astype(BF16)
    vf_ref[...] = va
    o += WIDTH_A

    cos = cos_ref[...]
    sin = sin_ref[...]
    lane = lax.broadcasted_iota(jnp.int32, cos.shape, 1)
    first_half = (lane % KEY_DIM_B) < (KEY_DIM_B // 2)

    def rot(z):
        sw = jnp.where(first_half, pltpu.roll(z, QK_WIDTH_B - KEY_DIM_B // 2, 1), pltpu.roll(z, KEY_DIM_B // 2, 1))
        return z * cos + sw * sin

    qb_ref[...] = rot(proj(o, o + QK_WIDTH_B)).astype(BF16)
    o += QK_WIDTH_B
    kb_ref[...] = (rot(proj(o, o + QK_WIDTH_B)) * (KEY_DIM_B ** -0.5)).astype(BF16)
    o += QK_WIDTH_B
    vb_ref[...] = proj(o, o + WIDTH_B).astype(BF16)
    o += WIDTH_B
    gb_ref[...] = proj(o, o + WIDTH_B).astype(BF16)


def _in_proj(x2d, w_bf, cos_t, sin_t, tm, seq_tiles, win_tiles):
    T = x2d.shape[0]
    nt = T // tm
    n_seq = nt // seq_tiles
    j0 = seq_tiles - win_tiles

    def tok(i):
        return (i, 0)

    def tab(i):
        return (i % seq_tiles, 0)

    def win(i):
        return ((i // seq_tiles) * win_tiles + jnp.maximum(i % seq_tiles - j0, 0), 0)

    def tspec(w):
        return pl.BlockSpec((tm, w), tok)

    out_shape = (
        jax.ShapeDtypeStruct((T, WIDTH_A), BF16), jax.ShapeDtypeStruct((T, WIDTH_A), BF16),
        jax.ShapeDtypeStruct((T, WIDTH_A), BF16),
        jax.ShapeDtypeStruct((T, QK_WIDTH_B), BF16), jax.ShapeDtypeStruct((T, QK_WIDTH_B), BF16),
        jax.ShapeDtypeStruct((T, WIDTH_B), BF16), jax.ShapeDtypeStruct((T, WIDTH_B), BF16),
        jax.ShapeDtypeStruct((n_seq * win_tiles * tm, WIDTH_A), F32),
        jax.ShapeDtypeStruct((n_seq * win_tiles * tm, WIDTH_A), F32),
    )
    return pl.pallas_call(
        _in_proj_body,
        grid=(nt,),
        in_specs=[tspec(D_MODEL), pl.BlockSpec(w_bf.shape, lambda i: (0, 0)),
                  pl.BlockSpec((tm, QK_WIDTH_B), tab), pl.BlockSpec((tm, QK_WIDTH_B), tab)],
        out_specs=(tspec(WIDTH_A), tspec(WIDTH_A), tspec(WIDTH_A), tspec(QK_WIDTH_B), tspec(QK_WIDTH_B),
                   tspec(WIDTH_B), tspec(WIDTH_B),
                   pl.BlockSpec((tm, WIDTH_A), win), pl.BlockSpec((tm, WIDTH_A), win)),
        out_shape=out_shape,
        compiler_params=_params(1),
        name="in_proj",
    )(x2d, w_bf, cos_t, sin_t)


def _rotary_tables(pos):
    half = KEY_DIM_B // 2
    inv_freq = 1.0 / (10000.0 ** jnp.linspace(0.0, 1.0, half, dtype=F32))
    ang = pos.astype(F32)[:, None] * inv_freq[None, :]
    cos = jnp.cos(ang)
    sin = jnp.sin(ang)
    cos_h = jnp.concatenate([cos, cos], axis=-1)
    sin_h = jnp.concatenate([-sin, sin], axis=-1)
    return jnp.tile(cos_h, (1, N_HEADS_B)), jnp.tile(sin_h, (1, N_HEADS_B))


def _attn_body(q_ref, k_ref, v_ref, bias_ref, o_ref, lse_ref, pk_ref, pv_ref):
    n = pl.program_id(2)

    @pl.when(n == 0)
    def _():
        pk_ref[...] = jnp.zeros_like(pk_ref)
        pv_ref[...] = jnp.zeros_like(pv_ref)

    q = q_ref[...]
    kc = k_ref[...]
    vc = v_ref[...]
    kp = pk_ref[...]
    vp = pv_ref[...]
    lane = lax.broadcasted_iota(jnp.int32, (BLK, LANES), 1)
    lse_tile = jnp.zeros((BLK, LANES), F32)
    for h in range(N_HEADS_A):
        sl = slice(h * HEAD_DIM_A, (h + 1) * HEAD_DIM_A)
        kh = jnp.concatenate([kp[:, sl], kc[:, sl]], axis=0)
        vh = jnp.concatenate([vp[:, sl], vc[:, sl]], axis=0)
        s = lax.dot_general(q[:, sl], kh, (((1,), (1,)), ((), ())), preferred_element_type=F32) + bias_ref[h]
        m = jnp.max(s, axis=-1, keepdims=True)
        e = jnp.exp(s - m)
        den = jnp.sum(e, axis=-1, keepdims=True)
        o = jnp.dot(e.astype(BF16), vh, preferred_element_type=F32) / den
        o_ref[:, sl] = o.astype(BF16)
        lse_tile = jnp.where(lane == h, m + jnp.log(den), lse_tile)
    lse_ref[...] = lse_tile
    pk_ref[...] = kc
    pv_ref[...] = vc


def _attn_bias_tables(rel_bias, window, dil):
    n_keys = window // dil
    i = np.arange(BLK)[:, None]
    j = np.arange(2 * BLK)[None, :]
    rel = BLK + i - j
    in_band = (rel >= 0) & (rel <= n_keys)
    bias = rel_bias[_t5_bucket(np.clip(rel, 0, None) * dil)].astype(F32).transpose(2, 0, 1)
    later = jnp.where(jnp.asarray(in_band)[None], bias, NEG_INF)
    first = jnp.where(jnp.asarray(in_band & (j >= BLK))[None], bias, NEG_INF)
    return jnp.stack([first, later])


def _dilated_branch(q, k, v, bias_tab, dil):
    B, S, _ = q.shape
    L = S // dil
    nb = L // BLK

    def view(t):
        return t.reshape(B, L, dil * t.shape[-1])

    def cls(b, r, n):
        return (b, n, r)

    qkv_spec = pl.BlockSpec((None, BLK, WIDTH_A), cls)
    o, lse = pl.pallas_call(
        _attn_body,
        grid=(B, dil, nb),
        in_specs=[qkv_spec, qkv_spec, qkv_spec,
                  pl.BlockSpec((None, N_HEADS_A, BLK, 2 * BLK), lambda b, r, n: (jnp.minimum(n, 1), 0, 0, 0))],
        out_specs=(qkv_spec, pl.BlockSpec((None, BLK, LANES), cls)),
        out_shape=(jax.ShapeDtypeStruct((B, L, dil * WIDTH_A), BF16),
                   jax.ShapeDtypeStruct((B, L, dil * LANES), F32)),
        scratch_shapes=[pltpu.VMEM((BLK, WIDTH_A), BF16), pltpu.VMEM((BLK, WIDTH_A), BF16)],
        compiler_params=_params(3),
        name=f"dil_attn_d{dil}",
    )(view(q), view(k), view(v), bias_tab)
    return o.reshape(B, S, WIDTH_A), lse.reshape(B, S, LANES)


def _ret_body(*refs, n_branch):
    qb_ref, kb_ref, vb_ref, gb_ref = refs[:4]
    p = 4
    if n_branch:
        o_refs = refs[p:p + n_branch]
        l_refs = refs[p + n_branch:p + 2 * n_branch]
        exp_ref = refs[p + 2 * n_branch]
        p += 2 * n_branch + 1
    else:
        oa_ref = refs[p]
        p += 1
    st0_ref, dmat_ref, qdec_ref, kdec_ref, cdec_ref, cat_ref, sto_ref, st_ref = refs[p:]

    @pl.when(pl.program_id(1) == 0)
    def _():
        st_ref[...] = st0_ref[...]

    if n_branch:
        ls = [r[...] for r in l_refs]
        mx = functools.reduce(jnp.maximum, ls)
        ws = [jnp.exp(l - mx) for l in ls]
        tot = functools.reduce(lambda a, b: a + b, ws)
        oa = None
        for w, o_ref in zip(ws, o_refs):
            w = w / tot
            w_hi = w.astype(BF16)
            w_lo = (w - w_hi.astype(F32)).astype(BF16)
            w_full = (jnp.dot(w_hi, exp_ref[...], preferred_element_type=F32)
                      + jnp.dot(w_lo, exp_ref[...], preferred_element_type=F32))
            term = w_full * o_ref[...].astype(F32)
            oa = term if oa is None else oa + term
        cat_ref[:, :WIDTH_A] = oa.astype(BF16)
    else:
        cat_ref[:, :WIDTH_A] = oa_ref[...].astype(BF16)

    for h in range(N_HEADS_B):
        ks = slice(h * KEY_DIM_B, (h + 1) * KEY_DIM_B)
        vs = slice(h * VAL_DIM_B, (h + 1) * VAL_DIM_B)
        q = qb_ref[:, ks]
        k = kb_ref[:, ks]
        v = vb_ref[:, vs]
        st = st_ref[h]
        a = lax.dot_general(q, k, (((1,), (1,)), ((), ())), preferred_element_type=F32) * dmat_ref[h]
        o = (jnp.dot(a.astype(BF16), v, preferred_element_type=F32)
             + jnp.dot(q, st.astype(BF16), preferred_element_type=F32) * qdec_ref[h])
        kd = (k.astype(F32) * kdec_ref[h]).astype(BF16)
        st_new = st * cdec_ref[h] + lax.dot_general(kd, v, (((0,), (0,)), ((), ())), preferred_element_type=F32)
        st_ref[h] = st_new
        sto_ref[h] = st_new
        mu = jnp.mean(o, axis=-1, keepdims=True)
        var = jnp.mean(jnp.square(o - mu), axis=-1, keepdims=True)
        obn = (o - mu) * lax.rsqrt(var + GN_EPS)
        g = gb_ref[:, vs].astype(F32)
        gated = g * (1.0 / (1.0 + jnp.exp(-g))) * obn
        cat_ref[:, WIDTH_A + h * VAL_DIM_B:WIDTH_A + (h + 1) * VAL_DIM_B] = gated.astype(BF16)


def _decay_tables(chunk, rows):
    H = N_HEADS_B
    log_g = jnp.log(1.0 - 2.0 ** (-5.0 - jnp.arange(H, dtype=F32)))
    i = jnp.arange(rows, dtype=F32)
    live = np.arange(rows) < chunk
    diff = i[:, None] - i[None, :]
    causal = (diff >= 0) & jnp.asarray(live[:, None] & live[None, :])
    dmat = jnp.where(causal[None], jnp.exp(jnp.where(causal, diff, 0.0)[None] * log_g[:, None, None]), 0.0)
    q_decay = jnp.where(jnp.asarray(live)[None], jnp.exp((i[None, :] + 1.0) * log_g[:, None]), 0.0)
    k_decay = jnp.where(jnp.asarray(live)[None], jnp.exp((chunk - 1.0 - i)[None, :] * log_g[:, None]), 0.0)
    c_decay = jnp.exp(chunk * log_g)
    qdec = jnp.broadcast_to(q_decay[:, :, None], (H, rows, VAL_DIM_B))
    kdec = jnp.broadcast_to(k_decay[:, :, None], (H, rows, KEY_DIM_B))
    cdec = jnp.broadcast_to(c_decay[:, None, None], (H, KEY_DIM_B, VAL_DIM_B))
    return dmat.astype(F32), qdec.astype(F32), kdec.astype(F32), cdec.astype(F32)


def _ret_mix(qb, kb, vb, gb, attn, state0, chunk):
    B, S, _ = qb.shape
    rows = RET_CHUNK
    nc = S // rows
    tables = _decay_tables(chunk, rows)

    def tok(b, c):
        return (b, c, 0)

    def tspec(w):
        return pl.BlockSpec((None, rows, w), tok)

    def const(shape):
        return pl.BlockSpec(shape, lambda b, c: (0,) * len(shape))

    ins = [qb, kb, vb, gb]
    in_specs = [tspec(QK_WIDTH_B), tspec(QK_WIDTH_B), tspec(WIDTH_B), tspec(WIDTH_B)]
    if isinstance(attn, tuple):
        outs_a, lses = attn
        n_branch = len(outs_a)
        expand = np.zeros((LANES, WIDTH_A), np.float32)
        for h in range(N_HEADS_A):
            expand[h, h * HEAD_DIM_A:(h + 1) * HEAD_DIM_A] = 1.0
        ins += list(outs_a) + list(lses) + [jnp.asarray(expand, BF16)]
        in_specs += [tspec(WIDTH_A)] * n_branch + [tspec(LANES)] * n_branch + [const((LANES, WIDTH_A))]
    else:
        n_branch = 0
        ins.append(attn)
        in_specs.append(tspec(WIDTH_A))
    st_shape = (N_HEADS_B, KEY_DIM_B, VAL_DIM_B)
    st_spec = pl.BlockSpec((None,) + st_shape, lambda b, c: (b, 0, 0, 0))
    ins += [state0] + list(tables)
    in_specs += [st_spec] + [const(t.shape) for t in tables]
    return pl.pallas_call(
        functools.partial(_ret_body, n_branch=n_branch),
        grid=(B, nc),
        in_specs=in_specs,
        out_specs=(tspec(WIDTH_A + WIDTH_B), st_spec),
        out_shape=(jax.ShapeDtypeStruct((B, S, WIDTH_A + WIDTH_B), BF16),
                   jax.ShapeDtypeStruct((B,) + st_shape, F32)),
        scratch_shapes=[pltpu.VMEM(st_shape, F32)],
        compiler_params=_params(2),
        name=f"ret_mix_{n_branch}",
    )(*ins)


SAMP_Q_ROWS = 64
SAMP_NEW_ROWS = 128


def _samp_attn_body(q_ref, kc_ref, vc_ref, kn_ref, vn_ref, bc_ref, bn_ref, hm_ref, o_ref):
    q = q_ref[...]
    kc = kc_ref[...].astype(BF16)
    nt = (((1,), (1,)), ((), ()))
    s_c = lax.dot_general(q, kc, nt, preferred_element_type=F32)
    s_n = lax.dot_general(q, kn_ref[...], nt, preferred_element_type=F32)
    es_c, es_n, dens, lses = [], [], [], []
    for n in range(len(DILATED_BRANCHES)):
        sc = s_c + bc_ref[n]
        sn = s_n + bn_ref[n]
        m = jnp.maximum(jnp.max(sc, axis=-1, keepdims=True), jnp.max(sn, axis=-1, keepdims=True))
        ec = jnp.exp(sc - m)
        en = jnp.exp(sn - m)
        den = jnp.sum(ec, axis=-1, keepdims=True) + jnp.sum(en, axis=-1, keepdims=True)
        es_c.append(ec)
        es_n.append(en)
        dens.append(den)
        lses.append(m + jnp.log(den))
    mx = functools.reduce(jnp.maximum, lses)
    ws = [jnp.exp(l - mx) for l in lses]
    tot = functools.reduce(lambda a, b: a + b, ws)
    p_c = None
    p_n = None
    for w, den, ec, en in zip(ws, dens, es_c, es_n):
        coef = w / (tot * den)
        p_c = coef * ec if p_c is None else p_c + coef * ec
        p_n = coef * en if p_n is None else p_n + coef * en
    o = (jnp.dot(p_c.astype(BF16), vc_ref[...].astype(BF16), preferred_element_type=F32)
         + jnp.dot(p_n.astype(BF16), vn_ref[...], preferred_element_type=F32))
    o = o * hm_ref[...]
    o_ref[...] = jnp.sum(o.reshape(SUBLANES, N_HEADS_A, WIDTH_A), axis=1)


def _samp_bias_tables(rel_bias, w_buf, ds):
    tabs_c, tabs_n = [], []
    s = np.arange(SUBLANES)[:, None]
    live_s = s < ds
    for window, dil in DILATED_BRANCHES:
        n_keys = window // dil
        for keys, live_k, tabs in ((np.arange(w_buf)[None, :], True, tabs_c),
                                   (w_buf + np.arange(SAMP_NEW_ROWS)[None, :],
                                    np.arange(SAMP_NEW_ROWS)[None, :] < ds, tabs_n)):
            dist = w_buf + s - keys
            valid = (dist >= 0) & (dist % dil == 0) & (dist // dil <= n_keys) & live_k
            bias = rel_bias[_t5_bucket(np.clip(dist, 0, None))].astype(F32)
            tab = jnp.where(jnp.asarray(valid)[..., None], bias, NEG_INF)
            pad = jnp.where(jnp.asarray(np.broadcast_to(live_k, dist.shape))[..., None], 0.0, NEG_INF)
            tab = jnp.where(jnp.asarray(live_s)[..., None], tab, pad)
            tabs.append(tab.transpose(0, 2, 1).reshape(SAMP_Q_ROWS, keys.shape[1]))
    return jnp.stack(tabs_c), jnp.stack(tabs_n)


def _samp_attn(qa, cache_k, cache_v, k_new, v_new, rel_bias):
    DB, DS, _ = qa.shape
    W = cache_k.shape[1]
    head_of_lane = np.arange(WIDTH_A) // HEAD_DIM_A
    hmask = (np.arange(SAMP_Q_ROWS)[:, None] % N_HEADS_A == head_of_lane[None, :])
    q8 = jnp.pad(qa, ((0, 0), (0, SUBLANES - DS), (0, 0)))
    q_rows = jnp.where(jnp.asarray(hmask)[None], jnp.repeat(q8, N_HEADS_A, axis=1), jnp.zeros((), BF16))

    def pad_new(t):
        return jnp.pad(t.astype(BF16), ((0, 0), (0, SAMP_NEW_ROWS - DS), (0, 0)))

    bias_c, bias_n = _samp_bias_tables(rel_bias, W, DS)

    def per_b(w, rows):
        return pl.BlockSpec((None, rows, w), lambda b: (b, 0, 0))

    def const(a):
        return pl.BlockSpec(a.shape, lambda b: (0,) * a.ndim)

    hm = jnp.asarray(hmask, F32)
    return pl.pallas_call(
        _samp_attn_body,
        grid=(DB,),
        in_specs=[per_b(WIDTH_A, SAMP_Q_ROWS), per_b(WIDTH_A, W), per_b(WIDTH_A, W),
                  per_b(WIDTH_A, SAMP_NEW_ROWS), per_b(WIDTH_A, SAMP_NEW_ROWS),
                  const(bias_c), const(bias_n), const(hm)],
        out_specs=per_b(WIDTH_A, SUBLANES),
        out_shape=jax.ShapeDtypeStruct((DB, SUBLANES, WIDTH_A), F32),
        compiler_params=_params(1),
        name="samp_attn",
    )(q_rows, cache_k, cache_v, pad_new(k_new), pad_new(v_new), bias_c, bias_n, hm)


def _route_body(cat_ref, x_ref, wout_ref, g_ref, b_ref, rwh_ref, rwl_ref, rb_ref, tril_ref, base_ref,
                h_ref, hp_ref, route_ref, cnt_ref, *, alpha):
    @pl.when(pl.program_id(0) == 0)
    def _():
        cnt_ref[...] = base_ref[...]

    mix = jnp.dot(cat_ref[...], wout_ref[...], preferred_element_type=F32)
    y = alpha * x_ref[...] + mix
    mu = jnp.mean(y, axis=-1, keepdims=True)
    var = jnp.mean(jnp.square(y - mu), axis=-1, keepdims=True)
    h = (y - mu) * lax.rsqrt(var + LN_EPS) * g_ref[...] + b_ref[...]
    h_ref[...] = h
    hb = h.astype(BF16)
    hp_ref[...] = _pack_bf16_pairs(h)
    hl = (h - hb.astype(F32)).astype(BF16)
    logits = (jnp.dot(hb, rwh_ref[...], preferred_element_type=F32)
              + jnp.dot(hb, rwl_ref[...], preferred_element_type=F32)
              + jnp.dot(hl, rwh_ref[...], preferred_element_type=F32)) + rb_ref[...]

    tm = logits.shape[0]
    lane = lax.broadcasted_iota(jnp.int32, (tm, LANES), 1)
    work = logits
    vals, idxs = [], []
    for _ in range(TOP_K):
        m = jnp.max(work, axis=-1, keepdims=True)
        idx = jnp.min(jnp.where(work == m, lane, LANES), axis=-1, keepdims=True)
        vals.append(m)
        idxs.append(idx)
        work = jnp.where(lane == idx, -jnp.inf, work)
    es = [jnp.exp(v - vals[0]) for v in vals]
    tot = functools.reduce(lambda a, b: a + b, es)
    onehot = jnp.zeros((tm, LANES), F32)
    for idx in idxs:
        onehot = onehot + (lane == idx).astype(F32)
    before = jnp.dot(tril_ref[...], onehot.astype(BF16), preferred_element_type=F32) + cnt_ref[0:1, :]
    route = jnp.zeros((tm, LANES), F32)
    for k in range(TOP_K):
        rank = jnp.sum(jnp.where(lane == idxs[k], before, 0.0), axis=-1, keepdims=True)
        route = jnp.where(lane == k, idxs[k].astype(F32), route)
        route = jnp.where(lane == TOP_K + k, es[k] / tot, route)
        route = jnp.where(lane == 2 * TOP_K + k, rank, route)
    route_ref[...] = route
    cnt_ref[...] = cnt_ref[...] + jnp.sum(onehot, axis=0, keepdims=True)


def _out_route(cat, x2d, w_out_bf, ln_g, ln_b, rw_hi, rw_lo, rb, base, tm, alpha):
    T = x2d.shape[0]
    tril = jnp.asarray(np.tril(np.ones((tm, tm), np.float32), -1), BF16)

    def tok(i):
        return (i, 0)

    def const(a):
        return pl.BlockSpec(a.shape, lambda i: (0,) * a.ndim)

    ins = (cat, x2d, w_out_bf, ln_g, ln_b, rw_hi, rw_lo, rb, tril, base)
    in_specs = [pl.BlockSpec((tm, cat.shape[1]), tok), pl.BlockSpec((tm, D_MODEL), tok)] + [const(a) for a in ins[2:]]
    return pl.pallas_call(
        functools.partial(_route_body, alpha=alpha),
        grid=(T // tm,),
        in_specs=in_specs,
        out_specs=(pl.BlockSpec((tm, D_MODEL), tok), pl.BlockSpec((tm, D_MODEL // 2), tok),
                   pl.BlockSpec((tm, LANES), tok), pl.BlockSpec((SUBLANES, LANES), lambda i: (0, 0))),
        out_shape=(jax.ShapeDtypeStruct((T, D_MODEL), F32), jax.ShapeDtypeStruct((T, D_MODEL // 2), jnp.int32),
                   jax.ShapeDtypeStruct((T, LANES), F32), jax.ShapeDtypeStruct((SUBLANES, LANES), F32)),
        compiler_params=_params(1),
        name="out_route",
    )(*ins)


MOE_CAST_ROWS = 128


def _moe_body(be_ref, nused_ref, x_ref, wgu_ref, bgu_ref, wdn_ref, bdn_ref, y_ref, wgu_bf, wdn_bf):
    b = pl.program_id(0)
    e = be_ref[b]
    prev = be_ref[jnp.maximum(b - 1, 0)]
    d_exp = wdn_ref.shape[0]

    @pl.when((b == 0) | (e != prev))
    def _():
        def cast_gu(i, c):
            r = pl.ds(pl.multiple_of(i * MOE_CAST_ROWS, MOE_CAST_ROWS), MOE_CAST_ROWS)
            wgu_bf[r, :] = wgu_ref[r, :].astype(BF16)
            return c

        def cast_dn(i, c):
            r = pl.ds(pl.multiple_of(i * MOE_CAST_ROWS, MOE_CAST_ROWS), MOE_CAST_ROWS)
            wdn_bf[r, :] = wdn_ref[r, :].astype(BF16)
            return c

        lax.fori_loop(0, wgu_ref.shape[0] // MOE_CAST_ROWS, cast_gu, 0)
        lax.fori_loop(0, d_exp // MOE_CAST_ROWS, cast_dn, 0)

    @pl.when(b < nused_ref[0])
    def _():
        x_lo, x_hi = _unpack_bf16_pairs(x_ref[...])
        x_lo = x_lo.astype(BF16)
        x_hi = x_hi.astype(BF16)
        dh = x_lo.shape[1]

        def xw(cols):
            return (jnp.dot(x_lo, wgu_bf[:dh, cols], preferred_element_type=F32)
                    + jnp.dot(x_hi, wgu_bf[dh:, cols], preferred_element_type=F32) + bgu_ref[:, cols])

        half = d_exp // 2
        y = None
        for c in range(2):
            lo = c * half
            gate = jnp.minimum(xw(slice(lo, lo + half)), SWIGLU_LIMIT)
            up = jnp.clip(xw(slice(d_exp + lo, d_exp + lo + half)), -SWIGLU_LIMIT, SWIGLU_LIMIT)
            act = (up + 1.0) * gate * (1.0 / (1.0 + jnp.exp(-SWIGLU_ALPHA * gate)))
            part = jnp.dot(act.astype(BF16), wdn_bf[lo:lo + half, :], preferred_element_type=F32)
            y = part if y is None else y + part
        y_ref[...] = _pack_bf16_pairs(y + bdn_ref[...])

    @pl.when(b >= nused_ref[0])
    def _():
        y_ref[...] = jnp.zeros_like(y_ref)


def _moe_ffn(xs, rows, block_e, n_used, w_gu, b_gu, w_dn, b_dn):
    E, D, two_de = w_gu.shape
    d_exp = two_de // 2
    nb = rows // MOE_BLOCK
    grid_spec = pltpu.PrefetchScalarGridSpec(
        num_scalar_prefetch=2,
        grid=(nb,),
        in_specs=[
            pl.BlockSpec((MOE_BLOCK, D // 2), lambda b, be, nu: (b, 0)),
            pl.BlockSpec((None, D, two_de), lambda b, be, nu: (be[b], 0, 0)),
            pl.BlockSpec((None, 1, two_de), lambda b, be, nu: (be[b], 0, 0)),
            pl.BlockSpec((None, d_exp, D), lambda b, be, nu: (be[b], 0, 0)),
            pl.BlockSpec((None, 1, D), lambda b, be, nu: (be[b], 0, 0)),
        ],
        out_specs=pl.BlockSpec((MOE_BLOCK, D // 2), lambda b, be, nu: (b, 0)),
        scratch_shapes=[pltpu.VMEM((D, two_de), BF16), pltpu.VMEM((d_exp, D), BF16)],
    )
    return pl.pallas_call(
        _moe_body,
        grid_spec=grid_spec,
        out_shape=jax.ShapeDtypeStruct((rows, D // 2), jnp.int32),
        compiler_params=_params(1),
        name="moe_ffn",
    )(block_e, n_used, xs, w_gu, b_gu.reshape(E, 1, two_de), w_dn, b_dn.reshape(E, 1, D))


def _ple_body(h_ref, ys_ref, route_ref, p_ref, g_ref, b_ref, wpg_ref, bpg_ref, wpp_ref, o_ref, *, alpha):
    route = route_ref[...]
    f_lo = None
    f_hi = None
    for k in range(TOP_K):
        lo, hi = _unpack_bf16_pairs(ys_ref[k])
        g = route[:, TOP_K + k:TOP_K + k + 1]
        f_lo = g * lo if f_lo is None else f_lo + g * lo
        f_hi = g * hi if f_hi is None else f_hi + g * hi
    y = alpha * h_ref[...] + jnp.concatenate([f_lo, f_hi], axis=1)
    mu = jnp.mean(y, axis=-1, keepdims=True)
    var = jnp.mean(jnp.square(y - mu), axis=-1, keepdims=True)
    h2 = (y - mu) * lax.rsqrt(var + LN_EPS) * g_ref[...] + b_ref[...]
    z = jnp.dot(h2.astype(BF16), wpg_ref[...], preferred_element_type=F32) + bpg_ref[...]
    gate = 1.0 / (1.0 + jnp.exp(-z))
    proj = jnp.dot(p_ref[...].astype(BF16), wpp_ref[...], preferred_element_type=F32)
    o_ref[...] = h2 + gate * proj


def _ffn_ple(h, y_slots, route, p, ln_g, ln_b, w_pg_bf, b_pg, w_pp_bf, tm, tile0, alpha):
    T = h.shape[0]

    def tok(i):
        return (i, 0)

    def const(a):
        return pl.BlockSpec(a.shape, lambda i: (0,) * a.ndim)

    consts = (ln_g, ln_b, w_pg_bf, b_pg, w_pp_bf)
    return pl.pallas_call(
        functools.partial(_ple_body, alpha=alpha),
        grid=(T // tm,),
        in_specs=[pl.BlockSpec((tm, D_MODEL), tok),
                  pl.BlockSpec((TOP_K, tm, D_MODEL // 2), lambda i: (0, tile0 + i, 0)),
                  pl.BlockSpec((tm, LANES), tok),
                  pl.BlockSpec((tm, p.shape[1]), tok)] + [const(a) for a in consts],
        out_specs=pl.BlockSpec((tm, D_MODEL), tok),
        out_shape=jax.ShapeDtypeStruct((T, D_MODEL), F32),
        compiler_params=_params(1),
        name="ffn_ple",
    )(h, y_slots, route, p, *consts)


def _row(v):
    return v.reshape(1, -1).astype(F32)


def kernel(x_prompt, x_sample, cache_win_k, cache_win_v, state_ret, p_prompt, p_sample, rel_bias, w_in, w_out,
           ln1_g, ln1_b, router_w, router_b, w_gate_up, b_gate_up, w_down, b_down, ln2_g, ln2_b,
           w_ple_gate, b_ple_gate, w_ple_proj):
    B, S, D = x_prompt.shape
    DB, DS, _ = x_sample.shape
    depth = w_in.shape[0]
    w_buf = cache_win_k.shape[2]
    n_exp = router_w.shape[-1]
    alpha = (2.0 * depth) ** 0.25
    assert depth == 1 and D == D_MODEL
    assert S % (BLK * MAX_DIL) == 0 and S >= WINDOW_MAX and w_buf == WINDOW_MAX and DS <= SUBLANES
    tm_p = 512
    Tp, Ts = B * S, DB * DS
    assert Tp % tm_p == 0 and Ts % SUBLANES == 0

    i = 0
    w_in_bf = w_in[i].astype(BF16)
    w_out_bf = w_out[i].astype(BF16)
    w_pg_bf = w_ple_gate[i].astype(BF16)
    w_pp_bf = w_ple_proj[i].astype(BF16)
    rw = jnp.pad(router_w[i], ((0, 0), (0, LANES - n_exp)))
    rw_hi = rw.astype(BF16)
    rw_lo = (rw - rw_hi.astype(F32)).astype(BF16)
    rb = jnp.pad(router_b[i], (0, LANES - n_exp), constant_values=NEG_INF).reshape(1, LANES)

    cos_p, sin_p = _rotary_tables(jnp.arange(S, dtype=jnp.int32))
    qa, ka, va, qb, kb, vb, gb, kf, vf = _in_proj(x_prompt.reshape(Tp, D), w_in_bf, cos_p, sin_p, tm_p,
                                                   S // tm_p, WINDOW_MAX // tm_p)

    def seq(t):
        return t.reshape(B, S, t.shape[-1])

    outs_a, lses = [], []
    for window, dil in DILATED_BRANCHES:
        o_n, l_n = _dilated_branch(seq(qa), seq(ka), seq(va), _attn_bias_tables(rel_bias, window, dil), dil)
        outs_a.append(o_n)
        lses.append(l_n)
    st_zero = jnp.zeros((B, N_HEADS_B, KEY_DIM_B, VAL_DIM_B), F32)
    cat_p, rst_p = _ret_mix(seq(qb), seq(kb), seq(vb), seq(gb), (outs_a, lses), st_zero, RET_CHUNK)
    base0 = jnp.zeros((SUBLANES, LANES), F32)
    h_p, hp_p, route_p, cnt_p = _out_route(cat_p.reshape(Tp, -1), x_prompt.reshape(Tp, D), w_out_bf,
                                           _row(ln1_g[i]), _row(ln1_b[i]), rw_hi, rw_lo, rb, base0, tm_p, alpha)

    pos_s = jnp.tile(PAST_LEN + jnp.arange(DS, dtype=jnp.int32), DB)
    cos_s, sin_s = _rotary_tables(pos_s)
    qa_s, _, _, qb_s, kb_s, vb_s, gb_s, kf_s, vf_s = _in_proj(x_sample.reshape(Ts, D), w_in_bf, cos_s, sin_s,
                                                             Ts, 1, 1)
    ck = cache_win_k[i].reshape(DB, w_buf, WIDTH_A)
    cv = cache_win_v[i].reshape(DB, w_buf, WIDTH_A)
    k_new = kf_s.reshape(DB, DS, WIDTH_A)
    v_new = vf_s.reshape(DB, DS, WIDTH_A)
    oa_s = _samp_attn(qa_s.reshape(DB, DS, WIDTH_A), ck, cv, k_new, v_new, rel_bias)

    def pad_rows(t, rows):
        t = t.reshape(DB, -1, t.shape[-1])
        return jnp.pad(t, ((0, 0), (0, rows - t.shape[1]), (0, 0)))

    cat_s, rst_s = _ret_mix(pad_rows(qb_s, RET_CHUNK), pad_rows(kb_s, RET_CHUNK), pad_rows(vb_s, RET_CHUNK),
                            pad_rows(gb_s, RET_CHUNK), pad_rows(oa_s, RET_CHUNK), state_ret[i].astype(F32), DS)
    cat_s = cat_s[:, :DS].reshape(Ts, -1)
    h_s, hp_s, route_s, cnt = _out_route(cat_s, x_sample.reshape(Ts, D), w_out_bf, _row(ln1_g[i]), _row(ln1_b[i]),
                                         rw_hi, rw_lo, rb, cnt_p, Ts, alpha)

    T = Tp + Ts
    t_align = SC_WORKERS * SC_ALIGN
    T_pad = -(-T // t_align) * t_align
    route = jnp.concatenate([route_p, route_s], axis=0)
    top_idx = route[:, :TOP_K].astype(jnp.int32)
    rank = route[:, 2 * TOP_K:3 * TOP_K].astype(jnp.int32)
    counts = cnt[0, :n_exp].astype(jnp.int32)
    padded = (counts + MOE_BLOCK - 1) // MOE_BLOCK * MOE_BLOCK
    pad_end = jnp.cumsum(padded)
    pad_start = pad_end - padded
    dest = pad_start[top_idx] + rank
    n_blocks = -(-T * TOP_K // MOE_BLOCK) + n_exp
    rows = n_blocks * MOE_BLOCK
    block_start = jnp.arange(n_blocks, dtype=jnp.int32) * MOE_BLOCK
    block_e = jnp.minimum(jnp.sum(pad_end[None, :] <= block_start[:, None], axis=1), n_exp - 1).astype(jnp.int32)
    n_used = (pad_end[-1:] // MOE_BLOCK).astype(jnp.int32)
    n_fill = T_pad - T
    spare = rows + jnp.arange(n_fill * TOP_K, dtype=jnp.int32).reshape(n_fill, TOP_K)
    dest_sc = jnp.concatenate([dest, spare], axis=0).T.reshape(-1)
    dest_ga = jnp.concatenate([dest, jnp.zeros((n_fill, TOP_K), jnp.int32)], axis=0).T.reshape(-1)
    hp_all = jnp.concatenate([hp_p, hp_s, jnp.zeros((n_fill, D // 2), jnp.int32)], axis=0)
    xs = _sc_scatter_rows(hp_all, dest_sc, rows + n_fill * TOP_K)
    ys = _moe_ffn(xs, rows, block_e, n_used, w_gate_up[i], b_gate_up[i], w_down[i], b_down[i])
    y_slots = _sc_gather_rows(ys, dest_ga).reshape(TOP_K, T_pad, D // 2)

    ple_args = (_row(ln2_g[i]), _row(ln2_b[i]), w_pg_bf, _row(b_ple_gate[i]), w_pp_bf)
    assert Tp % Ts == 0
    y_p = _ffn_ple(h_p, y_slots, route_p, p_prompt[i].reshape(Tp, D_PLE), *ple_args, tm_p, 0, alpha)
    y_s = _ffn_ple(h_s, y_slots, route_s, p_sample[i].reshape(Ts, D_PLE), *ple_args, Ts, Tp // Ts, alpha)

    w_prompt = min(WINDOW_MAX, S)
    kv_shape = (1, B, w_prompt, N_HEADS_A, HEAD_DIM_A)
    ks_out = jnp.concatenate([ck[:, DS:], k_new], axis=1).reshape(1, DB, w_buf, N_HEADS_A, HEAD_DIM_A)
    vs_out = jnp.concatenate([cv[:, DS:], v_new], axis=1).reshape(1, DB, w_buf, N_HEADS_A, HEAD_DIM_A)
    return (y_p.reshape(B, S, D), y_s.reshape(DB, DS, D), kf.reshape(kv_shape), vf.reshape(kv_shape),
            rst_p[None], ks_out, vs_out, rst_s[None])
```

```python
import functools

import numpy as np
import jax
import jax.numpy as jnp
from jax import lax
from jax.experimental import pallas as pl
from jax.experimental.pallas import tpu as pltpu
from jax.experimental.pallas import tpu_sc as plsc

F32 = jnp.float32
BF16 = jnp.bfloat16

D_MODEL = 1024
D_PLE = 256
N_HEADS_A = 8
HEAD_DIM_A = 64
WIDTH_A = N_HEADS_A * HEAD_DIM_A
DILATED_BRANCHES = ((128, 1), (512, 4), (2048, 16))
BLK = 128
WINDOW_MAX = 2048
MAX_DIL = 16
NUM_BUCKETS = 32
MAX_DISTANCE = 2048
N_HEADS_B = 4
KEY_DIM_B = 64
VAL_DIM_B = 128
QK_WIDTH_B = N_HEADS_B * KEY_DIM_B
WIDTH_B = N_HEADS_B * VAL_DIM_B
RET_CHUNK = 128
GN_EPS = 1e-6
TOP_K = 4
SWIGLU_LIMIT = 7.0
SWIGLU_ALPHA = 1.702
LN_EPS = 1e-5
NEG_INF = -1e30
PAST_LEN = 16384
MOE_BLOCK = 256
LANES = 128
SUBLANES = 8
VMEM_LIMIT = 52 * 1024 * 1024


def _params(n_axes, vmem=VMEM_LIMIT):
    return pltpu.CompilerParams(dimension_semantics=("arbitrary",) * n_axes, vmem_limit_bytes=vmem)


def _t5_bucket(dist):
    dist = np.asarray(dist, dtype=np.int32)
    max_exact = NUM_BUCKETS // 2
    d = np.maximum(dist, 1).astype(np.float32)
    large = max_exact + (np.log(d / max_exact) / np.log(MAX_DISTANCE / max_exact)
                         * (NUM_BUCKETS - max_exact)).astype(np.int32)
    large = np.minimum(large, NUM_BUCKETS - 1)
    return np.where(dist < max_exact, dist, large).astype(np.int32)


def _bias_by_bucket(rel_bias, buckets):
    onehot = np.eye(NUM_BUCKETS, dtype=np.float32)[np.asarray(buckets).reshape(-1)]
    return jnp.dot(jnp.asarray(onehot), rel_bias.astype(F32), precision=lax.Precision.HIGHEST)


def _pack_bf16_pairs(v):
    w = v.shape[1] // 2
    lo = lax.bitcast_convert_type(v[:, :w].astype(BF16).astype(F32), jnp.uint32) >> 16
    hi = lax.bitcast_convert_type(v[:, w:].astype(BF16).astype(F32), jnp.uint32) & jnp.uint32(0xFFFF0000)
    return lax.bitcast_convert_type(lo | hi, jnp.int32)


def _unpack_bf16_pairs(p):
    u = lax.bitcast_convert_type(p, jnp.uint32)
    lo = lax.bitcast_convert_type(u << 16, F32)
    hi = lax.bitcast_convert_type(u & jnp.uint32(0xFFFF0000), F32)
    return lo, hi


SC_CORES = 2
SC_SUBCORES = 16
SC_WORKERS = SC_CORES * SC_SUBCORES
SC_ALIGN = 8
SC_CHUNK_ROWS = 80


def _sc_mesh():
    return plsc.VectorSubcoreMesh(core_axis_name="c", subcore_axis_name="s")


def _sc_chunk(per_worker):
    c = max(d for d in range(SC_ALIGN, SC_CHUNK_ROWS + 1, SC_ALIGN) if per_worker % d == 0)
    return c


def _sc_scatter_rows(src, dest_flat, n_out):
    T, W = src.shape
    K = dest_flat.shape[0] // T
    per_w = T // SC_WORKERS
    assert per_w * SC_WORKERS == T and per_w % SC_ALIGN == 0
    chunk = _sc_chunk(per_w)

    @functools.partial(
        pl.kernel, mesh=_sc_mesh(), out_type=jax.ShapeDtypeStruct((n_out, W), src.dtype),
        scratch_types=[pltpu.VMEM((chunk, W), src.dtype)] + [pltpu.VMEM((chunk,), jnp.int32)] * K
        + [pltpu.SemaphoreType.DMA],
        name="sc_scatter_rows")
    def k(src_hbm, dest_hbm, out_hbm, rows_v, *rest):
        idx_vs, sem = rest[:K], rest[K]
        base = (lax.axis_index("s") * SC_CORES + lax.axis_index("c")) * per_w

        @pl.loop(0, per_w // chunk)
        def _(j):
            off = pl.multiple_of(base + j * chunk, SC_ALIGN)
            pltpu.sync_copy(src_hbm.at[pl.ds(off, chunk)], rows_v)
            for kk in range(K):
                pltpu.sync_copy(dest_hbm.at[pl.ds(kk * T + off, chunk)], idx_vs[kk])
            copies = [pltpu.async_copy(rows_v, out_hbm.at[idx_vs[kk]], sem) for kk in range(K)]
            for c in copies:
                c.wait()

    return k(src, dest_flat)


def _sc_gather_rows(table, idx):
    B = idx.shape[0]
    W = table.shape[1]
    per_w = B // SC_WORKERS
    assert per_w * SC_WORKERS == B and per_w % SC_ALIGN == 0
    chunk = _sc_chunk(per_w)

    @functools.partial(
        pl.kernel, mesh=_sc_mesh(), out_type=jax.ShapeDtypeStruct((B, W), table.dtype),
        scratch_types=[pltpu.VMEM((chunk,), jnp.int32), pltpu.VMEM((chunk, W), table.dtype), pltpu.SemaphoreType.DMA],
        name="sc_gather_rows")
    def k(table_hbm, idx_hbm, out_hbm, idx_v, rows_v, sem):
        base = (lax.axis_index("s") * SC_CORES + lax.axis_index("c")) * per_w

        @pl.loop(0, per_w // chunk)
        def _(j):
            off = pl.multiple_of(base + j * chunk, SC_ALIGN)
            pltpu.sync_copy(idx_hbm.at[pl.ds(off, chunk)], idx_v)
            pltpu.async_copy(table_hbm.at[idx_v], rows_v, sem).wait()
            pltpu.sync_copy(rows_v, out_hbm.at[pl.ds(off, chunk)])

    return k(table, idx)


def _in_proj_body(x_ref, w_ref, cos_ref, sin_ref,
                  qa_ref, ka_ref, va_ref, qb_ref, kb_ref, vb_ref, gb_ref, kt_ref, vt_ref, *, seq_tiles, first_win):
    x = x_ref[...].astype(BF16)
    in_window = pl.program_id(0) % seq_tiles >= first_win

    def proj(lo, hi):
        return jnp.dot(x, w_ref[:, lo:hi], preferred_element_type=F32)

    o = 0
    qa_ref[...] = (proj(o, o + WIDTH_A) * (HEAD_DIM_A ** -0.5)).astype(BF16)
    o += WIDTH_A
    ka = proj(o, o + WIDTH_A)
    ka_ref[...] = ka.astype(BF16)

    @pl.when(in_window)
    def _():
        kt_ref[...] = ka.T
    o += WIDTH_A
    va = proj(o, o + WIDTH_A)
    va_ref[...] = va.astype(BF16)

    @pl.when(in_window)
    def _():
        vt_ref[...] = va.T
    o += WIDTH_A

    cos = cos_ref[...]
    sin = sin_ref[...]
    lane = lax.broadcasted_iota(jnp.int32, cos.shape, 1)
    first_half = (lane % KEY_DIM_B) < (KEY_DIM_B // 2)

    def rot(z):
        sw = jnp.where(first_half, pltpu.roll(z, QK_WIDTH_B - KEY_DIM_B // 2, 1), pltpu.roll(z, KEY_DIM_B // 2, 1))
        return z * cos + sw * sin

    qb_ref[...] = rot(proj(o, o + QK_WIDTH_B)).astype(BF16)
    o += QK_WIDTH_B
    kb_ref[...] = (rot(proj(o, o + QK_WIDTH_B)) * (KEY_DIM_B ** -0.5)).astype(BF16)
    o += QK_WIDTH_B
    vb_ref[...] = proj(o, o + WIDTH_B).astype(BF16)
    o += WIDTH_B
    gb_ref[...] = proj(o, o + WIDTH_B).astype(BF16)


def _in_proj(x2d, w_bf, cos_t, sin_t, tm, seq_tiles, win_tiles):
    T = x2d.shape[0]
    nt = T // tm
    n_seq = nt // seq_tiles
    j0 = seq_tiles - win_tiles

    def tok(i):
        return (i, 0)

    def tab(i):
        return (i % seq_tiles, 0)

    def win(i):
        return (i // seq_tiles, 0, jnp.maximum(i % seq_tiles - j0, 0))

    def tspec(w):
        return pl.BlockSpec((tm, w), tok)

    out_shape = (
        jax.ShapeDtypeStruct((T, WIDTH_A), BF16), jax.ShapeDtypeStruct((T, WIDTH_A), BF16),
        jax.ShapeDtypeStruct((T, WIDTH_A), BF16),
        jax.ShapeDtypeStruct((T, QK_WIDTH_B), BF16), jax.ShapeDtypeStruct((T, QK_WIDTH_B), BF16),
        jax.ShapeDtypeStruct((T, WIDTH_B), BF16), jax.ShapeDtypeStruct((T, WIDTH_B), BF16),
        jax.ShapeDtypeStruct((n_seq, WIDTH_A, win_tiles * tm), F32),
        jax.ShapeDtypeStruct((n_seq, WIDTH_A, win_tiles * tm), F32),
    )
    return pl.pallas_call(
        functools.partial(_in_proj_body, seq_tiles=seq_tiles, first_win=j0),
        grid=(nt,),
        in_specs=[tspec(D_MODEL), pl.BlockSpec(w_bf.shape, lambda i: (0, 0)),
                  pl.BlockSpec((tm, QK_WIDTH_B), tab), pl.BlockSpec((tm, QK_WIDTH_B), tab)],
        out_specs=(tspec(WIDTH_A), tspec(WIDTH_A), tspec(WIDTH_A), tspec(QK_WIDTH_B), tspec(QK_WIDTH_B),
                   tspec(WIDTH_B), tspec(WIDTH_B),
                   pl.BlockSpec((None, WIDTH_A, tm), win), pl.BlockSpec((None, WIDTH_A, tm), win)),
        out_shape=out_shape,
        compiler_params=_params(1),
        name="in_proj",
    )(x2d, w_bf, cos_t, sin_t)


def _rotary_tables(pos):
    half = KEY_DIM_B // 2
    inv_freq = 1.0 / (10000.0 ** jnp.linspace(0.0, 1.0, half, dtype=F32))
    ang = pos.astype(F32)[:, None] * inv_freq[None, :]
    cos = jnp.cos(ang)
    sin = jnp.sin(ang)
    cos_h = jnp.concatenate([cos, cos], axis=-1)
    sin_h = jnp.concatenate([-sin, sin], axis=-1)
    return jnp.tile(cos_h, (1, N_HEADS_B)), jnp.tile(sin_h, (1, N_HEADS_B))


def _attn_body(q_ref, k_ref, v_ref, bias_ref, o_ref, lse_ref, pk_ref, pv_ref):
    n = pl.program_id(2)

    @pl.when(n == 0)
    def _():
        pk_ref[...] = jnp.zeros_like(pk_ref)
        pv_ref[...] = jnp.zeros_like(pv_ref)

    q = q_ref[...]
    kc = k_ref[...]
    vc = v_ref[...]
    kp = pk_ref[...]
    vp = pv_ref[...]
    lane = lax.broadcasted_iota(jnp.int32, (BLK, LANES), 1)
    lse_tile = jnp.zeros((BLK, LANES), F32)
    for h in range(N_HEADS_A):
        sl = slice(h * HEAD_DIM_A, (h + 1) * HEAD_DIM_A)
        kh = jnp.concatenate([kp[:, sl], kc[:, sl]], axis=0)
        vh = jnp.concatenate([vp[:, sl], vc[:, sl]], axis=0)
        s = lax.dot_general(q[:, sl], kh, (((1,), (1,)), ((), ())), preferred_element_type=F32) + bias_ref[h]
        m = jnp.max(s, axis=-1, keepdims=True)
        e = jnp.exp(s - m)
        den = jnp.sum(e, axis=-1, keepdims=True)
        o = jnp.dot(e.astype(BF16), vh, preferred_element_type=F32) / den
        o_ref[:, sl] = o.astype(BF16)
        lse_tile = jnp.where(lane == h, m + jnp.log(den), lse_tile)
    lse_ref[...] = lse_tile
    pk_ref[...] = kc
    pv_ref[...] = vc


def _attn_bias_tables(rel_bias, window, dil):
    n_keys = window // dil
    i = np.arange(BLK)[:, None]
    j = np.arange(2 * BLK)[None, :]
    rel = BLK + i - j
    in_band = (rel >= 0) & (rel <= n_keys)
    by_rel = _bias_by_bucket(rel_bias, _t5_bucket(np.clip(np.arange(-BLK, 2 * BLK), 0, None) * dil)).T
    bias = jnp.stack([jnp.flip(by_rel[:, r + 1:r + 1 + 2 * BLK], axis=1) for r in range(BLK)], axis=1)
    later = jnp.where(jnp.asarray(in_band)[None], bias, NEG_INF)
    first = jnp.where(jnp.asarray(in_band & (j >= BLK))[None], bias, NEG_INF)
    return jnp.stack([first, later])


def _dilated_branch(q, k, v, bias_tab, dil):
    B, S, _ = q.shape
    L = S // dil
    nb = L // BLK

    def view(t):
        return t.reshape(B, L, dil * t.shape[-1])

    def cls(b, r, n):
        return (b, n, r)

    qkv_spec = pl.BlockSpec((None, BLK, WIDTH_A), cls)
    o, lse = pl.pallas_call(
        _attn_body,
        grid=(B, dil, nb),
        in_specs=[qkv_spec, qkv_spec, qkv_spec,
                  pl.BlockSpec((None, N_HEADS_A, BLK, 2 * BLK), lambda b, r, n: (jnp.minimum(n, 1), 0, 0, 0))],
        out_specs=(qkv_spec, pl.BlockSpec((None, BLK, LANES), cls)),
        out_shape=(jax.ShapeDtypeStruct((B, L, dil * WIDTH_A), BF16),
                   jax.ShapeDtypeStruct((B, L, dil * LANES), F32)),
        scratch_shapes=[pltpu.VMEM((BLK, WIDTH_A), BF16), pltpu.VMEM((BLK, WIDTH_A), BF16)],
        compiler_params=_params(3),
        name=f"dil_attn_d{dil}",
    )(view(q), view(k), view(v), bias_tab)
    return o.reshape(B, S, WIDTH_A), lse.reshape(B, S, LANES)


def _ret_body(*refs, n_branch):
    qb_ref, kb_ref, vb_ref, gb_ref = refs[:4]
    p = 4
    if n_branch:
        o_refs = refs[p:p + n_branch]
        l_refs = refs[p + n_branch:p + 2 * n_branch]
        exp_ref = refs[p + 2 * n_branch]
        p += 2 * n_branch + 1
    else:
        oa_ref = refs[p]
        p += 1
    st0_ref, dmat_ref, qdec_ref, kdec_ref, cdec_ref, cat_ref, sto_ref, st_ref = refs[p:]

    @pl.when(pl.program_id(1) == 0)
    def _():
        st_ref[...] = st0_ref[...]

    if n_branch:
        ls = [r[...] for r in l_refs]
        mx = functools.reduce(jnp.maximum, ls)
        ws = [jnp.exp(l - mx) for l in ls]
        tot = functools.reduce(lambda a, b: a + b, ws)
        oa = None
        for w, o_ref in zip(ws, o_refs):
            w = w / tot
            w_hi = w.astype(BF16)
            w_lo = (w - w_hi.astype(F32)).astype(BF16)
            w_full = (jnp.dot(w_hi, exp_ref[...], preferred_element_type=F32)
                      + jnp.dot(w_lo, exp_ref[...], preferred_element_type=F32))
            term = w_full * o_ref[...].astype(F32)
            oa = term if oa is None else oa + term
        cat_ref[:, :WIDTH_A] = oa.astype(BF16)
    else:
        cat_ref[:, :WIDTH_A] = oa_ref[...].astype(BF16)

    for h in range(N_HEADS_B):
        ks = slice(h * KEY_DIM_B, (h + 1) * KEY_DIM_B)
        vs = slice(h * VAL_DIM_B, (h + 1) * VAL_DIM_B)
        q = qb_ref[:, ks]
        k = kb_ref[:, ks]
        v = vb_ref[:, vs]
        st = st_ref[h]
        a = lax.dot_general(q, k, (((1,), (1,)), ((), ())), preferred_element_type=F32) * dmat_ref[h]
        o = (jnp.dot(a.astype(BF16), v, preferred_element_type=F32)
             + jnp.dot(q, st.astype(BF16), preferred_element_type=F32) * qdec_ref[h])
        kd = (k.astype(F32) * kdec_ref[h]).astype(BF16)
        st_new = st * cdec_ref[h] + lax.dot_general(kd, v, (((0,), (0,)), ((), ())), preferred_element_type=F32)
        st_ref[h] = st_new
        sto_ref[h] = st_new
        mu = jnp.mean(o, axis=-1, keepdims=True)
        var = jnp.mean(jnp.square(o - mu), axis=-1, keepdims=True)
        obn = (o - mu) * lax.rsqrt(var + GN_EPS)
        g = gb_ref[:, vs].astype(F32)
        gated = g * (1.0 / (1.0 + jnp.exp(-g))) * obn
        cat_ref[:, WIDTH_A + h * VAL_DIM_B:WIDTH_A + (h + 1) * VAL_DIM_B] = gated.astype(BF16)


def _decay_tables(chunk, rows):
    H = N_HEADS_B
    log_g = jnp.log(1.0 - 2.0 ** (-5.0 - jnp.arange(H, dtype=F32)))
    i = jnp.arange(rows, dtype=F32)
    live = np.arange(rows) < chunk
    diff = i[:, None] - i[None, :]
    causal = (diff >= 0) & jnp.asarray(live[:, None] & live[None, :])
    dmat = jnp.where(causal[None], jnp.exp(jnp.where(causal, diff, 0.0)[None] * log_g[:, None, None]), 0.0)
    q_decay = jnp.where(jnp.asarray(live)[None], jnp.exp((i[None, :] + 1.0) * log_g[:, None]), 0.0)
    k_decay = jnp.where(jnp.asarray(live)[None], jnp.exp((chunk - 1.0 - i)[None, :] * log_g[:, None]), 0.0)
    c_decay = jnp.exp(chunk * log_g)
    qdec = jnp.broadcast_to(q_decay[:, :, None], (H, rows, VAL_DIM_B))
    kdec = jnp.broadcast_to(k_decay[:, :, None], (H, rows, KEY_DIM_B))
    cdec = jnp.broadcast_to(c_decay[:, None, None], (H, KEY_DIM_B, VAL_DIM_B))
    return dmat.astype(F32), qdec.astype(F32), kdec.astype(F32), cdec.astype(F32)


def _ret_mix(qb, kb, vb, gb, attn, state0, chunk):
    B, S, _ = qb.shape
    rows = RET_CHUNK
    nc = S // rows
    tables = _decay_tables(chunk, rows)

    def tok(b, c):
        return (b, c, 0)

    def tspec(w):
        return pl.BlockSpec((None, rows, w), tok)

    def const(shape):
        return pl.BlockSpec(shape, lambda b, c: (0,) * len(shape))

    ins = [qb, kb, vb, gb]
    in_specs = [tspec(QK_WIDTH_B), tspec(QK_WIDTH_B), tspec(WIDTH_B), tspec(WIDTH_B)]
    if isinstance(attn, tuple):
        outs_a, lses = attn
        n_branch = len(outs_a)
        expand = np.zeros((LANES, WIDTH_A), np.float32)
        for h in range(N_HEADS_A):
            expand[h, h * HEAD_DIM_A:(h + 1) * HEAD_DIM_A] = 1.0
        ins += list(outs_a) + list(lses) + [jnp.asarray(expand, BF16)]
        in_specs += [tspec(WIDTH_A)] * n_branch + [tspec(LANES)] * n_branch + [const((LANES, WIDTH_A))]
    else:
        n_branch = 0
        ins.append(attn)
        in_specs.append(tspec(WIDTH_A))
    st_shape = (N_HEADS_B, KEY_DIM_B, VAL_DIM_B)
    st_spec = pl.BlockSpec((None,) + st_shape, lambda b, c: (b, 0, 0, 0))
    ins += [state0] + list(tables)
    in_specs += [st_spec] + [const(t.shape) for t in tables]
    return pl.pallas_call(
        functools.partial(_ret_body, n_branch=n_branch),
        grid=(B, nc),
        in_specs=in_specs,
        out_specs=(tspec(WIDTH_A + WIDTH_B), st_spec),
        out_shape=(jax.ShapeDtypeStruct((B, S, WIDTH_A + WIDTH_B), BF16),
                   jax.ShapeDtypeStruct((B,) + st_shape, F32)),
        scratch_shapes=[pltpu.VMEM(st_shape, F32)],
        compiler_params=_params(2),
        name=f"ret_mix_{n_branch}",
    )(*ins)


SAMP_Q_ROWS = 64
SAMP_NEW_LANES = 128


def _samp_attn_body(q_ref, kt_ref, vt_ref, knt_ref, vnt_ref, bc_ref, bn_ref, hm_ref, o_ref, ko_ref, vo_ref, *, ds):
    q = q_ref[...]
    kt = kt_ref[...]
    vt = vt_ref[...]
    knt = knt_ref[...]
    vnt = vnt_ref[...]
    w = kt.shape[1]
    is_new = lax.broadcasted_iota(jnp.int32, knt.shape, 1) >= SAMP_NEW_LANES - ds
    for src, new, dst in ((kt, knt, ko_ref), (vt, vnt, vo_ref)):
        rolled = pltpu.roll(src, w - ds, 1)
        dst[:, :w - SAMP_NEW_LANES] = rolled[:, :w - SAMP_NEW_LANES]
        dst[:, w - SAMP_NEW_LANES:] = jnp.where(is_new, new, rolled[:, w - SAMP_NEW_LANES:])

    s_c = jnp.dot(q, kt.astype(BF16), preferred_element_type=F32)
    s_n = jnp.dot(q, knt.astype(BF16), preferred_element_type=F32)
    es_c, es_n, dens, lses = [], [], [], []
    for n in range(len(DILATED_BRANCHES)):
        sc = s_c + bc_ref[n]
        sn = s_n + bn_ref[n]
        m = jnp.maximum(jnp.max(sc, axis=-1, keepdims=True), jnp.max(sn, axis=-1, keepdims=True))
        ec = jnp.exp(sc - m)
        en = jnp.exp(sn - m)
        den = jnp.sum(ec, axis=-1, keepdims=True) + jnp.sum(en, axis=-1, keepdims=True)
        es_c.append(ec)
        es_n.append(en)
        dens.append(den)
        lses.append(m + jnp.log(den))
    mx = functools.reduce(jnp.maximum, lses)
    ws = [jnp.exp(l - mx) for l in lses]
    tot = functools.reduce(lambda a, b: a + b, ws)
    p_c = None
    p_n = None
    for w, den, ec, en in zip(ws, dens, es_c, es_n):
        coef = w / (tot * den)
        p_c = coef * ec if p_c is None else p_c + coef * ec
        p_n = coef * en if p_n is None else p_n + coef * en
    nt = (((1,), (1,)), ((), ()))
    o = (lax.dot_general(p_c.astype(BF16), vt.astype(BF16), nt, preferred_element_type=F32)
         + lax.dot_general(p_n.astype(BF16), vnt.astype(BF16), nt, preferred_element_type=F32))
    o = o * hm_ref[...]
    o_ref[...] = jnp.sum(o.reshape(SUBLANES, N_HEADS_A, WIDTH_A), axis=1)


def _samp_bias_tables(rel_bias, w_buf, ds):
    tabs_c, tabs_n = [], []
    s = np.arange(SUBLANES)[:, None]
    live_s = s < ds
    neg = SUBLANES
    by_dist = _bias_by_bucket(rel_bias, _t5_bucket(np.clip(np.arange(-neg, w_buf + SUBLANES), 0, None))).T
    first_new = SAMP_NEW_LANES - ds
    for window, dil in DILATED_BRANCHES:
        n_keys = window // dil
        for keys, live_k, tabs in ((np.arange(w_buf)[None, :], True, tabs_c),
                                   (w_buf - first_new + np.arange(SAMP_NEW_LANES)[None, :],
                                    np.arange(SAMP_NEW_LANES)[None, :] >= first_new, tabs_n)):
            n = keys.shape[1]
            dist = w_buf + s - keys
            valid = (dist >= 0) & (dist % dil == 0) & (dist // dil <= n_keys) & live_k
            rows = [jnp.flip(by_dist[:, int(dist[r, -1]) + neg:int(dist[r, -1]) + neg + n], axis=1)
                    for r in range(SUBLANES)]
            bias = jnp.stack(rows)
            tab = jnp.where(jnp.asarray(valid)[:, None, :], bias, NEG_INF)
            pad = jnp.where(jnp.asarray(np.broadcast_to(live_k, dist.shape))[:, None, :], 0.0, NEG_INF)
            tab = jnp.where(jnp.asarray(live_s)[:, :, None], tab, pad)
            tabs.append(tab.reshape(SAMP_Q_ROWS, n))
    return jnp.stack(tabs_c), jnp.stack(tabs_n)


def _samp_attn(qa, cache_kt, cache_vt, knt, vnt, rel_bias, ds):
    DB, DS, _ = qa.shape
    W = cache_kt.shape[2]
    head_of_lane = np.arange(WIDTH_A) // HEAD_DIM_A
    hmask = (np.arange(SAMP_Q_ROWS)[:, None] % N_HEADS_A == head_of_lane[None, :])
    q8 = jnp.pad(qa, ((0, 0), (0, SUBLANES - DS), (0, 0)))
    q_rows = jnp.where(jnp.asarray(hmask)[None], jnp.repeat(q8, N_HEADS_A, axis=1), jnp.zeros((), BF16))
    bias_c, bias_n = _samp_bias_tables(rel_bias, W, DS)

    def per_b(rows, w):
        return pl.BlockSpec((None, rows, w), lambda b: (b, 0, 0))

    def const(a):
        return pl.BlockSpec(a.shape, lambda b: (0,) * a.ndim)

    hm = jnp.asarray(hmask, F32)
    return pl.pallas_call(
        functools.partial(_samp_attn_body, ds=ds),
        grid=(DB,),
        in_specs=[per_b(SAMP_Q_ROWS, WIDTH_A), per_b(WIDTH_A, W), per_b(WIDTH_A, W),
                  per_b(WIDTH_A, SAMP_NEW_LANES), per_b(WIDTH_A, SAMP_NEW_LANES),
                  const(bias_c), const(bias_n), const(hm)],
        out_specs=(per_b(SUBLANES, WIDTH_A), per_b(WIDTH_A, W), per_b(WIDTH_A, W)),
        out_shape=(jax.ShapeDtypeStruct((DB, SUBLANES, WIDTH_A), F32),
                   jax.ShapeDtypeStruct((DB, WIDTH_A, W), F32), jax.ShapeDtypeStruct((DB, WIDTH_A, W), F32)),
        compiler_params=_params(1),
        name="samp_attn",
    )(q_rows, cache_kt, cache_vt, knt, vnt, bias_c, bias_n, hm)


def _route_body(cat_ref, x_ref, wout_ref, g_ref, b_ref, rwh_ref, rwl_ref, rb_ref, tril_ref, base_ref,
                h_ref, hp_ref, route_ref, cnt_ref, *, alpha):
    @pl.when(pl.program_id(0) == 0)
    def _():
        cnt_ref[...] = base_ref[...]

    mix = jnp.dot(cat_ref[...], wout_ref[...], preferred_element_type=F32)
    y = alpha * x_ref[...] + mix
    mu = jnp.mean(y, axis=-1, keepdims=True)
    var = jnp.mean(jnp.square(y - mu), axis=-1, keepdims=True)
    h = (y - mu) * lax.rsqrt(var + LN_EPS) * g_ref[...] + b_ref[...]
    h_ref[...] = h
    hb = h.astype(BF16)
    hp_ref[...] = _pack_bf16_pairs(h)
    hl = (h - hb.astype(F32)).astype(BF16)
    logits = (jnp.dot(hb, rwh_ref[...], preferred_element_type=F32)
              + jnp.dot(hb, rwl_ref[...], preferred_element_type=F32)
              + jnp.dot(hl, rwh_ref[...], preferred_element_type=F32)) + rb_ref[...]

    tm = logits.shape[0]
    lane = lax.broadcasted_iota(jnp.int32, (tm, LANES), 1)
    work = logits
    vals, idxs = [], []
    for _ in range(TOP_K):
        m = jnp.max(work, axis=-1, keepdims=True)
        idx = jnp.min(jnp.where(work == m, lane, LANES), axis=-1, keepdims=True)
        vals.append(m)
        idxs.append(idx)
        work = jnp.where(lane == idx, -jnp.inf, work)
    es = [jnp.exp(v - vals[0]) for v in vals]
    tot = functools.reduce(lambda a, b: a + b, es)
    onehot = jnp.zeros((tm, LANES), F32)
    for idx in idxs:
        onehot = onehot + (lane == idx).astype(F32)
    before = jnp.dot(tril_ref[...], onehot.astype(BF16), preferred_element_type=F32) + cnt_ref[0:1, :]
    route = jnp.zeros((tm, LANES), F32)
    for k in range(TOP_K):
        rank = jnp.sum(jnp.where(lane == idxs[k], before, 0.0), axis=-1, keepdims=True)
        route = jnp.where(lane == k, idxs[k].astype(F32), route)
        route = jnp.where(lane == TOP_K + k, es[k] / tot, route)
        route = jnp.where(lane == 2 * TOP_K + k, rank, route)
    route_ref[...] = route
    cnt_ref[...] = cnt_ref[...] + jnp.sum(onehot, axis=0, keepdims=True)


def _out_route(cat, x2d, w_out_bf, ln_g, ln_b, rw_hi, rw_lo, rb, base, tm, alpha):
    T = x2d.shape[0]
    tril = jnp.asarray(np.tril(np.ones((tm, tm), np.float32), -1), BF16)

    def tok(i):
        return (i, 0)

    def const(a):
        return pl.BlockSpec(a.shape, lambda i: (0,) * a.ndim)

    ins = (cat, x2d, w_out_bf, ln_g, ln_b, rw_hi, rw_lo, rb, tril, base)
    in_specs = [pl.BlockSpec((tm, cat.shape[1]), tok), pl.BlockSpec((tm, D_MODEL), tok)] + [const(a) for a in ins[2:]]
    return pl.pallas_call(
        functools.partial(_route_body, alpha=alpha),
        grid=(T // tm,),
        in_specs=in_specs,
        out_specs=(pl.BlockSpec((tm, D_MODEL), tok), pl.BlockSpec((tm, D_MODEL // 2), tok),
                   pl.BlockSpec((tm, LANES), tok), pl.BlockSpec((SUBLANES, LANES), lambda i: (0, 0))),
        out_shape=(jax.ShapeDtypeStruct((T, D_MODEL), F32), jax.ShapeDtypeStruct((T, D_MODEL // 2), jnp.int32),
                   jax.ShapeDtypeStruct((T, LANES), F32), jax.ShapeDtypeStruct((SUBLANES, LANES), F32)),
        compiler_params=_params(1),
        name="out_route",
    )(*ins)


MOE_CAST_ROWS = 128


def _moe_body(be_ref, nused_ref, x_ref, wgu_ref, bgu_ref, wdn_ref, bdn_ref, y_ref, wgu_bf, wdn_bf):
    b = pl.program_id(0)
    e = be_ref[b]
    prev = be_ref[jnp.maximum(b - 1, 0)]
    d_exp = wdn_ref.shape[0]

    @pl.when((b == 0) | (e != prev))
    def _():
        def cast_gu(i, c):
            r = pl.ds(pl.multiple_of(i * MOE_CAST_ROWS, MOE_CAST_ROWS), MOE_CAST_ROWS)
            wgu_bf[r, :] = wgu_ref[r, :].astype(BF16)
            return c

        def cast_dn(i, c):
            r = pl.ds(pl.multiple_of(i * MOE_CAST_ROWS, MOE_CAST_ROWS), MOE_CAST_ROWS)
            wdn_bf[r, :] = wdn_ref[r, :].astype(BF16)
            return c

        lax.fori_loop(0, wgu_ref.shape[0] // MOE_CAST_ROWS, cast_gu, 0)
        lax.fori_loop(0, d_exp // MOE_CAST_ROWS, cast_dn, 0)

    @pl.when(b < nused_ref[0])
    def _():
        x_lo, x_hi = _unpack_bf16_pairs(x_ref[...])
        x_lo = x_lo.astype(BF16)
        x_hi = x_hi.astype(BF16)
        dh = x_lo.shape[1]

        def xw(cols):
            return (jnp.dot(x_lo, wgu_bf[:dh, cols], preferred_element_type=F32)
                    + jnp.dot(x_hi, wgu_bf[dh:, cols], preferred_element_type=F32) + bgu_ref[:, cols])

        half = d_exp // 2
        y = None
        for c in range(2):
            lo = c * half
            gate = jnp.minimum(xw(slice(lo, lo + half)), SWIGLU_LIMIT)
            up = jnp.clip(xw(slice(d_exp + lo, d_exp + lo + half)), -SWIGLU_LIMIT, SWIGLU_LIMIT)
            act = (up + 1.0) * gate * (1.0 / (1.0 + jnp.exp(-SWIGLU_ALPHA * gate)))
            part = jnp.dot(act.astype(BF16), wdn_bf[lo:lo + half, :], preferred_element_type=F32)
            y = part if y is None else y + part
        y_ref[...] = _pack_bf16_pairs(y + bdn_ref[...])

    @pl.when(b >= nused_ref[0])
    def _():
        y_ref[...] = jnp.zeros_like(y_ref)


def _moe_ffn(xs, rows, block_e, n_used, w_gu, b_gu, w_dn, b_dn):
    E, D, two_de = w_gu.shape
    d_exp = two_de // 2
    nb = rows // MOE_BLOCK
    grid_spec = pltpu.PrefetchScalarGridSpec(
        num_scalar_prefetch=2,
        grid=(nb,),
        in_specs=[
            pl.BlockSpec((MOE_BLOCK, D // 2), lambda b, be, nu: (b, 0)),
            pl.BlockSpec((None, D, two_de), lambda b, be, nu: (be[b], 0, 0)),
            pl.BlockSpec((None, 1, two_de), lambda b, be, nu: (be[b], 0, 0)),
            pl.BlockSpec((None, d_exp, D), lambda b, be, nu: (be[b], 0, 0)),
            pl.BlockSpec((None, 1, D), lambda b, be, nu: (be[b], 0, 0)),
        ],
        out_specs=pl.BlockSpec((MOE_BLOCK, D // 2), lambda b, be, nu: (b, 0)),
        scratch_shapes=[pltpu.VMEM((D, two_de), BF16), pltpu.VMEM((d_exp, D), BF16)],
    )
    return pl.pallas_call(
        _moe_body,
        grid_spec=grid_spec,
        out_shape=jax.ShapeDtypeStruct((rows, D // 2), jnp.int32),
        compiler_params=_params(1),
        name="moe_ffn",
    )(block_e, n_used, xs, w_gu, b_gu.reshape(E, 1, two_de), w_dn, b_dn.reshape(E, 1, D))


def _ple_body(h_ref, ys_ref, route_ref, p_ref, g_ref, b_ref, wpg_ref, bpg_ref, wpp_ref, o_ref, *, alpha):
    route = route_ref[...]
    f_lo = None
    f_hi = None
    for k in range(TOP_K):
        lo, hi = _unpack_bf16_pairs(ys_ref[k])
        g = route[:, TOP_K + k:TOP_K + k + 1]
        f_lo = g * lo if f_lo is None else f_lo + g * lo
        f_hi = g * hi if f_hi is None else f_hi + g * hi
    y = alpha * h_ref[...] + jnp.concatenate([f_lo, f_hi], axis=1)
    mu = jnp.mean(y, axis=-1, keepdims=True)
    var = jnp.mean(jnp.square(y - mu), axis=-1, keepdims=True)
    h2 = (y - mu) * lax.rsqrt(var + LN_EPS) * g_ref[...] + b_ref[...]
    z = jnp.dot(h2.astype(BF16), wpg_ref[...], preferred_element_type=F32) + bpg_ref[...]
    gate = 1.0 / (1.0 + jnp.exp(-z))
    proj = jnp.dot(p_ref[...].astype(BF16), wpp_ref[...], preferred_element_type=F32)
    o_ref[...] = h2 + gate * proj


def _ffn_ple(h, y_slots, route, p, ln_g, ln_b, w_pg_bf, b_pg, w_pp_bf, tm, tile0, alpha):
    T = h.shape[0]

    def tok(i):
        return (i, 0)

    def const(a):
        return pl.BlockSpec(a.shape, lambda i: (0,) * a.ndim)

    consts = (ln_g, ln_b, w_pg_bf, b_pg, w_pp_bf)
    return pl.pallas_call(
        functools.partial(_ple_body, alpha=alpha),
        grid=(T // tm,),
        in_specs=[pl.BlockSpec((tm, D_MODEL), tok),
                  pl.BlockSpec((TOP_K, tm, D_MODEL // 2), lambda i: (0, tile0 + i, 0)),
                  pl.BlockSpec((tm, LANES), tok),
                  pl.BlockSpec((tm, p.shape[1]), tok)] + [const(a) for a in consts],
        out_specs=pl.BlockSpec((tm, D_MODEL), tok),
        out_shape=jax.ShapeDtypeStruct((T, D_MODEL), F32),
        compiler_params=_params(1),
        name="ffn_ple",
    )(h, y_slots, route, p, *consts)


def _row(v):
    return v.reshape(1, -1).astype(F32)


def kernel(x_prompt, x_sample, cache_win_k, cache_win_v, state_ret, p_prompt, p_sample, rel_bias, w_in, w_out,
           ln1_g, ln1_b, router_w, router_b, w_gate_up, b_gate_up, w_down, b_down, ln2_g, ln2_b,
           w_ple_gate, b_ple_gate, w_ple_proj):
    B, S, D = x_prompt.shape
    DB, DS, _ = x_sample.shape
    depth = w_in.shape[0]
    w_buf = cache_win_k.shape[2]
    n_exp = router_w.shape[-1]
    alpha = (2.0 * depth) ** 0.25
    assert depth == 1 and D == D_MODEL
    assert S % (BLK * MAX_DIL) == 0 and S >= WINDOW_MAX and w_buf == WINDOW_MAX and DS <= SUBLANES
    tm_p = 512
    Tp, Ts = B * S, DB * DS
    assert Tp % tm_p == 0 and Ts % SUBLANES == 0

    i = 0
    w_in_bf = w_in[i].astype(BF16)
    w_out_bf = w_out[i].astype(BF16)
    w_pg_bf = w_ple_gate[i].astype(BF16)
    w_pp_bf = w_ple_proj[i].astype(BF16)
    rw = jnp.pad(router_w[i], ((0, 0), (0, LANES - n_exp)))
    rw_hi = rw.astype(BF16)
    rw_lo = (rw - rw_hi.astype(F32)).astype(BF16)
    rb = jnp.pad(router_b[i], (0, LANES - n_exp), constant_values=NEG_INF).reshape(1, LANES)

    cos_p, sin_p = _rotary_tables(jnp.arange(S, dtype=jnp.int32))
    qa, ka, va, qb, kb, vb, gb, kf, vf = _in_proj(x_prompt.reshape(Tp, D), w_in_bf, cos_p, sin_p, tm_p,
                                                   S // tm_p, WINDOW_MAX // tm_p)

    def seq(t):
        return t.reshape(B, S, t.shape[-1])

    outs_a, lses = [], []
    for window, dil in DILATED_BRANCHES:
        o_n, l_n = _dilated_branch(seq(qa), seq(ka), seq(va), _attn_bias_tables(rel_bias, window, dil), dil)
        outs_a.append(o_n)
        lses.append(l_n)
    st_zero = jnp.zeros((B, N_HEADS_B, KEY_DIM_B, VAL_DIM_B), F32)
    cat_p, rst_p = _ret_mix(seq(qb), seq(kb), seq(vb), seq(gb), (outs_a, lses), st_zero, RET_CHUNK)
    base0 = jnp.zeros((SUBLANES, LANES), F32)
    h_p, hp_p, route_p, cnt_p = _out_route(cat_p.reshape(Tp, -1), x_prompt.reshape(Tp, D), w_out_bf,
                                           _row(ln1_g[i]), _row(ln1_b[i]), rw_hi, rw_lo, rb, base0, tm_p, alpha)

    pos_s = jnp.tile(PAST_LEN + jnp.arange(DS, dtype=jnp.int32), DB)
    cos_s, sin_s = _rotary_tables(pos_s)
    qa_s, _, _, qb_s, kb_s, vb_s, gb_s, kf_s, vf_s = _in_proj(x_sample.reshape(Ts, D), w_in_bf, cos_s, sin_s,
                                                             Ts, 1, 1)

    def positions_minor(t):
        return jnp.transpose(t, (0, 2, 3, 1)).reshape(DB, WIDTH_A, t.shape[1])

    def positions_major(t):
        return jnp.transpose(t.reshape(t.shape[0], N_HEADS_A, HEAD_DIM_A, t.shape[2]), (0, 3, 1, 2))[None]

    def new_columns(t):
        t = jnp.transpose(t.reshape(WIDTH_A, DB, DS), (1, 0, 2))
        return jnp.pad(t, ((0, 0), (0, 0), (SAMP_NEW_LANES - DS, 0)))

    oa_s, kt_out, vt_out = _samp_attn(qa_s.reshape(DB, DS, WIDTH_A), positions_minor(cache_win_k[i]),
                                      positions_minor(cache_win_v[i]), new_columns(kf_s), new_columns(vf_s),
                                      rel_bias, DS)

    def pad_rows(t, rows):
        t = t.reshape(DB, -1, t.shape[-1])
        return jnp.pad(t, ((0, 0), (0, rows - t.shape[1]), (0, 0)))

    cat_s, rst_s = _ret_mix(pad_rows(qb_s, RET_CHUNK), pad_rows(kb_s, RET_CHUNK), pad_rows(vb_s, RET_CHUNK),
                            pad_rows(gb_s, RET_CHUNK), pad_rows(oa_s, RET_CHUNK), state_ret[i].astype(F32), DS)
    cat_s = cat_s[:, :DS].reshape(Ts, -1)
    h_s, hp_s, route_s, cnt = _out_route(cat_s, x_sample.reshape(Ts, D), w_out_bf, _row(ln1_g[i]), _row(ln1_b[i]),
                                         rw_hi, rw_lo, rb, cnt_p, Ts, alpha)

    T = Tp + Ts
    t_align = SC_WORKERS * SC_ALIGN
    T_pad = -(-T // t_align) * t_align
    route = jnp.concatenate([route_p, route_s], axis=0)
    top_idx = route[:, :TOP_K].astype(jnp.int32)
    rank = route[:, 2 * TOP_K:3 * TOP_K].astype(jnp.int32)
    counts = cnt[0, :n_exp].astype(jnp.int32)
    padded = (counts + MOE_BLOCK - 1) // MOE_BLOCK * MOE_BLOCK
    pad_end = jnp.cumsum(padded)
    pad_start = pad_end - padded
    dest = pad_start[top_idx] + rank
    n_blocks = -(-T * TOP_K // MOE_BLOCK) + n_exp
    rows = n_blocks * MOE_BLOCK
    block_start = jnp.arange(n_blocks, dtype=jnp.int32) * MOE_BLOCK
    block_e = jnp.minimum(jnp.sum(pad_end[None, :] <= block_start[:, None], axis=1), n_exp - 1).astype(jnp.int32)
    n_used = (pad_end[-1:] // MOE_BLOCK).astype(jnp.int32)
    n_fill = T_pad - T
    spare = rows + jnp.arange(n_fill * TOP_K, dtype=jnp.int32).reshape(n_fill, TOP_K)
    dest_sc = jnp.concatenate([dest, spare], axis=0).T.reshape(-1)
    dest_ga = jnp.concatenate([dest, jnp.zeros((n_fill, TOP_K), jnp.int32)], axis=0).T.reshape(-1)
    hp_all = jnp.concatenate([hp_p, hp_s, jnp.zeros((n_fill, D // 2), jnp.int32)], axis=0)
    xs = _sc_scatter_rows(hp_all, dest_sc, rows + n_fill * TOP_K)
    ys = _moe_ffn(xs, rows, block_e, n_used, w_gate_up[i], b_gate_up[i], w_down[i], b_down[i])
    y_slots = _sc_gather_rows(ys, dest_ga).reshape(TOP_K, T_pad, D // 2)

    ple_args = (_row(ln2_g[i]), _row(ln2_b[i]), w_pg_bf, _row(b_ple_gate[i]), w_pp_bf)
    assert Tp % Ts == 0
    y_p = _ffn_ple(h_p, y_slots, route_p, p_prompt[i].reshape(Tp, D_PLE), *ple_args, tm_p, 0, alpha)
    y_s = _ffn_ple(h_s, y_slots, route_s, p_sample[i].reshape(Ts, D_PLE), *ple_args, Ts, Tp // Ts, alpha)

    return (y_p.reshape(B, S, D), y_s.reshape(DB, DS, D), positions_major(kf), positions_major(vf),
            rst_p[None], positions_major(kt_out), positions_major(vt_out), rst_s[None])
```

```python
import functools

import numpy as np
import jax
import jax.numpy as jnp
from jax import lax
from jax.experimental import pallas as pl
from jax.experimental.pallas import tpu as pltpu
from jax.experimental.pallas import tpu_sc as plsc

F32 = jnp.float32
BF16 = jnp.bfloat16

D_MODEL = 1024
D_PLE = 256
N_HEADS_A = 8
HEAD_DIM_A = 64
WIDTH_A = N_HEADS_A * HEAD_DIM_A
DILATED_BRANCHES = ((128, 1), (512, 4), (2048, 16))
BLK = 128
WINDOW_MAX = 2048
MAX_DIL = 16
NUM_BUCKETS = 32
MAX_DISTANCE = 2048
N_HEADS_B = 4
KEY_DIM_B = 64
VAL_DIM_B = 128
QK_WIDTH_B = N_HEADS_B * KEY_DIM_B
WIDTH_B = N_HEADS_B * VAL_DIM_B
RET_CHUNK = 128
GN_EPS = 1e-6
TOP_K = 4
SWIGLU_LIMIT = 7.0
SWIGLU_ALPHA = 1.702
LN_EPS = 1e-5
NEG_INF = -1e30
PAST_LEN = 16384
MOE_BLOCK = 256
LANES = 128
SUBLANES = 8
VMEM_LIMIT = 52 * 1024 * 1024


def _params(n_axes, vmem=VMEM_LIMIT):
    return pltpu.CompilerParams(dimension_semantics=("arbitrary",) * n_axes, vmem_limit_bytes=vmem)


def _t5_bucket(dist):
    dist = np.asarray(dist, dtype=np.int32)
    max_exact = NUM_BUCKETS // 2
    d = np.maximum(dist, 1).astype(np.float32)
    large = max_exact + (np.log(d / max_exact) / np.log(MAX_DISTANCE / max_exact)
                         * (NUM_BUCKETS - max_exact)).astype(np.int32)
    large = np.minimum(large, NUM_BUCKETS - 1)
    return np.where(dist < max_exact, dist, large).astype(np.int32)


def _bias_by_bucket(rel_bias, buckets):
    b = jnp.asarray(buckets, jnp.int32)
    ids = jnp.arange(NUM_BUCKETS, dtype=jnp.int32).reshape((NUM_BUCKETS, 1) + (1,) * b.ndim)
    vals = rel_bias.astype(F32).reshape((NUM_BUCKETS, rel_bias.shape[1]) + (1,) * b.ndim)
    return jnp.sum(jnp.where(b[None, None] == ids, vals, 0.0), axis=0)


def _pack_bf16_pairs(v):
    w = v.shape[1] // 2
    lo = lax.bitcast_convert_type(v[:, :w].astype(BF16).astype(F32), jnp.uint32) >> 16
    hi = lax.bitcast_convert_type(v[:, w:].astype(BF16).astype(F32), jnp.uint32) & jnp.uint32(0xFFFF0000)
    return lax.bitcast_convert_type(lo | hi, jnp.int32)


def _unpack_bf16_pairs(p):
    u = lax.bitcast_convert_type(p, jnp.uint32)
    lo = lax.bitcast_convert_type(u << 16, F32)
    hi = lax.bitcast_convert_type(u & jnp.uint32(0xFFFF0000), F32)
    return lo, hi


SC_CORES = 2
SC_SUBCORES = 16
SC_WORKERS = SC_CORES * SC_SUBCORES
SC_ALIGN = 8
SC_CHUNK_ROWS = 80


def _sc_mesh():
    return plsc.VectorSubcoreMesh(core_axis_name="c", subcore_axis_name="s")


def _sc_chunk(per_worker):
    c = max(d for d in range(SC_ALIGN, SC_CHUNK_ROWS + 1, SC_ALIGN) if per_worker % d == 0)
    return c


def _sc_scatter_rows(src, dest_flat, n_out):
    T, W = src.shape
    K = dest_flat.shape[0] // T
    per_w = T // SC_WORKERS
    assert per_w * SC_WORKERS == T and per_w % SC_ALIGN == 0
    chunk = _sc_chunk(per_w)

    @functools.partial(
        pl.kernel, mesh=_sc_mesh(), out_type=jax.ShapeDtypeStruct((n_out, W), src.dtype),
        scratch_types=[pltpu.VMEM((chunk, W), src.dtype)] + [pltpu.VMEM((chunk,), jnp.int32)] * K
        + [pltpu.SemaphoreType.DMA],
        name="sc_scatter_rows")
    def k(src_hbm, dest_hbm, out_hbm, rows_v, *rest):
        idx_vs, sem = rest[:K], rest[K]
        base = (lax.axis_index("s") * SC_CORES + lax.axis_index("c")) * per_w

        @pl.loop(0, per_w // chunk)
        def _(j):
            off = pl.multiple_of(base + j * chunk, SC_ALIGN)
            pltpu.sync_copy(src_hbm.at[pl.ds(off, chunk)], rows_v)
            for kk in range(K):
                pltpu.sync_copy(dest_hbm.at[pl.ds(kk * T + off, chunk)], idx_vs[kk])
            copies = [pltpu.async_copy(rows_v, out_hbm.at[idx_vs[kk]], sem) for kk in range(K)]
            for c in copies:
                c.wait()

    return k(src, dest_flat)


def _sc_gather_rows(table, idx):
    B = idx.shape[0]
    W = table.shape[1]
    per_w = B // SC_WORKERS
    assert per_w * SC_WORKERS == B and per_w % SC_ALIGN == 0
    chunk = _sc_chunk(per_w)

    @functools.partial(
        pl.kernel, mesh=_sc_mesh(), out_type=jax.ShapeDtypeStruct((B, W), table.dtype),
        scratch_types=[pltpu.VMEM((chunk,), jnp.int32), pltpu.VMEM((chunk, W), table.dtype), pltpu.SemaphoreType.DMA],
        name="sc_gather_rows")
    def k(table_hbm, idx_hbm, out_hbm, idx_v, rows_v, sem):
        base = (lax.axis_index("s") * SC_CORES + lax.axis_index("c")) * per_w

        @pl.loop(0, per_w // chunk)
        def _(j):
            off = pl.multiple_of(base + j * chunk, SC_ALIGN)
            pltpu.sync_copy(idx_hbm.at[pl.ds(off, chunk)], idx_v)
            pltpu.async_copy(table_hbm.at[idx_v], rows_v, sem).wait()
            pltpu.sync_copy(rows_v, out_hbm.at[pl.ds(off, chunk)])

    return k(table, idx)


def _in_proj_body(x_ref, w_ref, cos_ref, sin_ref,
                  qa_ref, ka_ref, va_ref, qb_ref, kb_ref, vb_ref, gb_ref, kt_ref, vt_ref, *, seq_tiles, first_win):
    x = x_ref[...].astype(BF16)
    in_window = pl.program_id(0) % seq_tiles >= first_win

    def proj(lo, hi):
        return jnp.dot(x, w_ref[:, lo:hi], preferred_element_type=F32)

    o = 0
    qa_ref[...] = (proj(o, o + WIDTH_A) * (HEAD_DIM_A ** -0.5)).astype(BF16)
    o += WIDTH_A
    ka = proj(o, o + WIDTH_A)
    ka_ref[...] = ka.astype(BF16)

    @pl.when(in_window)
    def _():
        kt_ref[...] = ka.T
    o += WIDTH_A
    va = proj(o, o + WIDTH_A)
    va_ref[...] = va.astype(BF16)

    @pl.when(in_window)
    def _():
        vt_ref[...] = va.T
    o += WIDTH_A

    cos = cos_ref[...]
    sin = sin_ref[...]
    lane = lax.broadcasted_iota(jnp.int32, cos.shape, 1)
    first_half = (lane % KEY_DIM_B) < (KEY_DIM_B // 2)

    def rot(z):
        sw = jnp.where(first_half, pltpu.roll(z, QK_WIDTH_B - KEY_DIM_B // 2, 1), pltpu.roll(z, KEY_DIM_B // 2, 1))
        return z * cos + sw * sin

    qb_ref[...] = rot(proj(o, o + QK_WIDTH_B)).astype(BF16)
    o += QK_WIDTH_B
    kb_ref[...] = (rot(proj(o, o + QK_WIDTH_B)) * (KEY_DIM_B ** -0.5)).astype(BF16)
    o += QK_WIDTH_B
    vb_ref[...] = proj(o, o + WIDTH_B).astype(BF16)
    o += WIDTH_B
    gb_ref[...] = proj(o, o + WIDTH_B).astype(BF16)


def _in_proj(x2d, w_bf, cos_t, sin_t, tm, seq_tiles, win_tiles):
    T = x2d.shape[0]
    nt = T // tm
    n_seq = nt // seq_tiles
    j0 = seq_tiles - win_tiles

    def tok(i):
        return (i, 0)

    def tab(i):
        return (i % seq_tiles, 0)

    def win(i):
        return (i // seq_tiles, 0, jnp.maximum(i % seq_tiles - j0, 0))

    def tspec(w):
        return pl.BlockSpec((tm, w), tok)

    out_shape = (
        jax.ShapeDtypeStruct((T, WIDTH_A), BF16), jax.ShapeDtypeStruct((T, WIDTH_A), BF16),
        jax.ShapeDtypeStruct((T, WIDTH_A), BF16),
        jax.ShapeDtypeStruct((T, QK_WIDTH_B), BF16), jax.ShapeDtypeStruct((T, QK_WIDTH_B), BF16),
        jax.ShapeDtypeStruct((T, WIDTH_B), BF16), jax.ShapeDtypeStruct((T, WIDTH_B), BF16),
        jax.ShapeDtypeStruct((n_seq, WIDTH_A, win_tiles * tm), F32),
        jax.ShapeDtypeStruct((n_seq, WIDTH_A, win_tiles * tm), F32),
    )
    return pl.pallas_call(
        functools.partial(_in_proj_body, seq_tiles=seq_tiles, first_win=j0),
        grid=(nt,),
        in_specs=[tspec(D_MODEL), pl.BlockSpec(w_bf.shape, lambda i: (0, 0)),
                  pl.BlockSpec((tm, QK_WIDTH_B), tab), pl.BlockSpec((tm, QK_WIDTH_B), tab)],
        out_specs=(tspec(WIDTH_A), tspec(WIDTH_A), tspec(WIDTH_A), tspec(QK_WIDTH_B), tspec(QK_WIDTH_B),
                   tspec(WIDTH_B), tspec(WIDTH_B),
                   pl.BlockSpec((None, WIDTH_A, tm), win), pl.BlockSpec((None, WIDTH_A, tm), win)),
        out_shape=out_shape,
        compiler_params=_params(1),
        name="in_proj",
    )(x2d, w_bf, cos_t, sin_t)


def _rotary_tables(pos):
    half = KEY_DIM_B // 2
    inv_freq = 1.0 / (10000.0 ** jnp.linspace(0.0, 1.0, half, dtype=F32))
    ang = pos.astype(F32)[:, None] * inv_freq[None, :]
    cos = jnp.cos(ang)
    sin = jnp.sin(ang)
    cos_h = jnp.concatenate([cos, cos], axis=-1)
    sin_h = jnp.concatenate([-sin, sin], axis=-1)
    return jnp.tile(cos_h, (1, N_HEADS_B)), jnp.tile(sin_h, (1, N_HEADS_B))


def _attn_body(q_ref, k_ref, v_ref, bias_ref, o_ref, lse_ref, pk_ref, pv_ref):
    n = pl.program_id(2)

    @pl.when(n == 0)
    def _():
        pk_ref[...] = jnp.zeros_like(pk_ref)
        pv_ref[...] = jnp.zeros_like(pv_ref)

    q = q_ref[...]
    kc = k_ref[...]
    vc = v_ref[...]
    kp = pk_ref[...]
    vp = pv_ref[...]
    lane = lax.broadcasted_iota(jnp.int32, (BLK, LANES), 1)
    lse_tile = jnp.zeros((BLK, LANES), F32)
    for h in range(N_HEADS_A):
        sl = slice(h * HEAD_DIM_A, (h + 1) * HEAD_DIM_A)
        kh = jnp.concatenate([kp[:, sl], kc[:, sl]], axis=0)
        vh = jnp.concatenate([vp[:, sl], vc[:, sl]], axis=0)
        s = lax.dot_general(q[:, sl], kh, (((1,), (1,)), ((), ())), preferred_element_type=F32) + bias_ref[h]
        m = jnp.max(s, axis=-1, keepdims=True)
        e = jnp.exp(s - m)
        den = jnp.sum(e, axis=-1, keepdims=True)
        o = jnp.dot(e.astype(BF16), vh, preferred_element_type=F32) / den
        o_ref[:, sl] = o.astype(BF16)
        lse_tile = jnp.where(lane == h, m + jnp.log(den), lse_tile)
    lse_ref[...] = lse_tile
    pk_ref[...] = kc
    pv_ref[...] = vc


def _attn_bias_tables(rel_bias, window, dil):
    n_keys = window // dil
    i = np.arange(BLK)[:, None]
    j = np.arange(2 * BLK)[None, :]
    rel = BLK + i - j
    in_band = (rel >= 0) & (rel <= n_keys)
    bias = _bias_by_bucket(rel_bias, _t5_bucket(np.clip(rel, 0, None) * dil))
    later = jnp.where(jnp.asarray(in_band)[None], bias, NEG_INF)
    first = jnp.where(jnp.asarray(in_band & (j >= BLK))[None], bias, NEG_INF)
    return jnp.stack([first, later])


def _dilated_branch(q, k, v, bias_tab, dil):
    B, S, _ = q.shape
    L = S // dil
    nb = L // BLK

    def view(t):
        return t.reshape(B, L, dil * t.shape[-1])

    def cls(b, r, n):
        return (b, n, r)

    qkv_spec = pl.BlockSpec((None, BLK, WIDTH_A), cls)
    o, lse = pl.pallas_call(
        _attn_body,
        grid=(B, dil, nb),
        in_specs=[qkv_spec, qkv_spec, qkv_spec,
                  pl.BlockSpec((None, N_HEADS_A, BLK, 2 * BLK), lambda b, r, n: (jnp.minimum(n, 1), 0, 0, 0))],
        out_specs=(qkv_spec, pl.BlockSpec((None, BLK, LANES), cls)),
        out_shape=(jax.ShapeDtypeStruct((B, L, dil * WIDTH_A), BF16),
                   jax.ShapeDtypeStruct((B, L, dil * LANES), F32)),
        scratch_shapes=[pltpu.VMEM((BLK, WIDTH_A), BF16), pltpu.VMEM((BLK, WIDTH_A), BF16)],
        compiler_params=_params(3),
        name=f"dil_attn_d{dil}",
    )(view(q), view(k), view(v), bias_tab)
    return o.reshape(B, S, WIDTH_A), lse.reshape(B, S, LANES)


def _ret_body(*refs, n_branch):
    qb_ref, kb_ref, vb_ref, gb_ref = refs[:4]
    p = 4
    if n_branch:
        o_refs = refs[p:p + n_branch]
        l_refs = refs[p + n_branch:p + 2 * n_branch]
        exp_ref = refs[p + 2 * n_branch]
        p += 2 * n_branch + 1
    else:
        oa_ref = refs[p]
        p += 1
    st0_ref, dmat_ref, qdec_ref, kdec_ref, cdec_ref, cat_ref, sto_ref, st_ref = refs[p:]

    @pl.when(pl.program_id(1) == 0)
    def _():
        st_ref[...] = st0_ref[...]

    if n_branch:
        ls = [r[...] for r in l_refs]
        mx = functools.reduce(jnp.maximum, ls)
        ws = [jnp.exp(l - mx) for l in ls]
        tot = functools.reduce(lambda a, b: a + b, ws)
        oa = None
        for w, o_ref in zip(ws, o_refs):
            w = w / tot
            w_hi = w.astype(BF16)
            w_lo = (w - w_hi.astype(F32)).astype(BF16)
            w_full = (jnp.dot(w_hi, exp_ref[...], preferred_element_type=F32)
                      + jnp.dot(w_lo, exp_ref[...], preferred_element_type=F32))
            term = w_full * o_ref[...].astype(F32)
            oa = term if oa is None else oa + term
        cat_ref[:, :WIDTH_A] = oa.astype(BF16)
    else:
        cat_ref[:, :WIDTH_A] = oa_ref[...].astype(BF16)

    for h in range(N_HEADS_B):
        ks = slice(h * KEY_DIM_B, (h + 1) * KEY_DIM_B)
        vs = slice(h * VAL_DIM_B, (h + 1) * VAL_DIM_B)
        q = qb_ref[:, ks]
        k = kb_ref[:, ks]
        v = vb_ref[:, vs]
        st = st_ref[h]
        a = lax.dot_general(q, k, (((1,), (1,)), ((), ())), preferred_element_type=F32) * dmat_ref[h]
        o = (jnp.dot(a.astype(BF16), v, preferred_element_type=F32)
             + jnp.dot(q, st.astype(BF16), preferred_element_type=F32) * qdec_ref[h])
        kd = (k.astype(F32) * kdec_ref[h]).astype(BF16)
        st_new = st * cdec_ref[h] + lax.dot_general(kd, v, (((0,), (0,)), ((), ())), preferred_element_type=F32)
        st_ref[h] = st_new
        sto_ref[h] = st_new
        mu = jnp.mean(o, axis=-1, keepdims=True)
        var = jnp.mean(jnp.square(o - mu), axis=-1, keepdims=True)
        obn = (o - mu) * lax.rsqrt(var + GN_EPS)
        g = gb_ref[:, vs].astype(F32)
        gated = g * (1.0 / (1.0 + jnp.exp(-g))) * obn
        cat_ref[:, WIDTH_A + h * VAL_DIM_B:WIDTH_A + (h + 1) * VAL_DIM_B] = gated.astype(BF16)


def _decay_tables(chunk, rows):
    H = N_HEADS_B
    log_g = jnp.log(1.0 - 2.0 ** (-5.0 - jnp.arange(H, dtype=F32)))
    i = jnp.arange(rows, dtype=F32)
    live = np.arange(rows) < chunk
    diff = i[:, None] - i[None, :]
    causal = (diff >= 0) & jnp.asarray(live[:, None] & live[None, :])
    dmat = jnp.where(causal[None], jnp.exp(jnp.where(causal, diff, 0.0)[None] * log_g[:, None, None]), 0.0)
    q_decay = jnp.where(jnp.asarray(live)[None], jnp.exp((i[None, :] + 1.0) * log_g[:, None]), 0.0)
    k_decay = jnp.where(jnp.asarray(live)[None], jnp.exp((chunk - 1.0 - i)[None, :] * log_g[:, None]), 0.0)
    c_decay = jnp.exp(chunk * log_g)
    qdec = jnp.broadcast_to(q_decay[:, :, None], (H, rows, VAL_DIM_B))
    kdec = jnp.broadcast_to(k_decay[:, :, None], (H, rows, KEY_DIM_B))
    cdec = jnp.broadcast_to(c_decay[:, None, None], (H, KEY_DIM_B, VAL_DIM_B))
    return dmat.astype(F32), qdec.astype(F32), kdec.astype(F32), cdec.astype(F32)


def _ret_mix(qb, kb, vb, gb, attn, state0, chunk):
    B, S, _ = qb.shape
    rows = RET_CHUNK
    nc = S // rows
    tables = _decay_tables(chunk, rows)

    def tok(b, c):
        return (b, c, 0)

    def tspec(w):
        return pl.BlockSpec((None, rows, w), tok)

    def const(shape):
        return pl.BlockSpec(shape, lambda b, c: (0,) * len(shape))

    ins = [qb, kb, vb, gb]
    in_specs = [tspec(QK_WIDTH_B), tspec(QK_WIDTH_B), tspec(WIDTH_B), tspec(WIDTH_B)]
    if isinstance(attn, tuple):
        outs_a, lses = attn
        n_branch = len(outs_a)
        expand = np.zeros((LANES, WIDTH_A), np.float32)
        for h in range(N_HEADS_A):
            expand[h, h * HEAD_DIM_A:(h + 1) * HEAD_DIM_A] = 1.0
        ins += list(outs_a) + list(lses) + [jnp.asarray(expand, BF16)]
        in_specs += [tspec(WIDTH_A)] * n_branch + [tspec(LANES)] * n_branch + [const((LANES, WIDTH_A))]
    else:
        n_branch = 0
        ins.append(attn)
        in_specs.append(tspec(WIDTH_A))
    st_shape = (N_HEADS_B, KEY_DIM_B, VAL_DIM_B)
    st_spec = pl.BlockSpec((None,) + st_shape, lambda b, c: (b, 0, 0, 0))
    ins += [state0] + list(tables)
    in_specs += [st_spec] + [const(t.shape) for t in tables]
    return pl.pallas_call(
        functools.partial(_ret_body, n_branch=n_branch),
        grid=(B, nc),
        in_specs=in_specs,
        out_specs=(tspec(WIDTH_A + WIDTH_B), st_spec),
        out_shape=(jax.ShapeDtypeStruct((B, S, WIDTH_A + WIDTH_B), BF16),
                   jax.ShapeDtypeStruct((B,) + st_shape, F32)),
        scratch_shapes=[pltpu.VMEM(st_shape, F32)],
        compiler_params=_params(2),
        name=f"ret_mix_{n_branch}",
    )(*ins)


SAMP_Q_ROWS = 64
SAMP_NEW_LANES = 128


def _samp_attn_body(q_ref, kt_ref, vt_ref, knt_ref, vnt_ref, bc_ref, bn_ref, hm_ref, o_ref, ko_ref, vo_ref, *, ds):
    q = q_ref[...]
    kt = kt_ref[...]
    vt = vt_ref[...]
    knt = knt_ref[...]
    vnt = vnt_ref[...]
    w = kt.shape[1]
    is_new = lax.broadcasted_iota(jnp.int32, knt.shape, 1) >= SAMP_NEW_LANES - ds
    for src, new, dst in ((kt, knt, ko_ref), (vt, vnt, vo_ref)):
        rolled = pltpu.roll(src, w - ds, 1)
        dst[:, :w - SAMP_NEW_LANES] = rolled[:, :w - SAMP_NEW_LANES]
        dst[:, w - SAMP_NEW_LANES:] = jnp.where(is_new, new, rolled[:, w - SAMP_NEW_LANES:])

    s_c = jnp.dot(q, kt.astype(BF16), preferred_element_type=F32)
    s_n = jnp.dot(q, knt.astype(BF16), preferred_element_type=F32)
    es_c, es_n, dens, lses = [], [], [], []
    for n in range(len(DILATED_BRANCHES)):
        sc = s_c + bc_ref[n]
        sn = s_n + bn_ref[n]
        m = jnp.maximum(jnp.max(sc, axis=-1, keepdims=True), jnp.max(sn, axis=-1, keepdims=True))
        ec = jnp.exp(sc - m)
        en = jnp.exp(sn - m)
        den = jnp.sum(ec, axis=-1, keepdims=True) + jnp.sum(en, axis=-1, keepdims=True)
        es_c.append(ec)
        es_n.append(en)
        dens.append(den)
        lses.append(m + jnp.log(den))
    mx = functools.reduce(jnp.maximum, lses)
    ws = [jnp.exp(l - mx) for l in lses]
    tot = functools.reduce(lambda a, b: a + b, ws)
    p_c = None
    p_n = None
    for w, den, ec, en in zip(ws, dens, es_c, es_n):
        coef = w / (tot * den)
        p_c = coef * ec if p_c is None else p_c + coef * ec
        p_n = coef * en if p_n is None else p_n + coef * en
    nt = (((1,), (1,)), ((), ()))
    o = (lax.dot_general(p_c.astype(BF16), vt.astype(BF16), nt, preferred_element_type=F32)
         + lax.dot_general(p_n.astype(BF16), vnt.astype(BF16), nt, preferred_element_type=F32))
    o = o * hm_ref[...]
    o_ref[...] = jnp.sum(o.reshape(SUBLANES, N_HEADS_A, WIDTH_A), axis=1)


def _samp_bias_tables(rel_bias, w_buf, ds):
    tabs_c, tabs_n = [], []
    s = np.arange(SUBLANES)[:, None]
    live_s = s < ds
    first_new = SAMP_NEW_LANES - ds
    for window, dil in DILATED_BRANCHES:
        n_keys = window // dil
        for keys, live_k, tabs in ((np.arange(w_buf)[None, :], True, tabs_c),
                                   (w_buf - first_new + np.arange(SAMP_NEW_LANES)[None, :],
                                    np.arange(SAMP_NEW_LANES)[None, :] >= first_new, tabs_n)):
            n = keys.shape[1]
            dist = w_buf + s - keys
            valid = (dist >= 0) & (dist % dil == 0) & (dist // dil <= n_keys) & live_k
            bias = _bias_by_bucket(rel_bias, _t5_bucket(np.clip(dist, 0, None))).transpose(1, 0, 2)
            tab = jnp.where(jnp.asarray(valid)[:, None, :], bias, NEG_INF)
            pad = jnp.where(jnp.asarray(np.broadcast_to(live_k, dist.shape))[:, None, :], 0.0, NEG_INF)
            tab = jnp.where(jnp.asarray(live_s)[:, :, None], tab, pad)
            tabs.append(tab.reshape(SAMP_Q_ROWS, n))
    return jnp.stack(tabs_c), jnp.stack(tabs_n)


def _samp_attn(qa, cache_kt, cache_vt, knt, vnt, rel_bias, ds):
    DB, DS, _ = qa.shape
    W = cache_kt.shape[2]
    head_of_lane = np.arange(WIDTH_A) // HEAD_DIM_A
    hmask = (np.arange(SAMP_Q_ROWS)[:, None] % N_HEADS_A == head_of_lane[None, :])
    q8 = jnp.pad(qa, ((0, 0), (0, SUBLANES - DS), (0, 0)))
    q_rows = jnp.where(jnp.asarray(hmask)[None], jnp.repeat(q8, N_HEADS_A, axis=1), jnp.zeros((), BF16))
    bias_c, bias_n = _samp_bias_tables(rel_bias, W, DS)

    def per_b(rows, w):
        return pl.BlockSpec((None, rows, w), lambda b: (b, 0, 0))

    def const(a):
        return pl.BlockSpec(a.shape, lambda b: (0,) * a.ndim)

    hm = jnp.asarray(hmask, F32)
    return pl.pallas_call(
        functools.partial(_samp_attn_body, ds=ds),
        grid=(DB,),
        in_specs=[per_b(SAMP_Q_ROWS, WIDTH_A), per_b(WIDTH_A, W), per_b(WIDTH_A, W),
                  per_b(WIDTH_A, SAMP_NEW_LANES), per_b(WIDTH_A, SAMP_NEW_LANES),
                  const(bias_c), const(bias_n), const(hm)],
        out_specs=(per_b(SUBLANES, WIDTH_A), per_b(WIDTH_A, W), per_b(WIDTH_A, W)),
        out_shape=(jax.ShapeDtypeStruct((DB, SUBLANES, WIDTH_A), F32),
                   jax.ShapeDtypeStruct((DB, WIDTH_A, W), F32), jax.ShapeDtypeStruct((DB, WIDTH_A, W), F32)),
        compiler_params=_params(1),
        name="samp_attn",
    )(q_rows, cache_kt, cache_vt, knt, vnt, bias_c, bias_n, hm)


def _route_body(cat_ref, x_ref, wout_ref, g_ref, b_ref, rwh_ref, rwl_ref, rb_ref, tril_ref, base_ref,
                h_ref, hp_ref, route_ref, cnt_ref, *, alpha):
    @pl.when(pl.program_id(0) == 0)
    def _():
        cnt_ref[...] = base_ref[...]

    mix = jnp.dot(cat_ref[...], wout_ref[...], preferred_element_type=F32)
    y = alpha * x_ref[...] + mix
    mu = jnp.mean(y, axis=-1, keepdims=True)
    var = jnp.mean(jnp.square(y - mu), axis=-1, keepdims=True)
    h = (y - mu) * lax.rsqrt(var + LN_EPS) * g_ref[...] + b_ref[...]
    h_ref[...] = h
    hb = h.astype(BF16)
    hp_ref[...] = _pack_bf16_pairs(h)
    hl = (h - hb.astype(F32)).astype(BF16)
    logits = (jnp.dot(hb, rwh_ref[...], preferred_element_type=F32)
              + jnp.dot(hb, rwl_ref[...], preferred_element_type=F32)
              + jnp.dot(hl, rwh_ref[...], preferred_element_type=F32)) + rb_ref[...]

    tm = logits.shape[0]
    lane = lax.broadcasted_iota(jnp.int32, (tm, LANES), 1)
    work = logits
    vals, idxs = [], []
    for _ in range(TOP_K):
        m = jnp.max(work, axis=-1, keepdims=True)
        idx = jnp.min(jnp.where(work == m, lane, LANES), axis=-1, keepdims=True)
        vals.append(m)
        idxs.append(idx)
        work = jnp.where(lane == idx, -jnp.inf, work)
    es = [jnp.exp(v - vals[0]) for v in vals]
    tot = functools.reduce(lambda a, b: a + b, es)
    onehot = jnp.zeros((tm, LANES), F32)
    for idx in idxs:
        onehot = onehot + (lane == idx).astype(F32)
    before = jnp.dot(tril_ref[...], onehot.astype(BF16), preferred_element_type=F32) + cnt_ref[0:1, :]
    route = jnp.zeros((tm, LANES), F32)
    for k in range(TOP_K):
        rank = jnp.sum(jnp.where(lane == idxs[k], before, 0.0), axis=-1, keepdims=True)
        route = jnp.where(lane == k, idxs[k].astype(F32), route)
        route = jnp.where(lane == TOP_K + k, es[k] / tot, route)
        route = jnp.where(lane == 2 * TOP_K + k, rank, route)
    route_ref[...] = route
    cnt_ref[...] = cnt_ref[...] + jnp.sum(onehot, axis=0, keepdims=True)


def _out_route(cat, x2d, w_out_bf, ln_g, ln_b, rw_hi, rw_lo, rb, base, tm, alpha):
    T = x2d.shape[0]
    tril = jnp.asarray(np.tril(np.ones((tm, tm), np.float32), -1), BF16)

    def tok(i):
        return (i, 0)

    def const(a):
        return pl.BlockSpec(a.shape, lambda i: (0,) * a.ndim)

    ins = (cat, x2d, w_out_bf, ln_g, ln_b, rw_hi, rw_lo, rb, tril, base)
    in_specs = [pl.BlockSpec((tm, cat.shape[1]), tok), pl.BlockSpec((tm, D_MODEL), tok)] + [const(a) for a in ins[2:]]
    return pl.pallas_call(
        functools.partial(_route_body, alpha=alpha),
        grid=(T // tm,),
        in_specs=in_specs,
        out_specs=(pl.BlockSpec((tm, D_MODEL), tok), pl.BlockSpec((tm, D_MODEL // 2), tok),
                   pl.BlockSpec((tm, LANES), tok), pl.BlockSpec((SUBLANES, LANES), lambda i: (0, 0))),
        out_shape=(jax.ShapeDtypeStruct((T, D_MODEL), F32), jax.ShapeDtypeStruct((T, D_MODEL // 2), jnp.int32),
                   jax.ShapeDtypeStruct((T, LANES), F32), jax.ShapeDtypeStruct((SUBLANES, LANES), F32)),
        compiler_params=_params(1),
        name="out_route",
    )(*ins)


MOE_CAST_ROWS = 128


def _moe_body(be_ref, nused_ref, x_ref, wgu_ref, bgu_ref, wdn_ref, bdn_ref, y_ref, wgu_bf, wdn_bf):
    b = pl.program_id(0)
    e = be_ref[b]
    prev = be_ref[jnp.maximum(b - 1, 0)]
    d_exp = wdn_ref.shape[0]

    @pl.when((b == 0) | (e != prev))
    def _():
        def cast_gu(i, c):
            r = pl.ds(pl.multiple_of(i * MOE_CAST_ROWS, MOE_CAST_ROWS), MOE_CAST_ROWS)
            wgu_bf[r, :] = wgu_ref[r, :].astype(BF16)
            return c

        def cast_dn(i, c):
            r = pl.ds(pl.multiple_of(i * MOE_CAST_ROWS, MOE_CAST_ROWS), MOE_CAST_ROWS)
            wdn_bf[r, :] = wdn_ref[r, :].astype(BF16)
            return c

        lax.fori_loop(0, wgu_ref.shape[0] // MOE_CAST_ROWS, cast_gu, 0)
        lax.fori_loop(0, d_exp // MOE_CAST_ROWS, cast_dn, 0)

    @pl.when(b < nused_ref[0])
    def _():
        x_lo, x_hi = _unpack_bf16_pairs(x_ref[...])
        x_lo = x_lo.astype(BF16)
        x_hi = x_hi.astype(BF16)
        dh = x_lo.shape[1]

        def xw(cols):
            return (jnp.dot(x_lo, wgu_bf[:dh, cols], preferred_element_type=F32)
                    + jnp.dot(x_hi, wgu_bf[dh:, cols], preferred_element_type=F32) + bgu_ref[:, cols])

        half = d_exp // 2
        y = None
        for c in range(2):
            lo = c * half
            gate = jnp.minimum(xw(slice(lo, lo + half)), SWIGLU_LIMIT)
            up = jnp.clip(xw(slice(d_exp + lo, d_exp + lo + half)), -SWIGLU_LIMIT, SWIGLU_LIMIT)
            act = (up + 1.0) * gate * (1.0 / (1.0 + jnp.exp(-SWIGLU_ALPHA * gate)))
            part = jnp.dot(act.astype(BF16), wdn_bf[lo:lo + half, :], preferred_element_type=F32)
            y = part if y is None else y + part
        y_ref[...] = _pack_bf16_pairs(y + bdn_ref[...])

    @pl.when(b >= nused_ref[0])
    def _():
        y_ref[...] = jnp.zeros_like(y_ref)


def _moe_ffn(xs, rows, block_e, n_used, w_gu, b_gu, w_dn, b_dn):
    E, D, two_de = w_gu.shape
    d_exp = two_de // 2
    nb = rows // MOE_BLOCK
    grid_spec = pltpu.PrefetchScalarGridSpec(
        num_scalar_prefetch=2,
        grid=(nb,),
        in_specs=[
            pl.BlockSpec((MOE_BLOCK, D // 2), lambda b, be, nu: (b, 0)),
            pl.BlockSpec((None, D, two_de), lambda b, be, nu: (be[b], 0, 0)),
            pl.BlockSpec((None, 1, two_de), lambda b, be, nu: (be[b], 0, 0)),
            pl.BlockSpec((None, d_exp, D), lambda b, be, nu: (be[b], 0, 0)),
            pl.BlockSpec((None, 1, D), lambda b, be, nu: (be[b], 0, 0)),
        ],
        out_specs=pl.BlockSpec((MOE_BLOCK, D // 2), lambda b, be, nu: (b, 0)),
        scratch_shapes=[pltpu.VMEM((D, two_de), BF16), pltpu.VMEM((d_exp, D), BF16)],
    )
    return pl.pallas_call(
        _moe_body,
        grid_spec=grid_spec,
        out_shape=jax.ShapeDtypeStruct((rows, D // 2), jnp.int32),
        compiler_params=_params(1),
        name="moe_ffn",
    )(block_e, n_used, xs, w_gu, b_gu.reshape(E, 1, two_de), w_dn, b_dn.reshape(E, 1, D))


def _ple_body(h_ref, ys_ref, route_ref, p_ref, g_ref, b_ref, wpg_ref, bpg_ref, wpp_ref, o_ref, *, alpha):
    route = route_ref[...]
    f_lo = None
    f_hi = None
    for k in range(TOP_K):
        lo, hi = _unpack_bf16_pairs(ys_ref[k])
        g = route[:, TOP_K + k:TOP_K + k + 1]
        f_lo = g * lo if f_lo is None else f_lo + g * lo
        f_hi = g * hi if f_hi is None else f_hi + g * hi
    y = alpha * h_ref[...] + jnp.concatenate([f_lo, f_hi], axis=1)
    mu = jnp.mean(y, axis=-1, keepdims=True)
    var = jnp.mean(jnp.square(y - mu), axis=-1, keepdims=True)
    h2 = (y - mu) * lax.rsqrt(var + LN_EPS) * g_ref[...] + b_ref[...]
    z = jnp.dot(h2.astype(BF16), wpg_ref[...], preferred_element_type=F32) + bpg_ref[...]
    gate = 1.0 / (1.0 + jnp.exp(-z))
    proj = jnp.dot(p_ref[...].astype(BF16), wpp_ref[...], preferred_element_type=F32)
    o_ref[...] = h2 + gate * proj


def _ffn_ple(h, y_slots, route, p, ln_g, ln_b, w_pg_bf, b_pg, w_pp_bf, tm, tile0, alpha):
    T = h.shape[0]

    def tok(i):
        return (i, 0)

    def const(a):
        return pl.BlockSpec(a.shape, lambda i: (0,) * a.ndim)

    consts = (ln_g, ln_b, w_pg_bf, b_pg, w_pp_bf)
    return pl.pallas_call(
        functools.partial(_ple_body, alpha=alpha),
        grid=(T // tm,),
        in_specs=[pl.BlockSpec((tm, D_MODEL), tok),
                  pl.BlockSpec((TOP_K, tm, D_MODEL // 2), lambda i: (0, tile0 + i, 0)),
                  pl.BlockSpec((tm, LANES), tok),
                  pl.BlockSpec((tm, p.shape[1]), tok)] + [const(a) for a in consts],
        out_specs=pl.BlockSpec((tm, D_MODEL), tok),
        out_shape=jax.ShapeDtypeStruct((T, D_MODEL), F32),
        compiler_params=_params(1),
        name="ffn_ple",
    )(h, y_slots, route, p, *consts)


def _row(v):
    return v.reshape(1, -1).astype(F32)


def kernel(x_prompt, x_sample, cache_win_k, cache_win_v, state_ret, p_prompt, p_sample, rel_bias, w_in, w_out,
           ln1_g, ln1_b, router_w, router_b, w_gate_up, b_gate_up, w_down, b_down, ln2_g, ln2_b,
           w_ple_gate, b_ple_gate, w_ple_proj):
    B, S, D = x_prompt.shape
    DB, DS, _ = x_sample.shape
    depth = w_in.shape[0]
    w_buf = cache_win_k.shape[2]
    n_exp = router_w.shape[-1]
    alpha = (2.0 * depth) ** 0.25
    assert depth == 1 and D == D_MODEL
    assert S % (BLK * MAX_DIL) == 0 and S >= WINDOW_MAX and w_buf == WINDOW_MAX and DS <= SUBLANES
    tm_p = 512
    Tp, Ts = B * S, DB * DS
    assert Tp % tm_p == 0 and Ts % SUBLANES == 0

    i = 0
    w_in_bf = w_in[i].astype(BF16)
    w_out_bf = w_out[i].astype(BF16)
    w_pg_bf = w_ple_gate[i].astype(BF16)
    w_pp_bf = w_ple_proj[i].astype(BF16)
    rw = jnp.pad(router_w[i], ((0, 0), (0, LANES - n_exp)))
    rw_hi = rw.astype(BF16)
    rw_lo = (rw - rw_hi.astype(F32)).astype(BF16)
    rb = jnp.pad(router_b[i], (0, LANES - n_exp), constant_values=NEG_INF).reshape(1, LANES)

    cos_p, sin_p = _rotary_tables(jnp.arange(S, dtype=jnp.int32))
    qa, ka, va, qb, kb, vb, gb, kf, vf = _in_proj(x_prompt.reshape(Tp, D), w_in_bf, cos_p, sin_p, tm_p,
                                                   S // tm_p, WINDOW_MAX // tm_p)

    def seq(t):
        return t.reshape(B, S, t.shape[-1])

    outs_a, lses = [], []
    for window, dil in DILATED_BRANCHES:
        o_n, l_n = _dilated_branch(seq(qa), seq(ka), seq(va), _attn_bias_tables(rel_bias, window, dil), dil)
        outs_a.append(o_n)
        lses.append(l_n)
    st_zero = jnp.zeros((B, N_HEADS_B, KEY_DIM_B, VAL_DIM_B), F32)
    cat_p, rst_p = _ret_mix(seq(qb), seq(kb), seq(vb), seq(gb), (outs_a, lses), st_zero, RET_CHUNK)
    base0 = jnp.zeros((SUBLANES, LANES), F32)
    h_p, hp_p, route_p, cnt_p = _out_route(cat_p.reshape(Tp, -1), x_prompt.reshape(Tp, D), w_out_bf,
                                           _row(ln1_g[i]), _row(ln1_b[i]), rw_hi, rw_lo, rb, base0, tm_p, alpha)

    pos_s = jnp.tile(PAST_LEN + jnp.arange(DS, dtype=jnp.int32), DB)
    cos_s, sin_s = _rotary_tables(pos_s)
    qa_s, _, _, qb_s, kb_s, vb_s, gb_s, kf_s, vf_s = _in_proj(x_sample.reshape(Ts, D), w_in_bf, cos_s, sin_s,
                                                             Ts, 1, 1)

    def positions_minor(t):
        return jnp.transpose(t, (0, 2, 3, 1)).reshape(DB, WIDTH_A, t.shape[1])

    def positions_major(t):
        return jnp.transpose(t.reshape(t.shape[0], N_HEADS_A, HEAD_DIM_A, t.shape[2]), (0, 3, 1, 2))[None]

    def new_columns(t):
        t = jnp.transpose(t.reshape(WIDTH_A, DB, DS), (1, 0, 2))
        return jnp.pad(t, ((0, 0), (0, 0), (SAMP_NEW_LANES - DS, 0)))

    oa_s, kt_out, vt_out = _samp_attn(qa_s.reshape(DB, DS, WIDTH_A), positions_minor(cache_win_k[i]),
                                      positions_minor(cache_win_v[i]), new_columns(kf_s), new_columns(vf_s),
                                      rel_bias, DS)

    def pad_rows(t, rows):
        t = t.reshape(DB, -1, t.shape[-1])
        return jnp.pad(t, ((0, 0), (0, rows - t.shape[1]), (0, 0)))

    cat_s, rst_s = _ret_mix(pad_rows(qb_s, RET_CHUNK), pad_rows(kb_s, RET_CHUNK), pad_rows(vb_s, RET_CHUNK),
                            pad_rows(gb_s, RET_CHUNK), pad_rows(oa_s, RET_CHUNK), state_ret[i].astype(F32), DS)
    cat_s = cat_s[:, :DS].reshape(Ts, -1)
    h_s, hp_s, route_s, cnt = _out_route(cat_s, x_sample.reshape(Ts, D), w_out_bf, _row(ln1_g[i]), _row(ln1_b[i]),
                                         rw_hi, rw_lo, rb, cnt_p, Ts, alpha)

    T = Tp + Ts
    t_align = SC_WORKERS * SC_ALIGN
    T_pad = -(-T // t_align) * t_align
    route = jnp.concatenate([route_p, route_s], axis=0)
    top_idx = route[:, :TOP_K].astype(jnp.int32)
    rank = route[:, 2 * TOP_K:3 * TOP_K].astype(jnp.int32)
    counts = cnt[0, :n_exp].astype(jnp.int32)
    padded = (counts + MOE_BLOCK - 1) // MOE_BLOCK * MOE_BLOCK
    pad_end = jnp.cumsum(padded)
    pad_start = pad_end - padded
    dest = pad_start[top_idx] + rank
    n_blocks = -(-T * TOP_K // MOE_BLOCK) + n_exp
    rows = n_blocks * MOE_BLOCK
    block_start = jnp.arange(n_blocks, dtype=jnp.int32) * MOE_BLOCK
    block_e = jnp.minimum(jnp.sum(pad_end[None, :] <= block_start[:, None], axis=1), n_exp - 1).astype(jnp.int32)
    n_used = (pad_end[-1:] // MOE_BLOCK).astype(jnp.int32)
    n_fill = T_pad - T
    spare = rows + jnp.arange(n_fill * TOP_K, dtype=jnp.int32).reshape(n_fill, TOP_K)
    dest_sc = jnp.concatenate([dest, spare], axis=0).T.reshape(-1)
    dest_ga = jnp.concatenate([dest, jnp.zeros((n_fill, TOP_K), jnp.int32)], axis=0).T.reshape(-1)
    hp_all = jnp.concatenate([hp_p, hp_s, jnp.zeros((n_fill, D // 2), jnp.int32)], axis=0)
    xs = _sc_scatter_rows(hp_all, dest_sc, rows + n_fill * TOP_K)
    ys = _moe_ffn(xs, rows, block_e, n_used, w_gate_up[i], b_gate_up[i], w_down[i], b_down[i])
    y_slots = _sc_gather_rows(ys, dest_ga).reshape(TOP_K, T_pad, D // 2)

    ple_args = (_row(ln2_g[i]), _row(ln2_b[i]), w_pg_bf, _row(b_ple_gate[i]), w_pp_bf)
    assert Tp % Ts == 0
    y_p = _ffn_ple(h_p, y_slots, route_p, p_prompt[i].reshape(Tp, D_PLE), *ple_args, tm_p, 0, alpha)
    y_s = _ffn_ple(h_s, y_slots, route_s, p_sample[i].reshape(Ts, D_PLE), *ple_args, Ts, Tp // Ts, alpha)

    return (y_p.reshape(B, S, D), y_s.reshape(DB, DS, D), positions_major(kf), positions_major(vf),
            rst_p[None], positions_major(kt_out), positions_major(vt_out), rst_s[None])
```

```python
import functools

import numpy as np
import jax
import jax.numpy as jnp
from jax import lax
from jax.experimental import pallas as pl
from jax.experimental.pallas import tpu as pltpu
from jax.experimental.pallas import tpu_sc as plsc

F32 = jnp.float32
BF16 = jnp.bfloat16

D_MODEL = 1024
D_PLE = 256
N_HEADS_A = 8
HEAD_DIM_A = 64
WIDTH_A = N_HEADS_A * HEAD_DIM_A
DILATED_BRANCHES = ((128, 1), (512, 4), (2048, 16))
BLK = 128
WINDOW_MAX = 2048
MAX_DIL = 16
NUM_BUCKETS = 32
MAX_DISTANCE = 2048
N_HEADS_B = 4
KEY_DIM_B = 64
VAL_DIM_B = 128
QK_WIDTH_B = N_HEADS_B * KEY_DIM_B
WIDTH_B = N_HEADS_B * VAL_DIM_B
RET_CHUNK = 128
GN_EPS = 1e-6
TOP_K = 4
SWIGLU_LIMIT = 7.0
SWIGLU_ALPHA = 1.702
LN_EPS = 1e-5
NEG_INF = -1e30
PAST_LEN = 16384
MOE_BLOCK = 256
LANES = 128
SUBLANES = 8
VMEM_LIMIT = 52 * 1024 * 1024


def _params(n_axes, vmem=VMEM_LIMIT):
    return pltpu.CompilerParams(dimension_semantics=("arbitrary",) * n_axes, vmem_limit_bytes=vmem)


def _t5_bucket(dist):
    dist = np.asarray(dist, dtype=np.int32)
    max_exact = NUM_BUCKETS // 2
    d = np.maximum(dist, 1).astype(np.float32)
    large = max_exact + (np.log(d / max_exact) / np.log(MAX_DISTANCE / max_exact)
                         * (NUM_BUCKETS - max_exact)).astype(np.int32)
    large = np.minimum(large, NUM_BUCKETS - 1)
    return np.where(dist < max_exact, dist, large).astype(np.int32)


def _bias_by_bucket(rel_bias, buckets):
    b = jnp.asarray(buckets, jnp.int32)
    ids = jnp.arange(NUM_BUCKETS, dtype=jnp.int32).reshape((NUM_BUCKETS, 1) + (1,) * b.ndim)
    vals = rel_bias.astype(F32).reshape((NUM_BUCKETS, rel_bias.shape[1]) + (1,) * b.ndim)
    return jnp.sum(jnp.where(b[None, None] == ids, vals, 0.0), axis=0)


def _pack_bf16_pairs(v):
    w = v.shape[1] // 2
    lo = lax.bitcast_convert_type(v[:, :w].astype(BF16).astype(F32), jnp.uint32) >> 16
    hi = lax.bitcast_convert_type(v[:, w:].astype(BF16).astype(F32), jnp.uint32) & jnp.uint32(0xFFFF0000)
    return lax.bitcast_convert_type(lo | hi, jnp.int32)


def _unpack_bf16_pairs(p):
    u = lax.bitcast_convert_type(p, jnp.uint32)
    lo = lax.bitcast_convert_type(u << 16, F32)
    hi = lax.bitcast_convert_type(u & jnp.uint32(0xFFFF0000), F32)
    return lo, hi


SC_CORES = 2
SC_SUBCORES = 16
SC_WORKERS = SC_CORES * SC_SUBCORES
SC_ALIGN = 8
SC_CHUNK_ROWS = 80


def _sc_mesh():
    return plsc.VectorSubcoreMesh(core_axis_name="c", subcore_axis_name="s")


def _sc_chunk(per_worker):
    c = max(d for d in range(SC_ALIGN, SC_CHUNK_ROWS + 1, SC_ALIGN) if per_worker % d == 0)
    return c


def _sc_scatter_rows(src, dest_flat, n_out):
    T, W = src.shape
    K = dest_flat.shape[0] // T
    per_w = T // SC_WORKERS
    assert per_w * SC_WORKERS == T and per_w % SC_ALIGN == 0
    chunk = _sc_chunk(per_w)

    @functools.partial(
        pl.kernel, mesh=_sc_mesh(), out_type=jax.ShapeDtypeStruct((n_out, W), src.dtype),
        scratch_types=[pltpu.VMEM((chunk, W), src.dtype)] + [pltpu.VMEM((chunk,), jnp.int32)] * K
        + [pltpu.SemaphoreType.DMA],
        name="sc_scatter_rows")
    def k(src_hbm, dest_hbm, out_hbm, rows_v, *rest):
        idx_vs, sem = rest[:K], rest[K]
        base = (lax.axis_index("s") * SC_CORES + lax.axis_index("c")) * per_w

        @pl.loop(0, per_w // chunk)
        def _(j):
            off = pl.multiple_of(base + j * chunk, SC_ALIGN)
            pltpu.sync_copy(src_hbm.at[pl.ds(off, chunk)], rows_v)
            for kk in range(K):
                pltpu.sync_copy(dest_hbm.at[pl.ds(kk * T + off, chunk)], idx_vs[kk])
            copies = [pltpu.async_copy(rows_v, out_hbm.at[idx_vs[kk]], sem) for kk in range(K)]
            for c in copies:
                c.wait()

    return k(src, dest_flat)


def _sc_gather_rows(table, idx):
    B = idx.shape[0]
    W = table.shape[1]
    per_w = B // SC_WORKERS
    assert per_w * SC_WORKERS == B and per_w % SC_ALIGN == 0
    chunk = _sc_chunk(per_w)

    @functools.partial(
        pl.kernel, mesh=_sc_mesh(), out_type=jax.ShapeDtypeStruct((B, W), table.dtype),
        scratch_types=[pltpu.VMEM((chunk,), jnp.int32), pltpu.VMEM((chunk, W), table.dtype), pltpu.SemaphoreType.DMA],
        name="sc_gather_rows")
    def k(table_hbm, idx_hbm, out_hbm, idx_v, rows_v, sem):
        base = (lax.axis_index("s") * SC_CORES + lax.axis_index("c")) * per_w

        @pl.loop(0, per_w // chunk)
        def _(j):
            off = pl.multiple_of(base + j * chunk, SC_ALIGN)
            pltpu.sync_copy(idx_hbm.at[pl.ds(off, chunk)], idx_v)
            pltpu.async_copy(table_hbm.at[idx_v], rows_v, sem).wait()
            pltpu.sync_copy(rows_v, out_hbm.at[pl.ds(off, chunk)])

    return k(table, idx)


def _in_proj_body(x_ref, w_ref, cos_ref, sin_ref,
                  qa_ref, ka_ref, va_ref, qb_ref, kb_ref, vb_ref, gb_ref, kt_ref, vt_ref, *, seq_tiles, first_win):
    x = x_ref[...].astype(BF16)
    in_window = pl.program_id(0) % seq_tiles >= first_win

    def proj(lo, hi):
        return jnp.dot(x, w_ref[:, lo:hi], preferred_element_type=F32)

    o = 0
    qa_ref[...] = (proj(o, o + WIDTH_A) * (HEAD_DIM_A ** -0.5)).astype(BF16)
    o += WIDTH_A
    ka = proj(o, o + WIDTH_A)
    ka_ref[...] = ka.astype(BF16)

    @pl.when(in_window)
    def _():
        kt_ref[...] = ka.T
    o += WIDTH_A
    va = proj(o, o + WIDTH_A)
    va_ref[...] = va.astype(BF16)

    @pl.when(in_window)
    def _():
        vt_ref[...] = va.T
    o += WIDTH_A

    cos = cos_ref[...]
    sin = sin_ref[...]
    lane = lax.broadcasted_iota(jnp.int32, cos.shape, 1)
    first_half = (lane % KEY_DIM_B) < (KEY_DIM_B // 2)

    def rot(z):
        sw = jnp.where(first_half, pltpu.roll(z, QK_WIDTH_B - KEY_DIM_B // 2, 1), pltpu.roll(z, KEY_DIM_B // 2, 1))
        return z * cos + sw * sin

    qb_ref[...] = rot(proj(o, o + QK_WIDTH_B)).astype(BF16)
    o += QK_WIDTH_B
    kb_ref[...] = (rot(proj(o, o + QK_WIDTH_B)) * (KEY_DIM_B ** -0.5)).astype(BF16)
    o += QK_WIDTH_B
    vb_ref[...] = proj(o, o + WIDTH_B).astype(BF16)
    o += WIDTH_B
    gb_ref[...] = proj(o, o + WIDTH_B).astype(BF16)


def _in_proj(x2d, w_bf, cos_t, sin_t, tm, seq_tiles, win_tiles):
    T = x2d.shape[0]
    nt = T // tm
    n_seq = nt // seq_tiles
    j0 = seq_tiles - win_tiles

    def tok(i):
        return (i, 0)

    def tab(i):
        return (i % seq_tiles, 0)

    def win(i):
        return (i // seq_tiles, 0, jnp.maximum(i % seq_tiles - j0, 0))

    def tspec(w):
        return pl.BlockSpec((tm, w), tok)

    out_shape = (
        jax.ShapeDtypeStruct((T, WIDTH_A), BF16), jax.ShapeDtypeStruct((T, WIDTH_A), BF16),
        jax.ShapeDtypeStruct((T, WIDTH_A), BF16),
        jax.ShapeDtypeStruct((T, QK_WIDTH_B), BF16), jax.ShapeDtypeStruct((T, QK_WIDTH_B), BF16),
        jax.ShapeDtypeStruct((T, WIDTH_B), BF16), jax.ShapeDtypeStruct((T, WIDTH_B), BF16),
        jax.ShapeDtypeStruct((n_seq, WIDTH_A, win_tiles * tm), F32),
        jax.ShapeDtypeStruct((n_seq, WIDTH_A, win_tiles * tm), F32),
    )
    return pl.pallas_call(
        functools.partial(_in_proj_body, seq_tiles=seq_tiles, first_win=j0),
        grid=(nt,),
        in_specs=[tspec(D_MODEL), pl.BlockSpec(w_bf.shape, lambda i: (0, 0)),
                  pl.BlockSpec((tm, QK_WIDTH_B), tab), pl.BlockSpec((tm, QK_WIDTH_B), tab)],
        out_specs=(tspec(WIDTH_A), tspec(WIDTH_A), tspec(WIDTH_A), tspec(QK_WIDTH_B), tspec(QK_WIDTH_B),
                   tspec(WIDTH_B), tspec(WIDTH_B),
                   pl.BlockSpec((None, WIDTH_A, tm), win), pl.BlockSpec((None, WIDTH_A, tm), win)),
        out_shape=out_shape,
        compiler_params=_params(1),
        name="in_proj",
    )(x2d, w_bf, cos_t, sin_t)


def _rotary_tables(pos):
    half = KEY_DIM_B // 2
    inv_freq = 1.0 / (10000.0 ** jnp.linspace(0.0, 1.0, half, dtype=F32))
    ang = pos.astype(F32)[:, None] * inv_freq[None, :]
    cos = jnp.cos(ang)
    sin = jnp.sin(ang)
    cos_h = jnp.concatenate([cos, cos], axis=-1)
    sin_h = jnp.concatenate([-sin, sin], axis=-1)
    return jnp.tile(cos_h, (1, N_HEADS_B)), jnp.tile(sin_h, (1, N_HEADS_B))


ATTN_BLOCKS_PER_STEP = 4


def _attn_body(q_ref, k_ref, v_ref, bias_ref, o_ref, lse_ref, kb_ref, vb_ref):
    n = pl.program_id(2)
    rows = q_ref.shape[0]

    @pl.when(n == 0)
    def _():
        kb_ref[:BLK, :] = jnp.zeros((BLK, WIDTH_A), BF16)
        vb_ref[:BLK, :] = jnp.zeros((BLK, WIDTH_A), BF16)

    kb_ref[BLK:, :] = k_ref[...]
    vb_ref[BLK:, :] = v_ref[...]
    lane = lax.broadcasted_iota(jnp.int32, (BLK, LANES), 1)
    low = lane < HEAD_DIM_A
    nt = (((1,), (1,)), ((), ()))

    def sub_block(j, carry):
        r0 = pl.multiple_of(j * BLK, BLK)
        table = jnp.where((n == 0) & (j == 0), 0, 1)
        lse_tile = jnp.zeros((BLK, LANES), F32)
        for p in range(N_HEADS_A // 2):
            cs = slice(p * LANES, (p + 1) * LANES)
            qp = q_ref[pl.ds(r0, BLK), cs]
            kp = kb_ref[pl.ds(r0, 2 * BLK), cs]
            vp = vb_ref[pl.ds(r0, 2 * BLK), cs]
            zero = jnp.zeros_like(qp)
            outs = []
            for half, qh in enumerate((jnp.where(low, qp, zero), jnp.where(low, zero, qp))):
                h = 2 * p + half
                s = lax.dot_general(qh, kp, nt, preferred_element_type=F32) + bias_ref[table, h]
                m = jnp.max(s, axis=-1, keepdims=True)
                e = jnp.exp(s - m)
                den = jnp.sum(e, axis=-1, keepdims=True)
                outs.append(jnp.dot(e.astype(BF16), vp, preferred_element_type=F32) / den)
                lse_tile = jnp.where(lane == h, m + jnp.log(den), lse_tile)
            o_ref[pl.ds(r0, BLK), cs] = jnp.where(low, outs[0], outs[1]).astype(BF16)
        lse_ref[pl.ds(r0, BLK), :] = lse_tile
        return carry

    lax.fori_loop(0, rows // BLK, sub_block, 0)
    kb_ref[:BLK, :] = k_ref[rows - BLK:, :]
    vb_ref[:BLK, :] = v_ref[rows - BLK:, :]


def _attn_bias_tables(rel_bias, window, dil):
    n_keys = window // dil
    i = np.arange(BLK)[:, None]
    j = np.arange(2 * BLK)[None, :]
    rel = BLK + i - j
    in_band = (rel >= 0) & (rel <= n_keys)
    bias = _bias_by_bucket(rel_bias, _t5_bucket(np.clip(rel, 0, None) * dil))
    later = jnp.where(jnp.asarray(in_band)[None], bias, NEG_INF)
    first = jnp.where(jnp.asarray(in_band & (j >= BLK))[None], bias, NEG_INF)
    return jnp.stack([first, later])


def _dilated_branch(q, k, v, bias_tab, dil):
    B, S, _ = q.shape
    L = S // dil
    rows = min(ATTN_BLOCKS_PER_STEP, L // BLK) * BLK
    assert L % rows == 0

    def view(t):
        return t.reshape(B, L, dil * t.shape[-1])

    def cls(b, r, n):
        return (b, n, r)

    qkv_spec = pl.BlockSpec((None, rows, WIDTH_A), cls)
    o, lse = pl.pallas_call(
        _attn_body,
        grid=(B, dil, L // rows),
        in_specs=[qkv_spec, qkv_spec, qkv_spec,
                  pl.BlockSpec(bias_tab.shape, lambda b, r, n: (0, 0, 0, 0))],
        out_specs=(qkv_spec, pl.BlockSpec((None, rows, LANES), cls)),
        out_shape=(jax.ShapeDtypeStruct((B, L, dil * WIDTH_A), BF16),
                   jax.ShapeDtypeStruct((B, L, dil * LANES), F32)),
        scratch_shapes=[pltpu.VMEM((BLK + rows, WIDTH_A), BF16), pltpu.VMEM((BLK + rows, WIDTH_A), BF16)],
        compiler_params=_params(3),
        name=f"dil_attn_d{dil}",
    )(view(q), view(k), view(v), bias_tab)
    return o.reshape(B, S, WIDTH_A), lse.reshape(B, S, LANES)


def _ret_body(*refs, n_branch):
    qb_ref, kb_ref, vb_ref, gb_ref = refs[:4]
    p = 4
    if n_branch:
        o_refs = refs[p:p + n_branch]
        l_refs = refs[p + n_branch:p + 2 * n_branch]
        exp_ref = refs[p + 2 * n_branch]
        p += 2 * n_branch + 1
    else:
        oa_ref = refs[p]
        p += 1
    st0_ref, dmat_ref, qdec_ref, kdec_ref, cdec_ref, cat_ref, sto_ref, st_ref = refs[p:]

    @pl.when(pl.program_id(1) == 0)
    def _():
        st_ref[...] = st0_ref[...]

    if n_branch:
        ls = [r[...] for r in l_refs]
        mx = functools.reduce(jnp.maximum, ls)
        ws = [jnp.exp(l - mx) for l in ls]
        tot = functools.reduce(lambda a, b: a + b, ws)
        oa = None
        for w, o_ref in zip(ws, o_refs):
            w = w / tot
            w_hi = w.astype(BF16)
            w_lo = (w - w_hi.astype(F32)).astype(BF16)
            w_full = (jnp.dot(w_hi, exp_ref[...], preferred_element_type=F32)
                      + jnp.dot(w_lo, exp_ref[...], preferred_element_type=F32))
            term = w_full * o_ref[...].astype(F32)
            oa = term if oa is None else oa + term
        cat_ref[:, :WIDTH_A] = oa.astype(BF16)
    else:
        cat_ref[:, :WIDTH_A] = oa_ref[...].astype(BF16)

    for h in range(N_HEADS_B):
        ks = slice(h * KEY_DIM_B, (h + 1) * KEY_DIM_B)
        vs = slice(h * VAL_DIM_B, (h + 1) * VAL_DIM_B)
        q = qb_ref[:, ks]
        k = kb_ref[:, ks]
        v = vb_ref[:, vs]
        st = st_ref[h]
        a = lax.dot_general(q, k, (((1,), (1,)), ((), ())), preferred_element_type=F32) * dmat_ref[h]
        o = (jnp.dot(a.astype(BF16), v, preferred_element_type=F32)
             + jnp.dot(q, st.astype(BF16), preferred_element_type=F32) * qdec_ref[h])
        kd = (k.astype(F32) * kdec_ref[h]).astype(BF16)
        st_new = st * cdec_ref[h] + lax.dot_general(kd, v, (((0,), (0,)), ((), ())), preferred_element_type=F32)
        st_ref[h] = st_new
        sto_ref[h] = st_new
        mu = jnp.mean(o, axis=-1, keepdims=True)
        var = jnp.mean(jnp.square(o - mu), axis=-1, keepdims=True)
        obn = (o - mu) * lax.rsqrt(var + GN_EPS)
        g = gb_ref[:, vs].astype(F32)
        gated = g * (1.0 / (1.0 + jnp.exp(-g))) * obn
        cat_ref[:, WIDTH_A + h * VAL_DIM_B:WIDTH_A + (h + 1) * VAL_DIM_B] = gated.astype(BF16)


def _decay_tables(chunk, rows):
    H = N_HEADS_B
    log_g = jnp.log(1.0 - 2.0 ** (-5.0 - jnp.arange(H, dtype=F32)))
    i = jnp.arange(rows, dtype=F32)
    live = np.arange(rows) < chunk
    diff = i[:, None] - i[None, :]
    causal = (diff >= 0) & jnp.asarray(live[:, None] & live[None, :])
    dmat = jnp.where(causal[None], jnp.exp(jnp.where(causal, diff, 0.0)[None] * log_g[:, None, None]), 0.0)
    q_decay = jnp.where(jnp.asarray(live)[None], jnp.exp((i[None, :] + 1.0) * log_g[:, None]), 0.0)
    k_decay = jnp.where(jnp.asarray(live)[None], jnp.exp((chunk - 1.0 - i)[None, :] * log_g[:, None]), 0.0)
    c_decay = jnp.exp(chunk * log_g)
    qdec = jnp.broadcast_to(q_decay[:, :, None], (H, rows, VAL_DIM_B))
    kdec = jnp.broadcast_to(k_decay[:, :, None], (H, rows, KEY_DIM_B))
    cdec = jnp.broadcast_to(c_decay[:, None, None], (H, KEY_DIM_B, VAL_DIM_B))
    return dmat.astype(F32), qdec.astype(F32), kdec.astype(F32), cdec.astype(F32)


def _ret_mix(qb, kb, vb, gb, attn, state0, chunk):
    B, S, _ = qb.shape
    rows = RET_CHUNK
    nc = S // rows
    tables = _decay_tables(chunk, rows)

    def tok(b, c):
        return (b, c, 0)

    def tspec(w):
        return pl.BlockSpec((None, rows, w), tok)

    def const(shape):
        return pl.BlockSpec(shape, lambda b, c: (0,) * len(shape))

    ins = [qb, kb, vb, gb]
    in_specs = [tspec(QK_WIDTH_B), tspec(QK_WIDTH_B), tspec(WIDTH_B), tspec(WIDTH_B)]
    if isinstance(attn, tuple):
        outs_a, lses = attn
        n_branch = len(outs_a)
        expand = np.zeros((LANES, WIDTH_A), np.float32)
        for h in range(N_HEADS_A):
            expand[h, h * HEAD_DIM_A:(h + 1) * HEAD_DIM_A] = 1.0
        ins += list(outs_a) + list(lses) + [jnp.asarray(expand, BF16)]
        in_specs += [tspec(WIDTH_A)] * n_branch + [tspec(LANES)] * n_branch + [const((LANES, WIDTH_A))]
    else:
        n_branch = 0
        ins.append(attn)
        in_specs.append(tspec(WIDTH_A))
    st_shape = (N_HEADS_B, KEY_DIM_B, VAL_DIM_B)
    st_spec = pl.BlockSpec((None,) + st_shape, lambda b, c: (b, 0, 0, 0))
    ins += [state0] + list(tables)
    in_specs += [st_spec] + [const(t.shape) for t in tables]
    return pl.pallas_call(
        functools.partial(_ret_body, n_branch=n_branch),
        grid=(B, nc),
        in_specs=in_specs,
        out_specs=(tspec(WIDTH_A + WIDTH_B), st_spec),
        out_shape=(jax.ShapeDtypeStruct((B, S, WIDTH_A + WIDTH_B), BF16),
                   jax.ShapeDtypeStruct((B,) + st_shape, F32)),
        scratch_shapes=[pltpu.VMEM(st_shape, F32)],
        compiler_params=_params(2),
        name=f"ret_mix_{n_branch}",
    )(*ins)


SAMP_Q_ROWS = 64
SAMP_NEW_LANES = 128


def _samp_attn_body(q_ref, kt_ref, vt_ref, knt_ref, vnt_ref, bc_ref, bn_ref, hm_ref, o_ref, ko_ref, vo_ref, *, ds):
    q = q_ref[...]
    kt = kt_ref[...]
    vt = vt_ref[...]
    knt = knt_ref[...]
    vnt = vnt_ref[...]
    w = kt.shape[1]
    is_new = lax.broadcasted_iota(jnp.int32, knt.shape, 1) >= SAMP_NEW_LANES - ds
    for src, new, dst in ((kt, knt, ko_ref), (vt, vnt, vo_ref)):
        rolled = pltpu.roll(src, w - ds, 1)
        dst[:, :w - SAMP_NEW_LANES] = rolled[:, :w - SAMP_NEW_LANES]
        dst[:, w - SAMP_NEW_LANES:] = jnp.where(is_new, new, rolled[:, w - SAMP_NEW_LANES:])

    s_c = jnp.dot(q, kt.astype(BF16), preferred_element_type=F32)
    s_n = jnp.dot(q, knt.astype(BF16), preferred_element_type=F32)
    es_c, es_n, dens, lses = [], [], [], []
    for n in range(len(DILATED_BRANCHES)):
        sc = s_c + bc_ref[n]
        sn = s_n + bn_ref[n]
        m = jnp.maximum(jnp.max(sc, axis=-1, keepdims=True), jnp.max(sn, axis=-1, keepdims=True))
        ec = jnp.exp(sc - m)
        en = jnp.exp(sn - m)
        den = jnp.sum(ec, axis=-1, keepdims=True) + jnp.sum(en, axis=-1, keepdims=True)
        es_c.append(ec)
        es_n.append(en)
        dens.append(den)
        lses.append(m + jnp.log(den))
    mx = functools.reduce(jnp.maximum, lses)
    ws = [jnp.exp(l - mx) for l in lses]
    tot = functools.reduce(lambda a, b: a + b, ws)
    p_c = None
    p_n = None
    for w, den, ec, en in zip(ws, dens, es_c, es_n):
        coef = w / (tot * den)
        p_c = coef * ec if p_c is None else p_c + coef * ec
        p_n = coef * en if p_n is None else p_n + coef * en
    nt = (((1,), (1,)), ((), ()))
    o = (lax.dot_general(p_c.astype(BF16), vt.astype(BF16), nt, preferred_element_type=F32)
         + lax.dot_general(p_n.astype(BF16), vnt.astype(BF16), nt, preferred_element_type=F32))
    o = o * hm_ref[...]
    o_ref[...] = jnp.sum(o.reshape(SUBLANES, N_HEADS_A, WIDTH_A), axis=1)


def _samp_bias_tables(rel_bias, w_buf, ds):
    tabs_c, tabs_n = [], []
    s = np.arange(SUBLANES)[:, None]
    live_s = s < ds
    first_new = SAMP_NEW_LANES - ds
    for window, dil in DILATED_BRANCHES:
        n_keys = window // dil
        for keys, live_k, tabs in ((np.arange(w_buf)[None, :], True, tabs_c),
                                   (w_buf - first_new + np.arange(SAMP_NEW_LANES)[None, :],
                                    np.arange(SAMP_NEW_LANES)[None, :] >= first_new, tabs_n)):
            n = keys.shape[1]
            dist = w_buf + s - keys
            valid = (dist >= 0) & (dist % dil == 0) & (dist // dil <= n_keys) & live_k
            bias = _bias_by_bucket(rel_bias, _t5_bucket(np.clip(dist, 0, None))).transpose(1, 0, 2)
            tab = jnp.where(jnp.asarray(valid)[:, None, :], bias, NEG_INF)
            pad = jnp.where(jnp.asarray(np.broadcast_to(live_k, dist.shape))[:, None, :], 0.0, NEG_INF)
            tab = jnp.where(jnp.asarray(live_s)[:, :, None], tab, pad)
            tabs.append(tab.reshape(SAMP_Q_ROWS, n))
    return jnp.stack(tabs_c), jnp.stack(tabs_n)


def _samp_attn(qa, cache_kt, cache_vt, knt, vnt, rel_bias, ds):
    DB, DS, _ = qa.shape
    W = cache_kt.shape[2]
    head_of_lane = np.arange(WIDTH_A) // HEAD_DIM_A
    hmask = (np.arange(SAMP_Q_ROWS)[:, None] % N_HEADS_A == head_of_lane[None, :])
    q8 = jnp.pad(qa, ((0, 0), (0, SUBLANES - DS), (0, 0)))
    q_rows = jnp.where(jnp.asarray(hmask)[None], jnp.repeat(q8, N_HEADS_A, axis=1), jnp.zeros((), BF16))
    bias_c, bias_n = _samp_bias_tables(rel_bias, W, DS)

    def per_b(rows, w):
        return pl.BlockSpec((None, rows, w), lambda b: (b, 0, 0))

    def const(a):
        return pl.BlockSpec(a.shape, lambda b: (0,) * a.ndim)

    hm = jnp.asarray(hmask, F32)
    return pl.pallas_call(
        functools.partial(_samp_attn_body, ds=ds),
        grid=(DB,),
        in_specs=[per_b(SAMP_Q_ROWS, WIDTH_A), per_b(WIDTH_A, W), per_b(WIDTH_A, W),
                  per_b(WIDTH_A, SAMP_NEW_LANES), per_b(WIDTH_A, SAMP_NEW_LANES),
                  const(bias_c), const(bias_n), const(hm)],
        out_specs=(per_b(SUBLANES, WIDTH_A), per_b(WIDTH_A, W), per_b(WIDTH_A, W)),
        out_shape=(jax.ShapeDtypeStruct((DB, SUBLANES, WIDTH_A), F32),
                   jax.ShapeDtypeStruct((DB, WIDTH_A, W), F32), jax.ShapeDtypeStruct((DB, WIDTH_A, W), F32)),
        compiler_params=_params(1),
        name="samp_attn",
    )(q_rows, cache_kt, cache_vt, knt, vnt, bias_c, bias_n, hm)


def _route_body(cat_ref, x_ref, wout_ref, g_ref, b_ref, rwh_ref, rwl_ref, rb_ref, tril_ref, base_ref,
                h_ref, hp_ref, route_ref, cnt_ref, *, alpha):
    @pl.when(pl.program_id(0) == 0)
    def _():
        cnt_ref[...] = base_ref[...]

    mix = jnp.dot(cat_ref[...], wout_ref[...], preferred_element_type=F32)
    y = alpha * x_ref[...] + mix
    mu = jnp.mean(y, axis=-1, keepdims=True)
    var = jnp.mean(jnp.square(y - mu), axis=-1, keepdims=True)
    h = (y - mu) * lax.rsqrt(var + LN_EPS) * g_ref[...] + b_ref[...]
    h_ref[...] = h
    hb = h.astype(BF16)
    hp_ref[...] = _pack_bf16_pairs(h)
    hl = (h - hb.astype(F32)).astype(BF16)
    logits = (jnp.dot(hb, rwh_ref[...], preferred_element_type=F32)
              + jnp.dot(hb, rwl_ref[...], preferred_element_type=F32)
              + jnp.dot(hl, rwh_ref[...], preferred_element_type=F32)) + rb_ref[...]

    tm = logits.shape[0]
    lane = lax.broadcasted_iota(jnp.int32, (tm, LANES), 1)
    work = logits
    vals, idxs = [], []
    for _ in range(TOP_K):
        m = jnp.max(work, axis=-1, keepdims=True)
        idx = jnp.min(jnp.where(work == m, lane, LANES), axis=-1, keepdims=True)
        vals.append(m)
        idxs.append(idx)
        work = jnp.where(lane == idx, -jnp.inf, work)
    es = [jnp.exp(v - vals[0]) for v in vals]
    tot = functools.reduce(lambda a, b: a + b, es)
    onehot = jnp.zeros((tm, LANES), F32)
    for idx in idxs:
        onehot = onehot + (lane == idx).astype(F32)
    before = jnp.dot(tril_ref[...], onehot.astype(BF16), preferred_element_type=F32) + cnt_ref[0:1, :]
    route = jnp.zeros((tm, LANES), F32)
    for k in range(TOP_K):
        rank = jnp.sum(jnp.where(lane == idxs[k], before, 0.0), axis=-1, keepdims=True)
        route = jnp.where(lane == k, idxs[k].astype(F32), route)
        route = jnp.where(lane == TOP_K + k, es[k] / tot, route)
        route = jnp.where(lane == 2 * TOP_K + k, rank, route)
    route_ref[...] = route
    cnt_ref[...] = cnt_ref[...] + jnp.sum(onehot, axis=0, keepdims=True)


def _out_route(cat, x2d, w_out_bf, ln_g, ln_b, rw_hi, rw_lo, rb, base, tm, alpha):
    T = x2d.shape[0]
    tril = jnp.asarray(np.tril(np.ones((tm, tm), np.float32), -1), BF16)

    def tok(i):
        return (i, 0)

    def const(a):
        return pl.BlockSpec(a.shape, lambda i: (0,) * a.ndim)

    ins = (cat, x2d, w_out_bf, ln_g, ln_b, rw_hi, rw_lo, rb, tril, base)
    in_specs = [pl.BlockSpec((tm, cat.shape[1]), tok), pl.BlockSpec((tm, D_MODEL), tok)] + [const(a) for a in ins[2:]]
    return pl.pallas_call(
        functools.partial(_route_body, alpha=alpha),
        grid=(T // tm,),
        in_specs=in_specs,
        out_specs=(pl.BlockSpec((tm, D_MODEL), tok), pl.BlockSpec((tm, D_MODEL // 2), tok),
                   pl.BlockSpec((tm, LANES), tok), pl.BlockSpec((SUBLANES, LANES), lambda i: (0, 0))),
        out_shape=(jax.ShapeDtypeStruct((T, D_MODEL), F32), jax.ShapeDtypeStruct((T, D_MODEL // 2), jnp.int32),
                   jax.ShapeDtypeStruct((T, LANES), F32), jax.ShapeDtypeStruct((SUBLANES, LANES), F32)),
        compiler_params=_params(1),
        name="out_route",
    )(*ins)


MOE_CAST_ROWS = 128


def _moe_body(be_ref, nused_ref, x_ref, wgu_ref, bgu_ref, wdn_ref, bdn_ref, y_ref, wgu_bf, wdn_bf):
    b = pl.program_id(0)
    e = be_ref[b]
    prev = be_ref[jnp.maximum(b - 1, 0)]
    d_exp = wdn_ref.shape[0]

    @pl.when((b == 0) | (e != prev))
    def _():
        def cast_gu(i, c):
            r = pl.ds(pl.multiple_of(i * MOE_CAST_ROWS, MOE_CAST_ROWS), MOE_CAST_ROWS)
            wgu_bf[r, :] = wgu_ref[r, :].astype(BF16)
            return c

        def cast_dn(i, c):
            r = pl.ds(pl.multiple_of(i * MOE_CAST_ROWS, MOE_CAST_ROWS), MOE_CAST_ROWS)
            wdn_bf[r, :] = wdn_ref[r, :].astype(BF16)
            return c

        lax.fori_loop(0, wgu_ref.shape[0] // MOE_CAST_ROWS, cast_gu, 0)
        lax.fori_loop(0, d_exp // MOE_CAST_ROWS, cast_dn, 0)

    @pl.when(b < nused_ref[0])
    def _():
        x_lo, x_hi = _unpack_bf16_pairs(x_ref[...])
        x_lo = x_lo.astype(BF16)
        x_hi = x_hi.astype(BF16)
        dh = x_lo.shape[1]

        def xw(cols):
            return (jnp.dot(x_lo, wgu_bf[:dh, cols], preferred_element_type=F32)
                    + jnp.dot(x_hi, wgu_bf[dh:, cols], preferred_element_type=F32) + bgu_ref[:, cols])

        half = d_exp // 2
        y = None
        for c in range(2):
            lo = c * half
            gate = jnp.minimum(xw(slice(lo, lo + half)), SWIGLU_LIMIT)
            up = jnp.clip(xw(slice(d_exp + lo, d_exp + lo + half)), -SWIGLU_LIMIT, SWIGLU_LIMIT)
            act = (up + 1.0) * gate * (1.0 / (1.0 + jnp.exp(-SWIGLU_ALPHA * gate)))
            part = jnp.dot(act.astype(BF16), wdn_bf[lo:lo + half, :], preferred_element_type=F32)
            y = part if y is None else y + part
        y_ref[...] = _pack_bf16_pairs(y + bdn_ref[...])

    @pl.when(b >= nused_ref[0])
    def _():
        y_ref[...] = jnp.zeros_like(y_ref)


def _moe_ffn(xs, rows, block_e, n_used, w_gu, b_gu, w_dn, b_dn):
    E, D, two_de = w_gu.shape
    d_exp = two_de // 2
    nb = rows // MOE_BLOCK
    grid_spec = pltpu.PrefetchScalarGridSpec(
        num_scalar_prefetch=2,
        grid=(nb,),
        in_specs=[
            pl.BlockSpec((MOE_BLOCK, D // 2), lambda b, be, nu: (b, 0)),
            pl.BlockSpec((None, D, two_de), lambda b, be, nu: (be[b], 0, 0)),
            pl.BlockSpec((None, 1, two_de), lambda b, be, nu: (be[b], 0, 0)),
            pl.BlockSpec((None, d_exp, D), lambda b, be, nu: (be[b], 0, 0)),
            pl.BlockSpec((None, 1, D), lambda b, be, nu: (be[b], 0, 0)),
        ],
        out_specs=pl.BlockSpec((MOE_BLOCK, D // 2), lambda b, be, nu: (b, 0)),
        scratch_shapes=[pltpu.VMEM((D, two_de), BF16), pltpu.VMEM((d_exp, D), BF16)],
    )
    return pl.pallas_call(
        _moe_body,
        grid_spec=grid_spec,
        out_shape=jax.ShapeDtypeStruct((rows, D // 2), jnp.int32),
        compiler_params=_params(1),
        name="moe_ffn",
    )(block_e, n_used, xs, w_gu, b_gu.reshape(E, 1, two_de), w_dn, b_dn.reshape(E, 1, D))


def _ple_body(h_ref, ys_ref, route_ref, p_ref, g_ref, b_ref, wpg_ref, bpg_ref, wpp_ref, o_ref, *, alpha):
    route = route_ref[...]
    f_lo = None
    f_hi = None
    for k in range(TOP_K):
        lo, hi = _unpack_bf16_pairs(ys_ref[k])
        g = route[:, TOP_K + k:TOP_K + k + 1]
        f_lo = g * lo if f_lo is None else f_lo + g * lo
        f_hi = g * hi if f_hi is None else f_hi + g * hi
    y = alpha * h_ref[...] + jnp.concatenate([f_lo, f_hi], axis=1)
    mu = jnp.mean(y, axis=-1, keepdims=True)
    var = jnp.mean(jnp.square(y - mu), axis=-1, keepdims=True)
    h2 = (y - mu) * lax.rsqrt(var + LN_EPS) * g_ref[...] + b_ref[...]
    z = jnp.dot(h2.astype(BF16), wpg_ref[...], preferred_element_type=F32) + bpg_ref[...]
    gate = 1.0 / (1.0 + jnp.exp(-z))
    proj = jnp.dot(p_ref[...].astype(BF16), wpp_ref[...], preferred_element_type=F32)
    o_ref[...] = h2 + gate * proj


def _ffn_ple(h, y_slots, route, p, ln_g, ln_b, w_pg_bf, b_pg, w_pp_bf, tm, tile0, alpha):
    T = h.shape[0]

    def tok(i):
        return (i, 0)

    def const(a):
        return pl.BlockSpec(a.shape, lambda i: (0,) * a.ndim)

    consts = (ln_g, ln_b, w_pg_bf, b_pg, w_pp_bf)
    return pl.pallas_call(
        functools.partial(_ple_body, alpha=alpha),
        grid=(T // tm,),
        in_specs=[pl.BlockSpec((tm, D_MODEL), tok),
                  pl.BlockSpec((TOP_K, tm, D_MODEL // 2), lambda i: (0, tile0 + i, 0)),
                  pl.BlockSpec((tm, LANES), tok),
                  pl.BlockSpec((tm, p.shape[1]), tok)] + [const(a) for a in consts],
        out_specs=pl.BlockSpec((tm, D_MODEL), tok),
        out_shape=jax.ShapeDtypeStruct((T, D_MODEL), F32),
        compiler_params=_params(1),
        name="ffn_ple",
    )(h, y_slots, route, p, *consts)


def _row(v):
    return v.reshape(1, -1).astype(F32)


def kernel(x_prompt, x_sample, cache_win_k, cache_win_v, state_ret, p_prompt, p_sample, rel_bias, w_in, w_out,
           ln1_g, ln1_b, router_w, router_b, w_gate_up, b_gate_up, w_down, b_down, ln2_g, ln2_b,
           w_ple_gate, b_ple_gate, w_ple_proj):
    B, S, D = x_prompt.shape
    DB, DS, _ = x_sample.shape
    depth = w_in.shape[0]
    w_buf = cache_win_k.shape[2]
    n_exp = router_w.shape[-1]
    alpha = (2.0 * depth) ** 0.25
    assert depth == 1 and D == D_MODEL
    assert S % (BLK * MAX_DIL) == 0 and S >= WINDOW_MAX and w_buf == WINDOW_MAX and DS <= SUBLANES
    tm_p = 512
    Tp, Ts = B * S, DB * DS
    assert Tp % tm_p == 0 and Ts % SUBLANES == 0

    i = 0
    w_in_bf = w_in[i].astype(BF16)
    w_out_bf = w_out[i].astype(BF16)
    w_pg_bf = w_ple_gate[i].astype(BF16)
    w_pp_bf = w_ple_proj[i].astype(BF16)
    rw = jnp.pad(router_w[i], ((0, 0), (0, LANES - n_exp)))
    rw_hi = rw.astype(BF16)
    rw_lo = (rw - rw_hi.astype(F32)).astype(BF16)
    rb = jnp.pad(router_b[i], (0, LANES - n_exp), constant_values=NEG_INF).reshape(1, LANES)

    cos_p, sin_p = _rotary_tables(jnp.arange(S, dtype=jnp.int32))
    qa, ka, va, qb, kb, vb, gb, kf, vf = _in_proj(x_prompt.reshape(Tp, D), w_in_bf, cos_p, sin_p, tm_p,
                                                   S // tm_p, WINDOW_MAX // tm_p)

    def seq(t):
        return t.reshape(B, S, t.shape[-1])

    outs_a, lses = [], []
    for window, dil in DILATED_BRANCHES:
        o_n, l_n = _dilated_branch(seq(qa), seq(ka), seq(va), _attn_bias_tables(rel_bias, window, dil), dil)
        outs_a.append(o_n)
        lses.append(l_n)
    st_zero = jnp.zeros((B, N_HEADS_B, KEY_DIM_B, VAL_DIM_B), F32)
    cat_p, rst_p = _ret_mix(seq(qb), seq(kb), seq(vb), seq(gb), (outs_a, lses), st_zero, RET_CHUNK)
    base0 = jnp.zeros((SUBLANES, LANES), F32)
    h_p, hp_p, route_p, cnt_p = _out_route(cat_p.reshape(Tp, -1), x_prompt.reshape(Tp, D), w_out_bf,
                                           _row(ln1_g[i]), _row(ln1_b[i]), rw_hi, rw_lo, rb, base0, tm_p, alpha)

    pos_s = jnp.tile(PAST_LEN + jnp.arange(DS, dtype=jnp.int32), DB)
    cos_s, sin_s = _rotary_tables(pos_s)
    qa_s, _, _, qb_s, kb_s, vb_s, gb_s, kf_s, vf_s = _in_proj(x_sample.reshape(Ts, D), w_in_bf, cos_s, sin_s,
                                                             Ts, 1, 1)

    def positions_minor(t):
        return jnp.transpose(t, (0, 2, 3, 1)).reshape(DB, WIDTH_A, t.shape[1])

    def positions_major(t):
        return jnp.transpose(t.reshape(t.shape[0], N_HEADS_A, HEAD_DIM_A, t.shape[2]), (0, 3, 1, 2))[None]

    def new_columns(t):
        t = jnp.transpose(t.reshape(WIDTH_A, DB, DS), (1, 0, 2))
        return jnp.pad(t, ((0, 0), (0, 0), (SAMP_NEW_LANES - DS, 0)))

    oa_s, kt_out, vt_out = _samp_attn(qa_s.reshape(DB, DS, WIDTH_A), positions_minor(cache_win_k[i]),
                                      positions_minor(cache_win_v[i]), new_columns(kf_s), new_columns(vf_s),
                                      rel_bias, DS)

    def pad_rows(t, rows):
        t = t.reshape(DB, -1, t.shape[-1])
        return jnp.pad(t, ((0, 0), (0, rows - t.shape[1]), (0, 0)))

    cat_s, rst_s = _ret_mix(pad_rows(qb_s, RET_CHUNK), pad_rows(kb_s, RET_CHUNK), pad_rows(vb_s, RET_CHUNK),
                            pad_rows(gb_s, RET_CHUNK), pad_rows(oa_s, RET_CHUNK), state_ret[i].astype(F32), DS)
    cat_s = cat_s[:, :DS].reshape(Ts, -1)
    h_s, hp_s, route_s, cnt = _out_route(cat_s, x_sample.reshape(Ts, D), w_out_bf, _row(ln1_g[i]), _row(ln1_b[i]),
                                         rw_hi, rw_lo, rb, cnt_p, Ts, alpha)

    T = Tp + Ts
    t_align = SC_WORKERS * SC_ALIGN
    T_pad = -(-T // t_align) * t_align
    route = jnp.concatenate([route_p, route_s], axis=0)
    top_idx = route[:, :TOP_K].astype(jnp.int32)
    rank = route[:, 2 * TOP_K:3 * TOP_K].astype(jnp.int32)
    counts = cnt[0, :n_exp].astype(jnp.int32)
    padded = (counts + MOE_BLOCK - 1) // MOE_BLOCK * MOE_BLOCK
    pad_end = jnp.cumsum(padded)
    pad_start = pad_end - padded
    dest = pad_start[top_idx] + rank
    n_blocks = -(-T * TOP_K // MOE_BLOCK) + n_exp
    rows = n_blocks * MOE_BLOCK
    block_start = jnp.arange(n_blocks, dtype=jnp.int32) * MOE_BLOCK
    block_e = jnp.minimum(jnp.sum(pad_end[None, :] <= block_start[:, None], axis=1), n_exp - 1).astype(jnp.int32)
    n_used = (pad_end[-1:] // MOE_BLOCK).astype(jnp.int32)
    n_fill = T_pad - T
    spare = rows + jnp.arange(n_fill * TOP_K, dtype=jnp.int32).reshape(n_fill, TOP_K)
    dest_sc = jnp.concatenate([dest, spare], axis=0).T.reshape(-1)
    dest_ga = jnp.concatenate([dest, jnp.zeros((n_fill, TOP_K), jnp.int32)], axis=0).T.reshape(-1)
    hp_all = jnp.concatenate([hp_p, hp_s, jnp.zeros((n_fill, D // 2), jnp.int32)], axis=0)
    xs = _sc_scatter_rows(hp_all, dest_sc, rows + n_fill * TOP_K)
    ys = _moe_ffn(xs, rows, block_e, n_used, w_gate_up[i], b_gate_up[i], w_down[i], b_down[i])
    y_slots = _sc_gather_rows(ys, dest_ga).reshape(TOP_K, T_pad, D // 2)

    ple_args = (_row(ln2_g[i]), _row(ln2_b[i]), w_pg_bf, _row(b_ple_gate[i]), w_pp_bf)
    assert Tp % Ts == 0
    y_p = _ffn_ple(h_p, y_slots, route_p, p_prompt[i].reshape(Tp, D_PLE), *ple_args, tm_p, 0, alpha)
    y_s = _ffn_ple(h_s, y_slots, route_s, p_sample[i].reshape(Ts, D_PLE), *ple_args, Ts, Tp // Ts, alpha)

    return (y_p.reshape(B, S, D), y_s.reshape(DB, DS, D), positions_major(kf), positions_major(vf),
            rst_p[None], positions_major(kt_out), positions_major(vt_out), rst_s[None])
```

```python
import functools

import numpy as np
import jax
import jax.numpy as jnp
from jax import lax
from jax.experimental import pallas as pl
from jax.experimental.pallas import tpu as pltpu
from jax.experimental.pallas import tpu_sc as plsc

F32 = jnp.float32
BF16 = jnp.bfloat16

D_MODEL = 1024
D_PLE = 256
N_HEADS_A = 8
HEAD_DIM_A = 64
WIDTH_A = N_HEADS_A * HEAD_DIM_A
DILATED_BRANCHES = ((128, 1), (512, 4), (2048, 16))
BLK = 128
WINDOW_MAX = 2048
MAX_DIL = 16
NUM_BUCKETS = 32
MAX_DISTANCE = 2048
N_HEADS_B = 4
KEY_DIM_B = 64
VAL_DIM_B = 128
QK_WIDTH_B = N_HEADS_B * KEY_DIM_B
WIDTH_B = N_HEADS_B * VAL_DIM_B
RET_CHUNK = 128
GN_EPS = 1e-6
TOP_K = 4
SWIGLU_LIMIT = 7.0
SWIGLU_ALPHA = 1.702
LN_EPS = 1e-5
NEG_INF = -1e30
PAST_LEN = 16384
MOE_BLOCK = 256
LANES = 128
SUBLANES = 8
VMEM_LIMIT = 52 * 1024 * 1024


def _params(n_axes, vmem=VMEM_LIMIT):
    return pltpu.CompilerParams(dimension_semantics=("arbitrary",) * n_axes, vmem_limit_bytes=vmem)


def _t5_bucket(dist):
    dist = np.asarray(dist, dtype=np.int32)
    max_exact = NUM_BUCKETS // 2
    d = np.maximum(dist, 1).astype(np.float32)
    large = max_exact + (np.log(d / max_exact) / np.log(MAX_DISTANCE / max_exact)
                         * (NUM_BUCKETS - max_exact)).astype(np.int32)
    large = np.minimum(large, NUM_BUCKETS - 1)
    return np.where(dist < max_exact, dist, large).astype(np.int32)


def _bias_by_bucket(rel_bias, buckets):
    b = jnp.asarray(buckets, jnp.int32)
    ids = jnp.arange(NUM_BUCKETS, dtype=jnp.int32).reshape((NUM_BUCKETS, 1) + (1,) * b.ndim)
    vals = rel_bias.astype(F32).reshape((NUM_BUCKETS, rel_bias.shape[1]) + (1,) * b.ndim)
    return jnp.sum(jnp.where(b[None, None] == ids, vals, 0.0), axis=0)


def _pack_bf16_pairs(v):
    w = v.shape[1] // 2
    lo = lax.bitcast_convert_type(v[:, :w].astype(BF16).astype(F32), jnp.uint32) >> 16
    hi = lax.bitcast_convert_type(v[:, w:].astype(BF16).astype(F32), jnp.uint32) & jnp.uint32(0xFFFF0000)
    return lax.bitcast_convert_type(lo | hi, jnp.int32)


def _unpack_bf16_pairs(p):
    u = lax.bitcast_convert_type(p, jnp.uint32)
    lo = lax.bitcast_convert_type(u << 16, F32)
    hi = lax.bitcast_convert_type(u & jnp.uint32(0xFFFF0000), F32)
    return lo, hi


SC_CORES = 2
SC_SUBCORES = 16
SC_WORKERS = SC_CORES * SC_SUBCORES
SC_ALIGN = 8
SC_CHUNK_ROWS = 80


def _sc_mesh():
    return plsc.VectorSubcoreMesh(core_axis_name="c", subcore_axis_name="s")


def _sc_chunk(per_worker):
    c = max(d for d in range(SC_ALIGN, SC_CHUNK_ROWS + 1, SC_ALIGN) if per_worker % d == 0)
    return c


def _sc_scatter_rows(src, dest_flat, n_out):
    T, W = src.shape
    K = dest_flat.shape[0] // T
    per_w = T // SC_WORKERS
    assert per_w * SC_WORKERS == T and per_w % SC_ALIGN == 0
    chunk = _sc_chunk(per_w)

    @functools.partial(
        pl.kernel, mesh=_sc_mesh(), out_type=jax.ShapeDtypeStruct((n_out, W), src.dtype),
        scratch_types=[pltpu.VMEM((chunk, W), src.dtype)] + [pltpu.VMEM((chunk,), jnp.int32)] * K
        + [pltpu.SemaphoreType.DMA],
        name="sc_scatter_rows")
    def k(src_hbm, dest_hbm, out_hbm, rows_v, *rest):
        idx_vs, sem = rest[:K], rest[K]
        base = (lax.axis_index("s") * SC_CORES + lax.axis_index("c")) * per_w

        @pl.loop(0, per_w // chunk)
        def _(j):
            off = pl.multiple_of(base + j * chunk, SC_ALIGN)
            pltpu.sync_copy(src_hbm.at[pl.ds(off, chunk)], rows_v)
            for kk in range(K):
                pltpu.sync_copy(dest_hbm.at[pl.ds(kk * T + off, chunk)], idx_vs[kk])
            copies = [pltpu.async_copy(rows_v, out_hbm.at[idx_vs[kk]], sem) for kk in range(K)]
            for c in copies:
                c.wait()

    return k(src, dest_flat)


def _sc_gather_rows(table, idx):
    B = idx.shape[0]
    W = table.shape[1]
    per_w = B // SC_WORKERS
    assert per_w * SC_WORKERS == B and per_w % SC_ALIGN == 0
    chunk = _sc_chunk(per_w)

    @functools.partial(
        pl.kernel, mesh=_sc_mesh(), out_type=jax.ShapeDtypeStruct((B, W), table.dtype),
        scratch_types=[pltpu.VMEM((chunk,), jnp.int32), pltpu.VMEM((chunk, W), table.dtype), pltpu.SemaphoreType.DMA],
        name="sc_gather_rows")
    def k(table_hbm, idx_hbm, out_hbm, idx_v, rows_v, sem):
        base = (lax.axis_index("s") * SC_CORES + lax.axis_index("c")) * per_w

        @pl.loop(0, per_w // chunk)
        def _(j):
            off = pl.multiple_of(base + j * chunk, SC_ALIGN)
            pltpu.sync_copy(idx_hbm.at[pl.ds(off, chunk)], idx_v)
            pltpu.async_copy(table_hbm.at[idx_v], rows_v, sem).wait()
            pltpu.sync_copy(rows_v, out_hbm.at[pl.ds(off, chunk)])

    return k(table, idx)


def _in_proj_body(x_ref, w_ref, cos_ref, sin_ref,
                  qa_ref, ka_ref, va_ref, qb_ref, kb_ref, vb_ref, gb_ref, kt_ref, vt_ref, *, seq_tiles, first_win):
    x = x_ref[...].astype(BF16)
    in_window = pl.program_id(0) % seq_tiles >= first_win

    def proj(lo, hi):
        return jnp.dot(x, w_ref[:, lo:hi], preferred_element_type=F32)

    o = 0
    qa_ref[...] = (proj(o, o + WIDTH_A) * (HEAD_DIM_A ** -0.5)).astype(BF16)
    o += WIDTH_A
    ka = proj(o, o + WIDTH_A)
    ka_ref[...] = ka.astype(BF16)

    @pl.when(in_window)
    def _():
        kt_ref[...] = ka.T
    o += WIDTH_A
    va = proj(o, o + WIDTH_A)
    va_ref[...] = va.astype(BF16)

    @pl.when(in_window)
    def _():
        vt_ref[...] = va.T
    o += WIDTH_A

    cos = cos_ref[...]
    sin = sin_ref[...]
    lane = lax.broadcasted_iota(jnp.int32, cos.shape, 1)
    first_half = (lane % KEY_DIM_B) < (KEY_DIM_B // 2)

    def rot(z):
        sw = jnp.where(first_half, pltpu.roll(z, QK_WIDTH_B - KEY_DIM_B // 2, 1), pltpu.roll(z, KEY_DIM_B // 2, 1))
        return z * cos + sw * sin

    qb_ref[...] = rot(proj(o, o + QK_WIDTH_B)).astype(BF16)
    o += QK_WIDTH_B
    kb_ref[...] = (rot(proj(o, o + QK_WIDTH_B)) * (KEY_DIM_B ** -0.5)).astype(BF16)
    o += QK_WIDTH_B
    vb_ref[...] = proj(o, o + WIDTH_B).astype(BF16)
    o += WIDTH_B
    gb_ref[...] = proj(o, o + WIDTH_B).astype(BF16)


def _in_proj(x2d, w_bf, cos_t, sin_t, tm, seq_tiles, win_tiles):
    T = x2d.shape[0]
    nt = T // tm
    n_seq = nt // seq_tiles
    j0 = seq_tiles - win_tiles

    def tok(i):
        return (i, 0)

    def tab(i):
        return (i % seq_tiles, 0)

    def win(i):
        return (i // seq_tiles, 0, jnp.maximum(i % seq_tiles - j0, 0))

    def tspec(w):
        return pl.BlockSpec((tm, w), tok)

    out_shape = (
        jax.ShapeDtypeStruct((T, WIDTH_A), BF16), jax.ShapeDtypeStruct((T, WIDTH_A), BF16),
        jax.ShapeDtypeStruct((T, WIDTH_A), BF16),
        jax.ShapeDtypeStruct((T, QK_WIDTH_B), BF16), jax.ShapeDtypeStruct((T, QK_WIDTH_B), BF16),
        jax.ShapeDtypeStruct((T, WIDTH_B), BF16), jax.ShapeDtypeStruct((T, WIDTH_B), BF16),
        jax.ShapeDtypeStruct((n_seq, WIDTH_A, win_tiles * tm), F32),
        jax.ShapeDtypeStruct((n_seq, WIDTH_A, win_tiles * tm), F32),
    )
    return pl.pallas_call(
        functools.partial(_in_proj_body, seq_tiles=seq_tiles, first_win=j0),
        grid=(nt,),
        in_specs=[tspec(D_MODEL), pl.BlockSpec(w_bf.shape, lambda i: (0, 0)),
                  pl.BlockSpec((tm, QK_WIDTH_B), tab), pl.BlockSpec((tm, QK_WIDTH_B), tab)],
        out_specs=(tspec(WIDTH_A), tspec(WIDTH_A), tspec(WIDTH_A), tspec(QK_WIDTH_B), tspec(QK_WIDTH_B),
                   tspec(WIDTH_B), tspec(WIDTH_B),
                   pl.BlockSpec((None, WIDTH_A, tm), win), pl.BlockSpec((None, WIDTH_A, tm), win)),
        out_shape=out_shape,
        compiler_params=_params(1),
        name="in_proj",
    )(x2d, w_bf, cos_t, sin_t)


def _rotary_tables(pos):
    half = KEY_DIM_B // 2
    inv_freq = 1.0 / (10000.0 ** jnp.linspace(0.0, 1.0, half, dtype=F32))
    ang = pos.astype(F32)[:, None] * inv_freq[None, :]
    cos = jnp.cos(ang)
    sin = jnp.sin(ang)
    cos_h = jnp.concatenate([cos, cos], axis=-1)
    sin_h = jnp.concatenate([-sin, sin], axis=-1)
    return jnp.tile(cos_h, (1, N_HEADS_B)), jnp.tile(sin_h, (1, N_HEADS_B))


ATTN_BLOCKS_PER_STEP = 4


def _attn_body(q_ref, k_ref, v_ref, bias_ref, o_ref, lse_ref, kb_ref, vb_ref):
    n = pl.program_id(2)
    rows = q_ref.shape[0]

    @pl.when(n == 0)
    def _():
        kb_ref[:BLK, :] = jnp.zeros((BLK, WIDTH_A), BF16)
        vb_ref[:BLK, :] = jnp.zeros((BLK, WIDTH_A), BF16)

    kb_ref[BLK:, :] = k_ref[...]
    vb_ref[BLK:, :] = v_ref[...]
    lane = lax.broadcasted_iota(jnp.int32, (BLK, LANES), 1)
    low = lane < HEAD_DIM_A
    nt = (((1,), (1,)), ((), ()))

    def sub_block(j, carry):
        r0 = pl.multiple_of(j * BLK, BLK)
        table = jnp.where((n == 0) & (j == 0), 0, 1)
        lse_tile = jnp.zeros((BLK, LANES), F32)
        for p in range(N_HEADS_A // 2):
            cs = slice(p * LANES, (p + 1) * LANES)
            qp = q_ref[pl.ds(r0, BLK), cs]
            kp = kb_ref[pl.ds(r0, 2 * BLK), cs]
            vp = vb_ref[pl.ds(r0, 2 * BLK), cs]
            zero = jnp.zeros_like(qp)
            outs = []
            for half, qh in enumerate((jnp.where(low, qp, zero), jnp.where(low, zero, qp))):
                h = 2 * p + half
                s = lax.dot_general(qh, kp, nt, preferred_element_type=F32) + bias_ref[table, h]
                m = jnp.max(s, axis=-1, keepdims=True)
                e = jnp.exp(s - m)
                den = jnp.sum(e, axis=-1, keepdims=True)
                outs.append(jnp.dot(e.astype(BF16), vp, preferred_element_type=F32) / den)
                lse_tile = jnp.where(lane == h, m + jnp.log(den), lse_tile)
            o_ref[pl.ds(r0, BLK), cs] = jnp.where(low, outs[0], outs[1]).astype(BF16)
        lse_ref[pl.ds(r0, BLK), :] = lse_tile
        return carry

    lax.fori_loop(0, rows // BLK, sub_block, 0)
    kb_ref[:BLK, :] = k_ref[rows - BLK:, :]
    vb_ref[:BLK, :] = v_ref[rows - BLK:, :]


def _attn_bias_tables(rel_bias, window, dil):
    n_keys = window // dil
    i = np.arange(BLK)[:, None]
    j = np.arange(2 * BLK)[None, :]
    rel = BLK + i - j
    in_band = (rel >= 0) & (rel <= n_keys)
    bias = _bias_by_bucket(rel_bias, _t5_bucket(np.clip(rel, 0, None) * dil))
    later = jnp.where(jnp.asarray(in_band)[None], bias, NEG_INF)
    first = jnp.where(jnp.asarray(in_band & (j >= BLK))[None], bias, NEG_INF)
    return jnp.stack([first, later])


def _dilated_branch(q, k, v, bias_tab, dil):
    B, S, _ = q.shape
    L = S // dil
    rows = min(ATTN_BLOCKS_PER_STEP, L // BLK) * BLK
    assert L % rows == 0

    def view(t):
        return t.reshape(B, L, dil * t.shape[-1])

    def cls(b, r, n):
        return (b, n, r)

    qkv_spec = pl.BlockSpec((None, rows, WIDTH_A), cls)
    o, lse = pl.pallas_call(
        _attn_body,
        grid=(B, dil, L // rows),
        in_specs=[qkv_spec, qkv_spec, qkv_spec,
                  pl.BlockSpec(bias_tab.shape, lambda b, r, n: (0, 0, 0, 0))],
        out_specs=(qkv_spec, pl.BlockSpec((None, rows, LANES), cls)),
        out_shape=(jax.ShapeDtypeStruct((B, L, dil * WIDTH_A), BF16),
                   jax.ShapeDtypeStruct((B, L, dil * LANES), F32)),
        scratch_shapes=[pltpu.VMEM((BLK + rows, WIDTH_A), BF16), pltpu.VMEM((BLK + rows, WIDTH_A), BF16)],
        compiler_params=_params(3),
        name=f"dil_attn_d{dil}",
    )(view(q), view(k), view(v), bias_tab)
    return o.reshape(B, S, WIDTH_A), lse.reshape(B, S, LANES)


def _ret_body(*refs, n_branch):
    qb_ref, kb_ref, vb_ref, gb_ref = refs[:4]
    p = 4
    if n_branch:
        o_refs = refs[p:p + n_branch]
        l_refs = refs[p + n_branch:p + 2 * n_branch]
        exp_ref = refs[p + 2 * n_branch]
        p += 2 * n_branch + 1
    else:
        oa_ref = refs[p]
        p += 1
    st0_ref, dmat_ref, qdec_ref, kdec_ref, cdec_ref, cat_ref, sto_ref, st_ref = refs[p:]

    @pl.when(pl.program_id(1) == 0)
    def _():
        st_ref[...] = st0_ref[...]

    if n_branch:
        ls = [r[...] for r in l_refs]
        mx = functools.reduce(jnp.maximum, ls)
        ws = [jnp.exp(l - mx) for l in ls]
        tot = functools.reduce(lambda a, b: a + b, ws)
        oa = None
        for w, o_ref in zip(ws, o_refs):
            w = w / tot
            w_hi = w.astype(BF16)
            w_lo = (w - w_hi.astype(F32)).astype(BF16)
            w_full = (jnp.dot(w_hi, exp_ref[...], preferred_element_type=F32)
                      + jnp.dot(w_lo, exp_ref[...], preferred_element_type=F32))
            term = w_full * o_ref[...].astype(F32)
            oa = term if oa is None else oa + term
        cat_ref[:, :WIDTH_A] = oa.astype(BF16)
    else:
        cat_ref[:, :WIDTH_A] = oa_ref[...].astype(BF16)

    for h in range(N_HEADS_B):
        ks = slice(h * KEY_DIM_B, (h + 1) * KEY_DIM_B)
        vs = slice(h * VAL_DIM_B, (h + 1) * VAL_DIM_B)
        q = qb_ref[:, ks]
        k = kb_ref[:, ks]
        v = vb_ref[:, vs]
        st = st_ref[h]
        a = lax.dot_general(q, k, (((1,), (1,)), ((), ())), preferred_element_type=F32) * dmat_ref[h]
        o = (jnp.dot(a.astype(BF16), v, preferred_element_type=F32)
             + jnp.dot(q, st.astype(BF16), preferred_element_type=F32) * qdec_ref[h])
        kd = (k.astype(F32) * kdec_ref[h]).astype(BF16)
        st_new = st * cdec_ref[h] + lax.dot_general(kd, v, (((0,), (0,)), ((), ())), preferred_element_type=F32)
        st_ref[h] = st_new
        sto_ref[h] = st_new
        mu = jnp.mean(o, axis=-1, keepdims=True)
        var = jnp.mean(jnp.square(o - mu), axis=-1, keepdims=True)
        obn = (o - mu) * lax.rsqrt(var + GN_EPS)
        g = gb_ref[:, vs].astype(F32)
        gated = g * (1.0 / (1.0 + jnp.exp(-g))) * obn
        cat_ref[:, WIDTH_A + h * VAL_DIM_B:WIDTH_A + (h + 1) * VAL_DIM_B] = gated.astype(BF16)


def _decay_tables(chunk, rows):
    H = N_HEADS_B
    log_g = jnp.log(1.0 - 2.0 ** (-5.0 - jnp.arange(H, dtype=F32)))
    i = jnp.arange(rows, dtype=F32)
    live = np.arange(rows) < chunk
    diff = i[:, None] - i[None, :]
    causal = (diff >= 0) & jnp.asarray(live[:, None] & live[None, :])
    dmat = jnp.where(causal[None], jnp.exp(jnp.where(causal, diff, 0.0)[None] * log_g[:, None, None]), 0.0)
    q_decay = jnp.where(jnp.asarray(live)[None], jnp.exp((i[None, :] + 1.0) * log_g[:, None]), 0.0)
    k_decay = jnp.where(jnp.asarray(live)[None], jnp.exp((chunk - 1.0 - i)[None, :] * log_g[:, None]), 0.0)
    c_decay = jnp.exp(chunk * log_g)
    qdec = jnp.broadcast_to(q_decay[:, :, None], (H, rows, VAL_DIM_B))
    kdec = jnp.broadcast_to(k_decay[:, :, None], (H, rows, KEY_DIM_B))
    cdec = jnp.broadcast_to(c_decay[:, None, None], (H, KEY_DIM_B, VAL_DIM_B))
    return dmat.astype(F32), qdec.astype(F32), kdec.astype(F32), cdec.astype(F32)


def _ret_mix(qb, kb, vb, gb, attn, state0, chunk):
    B, S, _ = qb.shape
    rows = RET_CHUNK
    nc = S // rows
    tables = _decay_tables(chunk, rows)

    def tok(b, c):
        return (b, c, 0)

    def tspec(w):
        return pl.BlockSpec((None, rows, w), tok)

    def const(shape):
        return pl.BlockSpec(shape, lambda b, c: (0,) * len(shape))

    ins = [qb, kb, vb, gb]
    in_specs = [tspec(QK_WIDTH_B), tspec(QK_WIDTH_B), tspec(WIDTH_B), tspec(WIDTH_B)]
    if isinstance(attn, tuple):
        outs_a, lses = attn
        n_branch = len(outs_a)
        expand = np.zeros((LANES, WIDTH_A), np.float32)
        for h in range(N_HEADS_A):
            expand[h, h * HEAD_DIM_A:(h + 1) * HEAD_DIM_A] = 1.0
        ins += list(outs_a) + list(lses) + [jnp.asarray(expand, BF16)]
        in_specs += [tspec(WIDTH_A)] * n_branch + [tspec(LANES)] * n_branch + [const((LANES, WIDTH_A))]
    else:
        n_branch = 0
        ins.append(attn)
        in_specs.append(tspec(WIDTH_A))
    st_shape = (N_HEADS_B, KEY_DIM_B, VAL_DIM_B)
    st_spec = pl.BlockSpec((None,) + st_shape, lambda b, c: (b, 0, 0, 0))
    ins += [state0] + list(tables)
    in_specs += [st_spec] + [const(t.shape) for t in tables]
    return pl.pallas_call(
        functools.partial(_ret_body, n_branch=n_branch),
        grid=(B, nc),
        in_specs=in_specs,
        out_specs=(tspec(WIDTH_A + WIDTH_B), st_spec),
        out_shape=(jax.ShapeDtypeStruct((B, S, WIDTH_A + WIDTH_B), BF16),
                   jax.ShapeDtypeStruct((B,) + st_shape, F32)),
        scratch_shapes=[pltpu.VMEM(st_shape, F32)],
        compiler_params=_params(2),
        name=f"ret_mix_{n_branch}",
    )(*ins)


SAMP_Q_ROWS = 64
SAMP_NEW_LANES = 128


def _samp_attn_body(q_ref, kt_ref, vt_ref, knt_ref, vnt_ref, bc_ref, bn_ref, hm_ref, o_ref, ko_ref, vo_ref, *, ds):
    q = q_ref[...]
    kt = kt_ref[...]
    vt = vt_ref[...]
    knt = knt_ref[...]
    vnt = vnt_ref[...]
    w = kt.shape[1]
    is_new = lax.broadcasted_iota(jnp.int32, knt.shape, 1) >= SAMP_NEW_LANES - ds
    for src, new, dst in ((kt, knt, ko_ref), (vt, vnt, vo_ref)):
        rolled = pltpu.roll(src, w - ds, 1)
        dst[:, :w - SAMP_NEW_LANES] = rolled[:, :w - SAMP_NEW_LANES]
        dst[:, w - SAMP_NEW_LANES:] = jnp.where(is_new, new, rolled[:, w - SAMP_NEW_LANES:])

    s_c = jnp.dot(q, kt.astype(BF16), preferred_element_type=F32)
    s_n = jnp.dot(q, knt.astype(BF16), preferred_element_type=F32)
    es_c, es_n, dens, lses = [], [], [], []
    for n in range(len(DILATED_BRANCHES)):
        sc = s_c + bc_ref[n]
        sn = s_n + bn_ref[n]
        m = jnp.maximum(jnp.max(sc, axis=-1, keepdims=True), jnp.max(sn, axis=-1, keepdims=True))
        ec = jnp.exp(sc - m)
        en = jnp.exp(sn - m)
        den = jnp.sum(ec, axis=-1, keepdims=True) + jnp.sum(en, axis=-1, keepdims=True)
        es_c.append(ec)
        es_n.append(en)
        dens.append(den)
        lses.append(m + jnp.log(den))
    mx = functools.reduce(jnp.maximum, lses)
    ws = [jnp.exp(l - mx) for l in lses]
    tot = functools.reduce(lambda a, b: a + b, ws)
    p_c = None
    p_n = None
    for w, den, ec, en in zip(ws, dens, es_c, es_n):
        coef = w / (tot * den)
        p_c = coef * ec if p_c is None else p_c + coef * ec
        p_n = coef * en if p_n is None else p_n + coef * en
    nt = (((1,), (1,)), ((), ()))
    o = (lax.dot_general(p_c.astype(BF16), vt.astype(BF16), nt, preferred_element_type=F32)
         + lax.dot_general(p_n.astype(BF16), vnt.astype(BF16), nt, preferred_element_type=F32))
    o = o * hm_ref[...]
    o_ref[...] = jnp.sum(o.reshape(SUBLANES, N_HEADS_A, WIDTH_A), axis=1)


def _samp_bias_tables(rel_bias, w_buf, ds):
    tabs_c, tabs_n = [], []
    s = np.arange(SUBLANES)[:, None]
    live_s = s < ds
    first_new = SAMP_NEW_LANES - ds
    for window, dil in DILATED_BRANCHES:
        n_keys = window // dil
        for keys, live_k, tabs in ((np.arange(w_buf)[None, :], True, tabs_c),
                                   (w_buf - first_new + np.arange(SAMP_NEW_LANES)[None, :],
                                    np.arange(SAMP_NEW_LANES)[None, :] >= first_new, tabs_n)):
            n = keys.shape[1]
            dist = w_buf + s - keys
            valid = (dist >= 0) & (dist % dil == 0) & (dist // dil <= n_keys) & live_k
            bias = _bias_by_bucket(rel_bias, _t5_bucket(np.clip(dist, 0, None))).transpose(1, 0, 2)
            tab = jnp.where(jnp.asarray(valid)[:, None, :], bias, NEG_INF)
            pad = jnp.where(jnp.asarray(np.broadcast_to(live_k, dist.shape))[:, None, :], 0.0, NEG_INF)
            tab = jnp.where(jnp.asarray(live_s)[:, :, None], tab, pad)
            tabs.append(tab.reshape(SAMP_Q_ROWS, n))
    return jnp.stack(tabs_c), jnp.stack(tabs_n)


def _samp_attn(qa, cache_kt, cache_vt, knt, vnt, rel_bias, ds):
    DB, DS, _ = qa.shape
    W = cache_kt.shape[2]
    head_of_lane = np.arange(WIDTH_A) // HEAD_DIM_A
    hmask = (np.arange(SAMP_Q_ROWS)[:, None] % N_HEADS_A == head_of_lane[None, :])
    q8 = jnp.pad(qa, ((0, 0), (0, SUBLANES - DS), (0, 0)))
    q_rows = jnp.where(jnp.asarray(hmask)[None], jnp.repeat(q8, N_HEADS_A, axis=1), jnp.zeros((), BF16))
    bias_c, bias_n = _samp_bias_tables(rel_bias, W, DS)

    def per_b(rows, w):
        return pl.BlockSpec((None, rows, w), lambda b: (b, 0, 0))

    def const(a):
        return pl.BlockSpec(a.shape, lambda b: (0,) * a.ndim)

    hm = jnp.asarray(hmask, F32)
    return pl.pallas_call(
        functools.partial(_samp_attn_body, ds=ds),
        grid=(DB,),
        in_specs=[per_b(SAMP_Q_ROWS, WIDTH_A), per_b(WIDTH_A, W), per_b(WIDTH_A, W),
                  per_b(WIDTH_A, SAMP_NEW_LANES), per_b(WIDTH_A, SAMP_NEW_LANES),
                  const(bias_c), const(bias_n), const(hm)],
        out_specs=(per_b(SUBLANES, WIDTH_A), per_b(WIDTH_A, W), per_b(WIDTH_A, W)),
        out_shape=(jax.ShapeDtypeStruct((DB, SUBLANES, WIDTH_A), F32),
                   jax.ShapeDtypeStruct((DB, WIDTH_A, W), F32), jax.ShapeDtypeStruct((DB, WIDTH_A, W), F32)),
        compiler_params=_params(1),
        name="samp_attn",
    )(q_rows, cache_kt, cache_vt, knt, vnt, bias_c, bias_n, hm)


def _route_body(cat_ref, x_ref, wout_ref, g_ref, b_ref, rwh_ref, rwl_ref, rb_ref, tril_ref, base_ref,
                h_ref, hp_ref, route_ref, cnt_ref, *, alpha):
    @pl.when(pl.program_id(0) == 0)
    def _():
        cnt_ref[...] = base_ref[...]

    mix = jnp.dot(cat_ref[...], wout_ref[...], preferred_element_type=F32)
    y = alpha * x_ref[...] + mix
    mu = jnp.mean(y, axis=-1, keepdims=True)
    var = jnp.mean(jnp.square(y - mu), axis=-1, keepdims=True)
    h = (y - mu) * lax.rsqrt(var + LN_EPS) * g_ref[...] + b_ref[...]
    h_ref[...] = h
    hb = h.astype(BF16)
    hp_ref[...] = _pack_bf16_pairs(h)
    hl = (h - hb.astype(F32)).astype(BF16)
    logits = (jnp.dot(hb, rwh_ref[...], preferred_element_type=F32)
              + jnp.dot(hb, rwl_ref[...], preferred_element_type=F32)
              + jnp.dot(hl, rwh_ref[...], preferred_element_type=F32)) + rb_ref[...]

    tm = logits.shape[0]
    lane = lax.broadcasted_iota(jnp.int32, (tm, LANES), 1)
    work = logits
    vals, idxs = [], []
    for _ in range(TOP_K):
        m = jnp.max(work, axis=-1, keepdims=True)
        idx = jnp.min(jnp.where(work == m, lane, LANES), axis=-1, keepdims=True)
        vals.append(m)
        idxs.append(idx)
        work = jnp.where(lane == idx, -jnp.inf, work)
    es = [jnp.exp(v - vals[0]) for v in vals]
    tot = functools.reduce(lambda a, b: a + b, es)
    onehot = jnp.zeros((tm, LANES), F32)
    for idx in idxs:
        onehot = onehot + (lane == idx).astype(F32)
    before = jnp.dot(tril_ref[...], onehot.astype(BF16), preferred_element_type=F32) + cnt_ref[0:1, :]
    route = jnp.zeros((tm, LANES), F32)
    for k in range(TOP_K):
        rank = jnp.sum(jnp.where(lane == idxs[k], before, 0.0), axis=-1, keepdims=True)
        route = jnp.where(lane == k, idxs[k].astype(F32), route)
        route = jnp.where(lane == TOP_K + k, es[k] / tot, route)
        route = jnp.where(lane == 2 * TOP_K + k, rank, route)
    route_ref[...] = route
    cnt_ref[...] = cnt_ref[...] + jnp.sum(onehot, axis=0, keepdims=True)


def _out_route(cat, x2d, w_out_bf, ln_g, ln_b, rw_hi, rw_lo, rb, base, tm, alpha):
    T = x2d.shape[0]
    tril = jnp.asarray(np.tril(np.ones((tm, tm), np.float32), -1), BF16)

    def tok(i):
        return (i, 0)

    def const(a):
        return pl.BlockSpec(a.shape, lambda i: (0,) * a.ndim)

    ins = (cat, x2d, w_out_bf, ln_g, ln_b, rw_hi, rw_lo, rb, tril, base)
    in_specs = [pl.BlockSpec((tm, cat.shape[1]), tok), pl.BlockSpec((tm, D_MODEL), tok)] + [const(a) for a in ins[2:]]
    return pl.pallas_call(
        functools.partial(_route_body, alpha=alpha),
        grid=(T // tm,),
        in_specs=in_specs,
        out_specs=(pl.BlockSpec((tm, D_MODEL), tok), pl.BlockSpec((tm, D_MODEL // 2), tok),
                   pl.BlockSpec((tm, LANES), tok), pl.BlockSpec((SUBLANES, LANES), lambda i: (0, 0))),
        out_shape=(jax.ShapeDtypeStruct((T, D_MODEL), F32), jax.ShapeDtypeStruct((T, D_MODEL // 2), jnp.int32),
                   jax.ShapeDtypeStruct((T, LANES), F32), jax.ShapeDtypeStruct((SUBLANES, LANES), F32)),
        compiler_params=_params(1),
        name="out_route",
    )(*ins)


MOE_CAST_ROWS = 128


def _moe_body(be_ref, first_ref, next_ref, nused_ref, x_ref, bgu_ref, bdn_ref, wgu_hbm, wdn_hbm, y_ref,
              gu_stage, dn_stage, wgu_bf, wdn_bf, sem):
    b = pl.program_id(0)
    d_exp = wdn_bf.shape[0]

    def fetch(e):
        return (pltpu.make_async_copy(wgu_hbm.at[e], gu_stage, sem.at[0]),
                pltpu.make_async_copy(wdn_hbm.at[e], dn_stage, sem.at[1]))

    @pl.when(b == 0)
    def _():
        for c in fetch(be_ref[0]):
            c.start()

    @pl.when(first_ref[b] == 1)
    def _():
        for c in fetch(be_ref[b]):
            c.wait()

        def cast_gu(i, c):
            r = pl.ds(pl.multiple_of(i * MOE_CAST_ROWS, MOE_CAST_ROWS), MOE_CAST_ROWS)
            wgu_bf[r, :] = gu_stage[r, :].astype(BF16)
            return c

        def cast_dn(i, c):
            r = pl.ds(pl.multiple_of(i * MOE_CAST_ROWS, MOE_CAST_ROWS), MOE_CAST_ROWS)
            wdn_bf[r, :] = dn_stage[r, :].astype(BF16)
            return c

        lax.fori_loop(0, gu_stage.shape[0] // MOE_CAST_ROWS, cast_gu, 0)
        lax.fori_loop(0, d_exp // MOE_CAST_ROWS, cast_dn, 0)

        @pl.when(next_ref[b] >= 0)
        def _():
            for c in fetch(next_ref[b]):
                c.start()

    @pl.when(b < nused_ref[0])
    def _():
        x_lo, x_hi = _unpack_bf16_pairs(x_ref[...])
        x_lo = x_lo.astype(BF16)
        x_hi = x_hi.astype(BF16)
        dh = x_lo.shape[1]

        def xw(cols):
            return (jnp.dot(x_lo, wgu_bf[:dh, cols], preferred_element_type=F32)
                    + jnp.dot(x_hi, wgu_bf[dh:, cols], preferred_element_type=F32) + bgu_ref[:, cols])

        half = d_exp // 2
        y = None
        for c in range(2):
            lo = c * half
            gate = jnp.minimum(xw(slice(lo, lo + half)), SWIGLU_LIMIT)
            up = jnp.clip(xw(slice(d_exp + lo, d_exp + lo + half)), -SWIGLU_LIMIT, SWIGLU_LIMIT)
            act = (up + 1.0) * gate * (1.0 / (1.0 + jnp.exp(-SWIGLU_ALPHA * gate)))
            part = jnp.dot(act.astype(BF16), wdn_bf[lo:lo + half, :], preferred_element_type=F32)
            y = part if y is None else y + part
        y_ref[...] = _pack_bf16_pairs(y + bdn_ref[...])

    @pl.when(b >= nused_ref[0])
    def _():
        y_ref[...] = jnp.zeros_like(y_ref)


def _moe_ffn(xs, rows, block_e, n_used, has_rows, w_gu, b_gu, w_dn, b_dn):
    E, D, two_de = w_gu.shape
    d_exp = two_de // 2
    nb = rows // MOE_BLOCK
    idx = jnp.arange(nb, dtype=jnp.int32)
    first = ((idx < n_used[0]) & ((idx == 0) | (block_e != jnp.roll(block_e, 1)))).astype(jnp.int32)
    ids = jnp.arange(E, dtype=jnp.int32)
    later = jnp.where((ids[None, :] > ids[:, None]) & has_rows[None, :], ids[None, :], E).min(axis=1)
    next_of = jnp.where(later == E, -1, later).astype(jnp.int32)
    next_e = jnp.sum(jnp.where(block_e[:, None] == ids[None, :], next_of[None, :], 0), axis=1).astype(jnp.int32)
    grid_spec = pltpu.PrefetchScalarGridSpec(
        num_scalar_prefetch=4,
        grid=(nb,),
        in_specs=[
            pl.BlockSpec((MOE_BLOCK, D // 2), lambda b, be, fi, nx, nu: (b, 0)),
            pl.BlockSpec((None, 1, two_de), lambda b, be, fi, nx, nu: (be[b], 0, 0)),
            pl.BlockSpec((None, 1, D), lambda b, be, fi, nx, nu: (be[b], 0, 0)),
            pl.BlockSpec(memory_space=pl.ANY),
            pl.BlockSpec(memory_space=pl.ANY),
        ],
        out_specs=pl.BlockSpec((MOE_BLOCK, D // 2), lambda b, be, fi, nx, nu: (b, 0)),
        scratch_shapes=[pltpu.VMEM((D, two_de), F32), pltpu.VMEM((d_exp, D), F32),
                        pltpu.VMEM((D, two_de), BF16), pltpu.VMEM((d_exp, D), BF16),
                        pltpu.SemaphoreType.DMA((2,))],
    )
    return pl.pallas_call(
        _moe_body,
        grid_spec=grid_spec,
        out_shape=jax.ShapeDtypeStruct((rows, D // 2), jnp.int32),
        compiler_params=_params(1),
        name="moe_ffn",
    )(block_e, first, next_e, n_used, xs, b_gu.reshape(E, 1, two_de), b_dn.reshape(E, 1, D), w_gu, w_dn)


def _ple_body(h_ref, ys_ref, route_ref, p_ref, g_ref, b_ref, wpg_ref, bpg_ref, wpp_ref, o_ref, *, alpha):
    route = route_ref[...]
    f_lo = None
    f_hi = None
    for k in range(TOP_K):
        lo, hi = _unpack_bf16_pairs(ys_ref[k])
        g = route[:, TOP_K + k:TOP_K + k + 1]
        f_lo = g * lo if f_lo is None else f_lo + g * lo
        f_hi = g * hi if f_hi is None else f_hi + g * hi
    y = alpha * h_ref[...] + jnp.concatenate([f_lo, f_hi], axis=1)
    mu = jnp.mean(y, axis=-1, keepdims=True)
    var = jnp.mean(jnp.square(y - mu), axis=-1, keepdims=True)
    h2 = (y - mu) * lax.rsqrt(var + LN_EPS) * g_ref[...] + b_ref[...]
    z = jnp.dot(h2.astype(BF16), wpg_ref[...], preferred_element_type=F32) + bpg_ref[...]
    gate = 1.0 / (1.0 + jnp.exp(-z))
    proj = jnp.dot(p_ref[...].astype(BF16), wpp_ref[...], preferred_element_type=F32)
    o_ref[...] = h2 + gate * proj


def _ffn_ple(h, y_slots, route, p, ln_g, ln_b, w_pg_bf, b_pg, w_pp_bf, tm, tile0, alpha):
    T = h.shape[0]

    def tok(i):
        return (i, 0)

    def const(a):
        return pl.BlockSpec(a.shape, lambda i: (0,) * a.ndim)

    consts = (ln_g, ln_b, w_pg_bf, b_pg, w_pp_bf)
    return pl.pallas_call(
        functools.partial(_ple_body, alpha=alpha),
        grid=(T // tm,),
        in_specs=[pl.BlockSpec((tm, D_MODEL), tok),
                  pl.BlockSpec((TOP_K, tm, D_MODEL // 2), lambda i: (0, tile0 + i, 0)),
                  pl.BlockSpec((tm, LANES), tok),
                  pl.BlockSpec((tm, p.shape[1]), tok)] + [const(a) for a in consts],
        out_specs=pl.BlockSpec((tm, D_MODEL), tok),
        out_shape=jax.ShapeDtypeStruct((T, D_MODEL), F32),
        compiler_params=_params(1),
        name="ffn_ple",
    )(h, y_slots, route, p, *consts)


def _row(v):
    return v.reshape(1, -1).astype(F32)


def kernel(x_prompt, x_sample, cache_win_k, cache_win_v, state_ret, p_prompt, p_sample, rel_bias, w_in, w_out,
           ln1_g, ln1_b, router_w, router_b, w_gate_up, b_gate_up, w_down, b_down, ln2_g, ln2_b,
           w_ple_gate, b_ple_gate, w_ple_proj):
    B, S, D = x_prompt.shape
    DB, DS, _ = x_sample.shape
    depth = w_in.shape[0]
    w_buf = cache_win_k.shape[2]
    n_exp = router_w.shape[-1]
    alpha = (2.0 * depth) ** 0.25
    assert depth == 1 and D == D_MODEL
    assert S % (BLK * MAX_DIL) == 0 and S >= WINDOW_MAX and w_buf == WINDOW_MAX and DS <= SUBLANES
    tm_p = 512
    Tp, Ts = B * S, DB * DS
    assert Tp % tm_p == 0 and Ts % SUBLANES == 0

    i = 0
    w_in_bf = w_in[i].astype(BF16)
    w_out_bf = w_out[i].astype(BF16)
    w_pg_bf = w_ple_gate[i].astype(BF16)
    w_pp_bf = w_ple_proj[i].astype(BF16)
    rw = jnp.pad(router_w[i], ((0, 0), (0, LANES - n_exp)))
    rw_hi = rw.astype(BF16)
    rw_lo = (rw - rw_hi.astype(F32)).astype(BF16)
    rb = jnp.pad(router_b[i], (0, LANES - n_exp), constant_values=NEG_INF).reshape(1, LANES)

    cos_p, sin_p = _rotary_tables(jnp.arange(S, dtype=jnp.int32))
    qa, ka, va, qb, kb, vb, gb, kf, vf = _in_proj(x_prompt.reshape(Tp, D), w_in_bf, cos_p, sin_p, tm_p,
                                                   S // tm_p, WINDOW_MAX // tm_p)

    def seq(t):
        return t.reshape(B, S, t.shape[-1])

    outs_a, lses = [], []
    for window, dil in DILATED_BRANCHES:
        o_n, l_n = _dilated_branch(seq(qa), seq(ka), seq(va), _attn_bias_tables(rel_bias, window, dil), dil)
        outs_a.append(o_n)
        lses.append(l_n)
    st_zero = jnp.zeros((B, N_HEADS_B, KEY_DIM_B, VAL_DIM_B), F32)
    cat_p, rst_p = _ret_mix(seq(qb), seq(kb), seq(vb), seq(gb), (outs_a, lses), st_zero, RET_CHUNK)
    base0 = jnp.zeros((SUBLANES, LANES), F32)
    h_p, hp_p, route_p, cnt_p = _out_route(cat_p.reshape(Tp, -1), x_prompt.reshape(Tp, D), w_out_bf,
                                           _row(ln1_g[i]), _row(ln1_b[i]), rw_hi, rw_lo, rb, base0, tm_p, alpha)

    pos_s = jnp.tile(PAST_LEN + jnp.arange(DS, dtype=jnp.int32), DB)
    cos_s, sin_s = _rotary_tables(pos_s)
    qa_s, _, _, qb_s, kb_s, vb_s, gb_s, kf_s, vf_s = _in_proj(x_sample.reshape(Ts, D), w_in_bf, cos_s, sin_s,
                                                             Ts, 1, 1)

    def positions_minor(t):
        return jnp.transpose(t, (0, 2, 3, 1)).reshape(DB, WIDTH_A, t.shape[1])

    def positions_major(t):
        return jnp.transpose(t.reshape(t.shape[0], N_HEADS_A, HEAD_DIM_A, t.shape[2]), (0, 3, 1, 2))[None]

    def new_columns(t):
        t = jnp.transpose(t.reshape(WIDTH_A, DB, DS), (1, 0, 2))
        return jnp.pad(t, ((0, 0), (0, 0), (SAMP_NEW_LANES - DS, 0)))

    oa_s, kt_out, vt_out = _samp_attn(qa_s.reshape(DB, DS, WIDTH_A), positions_minor(cache_win_k[i]),
                                      positions_minor(cache_win_v[i]), new_columns(kf_s), new_columns(vf_s),
                                      rel_bias, DS)

    def pad_rows(t, rows):
        t = t.reshape(DB, -1, t.shape[-1])
        return jnp.pad(t, ((0, 0), (0, rows - t.shape[1]), (0, 0)))

    cat_s, rst_s = _ret_mix(pad_rows(qb_s, RET_CHUNK), pad_rows(kb_s, RET_CHUNK), pad_rows(vb_s, RET_CHUNK),
                            pad_rows(gb_s, RET_CHUNK), pad_rows(oa_s, RET_CHUNK), state_ret[i].astype(F32), DS)
    cat_s = cat_s[:, :DS].reshape(Ts, -1)
    h_s, hp_s, route_s, cnt = _out_route(cat_s, x_sample.reshape(Ts, D), w_out_bf, _row(ln1_g[i]), _row(ln1_b[i]),
                                         rw_hi, rw_lo, rb, cnt_p, Ts, alpha)

    T = Tp + Ts
    t_align = SC_WORKERS * SC_ALIGN
    T_pad = -(-T // t_align) * t_align
    route = jnp.concatenate([route_p, route_s], axis=0)
    top_idx = route[:, :TOP_K].astype(jnp.int32)
    rank = route[:, 2 * TOP_K:3 * TOP_K].astype(jnp.int32)
    counts = cnt[0, :n_exp].astype(jnp.int32)
    padded = (counts + MOE_BLOCK - 1) // MOE_BLOCK * MOE_BLOCK
    pad_end = jnp.cumsum(padded)
    pad_start = pad_end - padded
    dest = pad_start[top_idx] + rank
    n_blocks = -(-T * TOP_K // MOE_BLOCK) + n_exp
    rows = n_blocks * MOE_BLOCK
    block_start = jnp.arange(n_blocks, dtype=jnp.int32) * MOE_BLOCK
    block_e = jnp.minimum(jnp.sum(pad_end[None, :] <= block_start[:, None], axis=1), n_exp - 1).astype(jnp.int32)
    n_used = (pad_end[-1:] // MOE_BLOCK).astype(jnp.int32)
    n_fill = T_pad - T
    spare = rows + jnp.arange(n_fill * TOP_K, dtype=jnp.int32).reshape(n_fill, TOP_K)
    dest_sc = jnp.concatenate([dest, spare], axis=0).T.reshape(-1)
    dest_ga = jnp.concatenate([dest, jnp.zeros((n_fill, TOP_K), jnp.int32)], axis=0).T.reshape(-1)
    hp_all = jnp.concatenate([hp_p, hp_s, jnp.zeros((n_fill, D // 2), jnp.int32)], axis=0)
    xs = _sc_scatter_rows(hp_all, dest_sc, rows + n_fill * TOP_K)
    ys = _moe_ffn(xs, rows, block_e, n_used, padded > 0, w_gate_up[i], b_gate_up[i], w_down[i], b_down[i])
    y_slots = _sc_gather_rows(ys, dest_ga).reshape(TOP_K, T_pad, D // 2)

    ple_args = (_row(ln2_g[i]), _row(ln2_b[i]), w_pg_bf, _row(b_ple_gate[i]), w_pp_bf)
    assert Tp % Ts == 0
    y_p = _ffn_ple(h_p, y_slots, route_p, p_prompt[i].reshape(Tp, D_PLE), *ple_args, tm_p, 0, alpha)
    y_s = _ffn_ple(h_s, y_slots, route_s, p_sample[i].reshape(Ts, D_PLE), *ple_args, Ts, Tp // Ts, alpha)

    return (y_p.reshape(B, S, D), y_s.reshape(DB, DS, D), positions_major(kf), positions_major(vf),
            rst_p[None], positions_major(kt_out), positions_major(vt_out), rst_s[None])
```

```python
import functools

import numpy as np
import jax
import jax.numpy as jnp
from jax import lax
from jax.experimental import pallas as pl
from jax.experimental.pallas import tpu as pltpu
from jax.experimental.pallas import tpu_sc as plsc

F32 = jnp.float32
BF16 = jnp.bfloat16

D_MODEL = 1024
D_PLE = 256
N_HEADS_A = 8
HEAD_DIM_A = 64
WIDTH_A = N_HEADS_A * HEAD_DIM_A
DILATED_BRANCHES = ((128, 1), (512, 4), (2048, 16))
BLK = 128
WINDOW_MAX = 2048
MAX_DIL = 16
NUM_BUCKETS = 32
MAX_DISTANCE = 2048
N_HEADS_B = 4
KEY_DIM_B = 64
VAL_DIM_B = 128
QK_WIDTH_B = N_HEADS_B * KEY_DIM_B
WIDTH_B = N_HEADS_B * VAL_DIM_B
RET_CHUNK = 128
GN_EPS = 1e-6
TOP_K = 4
SWIGLU_LIMIT = 7.0
SWIGLU_ALPHA = 1.702
LN_EPS = 1e-5
NEG_INF = -1e30
PAST_LEN = 16384
MOE_BLOCK = 256
LANES = 128
SUBLANES = 8
VMEM_LIMIT = 52 * 1024 * 1024


def _params(n_axes, vmem=VMEM_LIMIT):
    return pltpu.CompilerParams(dimension_semantics=("arbitrary",) * n_axes, vmem_limit_bytes=vmem)


def _t5_bucket(dist):
    dist = np.asarray(dist, dtype=np.int32)
    max_exact = NUM_BUCKETS // 2
    d = np.maximum(dist, 1).astype(np.float32)
    large = max_exact + (np.log(d / max_exact) / np.log(MAX_DISTANCE / max_exact)
                         * (NUM_BUCKETS - max_exact)).astype(np.int32)
    large = np.minimum(large, NUM_BUCKETS - 1)
    return np.where(dist < max_exact, dist, large).astype(np.int32)


def _bias_by_bucket(rel_bias, buckets):
    b = jnp.asarray(buckets, jnp.int32)
    ids = jnp.arange(NUM_BUCKETS, dtype=jnp.int32).reshape((NUM_BUCKETS, 1) + (1,) * b.ndim)
    vals = rel_bias.astype(F32).reshape((NUM_BUCKETS, rel_bias.shape[1]) + (1,) * b.ndim)
    return jnp.sum(jnp.where(b[None, None] == ids, vals, 0.0), axis=0)


def _pack_bf16_pairs(v):
    w = v.shape[1] // 2
    lo = lax.bitcast_convert_type(v[:, :w].astype(BF16).astype(F32), jnp.uint32) >> 16
    hi = lax.bitcast_convert_type(v[:, w:].astype(BF16).astype(F32), jnp.uint32) & jnp.uint32(0xFFFF0000)
    return lax.bitcast_convert_type(lo | hi, jnp.int32)


def _unpack_bf16_pairs(p):
    u = lax.bitcast_convert_type(p, jnp.uint32)
    lo = lax.bitcast_convert_type(u << 16, F32)
    hi = lax.bitcast_convert_type(u & jnp.uint32(0xFFFF0000), F32)
    return lo, hi


SC_CORES = 2
SC_SUBCORES = 16
SC_WORKERS = SC_CORES * SC_SUBCORES
SC_ALIGN = 8
SC_CHUNK_ROWS = 80


def _sc_mesh():
    return plsc.VectorSubcoreMesh(core_axis_name="c", subcore_axis_name="s")


def _sc_chunk(per_worker):
    c = max(d for d in range(SC_ALIGN, SC_CHUNK_ROWS + 1, SC_ALIGN) if per_worker % d == 0)
    return c


def _sc_scatter_rows(src, dest_flat, n_out):
    T, W = src.shape
    K = dest_flat.shape[0] // T
    per_w = T // SC_WORKERS
    assert per_w * SC_WORKERS == T and per_w % SC_ALIGN == 0
    chunk = _sc_chunk(per_w)

    @functools.partial(
        pl.kernel, mesh=_sc_mesh(), out_type=jax.ShapeDtypeStruct((n_out, W), src.dtype),
        scratch_types=[pltpu.VMEM((chunk, W), src.dtype)] + [pltpu.VMEM((chunk,), jnp.int32)] * K
        + [pltpu.SemaphoreType.DMA],
        name="sc_scatter_rows")
    def k(src_hbm, dest_hbm, out_hbm, rows_v, *rest):
        idx_vs, sem = rest[:K], rest[K]
        base = (lax.axis_index("s") * SC_CORES + lax.axis_index("c")) * per_w

        @pl.loop(0, per_w // chunk)
        def _(j):
            off = pl.multiple_of(base + j * chunk, SC_ALIGN)
            pltpu.sync_copy(src_hbm.at[pl.ds(off, chunk)], rows_v)
            for kk in range(K):
                pltpu.sync_copy(dest_hbm.at[pl.ds(kk * T + off, chunk)], idx_vs[kk])
            copies = [pltpu.async_copy(rows_v, out_hbm.at[idx_vs[kk]], sem) for kk in range(K)]
            for c in copies:
                c.wait()

    return k(src, dest_flat)


def _sc_gather_rows(table, idx):
    B = idx.shape[0]
    W = table.shape[1]
    per_w = B // SC_WORKERS
    assert per_w * SC_WORKERS == B and per_w % SC_ALIGN == 0
    chunk = _sc_chunk(per_w)

    @functools.partial(
        pl.kernel, mesh=_sc_mesh(), out_type=jax.ShapeDtypeStruct((B, W), table.dtype),
        scratch_types=[pltpu.VMEM((chunk,), jnp.int32), pltpu.VMEM((chunk, W), table.dtype), pltpu.SemaphoreType.DMA],
        name="sc_gather_rows")
    def k(table_hbm, idx_hbm, out_hbm, idx_v, rows_v, sem):
        base = (lax.axis_index("s") * SC_CORES + lax.axis_index("c")) * per_w

        @pl.loop(0, per_w // chunk)
        def _(j):
            off = pl.multiple_of(base + j * chunk, SC_ALIGN)
            pltpu.sync_copy(idx_hbm.at[pl.ds(off, chunk)], idx_v)
            pltpu.async_copy(table_hbm.at[idx_v], rows_v, sem).wait()
            pltpu.sync_copy(rows_v, out_hbm.at[pl.ds(off, chunk)])

    return k(table, idx)


def _in_proj_body(x_ref, w_ref, cos_ref, sin_ref, *refs, seq_tiles, first_win, dilations):
    qa_ref, ka_ref, va_ref, qb_ref, kb_ref, vb_ref, gb_ref, kt_ref, vt_ref = refs[:9]
    n_d = len(dilations)
    dil_refs = [refs[9 + t * n_d:9 + (t + 1) * n_d] for t in range(3)]
    zs_ref = refs[9 + 3 * n_d] if n_d else None
    x = x_ref[...].astype(BF16)
    in_window = pl.program_id(0) % seq_tiles >= first_win

    def emit(z, token_ref, class_refs):
        token_ref[...] = z.astype(BF16)
        if not class_refs:
            return
        for c in range(WIDTH_A // LANES):
            zs_ref[c] = z[:, c * LANES:(c + 1) * LANES]
        for d, ref in zip(dilations, class_refs):
            n = z.shape[0] // d
            for r in range(d):
                for c in range(WIDTH_A // LANES):
                    lo = r * WIDTH_A + c * LANES
                    ref[:, lo:lo + LANES] = zs_ref[c, pl.ds(r, n, stride=d), :].astype(BF16)

    def proj(lo, hi):
        return jnp.dot(x, w_ref[:, lo:hi], preferred_element_type=F32)

    o = 0
    emit(proj(o, o + WIDTH_A) * (HEAD_DIM_A ** -0.5), qa_ref, dil_refs[0])
    o += WIDTH_A
    ka = proj(o, o + WIDTH_A)
    emit(ka, ka_ref, dil_refs[1])

    @pl.when(in_window)
    def _():
        kt_ref[...] = ka.T
    o += WIDTH_A
    va = proj(o, o + WIDTH_A)
    emit(va, va_ref, dil_refs[2])

    @pl.when(in_window)
    def _():
        vt_ref[...] = va.T
    o += WIDTH_A

    cos = cos_ref[...]
    sin = sin_ref[...]
    lane = lax.broadcasted_iota(jnp.int32, cos.shape, 1)
    first_half = (lane % KEY_DIM_B) < (KEY_DIM_B // 2)

    def rot(z):
        sw = jnp.where(first_half, pltpu.roll(z, QK_WIDTH_B - KEY_DIM_B // 2, 1), pltpu.roll(z, KEY_DIM_B // 2, 1))
        return z * cos + sw * sin

    qb_ref[...] = rot(proj(o, o + QK_WIDTH_B)).astype(BF16)
    o += QK_WIDTH_B
    kb_ref[...] = (rot(proj(o, o + QK_WIDTH_B)) * (KEY_DIM_B ** -0.5)).astype(BF16)
    o += QK_WIDTH_B
    vb_ref[...] = proj(o, o + WIDTH_B).astype(BF16)
    o += WIDTH_B
    gb_ref[...] = proj(o, o + WIDTH_B).astype(BF16)


def _in_proj(x2d, w_bf, cos_t, sin_t, tm, seq_tiles, win_tiles, dilations=()):
    T = x2d.shape[0]
    nt = T // tm
    n_seq = nt // seq_tiles
    j0 = seq_tiles - win_tiles

    def tok(i):
        return (i, 0)

    def tab(i):
        return (i % seq_tiles, 0)

    def win(i):
        return (i // seq_tiles, 0, jnp.maximum(i % seq_tiles - j0, 0))

    def tspec(w):
        return pl.BlockSpec((tm, w), tok)

    out_shape = (
        jax.ShapeDtypeStruct((T, WIDTH_A), BF16), jax.ShapeDtypeStruct((T, WIDTH_A), BF16),
        jax.ShapeDtypeStruct((T, WIDTH_A), BF16),
        jax.ShapeDtypeStruct((T, QK_WIDTH_B), BF16), jax.ShapeDtypeStruct((T, QK_WIDTH_B), BF16),
        jax.ShapeDtypeStruct((T, WIDTH_B), BF16), jax.ShapeDtypeStruct((T, WIDTH_B), BF16),
        jax.ShapeDtypeStruct((n_seq, WIDTH_A, win_tiles * tm), F32),
        jax.ShapeDtypeStruct((n_seq, WIDTH_A, win_tiles * tm), F32),
    ) + tuple(jax.ShapeDtypeStruct((T // d, d * WIDTH_A), BF16) for _ in range(3) for d in dilations)
    class_specs = tuple(pl.BlockSpec((tm // d, d * WIDTH_A), tok) for _ in range(3) for d in dilations)
    return pl.pallas_call(
        functools.partial(_in_proj_body, seq_tiles=seq_tiles, first_win=j0, dilations=tuple(dilations)),
        grid=(nt,),
        in_specs=[tspec(D_MODEL), pl.BlockSpec(w_bf.shape, lambda i: (0, 0)),
                  pl.BlockSpec((tm, QK_WIDTH_B), tab), pl.BlockSpec((tm, QK_WIDTH_B), tab)],
        out_specs=(tspec(WIDTH_A), tspec(WIDTH_A), tspec(WIDTH_A), tspec(QK_WIDTH_B), tspec(QK_WIDTH_B),
                   tspec(WIDTH_B), tspec(WIDTH_B),
                   pl.BlockSpec((None, WIDTH_A, tm), win), pl.BlockSpec((None, WIDTH_A, tm), win)) + class_specs,
        out_shape=out_shape,
        scratch_shapes=[pltpu.VMEM((WIDTH_A // LANES, tm, LANES), F32)] if dilations else [],
        compiler_params=_params(1),
        name="in_proj",
    )(x2d, w_bf, cos_t, sin_t)


def _rotary_tables(pos):
    half = KEY_DIM_B // 2
    inv_freq = 1.0 / (10000.0 ** jnp.linspace(0.0, 1.0, half, dtype=F32))
    ang = pos.astype(F32)[:, None] * inv_freq[None, :]
    cos = jnp.cos(ang)
    sin = jnp.sin(ang)
    cos_h = jnp.concatenate([cos, cos], axis=-1)
    sin_h = jnp.concatenate([-sin, sin], axis=-1)
    return jnp.tile(cos_h, (1, N_HEADS_B)), jnp.tile(sin_h, (1, N_HEADS_B))


ATTN_BLOCKS_PER_STEP = 4


def _attn_body(q_ref, k_ref, v_ref, bias_ref, o_ref, lse_ref, kb_ref, vb_ref):
    n = pl.program_id(2)
    rows = q_ref.shape[0]

    @pl.when(n == 0)
    def _():
        kb_ref[:BLK, :] = jnp.zeros((BLK, WIDTH_A), BF16)
        vb_ref[:BLK, :] = jnp.zeros((BLK, WIDTH_A), BF16)

    kb_ref[BLK:, :] = k_ref[...]
    vb_ref[BLK:, :] = v_ref[...]
    lane = lax.broadcasted_iota(jnp.int32, (BLK, LANES), 1)
    low = lane < HEAD_DIM_A
    nt = (((1,), (1,)), ((), ()))

    def sub_block(j, carry):
        r0 = pl.multiple_of(j * BLK, BLK)
        table = jnp.where((n == 0) & (j == 0), 0, 1)
        lse_tile = jnp.zeros((BLK, LANES), F32)
        for p in range(N_HEADS_A // 2):
            cs = slice(p * LANES, (p + 1) * LANES)
            qp = q_ref[pl.ds(r0, BLK), cs]
            kp = kb_ref[pl.ds(r0, 2 * BLK), cs]
            vp = vb_ref[pl.ds(r0, 2 * BLK), cs]
            zero = jnp.zeros_like(qp)
            outs = []
            for half, qh in enumerate((jnp.where(low, qp, zero), jnp.where(low, zero, qp))):
                h = 2 * p + half
                s = lax.dot_general(qh, kp, nt, preferred_element_type=F32) + bias_ref[table, h]
                m = jnp.max(s, axis=-1, keepdims=True)
                e = jnp.exp(s - m)
                den = jnp.sum(e, axis=-1, keepdims=True)
                outs.append(jnp.dot(e.astype(BF16), vp, preferred_element_type=F32) / den)
                lse_tile = jnp.where(lane == h, m + jnp.log(den), lse_tile)
            o_ref[pl.ds(r0, BLK), cs] = jnp.where(low, outs[0], outs[1]).astype(o_ref.dtype)
        lse_ref[pl.ds(r0, BLK), :] = lse_tile
        return carry

    lax.fori_loop(0, rows // BLK, sub_block, 0)
    kb_ref[:BLK, :] = k_ref[rows - BLK:, :]
    vb_ref[:BLK, :] = v_ref[rows - BLK:, :]


def _attn_bias_tables(rel_bias, window, dil):
    n_keys = window // dil
    i = np.arange(BLK)[:, None]
    j = np.arange(2 * BLK)[None, :]
    rel = BLK + i - j
    in_band = (rel >= 0) & (rel <= n_keys)
    bias = _bias_by_bucket(rel_bias, _t5_bucket(np.clip(rel, 0, None) * dil))
    later = jnp.where(jnp.asarray(in_band)[None], bias, NEG_INF)
    first = jnp.where(jnp.asarray(in_band & (j >= BLK))[None], bias, NEG_INF)
    return jnp.stack([first, later])


def _dilated_branch(q, k, v, bias_tab, dil):
    B, L, _ = q.shape
    rows = min(ATTN_BLOCKS_PER_STEP, L // BLK) * BLK
    assert L % rows == 0
    o_dtype = BF16 if RET_CHUNK // dil >= 16 else F32

    def cls(b, r, n):
        return (b, n, r)

    qkv_spec = pl.BlockSpec((None, rows, WIDTH_A), cls)
    o, lse = pl.pallas_call(
        _attn_body,
        grid=(B, dil, L // rows),
        in_specs=[qkv_spec, qkv_spec, qkv_spec,
                  pl.BlockSpec(bias_tab.shape, lambda b, r, n: (0, 0, 0, 0))],
        out_specs=(qkv_spec, pl.BlockSpec((None, rows, LANES), cls)),
        out_shape=(jax.ShapeDtypeStruct((B, L, dil * WIDTH_A), o_dtype),
                   jax.ShapeDtypeStruct((B, L, dil * LANES), F32)),
        scratch_shapes=[pltpu.VMEM((BLK + rows, WIDTH_A), BF16), pltpu.VMEM((BLK + rows, WIDTH_A), BF16)],
        compiler_params=_params(3),
        name=f"dil_attn_d{dil}",
    )(q, k, v, bias_tab)
    return o, lse


def _ret_body(*refs, dilations):
    n_branch = len(dilations)
    qb_ref, kb_ref, vb_ref, gb_ref = refs[:4]
    p = 4
    if n_branch:
        o_refs = refs[p:p + n_branch]
        l_refs = refs[p + n_branch:p + 2 * n_branch]
        exp_ref = refs[p + 2 * n_branch]
        p += 2 * n_branch + 1
    else:
        oa_ref = refs[p]
        p += 1
    st0_ref, dmat_ref, qdec_ref, kdec_ref, cdec_ref, cat_ref, sto_ref, st_ref = refs[p:p + 8]
    if n_branch:
        us_ref, ls_ref = refs[p + 8:]

    def lse_token_order(l_ref, d):
        if d == 1:
            return l_ref[...]
        n = l_ref.shape[0]
        for r in range(d):
            ls_ref[pl.ds(r, n, stride=d), :] = l_ref[:, r * LANES:(r + 1) * LANES]
        return ls_ref[...]

    def out_token_order(o_ref, d):
        if d == 1:
            return o_ref[...].astype(F32)
        n = o_ref.shape[0]
        for r in range(d):
            for c in range(WIDTH_A // LANES):
                lo = r * WIDTH_A + c * LANES
                us_ref[c, pl.ds(r, n, stride=d), :] = o_ref[:, lo:lo + LANES].astype(F32)
        return jnp.concatenate([us_ref[c] for c in range(WIDTH_A // LANES)], axis=1)

    @pl.when(pl.program_id(1) == 0)
    def _():
        st_ref[...] = st0_ref[...]

    if n_branch:
        ls = [lse_token_order(l_ref, d) for l_ref, d in zip(l_refs, dilations)]
        mx = functools.reduce(jnp.maximum, ls)
        ws = [jnp.exp(l - mx) for l in ls]
        tot = functools.reduce(lambda a, b: a + b, ws)
        oa = None
        for w, o_ref, d in zip(ws, o_refs, dilations):
            o_tok = out_token_order(o_ref, d)
            w = w / tot
            w_hi = w.astype(BF16)
            w_lo = (w - w_hi.astype(F32)).astype(BF16)
            w_full = (jnp.dot(w_hi, exp_ref[...], preferred_element_type=F32)
                      + jnp.dot(w_lo, exp_ref[...], preferred_element_type=F32))
            term = w_full * o_tok
            oa = term if oa is None else oa + term
        cat_ref[:, :WIDTH_A] = oa.astype(BF16)
    else:
        cat_ref[:, :WIDTH_A] = oa_ref[...].astype(BF16)

    for h in range(N_HEADS_B):
        ks = slice(h * KEY_DIM_B, (h + 1) * KEY_DIM_B)
        vs = slice(h * VAL_DIM_B, (h + 1) * VAL_DIM_B)
        q = qb_ref[:, ks]
        k = kb_ref[:, ks]
        v = vb_ref[:, vs]
        st = st_ref[h]
        a = lax.dot_general(q, k, (((1,), (1,)), ((), ())), preferred_element_type=F32) * dmat_ref[h]
        o = (jnp.dot(a.astype(BF16), v, preferred_element_type=F32)
             + jnp.dot(q, st.astype(BF16), preferred_element_type=F32) * qdec_ref[h])
        kd = (k.astype(F32) * kdec_ref[h]).astype(BF16)
        st_new = st * cdec_ref[h] + lax.dot_general(kd, v, (((0,), (0,)), ((), ())), preferred_element_type=F32)
        st_ref[h] = st_new
        sto_ref[h] = st_new
        mu = jnp.mean(o, axis=-1, keepdims=True)
        var = jnp.mean(jnp.square(o - mu), axis=-1, keepdims=True)
        obn = (o - mu) * lax.rsqrt(var + GN_EPS)
        g = gb_ref[:, vs].astype(F32)
        gated = g * (1.0 / (1.0 + jnp.exp(-g))) * obn
        cat_ref[:, WIDTH_A + h * VAL_DIM_B:WIDTH_A + (h + 1) * VAL_DIM_B] = gated.astype(BF16)


def _decay_tables(chunk, rows):
    H = N_HEADS_B
    log_g = jnp.log(1.0 - 2.0 ** (-5.0 - jnp.arange(H, dtype=F32)))
    i = jnp.arange(rows, dtype=F32)
    live = np.arange(rows) < chunk
    diff = i[:, None] - i[None, :]
    causal = (diff >= 0) & jnp.asarray(live[:, None] & live[None, :])
    dmat = jnp.where(causal[None], jnp.exp(jnp.where(causal, diff, 0.0)[None] * log_g[:, None, None]), 0.0)
    q_decay = jnp.where(jnp.asarray(live)[None], jnp.exp((i[None, :] + 1.0) * log_g[:, None]), 0.0)
    k_decay = jnp.where(jnp.asarray(live)[None], jnp.exp((chunk - 1.0 - i)[None, :] * log_g[:, None]), 0.0)
    c_decay = jnp.exp(chunk * log_g)
    qdec = jnp.broadcast_to(q_decay[:, :, None], (H, rows, VAL_DIM_B))
    kdec = jnp.broadcast_to(k_decay[:, :, None], (H, rows, KEY_DIM_B))
    cdec = jnp.broadcast_to(c_decay[:, None, None], (H, KEY_DIM_B, VAL_DIM_B))
    return dmat.astype(F32), qdec.astype(F32), kdec.astype(F32), cdec.astype(F32)


def _ret_mix(qb, kb, vb, gb, attn, state0, chunk):
    B, S, _ = qb.shape
    rows = RET_CHUNK
    nc = S // rows
    tables = _decay_tables(chunk, rows)

    def tok(b, c):
        return (b, c, 0)

    def tspec(w):
        return pl.BlockSpec((None, rows, w), tok)

    def const(shape):
        return pl.BlockSpec(shape, lambda b, c: (0,) * len(shape))

    ins = [qb, kb, vb, gb]
    in_specs = [tspec(QK_WIDTH_B), tspec(QK_WIDTH_B), tspec(WIDTH_B), tspec(WIDTH_B)]
    scratch = [pltpu.VMEM((N_HEADS_B, KEY_DIM_B, VAL_DIM_B), F32)]
    if isinstance(attn, tuple):
        outs_a, lses, dilations = attn
        expand = np.zeros((LANES, WIDTH_A), np.float32)
        for h in range(N_HEADS_A):
            expand[h, h * HEAD_DIM_A:(h + 1) * HEAD_DIM_A] = 1.0
        ins += list(outs_a) + list(lses) + [jnp.asarray(expand, BF16)]
        in_specs += ([pl.BlockSpec((None, rows // d, d * WIDTH_A), tok) for d in dilations]
                     + [pl.BlockSpec((None, rows // d, d * LANES), tok) for d in dilations] + [const((LANES, WIDTH_A))])
        scratch += [pltpu.VMEM((WIDTH_A // LANES, rows, LANES), F32), pltpu.VMEM((rows, LANES), F32)]
    else:
        dilations = ()
        ins.append(attn)
        in_specs.append(tspec(WIDTH_A))
    st_shape = (N_HEADS_B, KEY_DIM_B, VAL_DIM_B)
    st_spec = pl.BlockSpec((None,) + st_shape, lambda b, c: (b, 0, 0, 0))
    ins += [state0] + list(tables)
    in_specs += [st_spec] + [const(t.shape) for t in tables]
    return pl.pallas_call(
        functools.partial(_ret_body, dilations=tuple(dilations)),
        grid=(B, nc),
        in_specs=in_specs,
        out_specs=(tspec(WIDTH_A + WIDTH_B), st_spec),
        out_shape=(jax.ShapeDtypeStruct((B, S, WIDTH_A + WIDTH_B), BF16),
                   jax.ShapeDtypeStruct((B,) + st_shape, F32)),
        scratch_shapes=scratch,
        compiler_params=_params(2),
        name=f"ret_mix_{len(dilations)}",
    )(*ins)


SAMP_Q_ROWS = 64
SAMP_NEW_LANES = 128


def _samp_attn_body(q_ref, kt_ref, vt_ref, knt_ref, vnt_ref, bc_ref, bn_ref, hm_ref, o_ref, ko_ref, vo_ref, *, ds):
    q = q_ref[...]
    kt = kt_ref[...]
    vt = vt_ref[...]
    knt = knt_ref[...]
    vnt = vnt_ref[...]
    w = kt.shape[1]
    is_new = lax.broadcasted_iota(jnp.int32, knt.shape, 1) >= SAMP_NEW_LANES - ds
    for src, new, dst in ((kt, knt, ko_ref), (vt, vnt, vo_ref)):
        rolled = pltpu.roll(src, w - ds, 1)
        dst[:, :w - SAMP_NEW_LANES] = rolled[:, :w - SAMP_NEW_LANES]
        dst[:, w - SAMP_NEW_LANES:] = jnp.where(is_new, new, rolled[:, w - SAMP_NEW_LANES:])

    s_c = jnp.dot(q, kt.astype(BF16), preferred_element_type=F32)
    s_n = jnp.dot(q, knt.astype(BF16), preferred_element_type=F32)
    es_c, es_n, dens, lses = [], [], [], []
    for n in range(len(DILATED_BRANCHES)):
        sc = s_c + bc_ref[n]
        sn = s_n + bn_ref[n]
        m = jnp.maximum(jnp.max(sc, axis=-1, keepdims=True), jnp.max(sn, axis=-1, keepdims=True))
        ec = jnp.exp(sc - m)
        en = jnp.exp(sn - m)
        den = jnp.sum(ec, axis=-1, keepdims=True) + jnp.sum(en, axis=-1, keepdims=True)
        es_c.append(ec)
        es_n.append(en)
        dens.append(den)
        lses.append(m + jnp.log(den))
    mx = functools.reduce(jnp.maximum, lses)
    ws = [jnp.exp(l - mx) for l in lses]
    tot = functools.reduce(lambda a, b: a + b, ws)
    p_c = None
    p_n = None
    for w, den, ec, en in zip(ws, dens, es_c, es_n):
        coef = w / (tot * den)
        p_c = coef * ec if p_c is None else p_c + coef * ec
        p_n = coef * en if p_n is None else p_n + coef * en
    nt = (((1,), (1,)), ((), ()))
    o = (lax.dot_general(p_c.astype(BF16), vt.astype(BF16), nt, preferred_element_type=F32)
         + lax.dot_general(p_n.astype(BF16), vnt.astype(BF16), nt, preferred_element_type=F32))
    o = o * hm_ref[...]
    o_ref[...] = jnp.sum(o.reshape(SUBLANES, N_HEADS_A, WIDTH_A), axis=1)


def _samp_bias_tables(rel_bias, w_buf, ds):
    tabs_c, tabs_n = [], []
    s = np.arange(SUBLANES)[:, None]
    live_s = s < ds
    first_new = SAMP_NEW_LANES - ds
    for window, dil in DILATED_BRANCHES:
        n_keys = window // dil
        for keys, live_k, tabs in ((np.arange(w_buf)[None, :], True, tabs_c),
                                   (w_buf - first_new + np.arange(SAMP_NEW_LANES)[None, :],
                                    np.arange(SAMP_NEW_LANES)[None, :] >= first_new, tabs_n)):
            n = keys.shape[1]
            dist = w_buf + s - keys
            valid = (dist >= 0) & (dist % dil == 0) & (dist // dil <= n_keys) & live_k
            bias = _bias_by_bucket(rel_bias, _t5_bucket(np.clip(dist, 0, None))).transpose(1, 0, 2)
            tab = jnp.where(jnp.asarray(valid)[:, None, :], bias, NEG_INF)
            pad = jnp.where(jnp.asarray(np.broadcast_to(live_k, dist.shape))[:, None, :], 0.0, NEG_INF)
            tab = jnp.where(jnp.asarray(live_s)[:, :, None], tab, pad)
            tabs.append(tab.reshape(SAMP_Q_ROWS, n))
    return jnp.stack(tabs_c), jnp.stack(tabs_n)


def _samp_attn(qa, cache_kt, cache_vt, knt, vnt, rel_bias, ds):
    DB, DS, _ = qa.shape
    W = cache_kt.shape[2]
    head_of_lane = np.arange(WIDTH_A) // HEAD_DIM_A
    hmask = (np.arange(SAMP_Q_ROWS)[:, None] % N_HEADS_A == head_of_lane[None, :])
    q8 = jnp.pad(qa, ((0, 0), (0, SUBLANES - DS), (0, 0)))
    q_rows = jnp.where(jnp.asarray(hmask)[None], jnp.repeat(q8, N_HEADS_A, axis=1), jnp.zeros((), BF16))
    bias_c, bias_n = _samp_bias_tables(rel_bias, W, DS)

    def per_b(rows, w):
        return pl.BlockSpec((None, rows, w), lambda b: (b, 0, 0))

    def const(a):
        return pl.BlockSpec(a.shape, lambda b: (0,) * a.ndim)

    hm = jnp.asarray(hmask, F32)
    return pl.pallas_call(
        functools.partial(_samp_attn_body, ds=ds),
        grid=(DB,),
        in_specs=[per_b(SAMP_Q_ROWS, WIDTH_A), per_b(WIDTH_A, W), per_b(WIDTH_A, W),
                  per_b(WIDTH_A, SAMP_NEW_LANES), per_b(WIDTH_A, SAMP_NEW_LANES),
                  const(bias_c), const(bias_n), const(hm)],
        out_specs=(per_b(SUBLANES, WIDTH_A), per_b(WIDTH_A, W), per_b(WIDTH_A, W)),
        out_shape=(jax.ShapeDtypeStruct((DB, SUBLANES, WIDTH_A), F32),
                   jax.ShapeDtypeStruct((DB, WIDTH_A, W), F32), jax.ShapeDtypeStruct((DB, WIDTH_A, W), F32)),
        compiler_params=_params(1),
        name="samp_attn",
    )(q_rows, cache_kt, cache_vt, knt, vnt, bias_c, bias_n, hm)


def _route_body(cat_ref, x_ref, wout_ref, g_ref, b_ref, rwh_ref, rwl_ref, rb_ref, tril_ref, base_ref,
                h_ref, hp_ref, route_ref, cnt_ref, *, alpha):
    @pl.when(pl.program_id(0) == 0)
    def _():
        cnt_ref[...] = base_ref[...]

    mix = jnp.dot(cat_ref[...], wout_ref[...], preferred_element_type=F32)
    y = alpha * x_ref[...] + mix
    mu = jnp.mean(y, axis=-1, keepdims=True)
    var = jnp.mean(jnp.square(y - mu), axis=-1, keepdims=True)
    h = (y - mu) * lax.rsqrt(var + LN_EPS) * g_ref[...] + b_ref[...]
    h_ref[...] = h
    hb = h.astype(BF16)
    hp_ref[...] = _pack_bf16_pairs(h)
    hl = (h - hb.astype(F32)).astype(BF16)
    logits = (jnp.dot(hb, rwh_ref[...], preferred_element_type=F32)
              + jnp.dot(hb, rwl_ref[...], preferred_element_type=F32)
              + jnp.dot(hl, rwh_ref[...], preferred_element_type=F32)) + rb_ref[...]

    tm = logits.shape[0]
    lane = lax.broadcasted_iota(jnp.int32, (tm, LANES), 1)
    work = logits
    vals, idxs = [], []
    for _ in range(TOP_K):
        m = jnp.max(work, axis=-1, keepdims=True)
        idx = jnp.min(jnp.where(work == m, lane, LANES), axis=-1, keepdims=True)
        vals.append(m)
        idxs.append(idx)
        work = jnp.where(lane == idx, -jnp.inf, work)
    es = [jnp.exp(v - vals[0]) for v in vals]
    tot = functools.reduce(lambda a, b: a + b, es)
    onehot = jnp.zeros((tm, LANES), F32)
    for idx in idxs:
        onehot = onehot + (lane == idx).astype(F32)
    before = jnp.dot(tril_ref[...], onehot.astype(BF16), preferred_element_type=F32) + cnt_ref[0:1, :]
    route = jnp.zeros((tm, LANES), F32)
    for k in range(TOP_K):
        rank = jnp.sum(jnp.where(lane == idxs[k], before, 0.0), axis=-1, keepdims=True)
        route = jnp.where(lane == k, idxs[k].astype(F32), route)
        route = jnp.where(lane == TOP_K + k, es[k] / tot, route)
        route = jnp.where(lane == 2 * TOP_K + k, rank, route)
    route_ref[...] = route
    cnt_ref[...] = cnt_ref[...] + jnp.sum(onehot, axis=0, keepdims=True)


def _out_route(cat, x2d, w_out_bf, ln_g, ln_b, rw_hi, rw_lo, rb, base, tm, alpha):
    T = x2d.shape[0]
    tril = jnp.asarray(np.tril(np.ones((tm, tm), np.float32), -1), BF16)

    def tok(i):
        return (i, 0)

    def const(a):
        return pl.BlockSpec(a.shape, lambda i: (0,) * a.ndim)

    ins = (cat, x2d, w_out_bf, ln_g, ln_b, rw_hi, rw_lo, rb, tril, base)
    in_specs = [pl.BlockSpec((tm, cat.shape[1]), tok), pl.BlockSpec((tm, D_MODEL), tok)] + [const(a) for a in ins[2:]]
    return pl.pallas_call(
        functools.partial(_route_body, alpha=alpha),
        grid=(T // tm,),
        in_specs=in_specs,
        out_specs=(pl.BlockSpec((tm, D_MODEL), tok), pl.BlockSpec((tm, D_MODEL // 2), tok),
                   pl.BlockSpec((tm, LANES), tok), pl.BlockSpec((SUBLANES, LANES), lambda i: (0, 0))),
        out_shape=(jax.ShapeDtypeStruct((T, D_MODEL), F32), jax.ShapeDtypeStruct((T, D_MODEL // 2), jnp.int32),
                   jax.ShapeDtypeStruct((T, LANES), F32), jax.ShapeDtypeStruct((SUBLANES, LANES), F32)),
        compiler_params=_params(1),
        name="out_route",
    )(*ins)


MOE_CAST_ROWS = 128


def _moe_body(be_ref, first_ref, next_ref, nused_ref, x_ref, bgu_ref, bdn_ref, wgu_hbm, wdn_hbm, y_ref,
              gu_stage, dn_stage, wgu_bf, wdn_bf, sem):
    b = pl.program_id(0)
    d_exp = wdn_bf.shape[0]

    def fetch(e):
        return (pltpu.make_async_copy(wgu_hbm.at[e], gu_stage, sem.at[0]),
                pltpu.make_async_copy(wdn_hbm.at[e], dn_stage, sem.at[1]))

    @pl.when(b == 0)
    def _():
        for c in fetch(be_ref[0]):
            c.start()

    @pl.when(first_ref[b] == 1)
    def _():
        for c in fetch(be_ref[b]):
            c.wait()

        def cast_gu(i, c):
            r = pl.ds(pl.multiple_of(i * MOE_CAST_ROWS, MOE_CAST_ROWS), MOE_CAST_ROWS)
            wgu_bf[r, :] = gu_stage[r, :].astype(BF16)
            return c

        def cast_dn(i, c):
            r = pl.ds(pl.multiple_of(i * MOE_CAST_ROWS, MOE_CAST_ROWS), MOE_CAST_ROWS)
            wdn_bf[r, :] = dn_stage[r, :].astype(BF16)
            return c

        lax.fori_loop(0, gu_stage.shape[0] // MOE_CAST_ROWS, cast_gu, 0)
        lax.fori_loop(0, d_exp // MOE_CAST_ROWS, cast_dn, 0)

        @pl.when(next_ref[b] >= 0)
        def _():
            for c in fetch(next_ref[b]):
                c.start()

    @pl.when(b < nused_ref[0])
    def _():
        x_lo, x_hi = _unpack_bf16_pairs(x_ref[...])
        x_lo = x_lo.astype(BF16)
        x_hi = x_hi.astype(BF16)
        dh = x_lo.shape[1]

        def xw(cols):
            return (jnp.dot(x_lo, wgu_bf[:dh, cols], preferred_element_type=F32)
                    + jnp.dot(x_hi, wgu_bf[dh:, cols], preferred_element_type=F32) + bgu_ref[:, cols])

        half = d_exp // 2
        y = None
        for c in range(2):
            lo = c * half
            gate = jnp.minimum(xw(slice(lo, lo + half)), SWIGLU_LIMIT)
            up = jnp.clip(xw(slice(d_exp + lo, d_exp + lo + half)), -SWIGLU_LIMIT, SWIGLU_LIMIT)
            act = (up + 1.0) * gate * (1.0 / (1.0 + jnp.exp(-SWIGLU_ALPHA * gate)))
            part = jnp.dot(act.astype(BF16), wdn_bf[lo:lo + half, :], preferred_element_type=F32)
            y = part if y is None else y + part
        y_ref[...] = _pack_bf16_pairs(y + bdn_ref[...])

    @pl.when(b >= nused_ref[0])
    def _():
        y_ref[...] = jnp.zeros_like(y_ref)


def _moe_ffn(xs, rows, block_e, n_used, has_rows, w_gu, b_gu, w_dn, b_dn):
    E, D, two_de = w_gu.shape
    d_exp = two_de // 2
    nb = rows // MOE_BLOCK
    idx = jnp.arange(nb, dtype=jnp.int32)
    first = ((idx < n_used[0]) & ((idx == 0) | (block_e != jnp.roll(block_e, 1)))).astype(jnp.int32)
    ids = jnp.arange(E, dtype=jnp.int32)
    later = jnp.where((ids[None, :] > ids[:, None]) & has_rows[None, :], ids[None, :], E).min(axis=1)
    next_of = jnp.where(later == E, -1, later).astype(jnp.int32)
    next_e = jnp.sum(jnp.where(block_e[:, None] == ids[None, :], next_of[None, :], 0), axis=1).astype(jnp.int32)
    grid_spec = pltpu.PrefetchScalarGridSpec(
        num_scalar_prefetch=4,
        grid=(nb,),
        in_specs=[
            pl.BlockSpec((MOE_BLOCK, D // 2), lambda b, be, fi, nx, nu: (b, 0)),
            pl.BlockSpec((None, 1, two_de), lambda b, be, fi, nx, nu: (be[b], 0, 0)),
            pl.BlockSpec((None, 1, D), lambda b, be, fi, nx, nu: (be[b], 0, 0)),
            pl.BlockSpec(memory_space=pl.ANY),
            pl.BlockSpec(memory_space=pl.ANY),
        ],
        out_specs=pl.BlockSpec((MOE_BLOCK, D // 2), lambda b, be, fi, nx, nu: (b, 0)),
        scratch_shapes=[pltpu.VMEM((D, two_de), F32), pltpu.VMEM((d_exp, D), F32),
                        pltpu.VMEM((D, two_de), BF16), pltpu.VMEM((d_exp, D), BF16),
                        pltpu.SemaphoreType.DMA((2,))],
    )
    return pl.pallas_call(
        _moe_body,
        grid_spec=grid_spec,
        out_shape=jax.ShapeDtypeStruct((rows, D // 2), jnp.int32),
        compiler_params=_params(1),
        name="moe_ffn",
    )(block_e, first, next_e, n_used, xs, b_gu.reshape(E, 1, two_de), b_dn.reshape(E, 1, D), w_gu, w_dn)


def _ple_body(h_ref, ys_ref, route_ref, p_ref, g_ref, b_ref, wpg_ref, bpg_ref, wpp_ref, o_ref, *, alpha):
    route = route_ref[...]
    f_lo = None
    f_hi = None
    for k in range(TOP_K):
        lo, hi = _unpack_bf16_pairs(ys_ref[k])
        g = route[:, TOP_K + k:TOP_K + k + 1]
        f_lo = g * lo if f_lo is None else f_lo + g * lo
        f_hi = g * hi if f_hi is None else f_hi + g * hi
    y = alpha * h_ref[...] + jnp.concatenate([f_lo, f_hi], axis=1)
    mu = jnp.mean(y, axis=-1, keepdims=True)
    var = jnp.mean(jnp.square(y - mu), axis=-1, keepdims=True)
    h2 = (y - mu) * lax.rsqrt(var + LN_EPS) * g_ref[...] + b_ref[...]
    z = jnp.dot(h2.astype(BF16), wpg_ref[...], preferred_element_type=F32) + bpg_ref[...]
    gate = 1.0 / (1.0 + jnp.exp(-z))
    proj = jnp.dot(p_ref[...].astype(BF16), wpp_ref[...], preferred_element_type=F32)
    o_ref[...] = h2 + gate * proj


def _ffn_ple(h, y_slots, route, p, ln_g, ln_b, w_pg_bf, b_pg, w_pp_bf, tm, tile0, alpha):
    T = h.shape[0]

    def tok(i):
        return (i, 0)

    def const(a):
        return pl.BlockSpec(a.shape, lambda i: (0,) * a.ndim)

    consts = (ln_g, ln_b, w_pg_bf, b_pg, w_pp_bf)
    return pl.pallas_call(
        functools.partial(_ple_body, alpha=alpha),
        grid=(T // tm,),
        in_specs=[pl.BlockSpec((tm, D_MODEL), tok),
                  pl.BlockSpec((TOP_K, tm, D_MODEL // 2), lambda i: (0, tile0 + i, 0)),
                  pl.BlockSpec((tm, LANES), tok),
                  pl.BlockSpec((tm, p.shape[1]), tok)] + [const(a) for a in consts],
        out_specs=pl.BlockSpec((tm, D_MODEL), tok),
        out_shape=jax.ShapeDtypeStruct((T, D_MODEL), F32),
        compiler_params=_params(1),
        name="ffn_ple",
    )(h, y_slots, route, p, *consts)


def _row(v):
    return v.reshape(1, -1).astype(F32)


def kernel(x_prompt, x_sample, cache_win_k, cache_win_v, state_ret, p_prompt, p_sample, rel_bias, w_in, w_out,
           ln1_g, ln1_b, router_w, router_b, w_gate_up, b_gate_up, w_down, b_down, ln2_g, ln2_b,
           w_ple_gate, b_ple_gate, w_ple_proj):
    B, S, D = x_prompt.shape
    DB, DS, _ = x_sample.shape
    depth = w_in.shape[0]
    w_buf = cache_win_k.shape[2]
    n_exp = router_w.shape[-1]
    alpha = (2.0 * depth) ** 0.25
    assert depth == 1 and D == D_MODEL
    assert S % (BLK * MAX_DIL) == 0 and S >= WINDOW_MAX and w_buf == WINDOW_MAX and DS <= SUBLANES
    tm_p = 512
    Tp, Ts = B * S, DB * DS
    assert Tp % tm_p == 0 and Ts % SUBLANES == 0

    i = 0
    w_in_bf = w_in[i].astype(BF16)
    w_out_bf = w_out[i].astype(BF16)
    w_pg_bf = w_ple_gate[i].astype(BF16)
    w_pp_bf = w_ple_proj[i].astype(BF16)
    rw = jnp.pad(router_w[i], ((0, 0), (0, LANES - n_exp)))
    rw_hi = rw.astype(BF16)
    rw_lo = (rw - rw_hi.astype(F32)).astype(BF16)
    rb = jnp.pad(router_b[i], (0, LANES - n_exp), constant_values=NEG_INF).reshape(1, LANES)

    cos_p, sin_p = _rotary_tables(jnp.arange(S, dtype=jnp.int32))
    dils = tuple(d for _, d in DILATED_BRANCHES)
    extra = tuple(d for d in dils if d > 1)
    outs = _in_proj(x_prompt.reshape(Tp, D), w_in_bf, cos_p, sin_p, tm_p, S // tm_p, WINDOW_MAX // tm_p, extra)
    qa, ka, va, qb, kb, vb, gb, kf, vf = outs[:9]
    qkv = {1: (qa, ka, va)}
    for t, d in enumerate(extra):
        qkv[d] = tuple(outs[9 + j * len(extra) + t] for j in range(3))

    def seq(t):
        return t.reshape(B, S, t.shape[-1])

    outs_a, lses = [], []
    for window, dil in DILATED_BRANCHES:
        q_d, k_d, v_d = (t.reshape(B, S // dil, dil * WIDTH_A) for t in qkv[dil])
        o_n, l_n = _dilated_branch(q_d, k_d, v_d, _attn_bias_tables(rel_bias, window, dil), dil)
        outs_a.append(o_n)
        lses.append(l_n)
    st_zero = jnp.zeros((B, N_HEADS_B, KEY_DIM_B, VAL_DIM_B), F32)
    cat_p, rst_p = _ret_mix(seq(qb), seq(kb), seq(vb), seq(gb), (outs_a, lses, dils), st_zero, RET_CHUNK)
    base0 = jnp.zeros((SUBLANES, LANES), F32)
    h_p, hp_p, route_p, cnt_p = _out_route(cat_p.reshape(Tp, -1), x_prompt.reshape(Tp, D), w_out_bf,
                                           _row(ln1_g[i]), _row(ln1_b[i]), rw_hi, rw_lo, rb, base0, tm_p, alpha)

    pos_s = jnp.tile(PAST_LEN + jnp.arange(DS, dtype=jnp.int32), DB)
    cos_s, sin_s = _rotary_tables(pos_s)
    qa_s, _, _, qb_s, kb_s, vb_s, gb_s, kf_s, vf_s = _in_proj(x_sample.reshape(Ts, D), w_in_bf, cos_s, sin_s,
                                                             Ts, 1, 1)

    def positions_minor(t):
        return jnp.transpose(t, (0, 2, 3, 1)).reshape(DB, WIDTH_A, t.shape[1])

    def positions_major(t):
        return jnp.transpose(t.reshape(t.shape[0], N_HEADS_A, HEAD_DIM_A, t.shape[2]), (0, 3, 1, 2))[None]

    def new_columns(t):
        t = jnp.transpose(t.reshape(WIDTH_A, DB, DS), (1, 0, 2))
        return jnp.pad(t, ((0, 0), (0, 0), (SAMP_NEW_LANES - DS, 0)))

    oa_s, kt_out, vt_out = _samp_attn(qa_s.reshape(DB, DS, WIDTH_A), positions_minor(cache_win_k[i]),
                                      positions_minor(cache_win_v[i]), new_columns(kf_s), new_columns(vf_s),
                                      rel_bias, DS)

    def pad_rows(t, rows):
        t = t.reshape(DB, -1, t.shape[-1])
        return jnp.pad(t, ((0, 0), (0, rows - t.shape[1]), (0, 0)))

    cat_s, rst_s = _ret_mix(pad_rows(qb_s, RET_CHUNK), pad_rows(kb_s, RET_CHUNK), pad_rows(vb_s, RET_CHUNK),
                            pad_rows(gb_s, RET_CHUNK), pad_rows(oa_s, RET_CHUNK), state_ret[i].astype(F32), DS)
    cat_s = cat_s[:, :DS].reshape(Ts, -1)
    h_s, hp_s, route_s, cnt = _out_route(cat_s, x_sample.reshape(Ts, D), w_out_bf, _row(ln1_g[i]), _row(ln1_b[i]),
                                         rw_hi, rw_lo, rb, cnt_p, Ts, alpha)

    T = Tp + Ts
    t_align = SC_WORKERS * SC_ALIGN
    T_pad = -(-T // t_align) * t_align
    route = jnp.concatenate([route_p, route_s], axis=0)
    top_idx = route[:, :TOP_K].astype(jnp.int32)
    rank = route[:, 2 * TOP_K:3 * TOP_K].astype(jnp.int32)
    counts = cnt[0, :n_exp].astype(jnp.int32)
    padded = (counts + MOE_BLOCK - 1) // MOE_BLOCK * MOE_BLOCK
    pad_end = jnp.cumsum(padded)
    pad_start = pad_end - padded
    dest = pad_start[top_idx] + rank
    n_blocks = -(-T * TOP_K // MOE_BLOCK) + n_exp
    rows = n_blocks * MOE_BLOCK
    block_start = jnp.arange(n_blocks, dtype=jnp.int32) * MOE_BLOCK
    block_e = jnp.minimum(jnp.sum(pad_end[None, :] <= block_start[:, None], axis=1), n_exp - 1).astype(jnp.int32)
    n_used = (pad_end[-1:] // MOE_BLOCK).astype(jnp.int32)
    n_fill = T_pad - T
    spare = rows + jnp.arange(n_fill * TOP_K, dtype=jnp.int32).reshape(n_fill, TOP_K)
    dest_sc = jnp.concatenate([dest, spare], axis=0).T.reshape(-1)
    dest_ga = jnp.concatenate([dest, jnp.zeros((n_fill, TOP_K), jnp.int32)], axis=0).T.reshape(-1)
    hp_all = jnp.concatenate([hp_p, hp_s, jnp.zeros((n_fill, D // 2), jnp.int32)], axis=0)
    xs = _sc_scatter_rows(hp_all, dest_sc, rows + n_fill * TOP_K)
    ys = _moe_ffn(xs, rows, block_e, n_used, padded > 0, w_gate_up[i], b_gate_up[i], w_down[i], b_down[i])
    y_slots = _sc_gather_rows(ys, dest_ga).reshape(TOP_K, T_pad, D // 2)

    ple_args = (_row(ln2_g[i]), _row(ln2_b[i]), w_pg_bf, _row(b_ple_gate[i]), w_pp_bf)
    assert Tp % Ts == 0
    y_p = _ffn_ple(h_p, y_slots, route_p, p_prompt[i].reshape(Tp, D_PLE), *ple_args, tm_p, 0, alpha)
    y_s = _ffn_ple(h_s, y_slots, route_s, p_sample[i].reshape(Ts, D_PLE), *ple_args, Ts, Tp // Ts, alpha)

    return (y_p.reshape(B, S, D), y_s.reshape(DB, DS, D), positions_major(kf), positions_major(vf),
            rst_p[None], positions_major(kt_out), positions_major(vt_out), rst_s[None])
```

```python
import functools

import numpy as np
import jax
import jax.numpy as jnp
from jax import lax
from jax.experimental import pallas as pl
from jax.experimental.pallas import tpu as pltpu
from jax.experimental.pallas import tpu_sc as plsc

F32 = jnp.float32
BF16 = jnp.bfloat16

D_MODEL = 1024
D_PLE = 256
N_HEADS_A = 8
HEAD_DIM_A = 64
WIDTH_A = N_HEADS_A * HEAD_DIM_A
DILATED_BRANCHES = ((128, 1), (512, 4), (2048, 16))
BLK = 128
WINDOW_MAX = 2048
MAX_DIL = 16
NUM_BUCKETS = 32
MAX_DISTANCE = 2048
N_HEADS_B = 4
KEY_DIM_B = 64
VAL_DIM_B = 128
QK_WIDTH_B = N_HEADS_B * KEY_DIM_B
WIDTH_B = N_HEADS_B * VAL_DIM_B
RET_CHUNK = 128
GN_EPS = 1e-6
TOP_K = 4
SWIGLU_LIMIT = 7.0
SWIGLU_ALPHA = 1.702
LN_EPS = 1e-5
NEG_INF = -1e30
PAST_LEN = 16384
MOE_BLOCK = 512
LANES = 128
SUBLANES = 8
VMEM_LIMIT = 52 * 1024 * 1024


def _params(n_axes, vmem=VMEM_LIMIT):
    return pltpu.CompilerParams(dimension_semantics=("arbitrary",) * n_axes, vmem_limit_bytes=vmem)


def _t5_bucket(dist):
    dist = np.asarray(dist, dtype=np.int32)
    max_exact = NUM_BUCKETS // 2
    d = np.maximum(dist, 1).astype(np.float32)
    large = max_exact + (np.log(d / max_exact) / np.log(MAX_DISTANCE / max_exact)
                         * (NUM_BUCKETS - max_exact)).astype(np.int32)
    large = np.minimum(large, NUM_BUCKETS - 1)
    return np.where(dist < max_exact, dist, large).astype(np.int32)


def _bias_by_bucket(rel_bias, buckets):
    b = jnp.asarray(buckets, jnp.int32)
    ids = jnp.arange(NUM_BUCKETS, dtype=jnp.int32).reshape((NUM_BUCKETS, 1) + (1,) * b.ndim)
    vals = rel_bias.astype(F32).reshape((NUM_BUCKETS, rel_bias.shape[1]) + (1,) * b.ndim)
    return jnp.sum(jnp.where(b[None, None] == ids, vals, 0.0), axis=0)


def _pack_bf16_pairs(v):
    w = v.shape[1] // 2
    lo = lax.bitcast_convert_type(v[:, :w].astype(BF16).astype(F32), jnp.uint32) >> 16
    hi = lax.bitcast_convert_type(v[:, w:].astype(BF16).astype(F32), jnp.uint32) & jnp.uint32(0xFFFF0000)
    return lax.bitcast_convert_type(lo | hi, jnp.int32)


def _unpack_bf16_pairs(p):
    u = lax.bitcast_convert_type(p, jnp.uint32)
    lo = lax.bitcast_convert_type(u << 16, F32)
    hi = lax.bitcast_convert_type(u & jnp.uint32(0xFFFF0000), F32)
    return lo, hi


SC_CORES = 2
SC_SUBCORES = 16
SC_WORKERS = SC_CORES * SC_SUBCORES
SC_ALIGN = 8
SC_CHUNK_ROWS = 80


def _sc_mesh():
    return plsc.VectorSubcoreMesh(core_axis_name="c", subcore_axis_name="s")


def _sc_chunk(per_worker):
    c = max(d for d in range(SC_ALIGN, SC_CHUNK_ROWS + 1, SC_ALIGN) if per_worker % d == 0)
    return c


def _sc_scatter_rows(src, dest_flat, n_out):
    T, W = src.shape
    K = dest_flat.shape[0] // T
    per_w = T // SC_WORKERS
    assert per_w * SC_WORKERS == T and per_w % SC_ALIGN == 0
    chunk = _sc_chunk(per_w)
    n_chunks = per_w // chunk

    @functools.partial(
        pl.kernel, mesh=_sc_mesh(), out_type=jax.ShapeDtypeStruct((n_out, W), src.dtype),
        scratch_types=[pltpu.VMEM((chunk, W), src.dtype)] * 2 + [pltpu.VMEM((chunk,), jnp.int32)] * (2 * K)
        + [pltpu.SemaphoreType.DMA] * 2,
        name="sc_scatter_rows")
    def k(src_hbm, dest_hbm, out_hbm, rows_a, rows_b, *rest):
        bufs = ((rows_a, rest[:K], rest[2 * K]), (rows_b, rest[K:2 * K], rest[2 * K + 1]))
        base = (lax.axis_index("s") * SC_CORES + lax.axis_index("c")) * per_w

        def scatters(buf):
            rows_v, idx_vs, sem = buf
            return [pltpu.make_async_copy(rows_v, out_hbm.at[idx_vs[kk]], sem) for kk in range(K)]

        def step(j, buf):
            rows_v, idx_vs, _ = buf

            @pl.when(j >= 2)
            def _():
                for c in scatters(buf):
                    c.wait()

            off = pl.multiple_of(base + j * chunk, SC_ALIGN)
            pltpu.sync_copy(src_hbm.at[pl.ds(off, chunk)], rows_v)
            for kk in range(K):
                pltpu.sync_copy(dest_hbm.at[pl.ds(kk * T + off, chunk)], idx_vs[kk])
            for c in scatters(buf):
                c.start()

        @pl.loop(0, n_chunks)
        def _(j):
            for parity in range(2):
                @pl.when(j % 2 == parity)
                def _():
                    step(j, bufs[parity])

        for j in range(max(n_chunks - 2, 0), n_chunks):
            for c in scatters(bufs[j % 2]):
                c.wait()

    return k(src, dest_flat)


def _sc_gather_rows(table, idx):
    B = idx.shape[0]
    W = table.shape[1]
    per_w = B // SC_WORKERS
    assert per_w * SC_WORKERS == B and per_w % SC_ALIGN == 0
    chunk = _sc_chunk(per_w)
    n_chunks = per_w // chunk

    @functools.partial(
        pl.kernel, mesh=_sc_mesh(), out_type=jax.ShapeDtypeStruct((B, W), table.dtype),
        scratch_types=[pltpu.VMEM((chunk,), jnp.int32)] * 2 + [pltpu.VMEM((chunk, W), table.dtype)] * 2
        + [pltpu.SemaphoreType.DMA] * 2,
        name="sc_gather_rows")
    def k(table_hbm, idx_hbm, out_hbm, idx_a, idx_b, rows_a, rows_b, sem_a, sem_b):
        bufs = ((idx_a, rows_a, sem_a), (idx_b, rows_b, sem_b))
        base = (lax.axis_index("s") * SC_CORES + lax.axis_index("c")) * per_w

        def gather(buf):
            idx_v, rows_v, sem = buf
            return pltpu.make_async_copy(table_hbm.at[idx_v], rows_v, sem)

        def finish(j, buf):
            gather(buf).wait()
            off = pl.multiple_of(base + j * chunk, SC_ALIGN)
            pltpu.sync_copy(buf[1], out_hbm.at[pl.ds(off, chunk)])

        def step(j, buf, other):
            off = pl.multiple_of(base + j * chunk, SC_ALIGN)
            pltpu.sync_copy(idx_hbm.at[pl.ds(off, chunk)], buf[0])
            gather(buf).start()

            @pl.when(j >= 1)
            def _():
                finish(j - 1, other)

        @pl.loop(0, n_chunks)
        def _(j):
            for parity in range(2):
                @pl.when(j % 2 == parity)
                def _():
                    step(j, bufs[parity], bufs[1 - parity])

        finish(n_chunks - 1, bufs[(n_chunks - 1) % 2])

    return k(table, idx)


def _in_proj_body(x_ref, w_ref, cos_ref, sin_ref, *refs, seq_tiles, first_win, dilations):
    qa_ref, ka_ref, va_ref, qb_ref, kb_ref, vb_ref, gb_ref, kt_ref, vt_ref = refs[:9]
    n_d = len(dilations)
    dil_refs = [refs[9 + t * n_d:9 + (t + 1) * n_d] for t in range(3)]
    zs_ref = refs[9 + 3 * n_d] if n_d else None
    x = x_ref[...].astype(BF16)
    in_window = pl.program_id(0) % seq_tiles >= first_win

    def emit(z, token_ref, class_refs):
        token_ref[...] = z.astype(BF16)
        if not class_refs:
            return
        for c in range(WIDTH_A // LANES):
            zs_ref[c] = z[:, c * LANES:(c + 1) * LANES]
        for d, ref in zip(dilations, class_refs):
            n = z.shape[0] // d
            for r in range(d):
                for c in range(WIDTH_A // LANES):
                    lo = r * WIDTH_A + c * LANES
                    ref[:, lo:lo + LANES] = zs_ref[c, pl.ds(r, n, stride=d), :].astype(BF16)

    def proj(lo, hi):
        return jnp.dot(x, w_ref[:, lo:hi], preferred_element_type=F32)

    o = 0
    emit(proj(o, o + WIDTH_A) * (HEAD_DIM_A ** -0.5), qa_ref, dil_refs[0])
    o += WIDTH_A
    ka = proj(o, o + WIDTH_A)
    emit(ka, ka_ref, dil_refs[1])

    @pl.when(in_window)
    def _():
        kt_ref[...] = ka.T
    o += WIDTH_A
    va = proj(o, o + WIDTH_A)
    emit(va, va_ref, dil_refs[2])

    @pl.when(in_window)
    def _():
        vt_ref[...] = va.T
    o += WIDTH_A

    cos = cos_ref[...]
    sin = sin_ref[...]
    lane = lax.broadcasted_iota(jnp.int32, cos.shape, 1)
    first_half = (lane % KEY_DIM_B) < (KEY_DIM_B // 2)

    def rot(z):
        sw = jnp.where(first_half, pltpu.roll(z, QK_WIDTH_B - KEY_DIM_B // 2, 1), pltpu.roll(z, KEY_DIM_B // 2, 1))
        return z * cos + sw * sin

    qb_ref[...] = rot(proj(o, o + QK_WIDTH_B)).astype(BF16)
    o += QK_WIDTH_B
    kb_ref[...] = (rot(proj(o, o + QK_WIDTH_B)) * (KEY_DIM_B ** -0.5)).astype(BF16)
    o += QK_WIDTH_B
    vb_ref[...] = proj(o, o + WIDTH_B).astype(BF16)
    o += WIDTH_B
    gb_ref[...] = proj(o, o + WIDTH_B).astype(BF16)


def _in_proj(x2d, w_bf, cos_t, sin_t, tm, seq_tiles, win_tiles, dilations=()):
    T = x2d.shape[0]
    nt = T // tm
    n_seq = nt // seq_tiles
    j0 = seq_tiles - win_tiles

    def tok(i):
        return (i, 0)

    def tab(i):
        return (i % seq_tiles, 0)

    def win(i):
        return (i // seq_tiles, 0, jnp.maximum(i % seq_tiles - j0, 0))

    def tspec(w):
        return pl.BlockSpec((tm, w), tok)

    out_shape = (
        jax.ShapeDtypeStruct((T, WIDTH_A), BF16), jax.ShapeDtypeStruct((T, WIDTH_A), BF16),
        jax.ShapeDtypeStruct((T, WIDTH_A), BF16),
        jax.ShapeDtypeStruct((T, QK_WIDTH_B), BF16), jax.ShapeDtypeStruct((T, QK_WIDTH_B), BF16),
        jax.ShapeDtypeStruct((T, WIDTH_B), BF16), jax.ShapeDtypeStruct((T, WIDTH_B), BF16),
        jax.ShapeDtypeStruct((n_seq, WIDTH_A, win_tiles * tm), F32),
        jax.ShapeDtypeStruct((n_seq, WIDTH_A, win_tiles * tm), F32),
    ) + tuple(jax.ShapeDtypeStruct((T // d, d * WIDTH_A), BF16) for _ in range(3) for d in dilations)
    class_specs = tuple(pl.BlockSpec((tm // d, d * WIDTH_A), tok) for _ in range(3) for d in dilations)
    return pl.pallas_call(
        functools.partial(_in_proj_body, seq_tiles=seq_tiles, first_win=j0, dilations=tuple(dilations)),
        grid=(nt,),
        in_specs=[tspec(D_MODEL), pl.BlockSpec(w_bf.shape, lambda i: (0, 0)),
                  pl.BlockSpec((tm, QK_WIDTH_B), tab), pl.BlockSpec((tm, QK_WIDTH_B), tab)],
        out_specs=(tspec(WIDTH_A), tspec(WIDTH_A), tspec(WIDTH_A), tspec(QK_WIDTH_B), tspec(QK_WIDTH_B),
                   tspec(WIDTH_B), tspec(WIDTH_B),
                   pl.BlockSpec((None, WIDTH_A, tm), win), pl.BlockSpec((None, WIDTH_A, tm), win)) + class_specs,
        out_shape=out_shape,
        scratch_shapes=[pltpu.VMEM((WIDTH_A // LANES, tm, LANES), F32)] if dilations else [],
        compiler_params=_params(1),
        name="in_proj",
    )(x2d, w_bf, cos_t, sin_t)


def _rotary_tables(pos):
    half = KEY_DIM_B // 2
    inv_freq = 1.0 / (10000.0 ** jnp.linspace(0.0, 1.0, half, dtype=F32))
    ang = pos.astype(F32)[:, None] * inv_freq[None, :]
    cos = jnp.cos(ang)
    sin = jnp.sin(ang)
    cos_h = jnp.concatenate([cos, cos], axis=-1)
    sin_h = jnp.concatenate([-sin, sin], axis=-1)
    return jnp.tile(cos_h, (1, N_HEADS_B)), jnp.tile(sin_h, (1, N_HEADS_B))


ATTN_BLOCKS_PER_STEP = 4


def _attn_body(q_ref, k_ref, v_ref, bias_ref, o_ref, lse_ref, kb_ref, vb_ref):
    n = pl.program_id(2)
    rows = q_ref.shape[0]

    @pl.when(n == 0)
    def _():
        kb_ref[:BLK, :] = jnp.zeros((BLK, WIDTH_A), BF16)
        vb_ref[:BLK, :] = jnp.zeros((BLK, WIDTH_A), BF16)

    kb_ref[BLK:, :] = k_ref[...]
    vb_ref[BLK:, :] = v_ref[...]
    lane = lax.broadcasted_iota(jnp.int32, (BLK, LANES), 1)
    low = lane < HEAD_DIM_A
    nt = (((1,), (1,)), ((), ()))

    def sub_block(j, carry):
        r0 = pl.multiple_of(j * BLK, BLK)
        table = jnp.where((n == 0) & (j == 0), 0, 1)
        lse_tile = jnp.zeros((BLK, LANES), F32)
        for p in range(N_HEADS_A // 2):
            cs = slice(p * LANES, (p + 1) * LANES)
            qp = q_ref[pl.ds(r0, BLK), cs]
            kp = kb_ref[pl.ds(r0, 2 * BLK), cs]
            vp = vb_ref[pl.ds(r0, 2 * BLK), cs]
            zero = jnp.zeros_like(qp)
            outs = []
            for half, qh in enumerate((jnp.where(low, qp, zero), jnp.where(low, zero, qp))):
                h = 2 * p + half
                s = lax.dot_general(qh, kp, nt, preferred_element_type=F32) + bias_ref[table, h]
                m = jnp.max(s, axis=-1, keepdims=True)
                e = jnp.exp(s - m)
                den = jnp.sum(e, axis=-1, keepdims=True)
                outs.append(jnp.dot(e.astype(BF16), vp, preferred_element_type=F32) / den)
                lse_tile = jnp.where(lane == h, m + jnp.log(den), lse_tile)
            o_ref[pl.ds(r0, BLK), cs] = jnp.where(low, outs[0], outs[1]).astype(o_ref.dtype)
        lse_ref[pl.ds(r0, BLK), :] = lse_tile
        return carry

    lax.fori_loop(0, rows // BLK, sub_block, 0)
    kb_ref[:BLK, :] = k_ref[rows - BLK:, :]
    vb_ref[:BLK, :] = v_ref[rows - BLK:, :]


def _attn_bias_tables(rel_bias, window, dil):
    n_keys = window // dil
    i = np.arange(BLK)[:, None]
    j = np.arange(2 * BLK)[None, :]
    rel = BLK + i - j
    in_band = (rel >= 0) & (rel <= n_keys)
    bias = _bias_by_bucket(rel_bias, _t5_bucket(np.clip(rel, 0, None) * dil))
    later = jnp.where(jnp.asarray(in_band)[None], bias, NEG_INF)
    first = jnp.where(jnp.asarray(in_band & (j >= BLK))[None], bias, NEG_INF)
    return jnp.stack([first, later])


def _dilated_branch(q, k, v, bias_tab, dil):
    B, L, _ = q.shape
    rows = min(ATTN_BLOCKS_PER_STEP, L // BLK) * BLK
    assert L % rows == 0
    o_dtype = BF16 if RET_CHUNK // dil >= 16 else F32

    def cls(b, r, n):
        return (b, n, r)

    qkv_spec = pl.BlockSpec((None, rows, WIDTH_A), cls)
    o, lse = pl.pallas_call(
        _attn_body,
        grid=(B, dil, L // rows),
        in_specs=[qkv_spec, qkv_spec, qkv_spec,
                  pl.BlockSpec(bias_tab.shape, lambda b, r, n: (0, 0, 0, 0))],
        out_specs=(qkv_spec, pl.BlockSpec((None, rows, LANES), cls)),
        out_shape=(jax.ShapeDtypeStruct((B, L, dil * WIDTH_A), o_dtype),
                   jax.ShapeDtypeStruct((B, L, dil * LANES), F32)),
        scratch_shapes=[pltpu.VMEM((BLK + rows, WIDTH_A), BF16), pltpu.VMEM((BLK + rows, WIDTH_A), BF16)],
        compiler_params=_params(3),
        name=f"dil_attn_d{dil}",
    )(q, k, v, bias_tab)
    return o, lse


def _ret_body(*refs, dilations):
    n_branch = len(dilations)
    qb_ref, kb_ref, vb_ref, gb_ref = refs[:4]
    p = 4
    if n_branch:
        o_refs = refs[p:p + n_branch]
        l_refs = refs[p + n_branch:p + 2 * n_branch]
        exp_ref = refs[p + 2 * n_branch]
        p += 2 * n_branch + 1
    else:
        oa_ref = refs[p]
        p += 1
    st0_ref, dmat_ref, qdec_ref, kdec_ref, cdec_ref, cat_ref, sto_ref, st_ref = refs[p:p + 8]
    if n_branch:
        us_ref, ls_ref = refs[p + 8:]

    def lse_token_order(l_ref, d):
        if d == 1:
            return l_ref[...]
        n = l_ref.shape[0]
        for r in range(d):
            ls_ref[pl.ds(r, n, stride=d), :] = l_ref[:, r * LANES:(r + 1) * LANES]
        return ls_ref[...]

    def out_token_order(o_ref, d):
        if d == 1:
            return o_ref[...].astype(F32)
        n = o_ref.shape[0]
        for r in range(d):
            for c in range(WIDTH_A // LANES):
                lo = r * WIDTH_A + c * LANES
                us_ref[c, pl.ds(r, n, stride=d), :] = o_ref[:, lo:lo + LANES].astype(F32)
        return jnp.concatenate([us_ref[c] for c in range(WIDTH_A // LANES)], axis=1)

    @pl.when(pl.program_id(1) == 0)
    def _():
        st_ref[...] = st0_ref[...]

    if n_branch:
        ls = [lse_token_order(l_ref, d) for l_ref, d in zip(l_refs, dilations)]
        mx = functools.reduce(jnp.maximum, ls)
        ws = [jnp.exp(l - mx) for l in ls]
        tot = functools.reduce(lambda a, b: a + b, ws)
        oa = None
        for w, o_ref, d in zip(ws, o_refs, dilations):
            o_tok = out_token_order(o_ref, d)
            w = w / tot
            w_hi = w.astype(BF16)
            w_lo = (w - w_hi.astype(F32)).astype(BF16)
            w_full = (jnp.dot(w_hi, exp_ref[...], preferred_element_type=F32)
                      + jnp.dot(w_lo, exp_ref[...], preferred_element_type=F32))
            term = w_full * o_tok
            oa = term if oa is None else oa + term
        cat_ref[:, :WIDTH_A] = oa.astype(BF16)
    else:
        cat_ref[:, :WIDTH_A] = oa_ref[...].astype(BF16)

    for h in range(N_HEADS_B):
        ks = slice(h * KEY_DIM_B, (h + 1) * KEY_DIM_B)
        vs = slice(h * VAL_DIM_B, (h + 1) * VAL_DIM_B)
        q = qb_ref[:, ks]
        k = kb_ref[:, ks]
        v = vb_ref[:, vs]
        st = st_ref[h]
        a = lax.dot_general(q, k, (((1,), (1,)), ((), ())), preferred_element_type=F32) * dmat_ref[h]
        o = (jnp.dot(a.astype(BF16), v, preferred_element_type=F32)
             + jnp.dot(q, st.astype(BF16), preferred_element_type=F32) * qdec_ref[h])
        kd = (k.astype(F32) * kdec_ref[h]).astype(BF16)
        st_new = st * cdec_ref[h] + lax.dot_general(kd, v, (((0,), (0,)), ((), ())), preferred_element_type=F32)
        st_ref[h] = st_new
        sto_ref[h] = st_new
        mu = jnp.mean(o, axis=-1, keepdims=True)
        var = jnp.mean(jnp.square(o - mu), axis=-1, keepdims=True)
        obn = (o - mu) * lax.rsqrt(var + GN_EPS)
        g = gb_ref[:, vs].astype(F32)
        gated = g * (1.0 / (1.0 + jnp.exp(-g))) * obn
        cat_ref[:, WIDTH_A + h * VAL_DIM_B:WIDTH_A + (h + 1) * VAL_DIM_B] = gated.astype(BF16)


def _decay_tables(chunk, rows):
    H = N_HEADS_B
    log_g = jnp.log(1.0 - 2.0 ** (-5.0 - jnp.arange(H, dtype=F32)))
    i = jnp.arange(rows, dtype=F32)
    live = np.arange(rows) < chunk
    diff = i[:, None] - i[None, :]
    causal = (diff >= 0) & jnp.asarray(live[:, None] & live[None, :])
    dmat = jnp.where(causal[None], jnp.exp(jnp.where(causal, diff, 0.0)[None] * log_g[:, None, None]), 0.0)
    q_decay = jnp.where(jnp.asarray(live)[None], jnp.exp((i[None, :] + 1.0) * log_g[:, None]), 0.0)
    k_decay = jnp.where(jnp.asarray(live)[None], jnp.exp((chunk - 1.0 - i)[None, :] * log_g[:, None]), 0.0)
    c_decay = jnp.exp(chunk * log_g)
    qdec = jnp.broadcast_to(q_decay[:, :, None], (H, rows, VAL_DIM_B))
    kdec = jnp.broadcast_to(k_decay[:, :, None], (H, rows, KEY_DIM_B))
    cdec = jnp.broadcast_to(c_decay[:, None, None], (H, KEY_DIM_B, VAL_DIM_B))
    return dmat.astype(F32), qdec.astype(F32), kdec.astype(F32), cdec.astype(F32)


def _ret_mix(qb, kb, vb, gb, attn, state0, chunk):
    B, S, _ = qb.shape
    rows = RET_CHUNK
    nc = S // rows
    tables = _decay_tables(chunk, rows)

    def tok(b, c):
        return (b, c, 0)

    def tspec(w):
        return pl.BlockSpec((None, rows, w), tok)

    def const(shape):
        return pl.BlockSpec(shape, lambda b, c: (0,) * len(shape))

    ins = [qb, kb, vb, gb]
    in_specs = [tspec(QK_WIDTH_B), tspec(QK_WIDTH_B), tspec(WIDTH_B), tspec(WIDTH_B)]
    scratch = [pltpu.VMEM((N_HEADS_B, KEY_DIM_B, VAL_DIM_B), F32)]
    if isinstance(attn, tuple):
        outs_a, lses, dilations = attn
        expand = np.zeros((LANES, WIDTH_A), np.float32)
        for h in range(N_HEADS_A):
            expand[h, h * HEAD_DIM_A:(h + 1) * HEAD_DIM_A] = 1.0
        ins += list(outs_a) + list(lses) + [jnp.asarray(expand, BF16)]
        in_specs += ([pl.BlockSpec((None, rows // d, d * WIDTH_A), tok) for d in dilations]
                     + [pl.BlockSpec((None, rows // d, d * LANES), tok) for d in dilations] + [const((LANES, WIDTH_A))])
        scratch += [pltpu.VMEM((WIDTH_A // LANES, rows, LANES), F32), pltpu.VMEM((rows, LANES), F32)]
    else:
        dilations = ()
        ins.append(attn)
        in_specs.append(tspec(WIDTH_A))
    st_shape = (N_HEADS_B, KEY_DIM_B, VAL_DIM_B)
    st_spec = pl.BlockSpec((None,) + st_shape, lambda b, c: (b, 0, 0, 0))
    ins += [state0] + list(tables)
    in_specs += [st_spec] + [const(t.shape) for t in tables]
    return pl.pallas_call(
        functools.partial(_ret_body, dilations=tuple(dilations)),
        grid=(B, nc),
        in_specs=in_specs,
        out_specs=(tspec(WIDTH_A + WIDTH_B), st_spec),
        out_shape=(jax.ShapeDtypeStruct((B, S, WIDTH_A + WIDTH_B), BF16),
                   jax.ShapeDtypeStruct((B,) + st_shape, F32)),
        scratch_shapes=scratch,
        compiler_params=_params(2),
        name=f"ret_mix_{len(dilations)}",
    )(*ins)


SAMP_Q_ROWS = 64
SAMP_NEW_LANES = 128


def _samp_attn_body(q_ref, kt_ref, vt_ref, knt_ref, vnt_ref, bc_ref, bn_ref, hm_ref, o_ref, ko_ref, vo_ref, *, ds):
    q = q_ref[...]
    kt = kt_ref[...]
    vt = vt_ref[...]
    knt = knt_ref[...]
    vnt = vnt_ref[...]
    w = kt.shape[1]
    is_new = lax.broadcasted_iota(jnp.int32, knt.shape, 1) >= SAMP_NEW_LANES - ds
    for src, new, dst in ((kt, knt, ko_ref), (vt, vnt, vo_ref)):
        rolled = pltpu.roll(src, w - ds, 1)
        dst[:, :w - SAMP_NEW_LANES] = rolled[:, :w - SAMP_NEW_LANES]
        dst[:, w - SAMP_NEW_LANES:] = jnp.where(is_new, new, rolled[:, w - SAMP_NEW_LANES:])

    s_c = jnp.dot(q, kt.astype(BF16), preferred_element_type=F32)
    s_n = jnp.dot(q, knt.astype(BF16), preferred_element_type=F32)
    es_c, es_n, dens, lses = [], [], [], []
    for n in range(len(DILATED_BRANCHES)):
        sc = s_c + bc_ref[n]
        sn = s_n + bn_ref[n]
        m = jnp.maximum(jnp.max(sc, axis=-1, keepdims=True), jnp.max(sn, axis=-1, keepdims=True))
        ec = jnp.exp(sc - m)
        en = jnp.exp(sn - m)
        den = jnp.sum(ec, axis=-1, keepdims=True) + jnp.sum(en, axis=-1, keepdims=True)
        es_c.append(ec)
        es_n.append(en)
        dens.append(den)
        lses.append(m + jnp.log(den))
    mx = functools.reduce(jnp.maximum, lses)
    ws = [jnp.exp(l - mx) for l in lses]
    tot = functools.reduce(lambda a, b: a + b, ws)
    p_c = None
    p_n = None
    for w, den, ec, en in zip(ws, dens, es_c, es_n):
        coef = w / (tot * den)
        p_c = coef * ec if p_c is None else p_c + coef * ec
        p_n = coef * en if p_n is None else p_n + coef * en
    nt = (((1,), (1,)), ((), ()))
    o = (lax.dot_general(p_c.astype(BF16), vt.astype(BF16), nt, preferred_element_type=F32)
         + lax.dot_general(p_n.astype(BF16), vnt.astype(BF16), nt, preferred_element_type=F32))
    o = o * hm_ref[...]
    o_ref[...] = jnp.sum(o.reshape(SUBLANES, N_HEADS_A, WIDTH_A), axis=1)


def _samp_bias_tables(rel_bias, w_buf, ds):
    tabs_c, tabs_n = [], []
    s = np.arange(SUBLANES)[:, None]
    live_s = s < ds
    first_new = SAMP_NEW_LANES - ds
    for window, dil in DILATED_BRANCHES:
        n_keys = window // dil
        for keys, live_k, tabs in ((np.arange(w_buf)[None, :], True, tabs_c),
                                   (w_buf - first_new + np.arange(SAMP_NEW_LANES)[None, :],
                                    np.arange(SAMP_NEW_LANES)[None, :] >= first_new, tabs_n)):
            n = keys.shape[1]
            dist = w_buf + s - keys
            valid = (dist >= 0) & (dist % dil == 0) & (dist // dil <= n_keys) & live_k
            bias = _bias_by_bucket(rel_bias, _t5_bucket(np.clip(dist, 0, None))).transpose(1, 0, 2)
            tab = jnp.where(jnp.asarray(valid)[:, None, :], bias, NEG_INF)
            pad = jnp.where(jnp.asarray(np.broadcast_to(live_k, dist.shape))[:, None, :], 0.0, NEG_INF)
            tab = jnp.where(jnp.asarray(live_s)[:, :, None], tab, pad)
            tabs.append(tab.reshape(SAMP_Q_ROWS, n))
    return jnp.stack(tabs_c), jnp.stack(tabs_n)


def _samp_attn(qa, cache_kt, cache_vt, knt, vnt, rel_bias, ds):
    DB, DS, _ = qa.shape
    W = cache_kt.shape[2]
    head_of_lane = np.arange(WIDTH_A) // HEAD_DIM_A
    hmask = (np.arange(SAMP_Q_ROWS)[:, None] % N_HEADS_A == head_of_lane[None, :])
    q8 = jnp.pad(qa, ((0, 0), (0, SUBLANES - DS), (0, 0)))
    q_rows = jnp.where(jnp.asarray(hmask)[None], jnp.repeat(q8, N_HEADS_A, axis=1), jnp.zeros((), BF16))
    bias_c, bias_n = _samp_bias_tables(rel_bias, W, DS)

    def per_b(rows, w):
        return pl.BlockSpec((None, rows, w), lambda b: (b, 0, 0))

    def const(a):
        return pl.BlockSpec(a.shape, lambda b: (0,) * a.ndim)

    hm = jnp.asarray(hmask, F32)
    return pl.pallas_call(
        functools.partial(_samp_attn_body, ds=ds),
        grid=(DB,),
        in_specs=[per_b(SAMP_Q_ROWS, WIDTH_A), per_b(WIDTH_A, W), per_b(WIDTH_A, W),
                  per_b(WIDTH_A, SAMP_NEW_LANES), per_b(WIDTH_A, SAMP_NEW_LANES),
                  const(bias_c), const(bias_n), const(hm)],
        out_specs=(per_b(SUBLANES, WIDTH_A), per_b(WIDTH_A, W), per_b(WIDTH_A, W)),
        out_shape=(jax.ShapeDtypeStruct((DB, SUBLANES, WIDTH_A), F32),
                   jax.ShapeDtypeStruct((DB, WIDTH_A, W), F32), jax.ShapeDtypeStruct((DB, WIDTH_A, W), F32)),
        compiler_params=_params(1),
        name="samp_attn",
    )(q_rows, cache_kt, cache_vt, knt, vnt, bias_c, bias_n, hm)


def _route_body(cat_ref, x_ref, wout_ref, g_ref, b_ref, rwh_ref, rwl_ref, rb_ref, tril_ref, base_ref,
                h_ref, hp_ref, route_ref, cnt_ref, *, alpha):
    @pl.when(pl.program_id(0) == 0)
    def _():
        cnt_ref[...] = base_ref[...]

    mix = jnp.dot(cat_ref[...], wout_ref[...], preferred_element_type=F32)
    y = alpha * x_ref[...] + mix
    mu = jnp.mean(y, axis=-1, keepdims=True)
    var = jnp.mean(jnp.square(y - mu), axis=-1, keepdims=True)
    h = (y - mu) * lax.rsqrt(var + LN_EPS) * g_ref[...] + b_ref[...]
    h_ref[...] = h
    hb = h.astype(BF16)
    hp_ref[...] = _pack_bf16_pairs(h)
    hl = (h - hb.astype(F32)).astype(BF16)
    logits = (jnp.dot(hb, rwh_ref[...], preferred_element_type=F32)
              + jnp.dot(hb, rwl_ref[...], preferred_element_type=F32)
              + jnp.dot(hl, rwh_ref[...], preferred_element_type=F32)) + rb_ref[...]

    tm = logits.shape[0]
    lane = lax.broadcasted_iota(jnp.int32, (tm, LANES), 1)
    work = logits
    vals, idxs = [], []
    for _ in range(TOP_K):
        m = jnp.max(work, axis=-1, keepdims=True)
        idx = jnp.min(jnp.where(work == m, lane, LANES), axis=-1, keepdims=True)
        vals.append(m)
        idxs.append(idx)
        work = jnp.where(lane == idx, -jnp.inf, work)
    es = [jnp.exp(v - vals[0]) for v in vals]
    tot = functools.reduce(lambda a, b: a + b, es)
    onehot = jnp.zeros((tm, LANES), F32)
    for idx in idxs:
        onehot = onehot + (lane == idx).astype(F32)
    before = jnp.dot(tril_ref[...], onehot.astype(BF16), preferred_element_type=F32) + cnt_ref[0:1, :]
    route = jnp.zeros((tm, LANES), F32)
    for k in range(TOP_K):
        rank = jnp.sum(jnp.where(lane == idxs[k], before, 0.0), axis=-1, keepdims=True)
        route = jnp.where(lane == k, idxs[k].astype(F32), route)
        route = jnp.where(lane == TOP_K + k, es[k] / tot, route)
        route = jnp.where(lane == 2 * TOP_K + k, rank, route)
    route_ref[...] = route
    cnt_ref[...] = cnt_ref[...] + jnp.sum(onehot, axis=0, keepdims=True)


def _out_route(cat, x2d, w_out_bf, ln_g, ln_b, rw_hi, rw_lo, rb, base, tm, alpha):
    T = x2d.shape[0]
    tril = jnp.asarray(np.tril(np.ones((tm, tm), np.float32), -1), BF16)

    def tok(i):
        return (i, 0)

    def const(a):
        return pl.BlockSpec(a.shape, lambda i: (0,) * a.ndim)

    ins = (cat, x2d, w_out_bf, ln_g, ln_b, rw_hi, rw_lo, rb, tril, base)
    in_specs = [pl.BlockSpec((tm, cat.shape[1]), tok), pl.BlockSpec((tm, D_MODEL), tok)] + [const(a) for a in ins[2:]]
    return pl.pallas_call(
        functools.partial(_route_body, alpha=alpha),
        grid=(T // tm,),
        in_specs=in_specs,
        out_specs=(pl.BlockSpec((tm, D_MODEL), tok), pl.BlockSpec((tm, D_MODEL // 2), tok),
                   pl.BlockSpec((tm, LANES), tok), pl.BlockSpec((SUBLANES, LANES), lambda i: (0, 0))),
        out_shape=(jax.ShapeDtypeStruct((T, D_MODEL), F32), jax.ShapeDtypeStruct((T, D_MODEL // 2), jnp.int32),
                   jax.ShapeDtypeStruct((T, LANES), F32), jax.ShapeDtypeStruct((SUBLANES, LANES), F32)),
        compiler_params=_params(1),
        name="out_route",
    )(*ins)


MOE_CAST_ROWS = 128


def _moe_body(be_ref, first_ref, next_ref, nused_ref, x_ref, bgu_ref, bdn_ref, wgu_hbm, wdn_hbm, y_ref,
              gu_stage, dn_stage, wgu_bf, wdn_bf, sem):
    b = pl.program_id(0)
    d_exp = wdn_bf.shape[0]

    def fetch(e):
        return (pltpu.make_async_copy(wgu_hbm.at[e], gu_stage, sem.at[0]),
                pltpu.make_async_copy(wdn_hbm.at[e], dn_stage, sem.at[1]))

    @pl.when(b == 0)
    def _():
        for c in fetch(be_ref[0]):
            c.start()

    @pl.when(first_ref[b] == 1)
    def _():
        for c in fetch(be_ref[b]):
            c.wait()

        def cast_gu(i, c):
            r = pl.ds(pl.multiple_of(i * MOE_CAST_ROWS, MOE_CAST_ROWS), MOE_CAST_ROWS)
            wgu_bf[r, :] = gu_stage[r, :].astype(BF16)
            return c

        def cast_dn(i, c):
            r = pl.ds(pl.multiple_of(i * MOE_CAST_ROWS, MOE_CAST_ROWS), MOE_CAST_ROWS)
            wdn_bf[r, :] = dn_stage[r, :].astype(BF16)
            return c

        lax.fori_loop(0, gu_stage.shape[0] // MOE_CAST_ROWS, cast_gu, 0)
        lax.fori_loop(0, d_exp // MOE_CAST_ROWS, cast_dn, 0)

        @pl.when(next_ref[b] >= 0)
        def _():
            for c in fetch(next_ref[b]):
                c.start()

    @pl.when(b < nused_ref[0])
    def _():
        x_lo, x_hi = _unpack_bf16_pairs(x_ref[...])
        x_lo = x_lo.astype(BF16)
        x_hi = x_hi.astype(BF16)
        dh = x_lo.shape[1]

        def xw(cols):
            return (jnp.dot(x_lo, wgu_bf[:dh, cols], preferred_element_type=F32)
                    + jnp.dot(x_hi, wgu_bf[dh:, cols], preferred_element_type=F32) + bgu_ref[:, cols])

        half = d_exp // 2
        y = None
        for c in range(2):
            lo = c * half
            gate = jnp.minimum(xw(slice(lo, lo + half)), SWIGLU_LIMIT)
            up = jnp.clip(xw(slice(d_exp + lo, d_exp + lo + half)), -SWIGLU_LIMIT, SWIGLU_LIMIT)
            act = (up + 1.0) * gate * (1.0 / (1.0 + jnp.exp(-SWIGLU_ALPHA * gate)))
            part = jnp.dot(act.astype(BF16), wdn_bf[lo:lo + half, :], preferred_element_type=F32)
            y = part if y is None else y + part
        y_ref[...] = _pack_bf16_pairs(y + bdn_ref[...])

    @pl.when(b >= nused_ref[0])
    def _():
        y_ref[...] = jnp.zeros_like(y_ref)


def _moe_ffn(xs, rows, block_e, n_used, has_rows, w_gu, b_gu, w_dn, b_dn):
    E, D, two_de = w_gu.shape
    d_exp = two_de // 2
    nb = rows // MOE_BLOCK
    idx = jnp.arange(nb, dtype=jnp.int32)
    first = ((idx < n_used[0]) & ((idx == 0) | (block_e != jnp.roll(block_e, 1)))).astype(jnp.int32)
    ids = jnp.arange(E, dtype=jnp.int32)
    later = jnp.where((ids[None, :] > ids[:, None]) & has_rows[None, :], ids[None, :], E).min(axis=1)
    next_of = jnp.where(later == E, -1, later).astype(jnp.int32)
    next_e = jnp.sum(jnp.where(block_e[:, None] == ids[None, :], next_of[None, :], 0), axis=1).astype(jnp.int32)
    grid_spec = pltpu.PrefetchScalarGridSpec(
        num_scalar_prefetch=4,
        grid=(nb,),
        in_specs=[
            pl.BlockSpec((MOE_BLOCK, D // 2), lambda b, be, fi, nx, nu: (b, 0)),
            pl.BlockSpec((None, 1, two_de), lambda b, be, fi, nx, nu: (be[b], 0, 0)),
            pl.BlockSpec((None, 1, D), lambda b, be, fi, nx, nu: (be[b], 0, 0)),
            pl.BlockSpec(memory_space=pl.ANY),
            pl.BlockSpec(memory_space=pl.ANY),
        ],
        out_specs=pl.BlockSpec((MOE_BLOCK, D // 2), lambda b, be, fi, nx, nu: (b, 0)),
        scratch_shapes=[pltpu.VMEM((D, two_de), F32), pltpu.VMEM((d_exp, D), F32),
                        pltpu.VMEM((D, two_de), BF16), pltpu.VMEM((d_exp, D), BF16),
                        pltpu.SemaphoreType.DMA((2,))],
    )
    return pl.pallas_call(
        _moe_body,
        grid_spec=grid_spec,
        out_shape=jax.ShapeDtypeStruct((rows, D // 2), jnp.int32),
        compiler_params=_params(1),
        name="moe_ffn",
    )(block_e, first, next_e, n_used, xs, b_gu.reshape(E, 1, two_de), b_dn.reshape(E, 1, D), w_gu, w_dn)


def _ple_body(h_ref, ys_ref, route_ref, p_ref, g_ref, b_ref, wpg_ref, bpg_ref, wpp_ref, o_ref, *, alpha):
    route = route_ref[...]
    f_lo = None
    f_hi = None
    for k in range(TOP_K):
        lo, hi = _unpack_bf16_pairs(ys_ref[k])
        g = route[:, TOP_K + k:TOP_K + k + 1]
        f_lo = g * lo if f_lo is None else f_lo + g * lo
        f_hi = g * hi if f_hi is None else f_hi + g * hi
    y = alpha * h_ref[...] + jnp.concatenate([f_lo, f_hi], axis=1)
    mu = jnp.mean(y, axis=-1, keepdims=True)
    var = jnp.mean(jnp.square(y - mu), axis=-1, keepdims=True)
    h2 = (y - mu) * lax.rsqrt(var + LN_EPS) * g_ref[...] + b_ref[...]
    z = jnp.dot(h2.astype(BF16), wpg_ref[...], preferred_element_type=F32) + bpg_ref[...]
    gate = 1.0 / (1.0 + jnp.exp(-z))
    proj = jnp.dot(p_ref[...].astype(BF16), wpp_ref[...], preferred_element_type=F32)
    o_ref[...] = h2 + gate * proj


def _ffn_ple(h, y_slots, route, p, ln_g, ln_b, w_pg_bf, b_pg, w_pp_bf, tm, tile0, alpha):
    T = h.shape[0]

    def tok(i):
        return (i, 0)

    def const(a):
        return pl.BlockSpec(a.shape, lambda i: (0,) * a.ndim)

    consts = (ln_g, ln_b, w_pg_bf, b_pg, w_pp_bf)
    return pl.pallas_call(
        functools.partial(_ple_body, alpha=alpha),
        grid=(T // tm,),
        in_specs=[pl.BlockSpec((tm, D_MODEL), tok),
                  pl.BlockSpec((TOP_K, tm, D_MODEL // 2), lambda i: (0, tile0 + i, 0)),
                  pl.BlockSpec((tm, LANES), tok),
                  pl.BlockSpec((tm, p.shape[1]), tok)] + [const(a) for a in consts],
        out_specs=pl.BlockSpec((tm, D_MODEL), tok),
        out_shape=jax.ShapeDtypeStruct((T, D_MODEL), F32),
        compiler_params=_params(1),
        name="ffn_ple",
    )(h, y_slots, route, p, *consts)


def _row(v):
    return v.reshape(1, -1).astype(F32)


def kernel(x_prompt, x_sample, cache_win_k, cache_win_v, state_ret, p_prompt, p_sample, rel_bias, w_in, w_out,
           ln1_g, ln1_b, router_w, router_b, w_gate_up, b_gate_up, w_down, b_down, ln2_g, ln2_b,
           w_ple_gate, b_ple_gate, w_ple_proj):
    B, S, D = x_prompt.shape
    DB, DS, _ = x_sample.shape
    depth = w_in.shape[0]
    w_buf = cache_win_k.shape[2]
    n_exp = router_w.shape[-1]
    alpha = (2.0 * depth) ** 0.25
    assert depth == 1 and D == D_MODEL
    assert S % (BLK * MAX_DIL) == 0 and S >= WINDOW_MAX and w_buf == WINDOW_MAX and DS <= SUBLANES
    tm_p = 512
    Tp, Ts = B * S, DB * DS
    assert Tp % tm_p == 0 and Ts % SUBLANES == 0

    i = 0
    w_in_bf = w_in[i].astype(BF16)
    w_out_bf = w_out[i].astype(BF16)
    w_pg_bf = w_ple_gate[i].astype(BF16)
    w_pp_bf = w_ple_proj[i].astype(BF16)
    rw = jnp.pad(router_w[i], ((0, 0), (0, LANES - n_exp)))
    rw_hi = rw.astype(BF16)
    rw_lo = (rw - rw_hi.astype(F32)).astype(BF16)
    rb = jnp.pad(router_b[i], (0, LANES - n_exp), constant_values=NEG_INF).reshape(1, LANES)

    cos_p, sin_p = _rotary_tables(jnp.arange(S, dtype=jnp.int32))
    dils = tuple(d for _, d in DILATED_BRANCHES)
    extra = tuple(d for d in dils if d > 1)
    outs = _in_proj(x_prompt.reshape(Tp, D), w_in_bf, cos_p, sin_p, tm_p, S // tm_p, WINDOW_MAX // tm_p, extra)
    qa, ka, va, qb, kb, vb, gb, kf, vf = outs[:9]
    qkv = {1: (qa, ka, va)}
    for t, d in enumerate(extra):
        qkv[d] = tuple(outs[9 + j * len(extra) + t] for j in range(3))

    def seq(t):
        return t.reshape(B, S, t.shape[-1])

    outs_a, lses = [], []
    for window, dil in DILATED_BRANCHES:
        q_d, k_d, v_d = (t.reshape(B, S // dil, dil * WIDTH_A) for t in qkv[dil])
        o_n, l_n = _dilated_branch(q_d, k_d, v_d, _attn_bias_tables(rel_bias, window, dil), dil)
        outs_a.append(o_n)
        lses.append(l_n)
    st_zero = jnp.zeros((B, N_HEADS_B, KEY_DIM_B, VAL_DIM_B), F32)
    cat_p, rst_p = _ret_mix(seq(qb), seq(kb), seq(vb), seq(gb), (outs_a, lses, dils), st_zero, RET_CHUNK)
    base0 = jnp.zeros((SUBLANES, LANES), F32)
    h_p, hp_p, route_p, cnt_p = _out_route(cat_p.reshape(Tp, -1), x_prompt.reshape(Tp, D), w_out_bf,
                                           _row(ln1_g[i]), _row(ln1_b[i]), rw_hi, rw_lo, rb, base0, tm_p, alpha)

    pos_s = jnp.tile(PAST_LEN + jnp.arange(DS, dtype=jnp.int32), DB)
    cos_s, sin_s = _rotary_tables(pos_s)
    qa_s, _, _, qb_s, kb_s, vb_s, gb_s, kf_s, vf_s = _in_proj(x_sample.reshape(Ts, D), w_in_bf, cos_s, sin_s,
                                                             Ts, 1, 1)

    def positions_minor(t):
        return jnp.transpose(t, (0, 2, 3, 1)).reshape(DB, WIDTH_A, t.shape[1])

    def positions_major(t):
        return jnp.transpose(t.reshape(t.shape[0], N_HEADS_A, HEAD_DIM_A, t.shape[2]), (0, 3, 1, 2))[None]

    def new_columns(t):
        t = jnp.transpose(t.reshape(WIDTH_A, DB, DS), (1, 0, 2))
        return jnp.pad(t, ((0, 0), (0, 0), (SAMP_NEW_LANES - DS, 0)))

    oa_s, kt_out, vt_out = _samp_attn(qa_s.reshape(DB, DS, WIDTH_A), positions_minor(cache_win_k[i]),
                                      positions_minor(cache_win_v[i]), new_columns(kf_s), new_columns(vf_s),
                                      rel_bias, DS)

    def pad_rows(t, rows):
        t = t.reshape(DB, -1, t.shape[-1])
        return jnp.pad(t, ((0, 0), (0, rows - t.shape[1]), (0, 0)))

    cat_s, rst_s = _ret_mix(pad_rows(qb_s, RET_CHUNK), pad_rows(kb_s, RET_CHUNK), pad_rows(vb_s, RET_CHUNK),
                            pad_rows(gb_s, RET_CHUNK), pad_rows(oa_s, RET_CHUNK), state_ret[i].astype(F32), DS)
    cat_s = cat_s[:, :DS].reshape(Ts, -1)
    h_s, hp_s, route_s, cnt = _out_route(cat_s, x_sample.reshape(Ts, D), w_out_bf, _row(ln1_g[i]), _row(ln1_b[i]),
                                         rw_hi, rw_lo, rb, cnt_p, Ts, alpha)

    T = Tp + Ts
    t_align = SC_WORKERS * SC_ALIGN
    T_pad = -(-T // t_align) * t_align
    route = jnp.concatenate([route_p, route_s], axis=0)
    top_idx = route[:, :TOP_K].astype(jnp.int32)
    rank = route[:, 2 * TOP_K:3 * TOP_K].astype(jnp.int32)
    counts = cnt[0, :n_exp].astype(jnp.int32)
    padded = (counts + MOE_BLOCK - 1) // MOE_BLOCK * MOE_BLOCK
    pad_end = jnp.cumsum(padded)
    pad_start = pad_end - padded
    dest = pad_start[top_idx] + rank
    n_blocks = -(-T * TOP_K // MOE_BLOCK) + n_exp
    rows = n_blocks * MOE_BLOCK
    block_start = jnp.arange(n_blocks, dtype=jnp.int32) * MOE_BLOCK
    block_e = jnp.minimum(jnp.sum(pad_end[None, :] <= block_start[:, None], axis=1), n_exp - 1).astype(jnp.int32)
    n_used = (pad_end[-1:] // MOE_BLOCK).astype(jnp.int32)
    n_fill = T_pad - T
    spare = rows + jnp.arange(n_fill * TOP_K, dtype=jnp.int32).reshape(n_fill, TOP_K)
    dest_sc = jnp.concatenate([dest, spare], axis=0).T.reshape(-1)
    dest_ga = jnp.concatenate([dest, jnp.zeros((n_fill, TOP_K), jnp.int32)], axis=0).T.reshape(-1)
    hp_all = jnp.concatenate([hp_p, hp_s, jnp.zeros((n_fill, D // 2), jnp.int32)], axis=0)
    xs = _sc_scatter_rows(hp_all, dest_sc, rows + n_fill * TOP_K)
    ys = _moe_ffn(xs, rows, block_e, n_used, padded > 0, w_gate_up[i], b_gate_up[i], w_down[i], b_down[i])
    y_slots = _sc_gather_rows(ys, dest_ga).reshape(TOP_K, T_pad, D // 2)

    ple_args = (_row(ln2_g[i]), _row(ln2_b[i]), w_pg_bf, _row(b_ple_gate[i]), w_pp_bf)
    assert Tp % Ts == 0
    y_p = _ffn_ple(h_p, y_slots, route_p, p_prompt[i].reshape(Tp, D_PLE), *ple_args, tm_p, 0, alpha)
    y_s = _ffn_ple(h_s, y_slots, route_s, p_sample[i].reshape(Ts, D_PLE), *ple_args, Ts, Tp // Ts, alpha)

    return (y_p.reshape(B, S, D), y_s.reshape(DB, DS, D), positions_major(kf), positions_major(vf),
            rst_p[None], positions_major(kt_out), positions_major(vt_out), rst_s[None])
```

```python
import functools

import numpy as np
import jax
import jax.numpy as jnp
from jax import lax
from jax.experimental import pallas as pl
from jax.experimental.pallas import tpu as pltpu
from jax.experimental.pallas import tpu_sc as plsc

F32 = jnp.float32
BF16 = jnp.bfloat16

D_MODEL = 1024
D_PLE = 256
N_HEADS_A = 8
HEAD_DIM_A = 64
WIDTH_A = N_HEADS_A * HEAD_DIM_A
DILATED_BRANCHES = ((128, 1), (512, 4), (2048, 16))
BLK = 128
WINDOW_MAX = 2048
MAX_DIL = 16
NUM_BUCKETS = 32
MAX_DISTANCE = 2048
N_HEADS_B = 4
KEY_DIM_B = 64
VAL_DIM_B = 128
QK_WIDTH_B = N_HEADS_B * KEY_DIM_B
WIDTH_B = N_HEADS_B * VAL_DIM_B
RET_CHUNK = 128
GN_EPS = 1e-6
TOP_K = 4
SWIGLU_LIMIT = 7.0
SWIGLU_ALPHA = 1.702
LN_EPS = 1e-5
NEG_INF = -1e30
PAST_LEN = 16384
MOE_BLOCK = 512
LANES = 128
SUBLANES = 8
VMEM_LIMIT = 52 * 1024 * 1024


def _params(n_axes, vmem=VMEM_LIMIT):
    return pltpu.CompilerParams(dimension_semantics=("arbitrary",) * n_axes, vmem_limit_bytes=vmem)


def _t5_bucket(dist):
    dist = np.asarray(dist, dtype=np.int32)
    max_exact = NUM_BUCKETS // 2
    d = np.maximum(dist, 1).astype(np.float32)
    large = max_exact + (np.log(d / max_exact) / np.log(MAX_DISTANCE / max_exact)
                         * (NUM_BUCKETS - max_exact)).astype(np.int32)
    large = np.minimum(large, NUM_BUCKETS - 1)
    return np.where(dist < max_exact, dist, large).astype(np.int32)


def _bias_by_bucket(rel_bias, buckets):
    b = jnp.asarray(buckets, jnp.int32)
    ids = jnp.arange(NUM_BUCKETS, dtype=jnp.int32).reshape((NUM_BUCKETS, 1) + (1,) * b.ndim)
    vals = rel_bias.astype(F32).reshape((NUM_BUCKETS, rel_bias.shape[1]) + (1,) * b.ndim)
    return jnp.sum(jnp.where(b[None, None] == ids, vals, 0.0), axis=0)


def _pack_bf16_pairs(v):
    w = v.shape[1] // 2
    lo = lax.bitcast_convert_type(v[:, :w].astype(BF16).astype(F32), jnp.uint32) >> 16
    hi = lax.bitcast_convert_type(v[:, w:].astype(BF16).astype(F32), jnp.uint32) & jnp.uint32(0xFFFF0000)
    return lax.bitcast_convert_type(lo | hi, jnp.int32)


def _unpack_bf16_pairs(p):
    u = lax.bitcast_convert_type(p, jnp.uint32)
    lo = lax.bitcast_convert_type(u << 16, F32)
    hi = lax.bitcast_convert_type(u & jnp.uint32(0xFFFF0000), F32)
    return lo, hi


SC_CORES = 2
SC_SUBCORES = 16
SC_WORKERS = SC_CORES * SC_SUBCORES
SC_ALIGN = 8
SC_CHUNK_ROWS = 80


def _sc_mesh():
    return plsc.VectorSubcoreMesh(core_axis_name="c", subcore_axis_name="s")


def _sc_chunk(per_worker):
    c = max(d for d in range(SC_ALIGN, SC_CHUNK_ROWS + 1, SC_ALIGN) if per_worker % d == 0)
    return c


def _sc_scatter_rows(src, dest_flat, n_out):
    T, W = src.shape
    K = dest_flat.shape[0] // T
    per_w = T // SC_WORKERS
    assert per_w * SC_WORKERS == T and per_w % SC_ALIGN == 0
    chunk = _sc_chunk(per_w)
    n_chunks = per_w // chunk

    @functools.partial(
        pl.kernel, mesh=_sc_mesh(), out_type=jax.ShapeDtypeStruct((n_out, W), src.dtype),
        scratch_types=[pltpu.VMEM((chunk, W), src.dtype)] * 2 + [pltpu.VMEM((chunk,), jnp.int32)] * (2 * K)
        + [pltpu.SemaphoreType.DMA] * 2,
        name="sc_scatter_rows")
    def k(src_hbm, dest_hbm, out_hbm, rows_a, rows_b, *rest):
        bufs = ((rows_a, rest[:K], rest[2 * K]), (rows_b, rest[K:2 * K], rest[2 * K + 1]))
        base = (lax.axis_index("s") * SC_CORES + lax.axis_index("c")) * per_w

        def scatters(buf):
            rows_v, idx_vs, sem = buf
            return [pltpu.make_async_copy(rows_v, out_hbm.at[idx_vs[kk]], sem) for kk in range(K)]

        def step(j, buf):
            rows_v, idx_vs, _ = buf

            @pl.when(j >= 2)
            def _():
                for c in scatters(buf):
                    c.wait()

            off = pl.multiple_of(base + j * chunk, SC_ALIGN)
            pltpu.sync_copy(src_hbm.at[pl.ds(off, chunk)], rows_v)
            for kk in range(K):
                pltpu.sync_copy(dest_hbm.at[pl.ds(kk * T + off, chunk)], idx_vs[kk])
            for c in scatters(buf):
                c.start()

        @pl.loop(0, n_chunks)
        def _(j):
            for parity in range(2):
                @pl.when(j % 2 == parity)
                def _():
                    step(j, bufs[parity])

        for j in range(max(n_chunks - 2, 0), n_chunks):
            for c in scatters(bufs[j % 2]):
                c.wait()

    return k(src, dest_flat)


def _sc_gather_rows(table, idx):
    B = idx.shape[0]
    W = table.shape[1]
    per_w = B // SC_WORKERS
    assert per_w * SC_WORKERS == B and per_w % SC_ALIGN == 0
    chunk = _sc_chunk(per_w)
    n_chunks = per_w // chunk

    @functools.partial(
        pl.kernel, mesh=_sc_mesh(), out_type=jax.ShapeDtypeStruct((B, W), table.dtype),
        scratch_types=[pltpu.VMEM((chunk,), jnp.int32)] * 2 + [pltpu.VMEM((chunk, W), table.dtype)] * 2
        + [pltpu.SemaphoreType.DMA] * 2,
        name="sc_gather_rows")
    def k(table_hbm, idx_hbm, out_hbm, idx_a, idx_b, rows_a, rows_b, sem_a, sem_b):
        bufs = ((idx_a, rows_a, sem_a), (idx_b, rows_b, sem_b))
        base = (lax.axis_index("s") * SC_CORES + lax.axis_index("c")) * per_w

        def gather(buf):
            idx_v, rows_v, sem = buf
            return pltpu.make_async_copy(table_hbm.at[idx_v], rows_v, sem)

        def finish(j, buf):
            gather(buf).wait()
            off = pl.multiple_of(base + j * chunk, SC_ALIGN)
            pltpu.sync_copy(buf[1], out_hbm.at[pl.ds(off, chunk)])

        def step(j, buf, other):
            off = pl.multiple_of(base + j * chunk, SC_ALIGN)
            pltpu.sync_copy(idx_hbm.at[pl.ds(off, chunk)], buf[0])
            gather(buf).start()

            @pl.when(j >= 1)
            def _():
                finish(j - 1, other)

        @pl.loop(0, n_chunks)
        def _(j):
            for parity in range(2):
                @pl.when(j % 2 == parity)
                def _():
                    step(j, bufs[parity], bufs[1 - parity])

        finish(n_chunks - 1, bufs[(n_chunks - 1) % 2])

    return k(table, idx)


def _in_proj_body(x_ref, w_ref, cos_ref, sin_ref, *refs, seq_tiles, first_win, dilations):
    qa_ref, ka_ref, va_ref, qb_ref, kb_ref, vb_ref, gb_ref, kt_ref, vt_ref = refs[:9]
    n_d = len(dilations)
    dil_refs = [refs[9 + t * n_d:9 + (t + 1) * n_d] for t in range(3)]
    zs_ref = refs[9 + 3 * n_d] if n_d else None
    x = x_ref[...].astype(BF16)
    in_window = pl.program_id(0) % seq_tiles >= first_win

    def emit(z, token_ref, class_refs):
        token_ref[...] = z.astype(BF16)
        if not class_refs:
            return
        for c in range(WIDTH_A // LANES):
            zs_ref[c] = z[:, c * LANES:(c + 1) * LANES]
        for d, ref in zip(dilations, class_refs):
            n = z.shape[0] // d
            for r in range(d):
                for c in range(WIDTH_A // LANES):
                    lo = r * WIDTH_A + c * LANES
                    ref[:, lo:lo + LANES] = zs_ref[c, pl.ds(r, n, stride=d), :].astype(BF16)

    def proj(lo, hi):
        return jnp.dot(x, w_ref[:, lo:hi], preferred_element_type=F32)

    o = 0
    emit(proj(o, o + WIDTH_A) * (HEAD_DIM_A ** -0.5), qa_ref, dil_refs[0])
    o += WIDTH_A
    ka = proj(o, o + WIDTH_A)
    emit(ka, ka_ref, dil_refs[1])

    @pl.when(in_window)
    def _():
        kt_ref[...] = ka.T
    o += WIDTH_A
    va = proj(o, o + WIDTH_A)
    emit(va, va_ref, dil_refs[2])

    @pl.when(in_window)
    def _():
        vt_ref[...] = va.T
    o += WIDTH_A

    cos = cos_ref[...]
    sin = sin_ref[...]
    lane = lax.broadcasted_iota(jnp.int32, cos.shape, 1)
    first_half = (lane % KEY_DIM_B) < (KEY_DIM_B // 2)

    def rot(z):
        sw = jnp.where(first_half, pltpu.roll(z, QK_WIDTH_B - KEY_DIM_B // 2, 1), pltpu.roll(z, KEY_DIM_B // 2, 1))
        return z * cos + sw * sin

    qb_ref[...] = rot(proj(o, o + QK_WIDTH_B)).astype(BF16)
    o += QK_WIDTH_B
    kb_ref[...] = (rot(proj(o, o + QK_WIDTH_B)) * (KEY_DIM_B ** -0.5)).astype(BF16)
    o += QK_WIDTH_B
    vb_ref[...] = proj(o, o + WIDTH_B).astype(BF16)
    o += WIDTH_B
    gb_ref[...] = proj(o, o + WIDTH_B).astype(BF16)


def _in_proj(x2d, w_bf, cos_t, sin_t, tm, seq_tiles, win_tiles, dilations=()):
    T = x2d.shape[0]
    nt = T // tm
    n_seq = nt // seq_tiles
    j0 = seq_tiles - win_tiles

    def tok(i):
        return (i, 0)

    def tab(i):
        return (i % seq_tiles, 0)

    def win(i):
        return (i // seq_tiles, 0, jnp.maximum(i % seq_tiles - j0, 0))

    def tspec(w):
        return pl.BlockSpec((tm, w), tok)

    out_shape = (
        jax.ShapeDtypeStruct((T, WIDTH_A), BF16), jax.ShapeDtypeStruct((T, WIDTH_A), BF16),
        jax.ShapeDtypeStruct((T, WIDTH_A), BF16),
        jax.ShapeDtypeStruct((T, QK_WIDTH_B), BF16), jax.ShapeDtypeStruct((T, QK_WIDTH_B), BF16),
        jax.ShapeDtypeStruct((T, WIDTH_B), BF16), jax.ShapeDtypeStruct((T, WIDTH_B), BF16),
        jax.ShapeDtypeStruct((n_seq, WIDTH_A, win_tiles * tm), F32),
        jax.ShapeDtypeStruct((n_seq, WIDTH_A, win_tiles * tm), F32),
    ) + tuple(jax.ShapeDtypeStruct((T // d, d * WIDTH_A), BF16) for _ in range(3) for d in dilations)
    class_specs = tuple(pl.BlockSpec((tm // d, d * WIDTH_A), tok) for _ in range(3) for d in dilations)
    return pl.pallas_call(
        functools.partial(_in_proj_body, seq_tiles=seq_tiles, first_win=j0, dilations=tuple(dilations)),
        grid=(nt,),
        in_specs=[tspec(D_MODEL), pl.BlockSpec(w_bf.shape, lambda i: (0, 0)),
                  pl.BlockSpec((tm, QK_WIDTH_B), tab), pl.BlockSpec((tm, QK_WIDTH_B), tab)],
        out_specs=(tspec(WIDTH_A), tspec(WIDTH_A), tspec(WIDTH_A), tspec(QK_WIDTH_B), tspec(QK_WIDTH_B),
                   tspec(WIDTH_B), tspec(WIDTH_B),
                   pl.BlockSpec((None, WIDTH_A, tm), win), pl.BlockSpec((None, WIDTH_A, tm), win)) + class_specs,
        out_shape=out_shape,
        scratch_shapes=[pltpu.VMEM((WIDTH_A // LANES, tm, LANES), F32)] if dilations else [],
        compiler_params=_params(1),
        name="in_proj",
    )(x2d, w_bf, cos_t, sin_t)


def _rotary_tables(pos):
    half = KEY_DIM_B // 2
    inv_freq = 1.0 / (10000.0 ** jnp.linspace(0.0, 1.0, half, dtype=F32))
    ang = pos.astype(F32)[:, None] * inv_freq[None, :]
    cos = jnp.cos(ang)
    sin = jnp.sin(ang)
    cos_h = jnp.concatenate([cos, cos], axis=-1)
    sin_h = jnp.concatenate([-sin, sin], axis=-1)
    return jnp.tile(cos_h, (1, N_HEADS_B)), jnp.tile(sin_h, (1, N_HEADS_B))


ATTN_BLOCKS_PER_STEP = 4


def _attn_body(q_ref, k_ref, v_ref, bias_ref, o_ref, lse_ref, kb_ref, vb_ref):
    n = pl.program_id(2)
    rows = q_ref.shape[0]

    @pl.when(n == 0)
    def _():
        kb_ref[:BLK, :] = jnp.zeros((BLK, WIDTH_A), BF16)
        vb_ref[:BLK, :] = jnp.zeros((BLK, WIDTH_A), BF16)

    kb_ref[BLK:, :] = k_ref[...]
    vb_ref[BLK:, :] = v_ref[...]
    lane = lax.broadcasted_iota(jnp.int32, (BLK, LANES), 1)
    low = lane < HEAD_DIM_A
    nt = (((1,), (1,)), ((), ()))

    def sub_block(j, carry):
        r0 = pl.multiple_of(j * BLK, BLK)
        table = jnp.where((n == 0) & (j == 0), 0, 1)
        lse_tile = jnp.zeros((BLK, LANES), F32)
        for p in range(N_HEADS_A // 2):
            cs = slice(p * LANES, (p + 1) * LANES)
            qp = q_ref[pl.ds(r0, BLK), cs]
            kp = kb_ref[pl.ds(r0, 2 * BLK), cs]
            vp = vb_ref[pl.ds(r0, 2 * BLK), cs]
            zero = jnp.zeros_like(qp)
            outs = []
            for half, qh in enumerate((jnp.where(low, qp, zero), jnp.where(low, zero, qp))):
                h = 2 * p + half
                s = lax.dot_general(qh, kp, nt, preferred_element_type=F32) + bias_ref[table, h]
                m = jnp.max(s, axis=-1, keepdims=True)
                e = jnp.exp(s - m)
                den = jnp.sum(e, axis=-1, keepdims=True)
                outs.append(jnp.dot(e.astype(BF16), vp, preferred_element_type=F32) / den)
                lse_tile = jnp.where(lane == h, m + jnp.log(den), lse_tile)
            o_ref[pl.ds(r0, BLK), cs] = jnp.where(low, outs[0], outs[1]).astype(o_ref.dtype)
        lse_ref[pl.ds(r0, BLK), :] = lse_tile
        return carry

    lax.fori_loop(0, rows // BLK, sub_block, 0)
    kb_ref[:BLK, :] = k_ref[rows - BLK:, :]
    vb_ref[:BLK, :] = v_ref[rows - BLK:, :]


def _attn_bias_tables(rel_bias, window, dil):
    n_keys = window // dil
    i = np.arange(BLK)[:, None]
    j = np.arange(2 * BLK)[None, :]
    rel = BLK + i - j
    in_band = (rel >= 0) & (rel <= n_keys)
    bias = _bias_by_bucket(rel_bias, _t5_bucket(np.clip(rel, 0, None) * dil))
    later = jnp.where(jnp.asarray(in_band)[None], bias, NEG_INF)
    first = jnp.where(jnp.asarray(in_band & (j >= BLK))[None], bias, NEG_INF)
    return jnp.stack([first, later])


def _dilated_branch(q, k, v, bias_tab, dil):
    B, L, _ = q.shape
    rows = min(ATTN_BLOCKS_PER_STEP, L // BLK) * BLK
    assert L % rows == 0
    o_dtype = BF16 if RET_CHUNK // dil >= 16 else F32

    def cls(b, r, n):
        return (b, n, r)

    qkv_spec = pl.BlockSpec((None, rows, WIDTH_A), cls)
    o, lse = pl.pallas_call(
        _attn_body,
        grid=(B, dil, L // rows),
        in_specs=[qkv_spec, qkv_spec, qkv_spec,
                  pl.BlockSpec(bias_tab.shape, lambda b, r, n: (0, 0, 0, 0))],
        out_specs=(qkv_spec, pl.BlockSpec((None, rows, LANES), cls)),
        out_shape=(jax.ShapeDtypeStruct((B, L, dil * WIDTH_A), o_dtype),
                   jax.ShapeDtypeStruct((B, L, dil * LANES), F32)),
        scratch_shapes=[pltpu.VMEM((BLK + rows, WIDTH_A), BF16), pltpu.VMEM((BLK + rows, WIDTH_A), BF16)],
        compiler_params=_params(3),
        name=f"dil_attn_d{dil}",
    )(q, k, v, bias_tab)
    return o, lse


def _ret_body(*refs, dilations):
    n_branch = len(dilations)
    qb_ref, kb_ref, vb_ref, gb_ref = refs[:4]
    p = 4
    if n_branch:
        o_refs = refs[p:p + n_branch]
        l_refs = refs[p + n_branch:p + 2 * n_branch]
        exp_ref = refs[p + 2 * n_branch]
        p += 2 * n_branch + 1
    else:
        oa_ref = refs[p]
        p += 1
    st0_ref, dmat_ref, qdec_ref, kdec_ref, cdec_ref, cat_ref, sto_ref, st_ref = refs[p:p + 8]
    if n_branch:
        us_ref, ls_ref = refs[p + 8:]

    def lse_token_order(l_ref, d):
        if d == 1:
            return l_ref[...]
        n = l_ref.shape[0]
        for r in range(d):
            ls_ref[pl.ds(r, n, stride=d), :] = l_ref[:, r * LANES:(r + 1) * LANES]
        return ls_ref[...]

    def out_token_order(o_ref, d):
        if d == 1:
            return o_ref[...].astype(F32)
        n = o_ref.shape[0]
        for r in range(d):
            for c in range(WIDTH_A // LANES):
                lo = r * WIDTH_A + c * LANES
                us_ref[c, pl.ds(r, n, stride=d), :] = o_ref[:, lo:lo + LANES].astype(F32)
        return jnp.concatenate([us_ref[c] for c in range(WIDTH_A // LANES)], axis=1)

    @pl.when(pl.program_id(1) == 0)
    def _():
        st_ref[...] = st0_ref[...]

    if n_branch:
        ls = [lse_token_order(l_ref, d) for l_ref, d in zip(l_refs, dilations)]
        mx = functools.reduce(jnp.maximum, ls)
        ws = [jnp.exp(l - mx) for l in ls]
        tot = functools.reduce(lambda a, b: a + b, ws)
        oa = None
        for w, o_ref, d in zip(ws, o_refs, dilations):
            o_tok = out_token_order(o_ref, d)
            w = w / tot
            w_hi = w.astype(BF16)
            w_lo = (w - w_hi.astype(F32)).astype(BF16)
            w_full = (jnp.dot(w_hi, exp_ref[...], preferred_element_type=F32)
                      + jnp.dot(w_lo, exp_ref[...], preferred_element_type=F32))
            term = w_full * o_tok
            oa = term if oa is None else oa + term
        cat_ref[:, :WIDTH_A] = oa.astype(BF16)
    else:
        cat_ref[:, :WIDTH_A] = oa_ref[...].astype(BF16)

    for h in range(N_HEADS_B):
        ks = slice(h * KEY_DIM_B, (h + 1) * KEY_DIM_B)
        vs = slice(h * VAL_DIM_B, (h + 1) * VAL_DIM_B)
        q = qb_ref[:, ks]
        k = kb_ref[:, ks]
        v = vb_ref[:, vs]
        st = st_ref[h]
        a = lax.dot_general(q, k, (((1,), (1,)), ((), ())), preferred_element_type=F32) * dmat_ref[h]
        o = (jnp.dot(a.astype(BF16), v, preferred_element_type=F32)
             + jnp.dot(q, st.astype(BF16), preferred_element_type=F32) * qdec_ref[h])
        kd = (k.astype(F32) * kdec_ref[h]).astype(BF16)
        st_new = st * cdec_ref[h] + lax.dot_general(kd, v, (((0,), (0,)), ((), ())), preferred_element_type=F32)
        st_ref[h] = st_new
        sto_ref[h] = st_new
        mu = jnp.mean(o, axis=-1, keepdims=True)
        var = jnp.mean(jnp.square(o - mu), axis=-1, keepdims=True)
        obn = (o - mu) * lax.rsqrt(var + GN_EPS)
        g = gb_ref[:, vs].astype(F32)
        gated = g * (1.0 / (1.0 + jnp.exp(-g))) * obn
        cat_ref[:, WIDTH_A + h * VAL_DIM_B:WIDTH_A + (h + 1) * VAL_DIM_B] = gated.astype(BF16)


def _decay_tables(chunk, rows):
    H = N_HEADS_B
    log_g = jnp.log(1.0 - 2.0 ** (-5.0 - jnp.arange(H, dtype=F32)))
    i = jnp.arange(rows, dtype=F32)
    live = np.arange(rows) < chunk
    diff = i[:, None] - i[None, :]
    causal = (diff >= 0) & jnp.asarray(live[:, None] & live[None, :])
    dmat = jnp.where(causal[None], jnp.exp(jnp.where(causal, diff, 0.0)[None] * log_g[:, None, None]), 0.0)
    q_decay = jnp.where(jnp.asarray(live)[None], jnp.exp((i[None, :] + 1.0) * log_g[:, None]), 0.0)
    k_decay = jnp.where(jnp.asarray(live)[None], jnp.exp((chunk - 1.0 - i)[None, :] * log_g[:, None]), 0.0)
    c_decay = jnp.exp(chunk * log_g)
    qdec = jnp.broadcast_to(q_decay[:, :, None], (H, rows, VAL_DIM_B))
    kdec = jnp.broadcast_to(k_decay[:, :, None], (H, rows, KEY_DIM_B))
    cdec = jnp.broadcast_to(c_decay[:, None, None], (H, KEY_DIM_B, VAL_DIM_B))
    return dmat.astype(F32), qdec.astype(F32), kdec.astype(F32), cdec.astype(F32)


def _ret_mix(qb, kb, vb, gb, attn, state0, chunk):
    B, S, _ = qb.shape
    rows = RET_CHUNK
    nc = S // rows
    tables = _decay_tables(chunk, rows)

    def tok(b, c):
        return (b, c, 0)

    def tspec(w):
        return pl.BlockSpec((None, rows, w), tok)

    def const(shape):
        return pl.BlockSpec(shape, lambda b, c: (0,) * len(shape))

    ins = [qb, kb, vb, gb]
    in_specs = [tspec(QK_WIDTH_B), tspec(QK_WIDTH_B), tspec(WIDTH_B), tspec(WIDTH_B)]
    scratch = [pltpu.VMEM((N_HEADS_B, KEY_DIM_B, VAL_DIM_B), F32)]
    if isinstance(attn, tuple):
        outs_a, lses, dilations = attn
        expand = np.zeros((LANES, WIDTH_A), np.float32)
        for h in range(N_HEADS_A):
            expand[h, h * HEAD_DIM_A:(h + 1) * HEAD_DIM_A] = 1.0
        ins += list(outs_a) + list(lses) + [jnp.asarray(expand, BF16)]
        in_specs += ([pl.BlockSpec((None, rows // d, d * WIDTH_A), tok) for d in dilations]
                     + [pl.BlockSpec((None, rows // d, d * LANES), tok) for d in dilations] + [const((LANES, WIDTH_A))])
        scratch += [pltpu.VMEM((WIDTH_A // LANES, rows, LANES), F32), pltpu.VMEM((rows, LANES), F32)]
    else:
        dilations = ()
        ins.append(attn)
        in_specs.append(tspec(WIDTH_A))
    st_shape = (N_HEADS_B, KEY_DIM_B, VAL_DIM_B)
    st_spec = pl.BlockSpec((None,) + st_shape, lambda b, c: (b, 0, 0, 0))
    ins += [state0] + list(tables)
    in_specs += [st_spec] + [const(t.shape) for t in tables]
    return pl.pallas_call(
        functools.partial(_ret_body, dilations=tuple(dilations)),
        grid=(B, nc),
        in_specs=in_specs,
        out_specs=(tspec(WIDTH_A + WIDTH_B), st_spec),
        out_shape=(jax.ShapeDtypeStruct((B, S, WIDTH_A + WIDTH_B), BF16),
                   jax.ShapeDtypeStruct((B,) + st_shape, F32)),
        scratch_shapes=scratch,
        compiler_params=_params(2),
        name=f"ret_mix_{len(dilations)}",
    )(*ins)


SAMP_Q_ROWS = 64
SAMP_NEW_LANES = 128


def _samp_attn_body(q_ref, kt_ref, vt_ref, knt_ref, vnt_ref, bc_ref, bn_ref, hm_ref, o_ref, ko_ref, vo_ref, *, ds):
    q = q_ref[...]
    kt = kt_ref[...]
    vt = vt_ref[...]
    knt = knt_ref[...]
    vnt = vnt_ref[...]
    w = kt.shape[1]
    is_new = lax.broadcasted_iota(jnp.int32, knt.shape, 1) >= SAMP_NEW_LANES - ds
    for src, new, dst in ((kt, knt, ko_ref), (vt, vnt, vo_ref)):
        rolled = pltpu.roll(src, w - ds, 1)
        dst[:, :w - SAMP_NEW_LANES] = rolled[:, :w - SAMP_NEW_LANES]
        dst[:, w - SAMP_NEW_LANES:] = jnp.where(is_new, new, rolled[:, w - SAMP_NEW_LANES:])

    s_c = jnp.dot(q, kt.astype(BF16), preferred_element_type=F32)
    s_n = jnp.dot(q, knt.astype(BF16), preferred_element_type=F32)
    es_c, es_n, dens, lses = [], [], [], []
    for n in range(len(DILATED_BRANCHES)):
        sc = s_c + bc_ref[n]
        sn = s_n + bn_ref[n]
        m = jnp.maximum(jnp.max(sc, axis=-1, keepdims=True), jnp.max(sn, axis=-1, keepdims=True))
        ec = jnp.exp(sc - m)
        en = jnp.exp(sn - m)
        den = jnp.sum(ec, axis=-1, keepdims=True) + jnp.sum(en, axis=-1, keepdims=True)
        es_c.append(ec)
        es_n.append(en)
        dens.append(den)
        lses.append(m + jnp.log(den))
    mx = functools.reduce(jnp.maximum, lses)
    ws = [jnp.exp(l - mx) for l in lses]
    tot = functools.reduce(lambda a, b: a + b, ws)
    p_c = None
    p_n = None
    for w, den, ec, en in zip(ws, dens, es_c, es_n):
        coef = w / (tot * den)
        p_c = coef * ec if p_c is None else p_c + coef * ec
        p_n = coef * en if p_n is None else p_n + coef * en
    nt = (((1,), (1,)), ((), ()))
    o = (lax.dot_general(p_c.astype(BF16), vt.astype(BF16), nt, preferred_element_type=F32)
         + lax.dot_general(p_n.astype(BF16), vnt.astype(BF16), nt, preferred_element_type=F32))
    o = o * hm_ref[...]
    o_ref[...] = jnp.sum(o.reshape(SUBLANES, N_HEADS_A, WIDTH_A), axis=1)


def _samp_bias_tables(rel_bias, w_buf, ds):
    tabs_c, tabs_n = [], []
    s = np.arange(SUBLANES)[:, None]
    live_s = s < ds
    first_new = SAMP_NEW_LANES - ds
    for window, dil in DILATED_BRANCHES:
        n_keys = window // dil
        for keys, live_k, tabs in ((np.arange(w_buf)[None, :], True, tabs_c),
                                   (w_buf - first_new + np.arange(SAMP_NEW_LANES)[None, :],
                                    np.arange(SAMP_NEW_LANES)[None, :] >= first_new, tabs_n)):
            n = keys.shape[1]
            dist = w_buf + s - keys
            valid = (dist >= 0) & (dist % dil == 0) & (dist // dil <= n_keys) & live_k
            bias = _bias_by_bucket(rel_bias, _t5_bucket(np.clip(dist, 0, None))).transpose(1, 0, 2)
            tab = jnp.where(jnp.asarray(valid)[:, None, :], bias, NEG_INF)
            pad = jnp.where(jnp.asarray(np.broadcast_to(live_k, dist.shape))[:, None, :], 0.0, NEG_INF)
            tab = jnp.where(jnp.asarray(live_s)[:, :, None], tab, pad)
            tabs.append(tab.reshape(SAMP_Q_ROWS, n))
    return jnp.stack(tabs_c), jnp.stack(tabs_n)


def _samp_attn(qa, cache_kt, cache_vt, knt, vnt, rel_bias, ds):
    DB, DS, _ = qa.shape
    W = cache_kt.shape[2]
    head_of_lane = np.arange(WIDTH_A) // HEAD_DIM_A
    hmask = (np.arange(SAMP_Q_ROWS)[:, None] % N_HEADS_A == head_of_lane[None, :])
    q8 = jnp.pad(qa, ((0, 0), (0, SUBLANES - DS), (0, 0)))
    q_rows = jnp.where(jnp.asarray(hmask)[None], jnp.repeat(q8, N_HEADS_A, axis=1), jnp.zeros((), BF16))
    bias_c, bias_n = _samp_bias_tables(rel_bias, W, DS)

    def per_b(rows, w):
        return pl.BlockSpec((None, rows, w), lambda b: (b, 0, 0))

    def const(a):
        return pl.BlockSpec(a.shape, lambda b: (0,) * a.ndim)

    hm = jnp.asarray(hmask, F32)
    return pl.pallas_call(
        functools.partial(_samp_attn_body, ds=ds),
        grid=(DB,),
        in_specs=[per_b(SAMP_Q_ROWS, WIDTH_A), per_b(WIDTH_A, W), per_b(WIDTH_A, W),
                  per_b(WIDTH_A, SAMP_NEW_LANES), per_b(WIDTH_A, SAMP_NEW_LANES),
                  const(bias_c), const(bias_n), const(hm)],
        out_specs=(per_b(SUBLANES, WIDTH_A), per_b(WIDTH_A, W), per_b(WIDTH_A, W)),
        out_shape=(jax.ShapeDtypeStruct((DB, SUBLANES, WIDTH_A), F32),
                   jax.ShapeDtypeStruct((DB, WIDTH_A, W), F32), jax.ShapeDtypeStruct((DB, WIDTH_A, W), F32)),
        compiler_params=_params(1),
        name="samp_attn",
    )(q_rows, cache_kt, cache_vt, knt, vnt, bias_c, bias_n, hm)


def _route_body(cat_ref, x_ref, wout_ref, g_ref, b_ref, rwh_ref, rwl_ref, rb_ref, tril_ref, base_ref,
                h_ref, hp_ref, route_ref, cnt_ref, *, alpha):
    @pl.when(pl.program_id(0) == 0)
    def _():
        cnt_ref[...] = base_ref[...]

    mix = jnp.dot(cat_ref[...], wout_ref[...], preferred_element_type=F32)
    y = alpha * x_ref[...] + mix
    mu = jnp.mean(y, axis=-1, keepdims=True)
    var = jnp.mean(jnp.square(y - mu), axis=-1, keepdims=True)
    h = (y - mu) * lax.rsqrt(var + LN_EPS) * g_ref[...] + b_ref[...]
    h_ref[...] = h
    hb = h.astype(BF16)
    hp_ref[...] = _pack_bf16_pairs(h)
    hl = (h - hb.astype(F32)).astype(BF16)
    logits = (jnp.dot(hb, rwh_ref[...], preferred_element_type=F32)
              + jnp.dot(hb, rwl_ref[...], preferred_element_type=F32)
              + jnp.dot(hl, rwh_ref[...], preferred_element_type=F32)) + rb_ref[...]

    tm = logits.shape[0]
    lane = lax.broadcasted_iota(jnp.int32, (tm, LANES), 1)
    work = logits
    vals, idxs = [], []
    for _ in range(TOP_K):
        m = jnp.max(work, axis=-1, keepdims=True)
        idx = jnp.min(jnp.where(work == m, lane, LANES), axis=-1, keepdims=True)
        vals.append(m)
        idxs.append(idx)
        work = jnp.where(lane == idx, -jnp.inf, work)
    es = [jnp.exp(v - vals[0]) for v in vals]
    tot = functools.reduce(lambda a, b: a + b, es)
    onehot = jnp.zeros((tm, LANES), F32)
    for idx in idxs:
        onehot = onehot + (lane == idx).astype(F32)
    before = jnp.dot(tril_ref[...], onehot.astype(BF16), preferred_element_type=F32) + cnt_ref[0:1, :]
    route = jnp.zeros((tm, LANES), F32)
    for k in range(TOP_K):
        rank = jnp.sum(jnp.where(lane == idxs[k], before, 0.0), axis=-1, keepdims=True)
        route = jnp.where(lane == k, idxs[k].astype(F32), route)
        route = jnp.where(lane == TOP_K + k, es[k] / tot, route)
        route = jnp.where(lane == 2 * TOP_K + k, rank, route)
    route_ref[...] = route
    cnt_ref[...] = cnt_ref[...] + jnp.sum(onehot, axis=0, keepdims=True)


def _out_route(cat, x2d, w_out_bf, ln_g, ln_b, rw_hi, rw_lo, rb, base, tm, alpha):
    T = x2d.shape[0]
    tril = jnp.asarray(np.tril(np.ones((tm, tm), np.float32), -1), BF16)

    def tok(i):
        return (i, 0)

    def const(a):
        return pl.BlockSpec(a.shape, lambda i: (0,) * a.ndim)

    ins = (cat, x2d, w_out_bf, ln_g, ln_b, rw_hi, rw_lo, rb, tril, base)
    in_specs = [pl.BlockSpec((tm, cat.shape[1]), tok), pl.BlockSpec((tm, D_MODEL), tok)] + [const(a) for a in ins[2:]]
    return pl.pallas_call(
        functools.partial(_route_body, alpha=alpha),
        grid=(T // tm,),
        in_specs=in_specs,
        out_specs=(pl.BlockSpec((tm, D_MODEL), tok), pl.BlockSpec((tm, D_MODEL // 2), tok),
                   pl.BlockSpec((tm, LANES), tok), pl.BlockSpec((SUBLANES, LANES), lambda i: (0, 0))),
        out_shape=(jax.ShapeDtypeStruct((T, D_MODEL), F32), jax.ShapeDtypeStruct((T, D_MODEL // 2), jnp.int32),
                   jax.ShapeDtypeStruct((T, LANES), F32), jax.ShapeDtypeStruct((SUBLANES, LANES), F32)),
        compiler_params=_params(1),
        name="out_route",
    )(*ins)


MOE_CAST_ROWS = 128
WEIGHT_DMA_PRIORITY = 1


def _moe_body(be_ref, first_ref, next_ref, nused_ref, x_ref, bgu_ref, bdn_ref, wgu_hbm, wdn_hbm, y_ref,
              gu_stage, dn_stage, wgu_bf, wdn_bf, sem):
    b = pl.program_id(0)
    d_exp = wdn_bf.shape[0]

    def fetch(e):
        return (pltpu.make_async_copy(wgu_hbm.at[e], gu_stage, sem.at[0]),
                pltpu.make_async_copy(wdn_hbm.at[e], dn_stage, sem.at[1]))

    @pl.when(b == 0)
    def _():
        for c in fetch(be_ref[0]):
            c.start(priority=WEIGHT_DMA_PRIORITY)

    @pl.when(first_ref[b] == 1)
    def _():
        for c in fetch(be_ref[b]):
            c.wait()

        def cast_gu(i, c):
            r = pl.ds(pl.multiple_of(i * MOE_CAST_ROWS, MOE_CAST_ROWS), MOE_CAST_ROWS)
            wgu_bf[r, :] = gu_stage[r, :].astype(BF16)
            return c

        def cast_dn(i, c):
            r = pl.ds(pl.multiple_of(i * MOE_CAST_ROWS, MOE_CAST_ROWS), MOE_CAST_ROWS)
            wdn_bf[r, :] = dn_stage[r, :].astype(BF16)
            return c

        lax.fori_loop(0, gu_stage.shape[0] // MOE_CAST_ROWS, cast_gu, 0)
        lax.fori_loop(0, d_exp // MOE_CAST_ROWS, cast_dn, 0)

        @pl.when(next_ref[b] >= 0)
        def _():
            for c in fetch(next_ref[b]):
                c.start(priority=WEIGHT_DMA_PRIORITY)

    @pl.when(b < nused_ref[0])
    def _():
        x_lo, x_hi = _unpack_bf16_pairs(x_ref[...])
        x_lo = x_lo.astype(BF16)
        x_hi = x_hi.astype(BF16)
        dh = x_lo.shape[1]

        def xw(cols):
            return (jnp.dot(x_lo, wgu_bf[:dh, cols], preferred_element_type=F32)
                    + jnp.dot(x_hi, wgu_bf[dh:, cols], preferred_element_type=F32) + bgu_ref[:, cols])

        half = d_exp // 2
        y = None
        for c in range(2):
            lo = c * half
            gate = jnp.minimum(xw(slice(lo, lo + half)), SWIGLU_LIMIT)
            up = jnp.clip(xw(slice(d_exp + lo, d_exp + lo + half)), -SWIGLU_LIMIT, SWIGLU_LIMIT)
            act = (up + 1.0) * gate * (1.0 / (1.0 + jnp.exp(-SWIGLU_ALPHA * gate)))
            part = jnp.dot(act.astype(BF16), wdn_bf[lo:lo + half, :], preferred_element_type=F32)
            y = part if y is None else y + part
        y_ref[...] = _pack_bf16_pairs(y + bdn_ref[...])

    @pl.when(b >= nused_ref[0])
    def _():
        y_ref[...] = jnp.zeros_like(y_ref)


def _moe_ffn(xs, rows, block_e, n_used, has_rows, w_gu, b_gu, w_dn, b_dn):
    E, D, two_de = w_gu.shape
    d_exp = two_de // 2
    nb = rows // MOE_BLOCK
    idx = jnp.arange(nb, dtype=jnp.int32)
    first = ((idx < n_used[0]) & ((idx == 0) | (block_e != jnp.roll(block_e, 1)))).astype(jnp.int32)
    ids = jnp.arange(E, dtype=jnp.int32)
    later = jnp.where((ids[None, :] > ids[:, None]) & has_rows[None, :], ids[None, :], E).min(axis=1)
    next_of = jnp.where(later == E, -1, later).astype(jnp.int32)
    next_e = jnp.sum(jnp.where(block_e[:, None] == ids[None, :], next_of[None, :], 0), axis=1).astype(jnp.int32)
    grid_spec = pltpu.PrefetchScalarGridSpec(
        num_scalar_prefetch=4,
        grid=(nb,),
        in_specs=[
            pl.BlockSpec((MOE_BLOCK, D // 2), lambda b, be, fi, nx, nu: (b, 0)),
            pl.BlockSpec((None, 1, two_de), lambda b, be, fi, nx, nu: (be[b], 0, 0)),
            pl.BlockSpec((None, 1, D), lambda b, be, fi, nx, nu: (be[b], 0, 0)),
            pl.BlockSpec(memory_space=pl.ANY),
            pl.BlockSpec(memory_space=pl.ANY),
        ],
        out_specs=pl.BlockSpec((MOE_BLOCK, D // 2), lambda b, be, fi, nx, nu: (b, 0)),
        scratch_shapes=[pltpu.VMEM((D, two_de), F32), pltpu.VMEM((d_exp, D), F32),
                        pltpu.VMEM((D, two_de), BF16), pltpu.VMEM((d_exp, D), BF16),
                        pltpu.SemaphoreType.DMA((2,))],
    )
    return pl.pallas_call(
        _moe_body,
        grid_spec=grid_spec,
        out_shape=jax.ShapeDtypeStruct((rows, D // 2), jnp.int32),
        compiler_params=_params(1),
        name="moe_ffn",
    )(block_e, first, next_e, n_used, xs, b_gu.reshape(E, 1, two_de), b_dn.reshape(E, 1, D), w_gu, w_dn)


def _ple_body(h_ref, ys_ref, route_ref, p_ref, g_ref, b_ref, wpg_ref, bpg_ref, wpp_ref, o_ref, *, alpha):
    route = route_ref[...]
    f_lo = None
    f_hi = None
    for k in range(TOP_K):
        lo, hi = _unpack_bf16_pairs(ys_ref[k])
        g = route[:, TOP_K + k:TOP_K + k + 1]
        f_lo = g * lo if f_lo is None else f_lo + g * lo
        f_hi = g * hi if f_hi is None else f_hi + g * hi
    y = alpha * h_ref[...] + jnp.concatenate([f_lo, f_hi], axis=1)
    mu = jnp.mean(y, axis=-1, keepdims=True)
    var = jnp.mean(jnp.square(y - mu), axis=-1, keepdims=True)
    h2 = (y - mu) * lax.rsqrt(var + LN_EPS) * g_ref[...] + b_ref[...]
    z = jnp.dot(h2.astype(BF16), wpg_ref[...], preferred_element_type=F32) + bpg_ref[...]
    gate = 1.0 / (1.0 + jnp.exp(-z))
    proj = jnp.dot(p_ref[...].astype(BF16), wpp_ref[...], preferred_element_type=F32)
    o_ref[...] = h2 + gate * proj


def _ffn_ple(h, y_slots, route, p, ln_g, ln_b, w_pg_bf, b_pg, w_pp_bf, tm, tile0, alpha):
    T = h.shape[0]

    def tok(i):
        return (i, 0)

    def const(a):
        return pl.BlockSpec(a.shape, lambda i: (0,) * a.ndim)

    consts = (ln_g, ln_b, w_pg_bf, b_pg, w_pp_bf)
    return pl.pallas_call(
        functools.partial(_ple_body, alpha=alpha),
        grid=(T // tm,),
        in_specs=[pl.BlockSpec((tm, D_MODEL), tok),
                  pl.BlockSpec((TOP_K, tm, D_MODEL // 2), lambda i: (0, tile0 + i, 0)),
                  pl.BlockSpec((tm, LANES), tok),
                  pl.BlockSpec((tm, p.shape[1]), tok)] + [const(a) for a in consts],
        out_specs=pl.BlockSpec((tm, D_MODEL), tok),
        out_shape=jax.ShapeDtypeStruct((T, D_MODEL), F32),
        compiler_params=_params(1),
        name="ffn_ple",
    )(h, y_slots, route, p, *consts)


def _row(v):
    return v.reshape(1, -1).astype(F32)


def kernel(x_prompt, x_sample, cache_win_k, cache_win_v, state_ret, p_prompt, p_sample, rel_bias, w_in, w_out,
           ln1_g, ln1_b, router_w, router_b, w_gate_up, b_gate_up, w_down, b_down, ln2_g, ln2_b,
           w_ple_gate, b_ple_gate, w_ple_proj):
    B, S, D = x_prompt.shape
    DB, DS, _ = x_sample.shape
    depth = w_in.shape[0]
    w_buf = cache_win_k.shape[2]
    n_exp = router_w.shape[-1]
    alpha = (2.0 * depth) ** 0.25
    assert depth == 1 and D == D_MODEL
    assert S % (BLK * MAX_DIL) == 0 and S >= WINDOW_MAX and w_buf == WINDOW_MAX and DS <= SUBLANES
    tm_p = 512
    Tp, Ts = B * S, DB * DS
    assert Tp % tm_p == 0 and Ts % SUBLANES == 0

    i = 0
    w_in_bf = w_in[i].astype(BF16)
    w_out_bf = w_out[i].astype(BF16)
    w_pg_bf = w_ple_gate[i].astype(BF16)
    w_pp_bf = w_ple_proj[i].astype(BF16)
    rw = jnp.pad(router_w[i], ((0, 0), (0, LANES - n_exp)))
    rw_hi = rw.astype(BF16)
    rw_lo = (rw - rw_hi.astype(F32)).astype(BF16)
    rb = jnp.pad(router_b[i], (0, LANES - n_exp), constant_values=NEG_INF).reshape(1, LANES)

    cos_p, sin_p = _rotary_tables(jnp.arange(S, dtype=jnp.int32))
    dils = tuple(d for _, d in DILATED_BRANCHES)
    extra = tuple(d for d in dils if d > 1)
    outs = _in_proj(x_prompt.reshape(Tp, D), w_in_bf, cos_p, sin_p, tm_p, S // tm_p, WINDOW_MAX // tm_p, extra)
    qa, ka, va, qb, kb, vb, gb, kf, vf = outs[:9]
    qkv = {1: (qa, ka, va)}
    for t, d in enumerate(extra):
        qkv[d] = tuple(outs[9 + j * len(extra) + t] for j in range(3))

    def seq(t):
        return t.reshape(B, S, t.shape[-1])

    outs_a, lses = [], []
    for window, dil in DILATED_BRANCHES:
        q_d, k_d, v_d = (t.reshape(B, S // dil, dil * WIDTH_A) for t in qkv[dil])
        o_n, l_n = _dilated_branch(q_d, k_d, v_d, _attn_bias_tables(rel_bias, window, dil), dil)
        outs_a.append(o_n)
        lses.append(l_n)
    st_zero = jnp.zeros((B, N_HEADS_B, KEY_DIM_B, VAL_DIM_B), F32)
    cat_p, rst_p = _ret_mix(seq(qb), seq(kb), seq(vb), seq(gb), (outs_a, lses, dils), st_zero, RET_CHUNK)
    base0 = jnp.zeros((SUBLANES, LANES), F32)
    h_p, hp_p, route_p, cnt_p = _out_route(cat_p.reshape(Tp, -1), x_prompt.reshape(Tp, D), w_out_bf,
                                           _row(ln1_g[i]), _row(ln1_b[i]), rw_hi, rw_lo, rb, base0, tm_p, alpha)

    pos_s = jnp.tile(PAST_LEN + jnp.arange(DS, dtype=jnp.int32), DB)
    cos_s, sin_s = _rotary_tables(pos_s)
    qa_s, _, _, qb_s, kb_s, vb_s, gb_s, kf_s, vf_s = _in_proj(x_sample.reshape(Ts, D), w_in_bf, cos_s, sin_s,
                                                             Ts, 1, 1)

    def positions_minor(t):
        return jnp.transpose(t, (0, 2, 3, 1)).reshape(DB, WIDTH_A, t.shape[1])

    def positions_major(t):
        return jnp.transpose(t.reshape(t.shape[0], N_HEADS_A, HEAD_DIM_A, t.shape[2]), (0, 3, 1, 2))[None]

    def new_columns(t):
        t = jnp.transpose(t.reshape(WIDTH_A, DB, DS), (1, 0, 2))
        return jnp.pad(t, ((0, 0), (0, 0), (SAMP_NEW_LANES - DS, 0)))

    oa_s, kt_out, vt_out = _samp_attn(qa_s.reshape(DB, DS, WIDTH_A), positions_minor(cache_win_k[i]),
                                      positions_minor(cache_win_v[i]), new_columns(kf_s), new_columns(vf_s),
                                      rel_bias, DS)

    def pad_rows(t, rows):
        t = t.reshape(DB, -1, t.shape[-1])
        return jnp.pad(t, ((0, 0), (0, rows - t.shape[1]), (0, 0)))

    cat_s, rst_s = _ret_mix(pad_rows(qb_s, RET_CHUNK), pad_rows(kb_s, RET_CHUNK), pad_rows(vb_s, RET_CHUNK),
                            pad_rows(gb_s, RET_CHUNK), pad_rows(oa_s, RET_CHUNK), state_ret[i].astype(F32), DS)
    cat_s = cat_s[:, :DS].reshape(Ts, -1)
    h_s, hp_s, route_s, cnt = _out_route(cat_s, x_sample.reshape(Ts, D), w_out_bf, _row(ln1_g[i]), _row(ln1_b[i]),
                                         rw_hi, rw_lo, rb, cnt_p, Ts, alpha)

    T = Tp + Ts
    t_align = SC_WORKERS * SC_ALIGN
    T_pad = -(-T // t_align) * t_align
    route = jnp.concatenate([route_p, route_s], axis=0)
    top_idx = route[:, :TOP_K].astype(jnp.int32)
    rank = route[:, 2 * TOP_K:3 * TOP_K].astype(jnp.int32)
    counts = cnt[0, :n_exp].astype(jnp.int32)
    padded = (counts + MOE_BLOCK - 1) // MOE_BLOCK * MOE_BLOCK
    pad_end = jnp.cumsum(padded)
    pad_start = pad_end - padded
    dest = pad_start[top_idx] + rank
    n_blocks = -(-T * TOP_K // MOE_BLOCK) + n_exp
    rows = n_blocks * MOE_BLOCK
    block_start = jnp.arange(n_blocks, dtype=jnp.int32) * MOE_BLOCK
    block_e = jnp.minimum(jnp.sum(pad_end[None, :] <= block_start[:, None], axis=1), n_exp - 1).astype(jnp.int32)
    n_used = (pad_end[-1:] // MOE_BLOCK).astype(jnp.int32)
    n_fill = T_pad - T
    spare = rows + jnp.arange(n_fill * TOP_K, dtype=jnp.int32).reshape(n_fill, TOP_K)
    dest_sc = jnp.concatenate([dest, spare], axis=0).T.reshape(-1)
    dest_ga = jnp.concatenate([dest, jnp.zeros((n_fill, TOP_K), jnp.int32)], axis=0).T.reshape(-1)
    hp_all = jnp.concatenate([hp_p, hp_s, jnp.zeros((n_fill, D // 2), jnp.int32)], axis=0)
    xs = _sc_scatter_rows(hp_all, dest_sc, rows + n_fill * TOP_K)
    ys = _moe_ffn(xs, rows, block_e, n_used, padded > 0, w_gate_up[i], b_gate_up[i], w_down[i], b_down[i])
    y_slots = _sc_gather_rows(ys, dest_ga).reshape(TOP_K, T_pad, D // 2)

    ple_args = (_row(ln2_g[i]), _row(ln2_b[i]), w_pg_bf, _row(b_ple_gate[i]), w_pp_bf)
    assert Tp % Ts == 0
    y_p = _ffn_ple(h_p, y_slots, route_p, p_prompt[i].reshape(Tp, D_PLE), *ple_args, tm_p, 0, alpha)
    y_s = _ffn_ple(h_s, y_slots, route_s, p_sample[i].reshape(Ts, D_PLE), *ple_args, Ts, Tp // Ts, alpha)

    return (y_p.reshape(B, S, D), y_s.reshape(DB, DS, D), positions_major(kf), positions_major(vf),
            rst_p[None], positions_major(kt_out), positions_major(vt_out), rst_s[None])
```

```python
import functools

import numpy as np
import jax
import jax.numpy as jnp
from jax import lax
from jax.experimental import pallas as pl
from jax.experimental.pallas import tpu as pltpu
from jax.experimental.pallas import tpu_sc as plsc

F32 = jnp.float32
BF16 = jnp.bfloat16

D_MODEL = 1024
D_PLE = 256
N_HEADS_A = 8
HEAD_DIM_A = 64
WIDTH_A = N_HEADS_A * HEAD_DIM_A
DILATED_BRANCHES = ((128, 1), (512, 4), (2048, 16))
BLK = 128
WINDOW_MAX = 2048
MAX_DIL = 16
NUM_BUCKETS = 32
MAX_DISTANCE = 2048
N_HEADS_B = 4
KEY_DIM_B = 64
VAL_DIM_B = 128
QK_WIDTH_B = N_HEADS_B * KEY_DIM_B
WIDTH_B = N_HEADS_B * VAL_DIM_B
RET_CHUNK = 128
GN_EPS = 1e-6
TOP_K = 4
SWIGLU_LIMIT = 7.0
SWIGLU_ALPHA = 1.702
LN_EPS = 1e-5
NEG_INF = -1e30
PAST_LEN = 16384
MOE_BLOCK = 512
LANES = 128
SUBLANES = 8
VMEM_LIMIT = 52 * 1024 * 1024


def _params(n_axes, vmem=VMEM_LIMIT):
    return pltpu.CompilerParams(dimension_semantics=("arbitrary",) * n_axes, vmem_limit_bytes=vmem)


def _t5_bucket(dist):
    dist = np.asarray(dist, dtype=np.int32)
    max_exact = NUM_BUCKETS // 2
    d = np.maximum(dist, 1).astype(np.float32)
    large = max_exact + (np.log(d / max_exact) / np.log(MAX_DISTANCE / max_exact)
                         * (NUM_BUCKETS - max_exact)).astype(np.int32)
    large = np.minimum(large, NUM_BUCKETS - 1)
    return np.where(dist < max_exact, dist, large).astype(np.int32)


def _bias_by_bucket(rel_bias, buckets):
    b = jnp.asarray(buckets, jnp.int32)
    ids = jnp.arange(NUM_BUCKETS, dtype=jnp.int32).reshape((NUM_BUCKETS, 1) + (1,) * b.ndim)
    vals = rel_bias.astype(F32).reshape((NUM_BUCKETS, rel_bias.shape[1]) + (1,) * b.ndim)
    return jnp.sum(jnp.where(b[None, None] == ids, vals, 0.0), axis=0)


def _pack_bf16_pairs(v):
    w = v.shape[1] // 2
    lo = lax.bitcast_convert_type(v[:, :w].astype(BF16).astype(F32), jnp.uint32) >> 16
    hi = lax.bitcast_convert_type(v[:, w:].astype(BF16).astype(F32), jnp.uint32) & jnp.uint32(0xFFFF0000)
    return lax.bitcast_convert_type(lo | hi, jnp.int32)


def _unpack_bf16_pairs(p):
    u = lax.bitcast_convert_type(p, jnp.uint32)
    lo = lax.bitcast_convert_type(u << 16, F32)
    hi = lax.bitcast_convert_type(u & jnp.uint32(0xFFFF0000), F32)
    return lo, hi


SC_CORES = 2
SC_SUBCORES = 16
SC_WORKERS = SC_CORES * SC_SUBCORES
SC_ALIGN = 8
SC_CHUNK_ROWS = 80


def _sc_mesh():
    return plsc.VectorSubcoreMesh(core_axis_name="c", subcore_axis_name="s")


def _sc_chunk(per_worker):
    c = max(d for d in range(SC_ALIGN, SC_CHUNK_ROWS + 1, SC_ALIGN) if per_worker % d == 0)
    return c


def _sc_scatter_rows(src, dest_flat, n_out):
    T, W = src.shape
    K = dest_flat.shape[0] // T
    per_w = T // SC_WORKERS
    assert per_w * SC_WORKERS == T and per_w % SC_ALIGN == 0
    chunk = _sc_chunk(per_w)
    n_chunks = per_w // chunk

    @functools.partial(
        pl.kernel, mesh=_sc_mesh(), out_type=jax.ShapeDtypeStruct((n_out, W), src.dtype),
        scratch_types=[pltpu.VMEM((chunk, W), src.dtype)] * 2 + [pltpu.VMEM((chunk,), jnp.int32)] * (2 * K)
        + [pltpu.SemaphoreType.DMA] * 2,
        name="sc_scatter_rows")
    def k(src_hbm, dest_hbm, out_hbm, rows_a, rows_b, *rest):
        bufs = ((rows_a, rest[:K], rest[2 * K]), (rows_b, rest[K:2 * K], rest[2 * K + 1]))
        base = (lax.axis_index("s") * SC_CORES + lax.axis_index("c")) * per_w

        def scatters(buf):
            rows_v, idx_vs, sem = buf
            return [pltpu.make_async_copy(rows_v, out_hbm.at[idx_vs[kk]], sem) for kk in range(K)]

        def step(j, buf):
            rows_v, idx_vs, _ = buf

            @pl.when(j >= 2)
            def _():
                for c in scatters(buf):
                    c.wait()

            off = pl.multiple_of(base + j * chunk, SC_ALIGN)
            pltpu.sync_copy(src_hbm.at[pl.ds(off, chunk)], rows_v)
            for kk in range(K):
                pltpu.sync_copy(dest_hbm.at[pl.ds(kk * T + off, chunk)], idx_vs[kk])
            for c in scatters(buf):
                c.start()

        @pl.loop(0, n_chunks)
        def _(j):
            for parity in range(2):
                @pl.when(j % 2 == parity)
                def _():
                    step(j, bufs[parity])

        for j in range(max(n_chunks - 2, 0), n_chunks):
            for c in scatters(bufs[j % 2]):
                c.wait()

    return k(src, dest_flat)


def _sc_gather_rows(table, idx):
    B = idx.shape[0]
    W = table.shape[1]
    per_w = B // SC_WORKERS
    assert per_w * SC_WORKERS == B and per_w % SC_ALIGN == 0
    chunk = _sc_chunk(per_w)
    n_chunks = per_w // chunk

    @functools.partial(
        pl.kernel, mesh=_sc_mesh(), out_type=jax.ShapeDtypeStruct((B, W), table.dtype),
        scratch_types=[pltpu.VMEM((chunk,), jnp.int32)] * 2 + [pltpu.VMEM((chunk, W), table.dtype)] * 2
        + [pltpu.SemaphoreType.DMA] * 2,
        name="sc_gather_rows")
    def k(table_hbm, idx_hbm, out_hbm, idx_a, idx_b, rows_a, rows_b, sem_a, sem_b):
        bufs = ((idx_a, rows_a, sem_a), (idx_b, rows_b, sem_b))
        base = (lax.axis_index("s") * SC_CORES + lax.axis_index("c")) * per_w

        def gather(buf):
            idx_v, rows_v, sem = buf
            return pltpu.make_async_copy(table_hbm.at[idx_v], rows_v, sem)

        def finish(j, buf):
            gather(buf).wait()
            off = pl.multiple_of(base + j * chunk, SC_ALIGN)
            pltpu.sync_copy(buf[1], out_hbm.at[pl.ds(off, chunk)])

        def step(j, buf, other):
            off = pl.multiple_of(base + j * chunk, SC_ALIGN)
            pltpu.sync_copy(idx_hbm.at[pl.ds(off, chunk)], buf[0])
            gather(buf).start()

            @pl.when(j >= 1)
            def _():
                finish(j - 1, other)

        @pl.loop(0, n_chunks)
        def _(j):
            for parity in range(2):
                @pl.when(j % 2 == parity)
                def _():
                    step(j, bufs[parity], bufs[1 - parity])

        finish(n_chunks - 1, bufs[(n_chunks - 1) % 2])

    return k(table, idx)


def _in_proj_body(x_ref, w_ref, cos_ref, sin_ref, *refs, seq_tiles, first_win, dilations):
    qa_ref, ka_ref, va_ref, qb_ref, kb_ref, vb_ref, gb_ref, kt_ref, vt_ref = refs[:9]
    n_d = len(dilations)
    dil_refs = [refs[9 + t * n_d:9 + (t + 1) * n_d] for t in range(3)]
    zs_ref = refs[9 + 3 * n_d] if n_d else None
    x = x_ref[...].astype(BF16)
    in_window = pl.program_id(0) % seq_tiles >= first_win

    def emit(z, token_ref, class_refs):
        token_ref[...] = z.astype(BF16)
        if not class_refs:
            return
        for c in range(WIDTH_A // LANES):
            zs_ref[c] = z[:, c * LANES:(c + 1) * LANES]
        for d, ref in zip(dilations, class_refs):
            n = z.shape[0] // d
            for r in range(d):
                for c in range(WIDTH_A // LANES):
                    lo = r * WIDTH_A + c * LANES
                    ref[:, lo:lo + LANES] = zs_ref[c, pl.ds(r, n, stride=d), :].astype(BF16)

    def proj(lo, hi):
        return jnp.dot(x, w_ref[:, lo:hi], preferred_element_type=F32)

    o = 0
    emit(proj(o, o + WIDTH_A) * (HEAD_DIM_A ** -0.5), qa_ref, dil_refs[0])
    o += WIDTH_A
    ka = proj(o, o + WIDTH_A)
    emit(ka, ka_ref, dil_refs[1])

    @pl.when(in_window)
    def _():
        kt_ref[...] = ka.T
    o += WIDTH_A
    va = proj(o, o + WIDTH_A)
    emit(va, va_ref, dil_refs[2])

    @pl.when(in_window)
    def _():
        vt_ref[...] = va.T
    o += WIDTH_A

    cos = cos_ref[...]
    sin = sin_ref[...]
    lane = lax.broadcasted_iota(jnp.int32, cos.shape, 1)
    first_half = (lane % KEY_DIM_B) < (KEY_DIM_B // 2)

    def rot(z):
        sw = jnp.where(first_half, pltpu.roll(z, QK_WIDTH_B - KEY_DIM_B // 2, 1), pltpu.roll(z, KEY_DIM_B // 2, 1))
        return z * cos + sw * sin

    qb_ref[...] = rot(proj(o, o + QK_WIDTH_B)).astype(BF16)
    o += QK_WIDTH_B
    kb_ref[...] = (rot(proj(o, o + QK_WIDTH_B)) * (KEY_DIM_B ** -0.5)).astype(BF16)
    o += QK_WIDTH_B
    vb_ref[...] = proj(o, o + WIDTH_B).astype(BF16)
    o += WIDTH_B
    gb_ref[...] = proj(o, o + WIDTH_B).astype(BF16)


def _in_proj(x2d, w_bf, cos_t, sin_t, tm, seq_tiles, win_tiles, dilations=()):
    T = x2d.shape[0]
    nt = T // tm
    n_seq = nt // seq_tiles
    j0 = seq_tiles - win_tiles

    def tok(i):
        return (i, 0)

    def tab(i):
        return (i % seq_tiles, 0)

    def win(i):
        return (i // seq_tiles, 0, jnp.maximum(i % seq_tiles - j0, 0))

    def tspec(w):
        return pl.BlockSpec((tm, w), tok)

    out_shape = (
        jax.ShapeDtypeStruct((T, WIDTH_A), BF16), jax.ShapeDtypeStruct((T, WIDTH_A), BF16),
        jax.ShapeDtypeStruct((T, WIDTH_A), BF16),
        jax.ShapeDtypeStruct((T, QK_WIDTH_B), BF16), jax.ShapeDtypeStruct((T, QK_WIDTH_B), BF16),
        jax.ShapeDtypeStruct((T, WIDTH_B), BF16), jax.ShapeDtypeStruct((T, WIDTH_B), BF16),
        jax.ShapeDtypeStruct((n_seq, WIDTH_A, win_tiles * tm), F32),
        jax.ShapeDtypeStruct((n_seq, WIDTH_A, win_tiles * tm), F32),
    ) + tuple(jax.ShapeDtypeStruct((T // d, d * WIDTH_A), BF16) for _ in range(3) for d in dilations)
    class_specs = tuple(pl.BlockSpec((tm // d, d * WIDTH_A), tok) for _ in range(3) for d in dilations)
    return pl.pallas_call(
        functools.partial(_in_proj_body, seq_tiles=seq_tiles, first_win=j0, dilations=tuple(dilations)),
        grid=(nt,),
        in_specs=[tspec(D_MODEL), pl.BlockSpec(w_bf.shape, lambda i: (0, 0)),
                  pl.BlockSpec((tm, QK_WIDTH_B), tab), pl.BlockSpec((tm, QK_WIDTH_B), tab)],
        out_specs=(tspec(WIDTH_A), tspec(WIDTH_A), tspec(WIDTH_A), tspec(QK_WIDTH_B), tspec(QK_WIDTH_B),
                   tspec(WIDTH_B), tspec(WIDTH_B),
                   pl.BlockSpec((None, WIDTH_A, tm), win), pl.BlockSpec((None, WIDTH_A, tm), win)) + class_specs,
        out_shape=out_shape,
        scratch_shapes=[pltpu.VMEM((WIDTH_A // LANES, tm, LANES), F32)] if dilations else [],
        compiler_params=_params(1),
        name="in_proj",
    )(x2d, w_bf, cos_t, sin_t)


def _rotary_tables(pos):
    half = KEY_DIM_B // 2
    inv_freq = 1.0 / (10000.0 ** jnp.linspace(0.0, 1.0, half, dtype=F32))
    ang = pos.astype(F32)[:, None] * inv_freq[None, :]
    cos = jnp.cos(ang)
    sin = jnp.sin(ang)
    cos_h = jnp.concatenate([cos, cos], axis=-1)
    sin_h = jnp.concatenate([-sin, sin], axis=-1)
    return jnp.tile(cos_h, (1, N_HEADS_B)), jnp.tile(sin_h, (1, N_HEADS_B))


ATTN_BLOCKS_PER_STEP = 4


def _attn_body(q_ref, k_ref, v_ref, bias_ref, o_ref, lse_ref, kb_ref, vb_ref):
    n = pl.program_id(2)
    rows = q_ref.shape[0]

    @pl.when(n == 0)
    def _():
        kb_ref[:BLK, :] = jnp.zeros((BLK, WIDTH_A), BF16)
        vb_ref[:BLK, :] = jnp.zeros((BLK, WIDTH_A), BF16)

    kb_ref[BLK:, :] = k_ref[...]
    vb_ref[BLK:, :] = v_ref[...]
    lane = lax.broadcasted_iota(jnp.int32, (BLK, LANES), 1)
    low = lane < HEAD_DIM_A
    nt = (((1,), (1,)), ((), ()))

    def sub_block(j, carry):
        r0 = pl.multiple_of(j * BLK, BLK)
        table = jnp.where((n == 0) & (j == 0), 0, 1)
        lse_tile = jnp.zeros((BLK, LANES), F32)
        for p in range(N_HEADS_A // 2):
            cs = slice(p * LANES, (p + 1) * LANES)
            qp = q_ref[pl.ds(r0, BLK), cs]
            kp = kb_ref[pl.ds(r0, 2 * BLK), cs]
            vp = vb_ref[pl.ds(r0, 2 * BLK), cs]
            zero = jnp.zeros_like(qp)
            outs = []
            for half, qh in enumerate((jnp.where(low, qp, zero), jnp.where(low, zero, qp))):
                h = 2 * p + half
                s = lax.dot_general(qh, kp, nt, preferred_element_type=F32) + bias_ref[table, h]
                m = jnp.max(s, axis=-1, keepdims=True)
                e = jnp.exp(s - m)
                den = jnp.sum(e, axis=-1, keepdims=True)
                outs.append(jnp.dot(e.astype(BF16), vp, preferred_element_type=F32) / den)
                lse_tile = jnp.where(lane == h, m + jnp.log(den), lse_tile)
            o_ref[pl.ds(r0, BLK), cs] = jnp.where(low, outs[0], outs[1]).astype(o_ref.dtype)
        lse_ref[pl.ds(r0, BLK), :] = lse_tile
        return carry

    lax.fori_loop(0, rows // BLK, sub_block, 0)
    kb_ref[:BLK, :] = k_ref[rows - BLK:, :]
    vb_ref[:BLK, :] = v_ref[rows - BLK:, :]


def _attn_bias_tables(rel_bias, window, dil):
    n_keys = window // dil
    i = np.arange(BLK)[:, None]
    j = np.arange(2 * BLK)[None, :]
    rel = BLK + i - j
    in_band = (rel >= 0) & (rel <= n_keys)
    bias = _bias_by_bucket(rel_bias, _t5_bucket(np.clip(rel, 0, None) * dil))
    later = jnp.where(jnp.asarray(in_band)[None], bias, NEG_INF)
    first = jnp.where(jnp.asarray(in_band & (j >= BLK))[None], bias, NEG_INF)
    return jnp.stack([first, later])


def _dilated_branch(q, k, v, bias_tab, dil):
    B, L, _ = q.shape
    rows = min(ATTN_BLOCKS_PER_STEP, L // BLK) * BLK
    assert L % rows == 0
    o_dtype = BF16 if RET_CHUNK // dil >= 16 else F32

    def cls(b, r, n):
        return (b, n, r)

    qkv_spec = pl.BlockSpec((None, rows, WIDTH_A), cls)
    o, lse = pl.pallas_call(
        _attn_body,
        grid=(B, dil, L // rows),
        in_specs=[qkv_spec, qkv_spec, qkv_spec,
                  pl.BlockSpec(bias_tab.shape, lambda b, r, n: (0, 0, 0, 0))],
        out_specs=(qkv_spec, pl.BlockSpec((None, rows, LANES), cls)),
        out_shape=(jax.ShapeDtypeStruct((B, L, dil * WIDTH_A), o_dtype),
                   jax.ShapeDtypeStruct((B, L, dil * LANES), F32)),
        scratch_shapes=[pltpu.VMEM((BLK + rows, WIDTH_A), BF16), pltpu.VMEM((BLK + rows, WIDTH_A), BF16)],
        compiler_params=_params(3),
        name=f"dil_attn_d{dil}",
    )(q, k, v, bias_tab)
    return o, lse


def _ret_body(*refs, dilations):
    n_branch = len(dilations)
    qb_ref, kb_ref, vb_ref, gb_ref = refs[:4]
    p = 4
    if n_branch:
        o_refs = refs[p:p + n_branch]
        l_refs = refs[p + n_branch:p + 2 * n_branch]
        exp_ref = refs[p + 2 * n_branch]
        p += 2 * n_branch + 1
    else:
        oa_ref = refs[p]
        p += 1
    st0_ref, dmat_ref, qdec_ref, kdec_ref, cdec_ref, cat_ref, sto_ref, st_ref = refs[p:p + 8]
    if n_branch:
        us_ref, ls_ref = refs[p + 8:]

    def lse_token_order(l_ref, d):
        if d == 1:
            return l_ref[...]
        n = l_ref.shape[0]
        for r in range(d):
            ls_ref[pl.ds(r, n, stride=d), :] = l_ref[:, r * LANES:(r + 1) * LANES]
        return ls_ref[...]

    def out_token_order(o_ref, d):
        if d == 1:
            return o_ref[...].astype(F32)
        n = o_ref.shape[0]
        for r in range(d):
            for c in range(WIDTH_A // LANES):
                lo = r * WIDTH_A + c * LANES
                us_ref[c, pl.ds(r, n, stride=d), :] = o_ref[:, lo:lo + LANES].astype(F32)
        return jnp.concatenate([us_ref[c] for c in range(WIDTH_A // LANES)], axis=1)

    @pl.when(pl.program_id(1) == 0)
    def _():
        st_ref[...] = st0_ref[...]

    if n_branch:
        ls = [lse_token_order(l_ref, d) for l_ref, d in zip(l_refs, dilations)]
        mx = functools.reduce(jnp.maximum, ls)
        ws = [jnp.exp(l - mx) for l in ls]
        tot = functools.reduce(lambda a, b: a + b, ws)
        oa = None
        for w, o_ref, d in zip(ws, o_refs, dilations):
            o_tok = out_token_order(o_ref, d)
            w = w / tot
            w_hi = w.astype(BF16)
            w_lo = (w - w_hi.astype(F32)).astype(BF16)
            w_full = (jnp.dot(w_hi, exp_ref[...], preferred_element_type=F32)
                      + jnp.dot(w_lo, exp_ref[...], preferred_element_type=F32))
            term = w_full * o_tok
            oa = term if oa is None else oa + term
        cat_ref[:, :WIDTH_A] = oa.astype(BF16)
    else:
        cat_ref[:, :WIDTH_A] = oa_ref[...].astype(BF16)

    for h in range(N_HEADS_B):
        ks = slice(h * KEY_DIM_B, (h + 1) * KEY_DIM_B)
        vs = slice(h * VAL_DIM_B, (h + 1) * VAL_DIM_B)
        q = qb_ref[:, ks]
        k = kb_ref[:, ks]
        v = vb_ref[:, vs]
        st = st_ref[h]
        a = lax.dot_general(q, k, (((1,), (1,)), ((), ())), preferred_element_type=F32) * dmat_ref[h]
        o = (jnp.dot(a.astype(BF16), v, preferred_element_type=F32)
             + jnp.dot(q, st.astype(BF16), preferred_element_type=F32) * qdec_ref[h])
        kd = (k.astype(F32) * kdec_ref[h]).astype(BF16)
        st_new = st * cdec_ref[h] + lax.dot_general(kd, v, (((0,), (0,)), ((), ())), preferred_element_type=F32)
        st_ref[h] = st_new
        sto_ref[h] = st_new
        mu = jnp.mean(o, axis=-1, keepdims=True)
        var = jnp.mean(jnp.square(o - mu), axis=-1, keepdims=True)
        obn = (o - mu) * lax.rsqrt(var + GN_EPS)
        g = gb_ref[:, vs].astype(F32)
        gated = g * (1.0 / (1.0 + jnp.exp(-g))) * obn
        cat_ref[:, WIDTH_A + h * VAL_DIM_B:WIDTH_A + (h + 1) * VAL_DIM_B] = gated.astype(BF16)


def _decay_tables(chunk, rows):
    H = N_HEADS_B
    log_g = jnp.log(1.0 - 2.0 ** (-5.0 - jnp.arange(H, dtype=F32)))
    i = jnp.arange(rows, dtype=F32)
    live = np.arange(rows) < chunk
    diff = i[:, None] - i[None, :]
    causal = (diff >= 0) & jnp.asarray(live[:, None] & live[None, :])
    dmat = jnp.where(causal[None], jnp.exp(jnp.where(causal, diff, 0.0)[None] * log_g[:, None, None]), 0.0)
    q_decay = jnp.where(jnp.asarray(live)[None], jnp.exp((i[None, :] + 1.0) * log_g[:, None]), 0.0)
    k_decay = jnp.where(jnp.asarray(live)[None], jnp.exp((chunk - 1.0 - i)[None, :] * log_g[:, None]), 0.0)
    c_decay = jnp.exp(chunk * log_g)
    qdec = jnp.broadcast_to(q_decay[:, :, None], (H, rows, VAL_DIM_B))
    kdec = jnp.broadcast_to(k_decay[:, :, None], (H, rows, KEY_DIM_B))
    cdec = jnp.broadcast_to(c_decay[:, None, None], (H, KEY_DIM_B, VAL_DIM_B))
    return dmat.astype(F32), qdec.astype(F32), kdec.astype(F32), cdec.astype(F32)


def _ret_mix(qb, kb, vb, gb, attn, state0, chunk):
    B, S, _ = qb.shape
    rows = RET_CHUNK
    nc = S // rows
    tables = _decay_tables(chunk, rows)

    def tok(b, c):
        return (b, c, 0)

    def tspec(w):
        return pl.BlockSpec((None, rows, w), tok)

    def const(shape):
        return pl.BlockSpec(shape, lambda b, c: (0,) * len(shape))

    ins = [qb, kb, vb, gb]
    in_specs = [tspec(QK_WIDTH_B), tspec(QK_WIDTH_B), tspec(WIDTH_B), tspec(WIDTH_B)]
    scratch = [pltpu.VMEM((N_HEADS_B, KEY_DIM_B, VAL_DIM_B), F32)]
    if isinstance(attn, tuple):
        outs_a, lses, dilations = attn
        expand = np.zeros((LANES, WIDTH_A), np.float32)
        for h in range(N_HEADS_A):
            expand[h, h * HEAD_DIM_A:(h + 1) * HEAD_DIM_A] = 1.0
        ins += list(outs_a) + list(lses) + [jnp.asarray(expand, BF16)]
        in_specs += ([pl.BlockSpec((None, rows // d, d * WIDTH_A), tok) for d in dilations]
                     + [pl.BlockSpec((None, rows // d, d * LANES), tok) for d in dilations] + [const((LANES, WIDTH_A))])
        scratch += [pltpu.VMEM((WIDTH_A // LANES, rows, LANES), F32), pltpu.VMEM((rows, LANES), F32)]
    else:
        dilations = ()
        ins.append(attn)
        in_specs.append(tspec(WIDTH_A))
    st_shape = (N_HEADS_B, KEY_DIM_B, VAL_DIM_B)
    st_spec = pl.BlockSpec((None,) + st_shape, lambda b, c: (b, 0, 0, 0))
    ins += [state0] + list(tables)
    in_specs += [st_spec] + [const(t.shape) for t in tables]
    return pl.pallas_call(
        functools.partial(_ret_body, dilations=tuple(dilations)),
        grid=(B, nc),
        in_specs=in_specs,
        out_specs=(tspec(WIDTH_A + WIDTH_B), st_spec),
        out_shape=(jax.ShapeDtypeStruct((B, S, WIDTH_A + WIDTH_B), BF16),
                   jax.ShapeDtypeStruct((B,) + st_shape, F32)),
        scratch_shapes=scratch,
        compiler_params=_params(2),
        name=f"ret_mix_{len(dilations)}",
    )(*ins)


SAMP_Q_ROWS = 64
SAMP_NEW_LANES = 128


def _samp_attn_body(q_ref, kt_ref, vt_ref, knt_ref, vnt_ref, bc_ref, bn_ref, hm_ref, o_ref, ko_ref, vo_ref, *, ds):
    q = q_ref[...]
    kt = kt_ref[...]
    vt = vt_ref[...]
    knt = knt_ref[...]
    vnt = vnt_ref[...]
    w = kt.shape[1]
    is_new = lax.broadcasted_iota(jnp.int32, knt.shape, 1) >= SAMP_NEW_LANES - ds
    for src, new, dst in ((kt, knt, ko_ref), (vt, vnt, vo_ref)):
        rolled = pltpu.roll(src, w - ds, 1)
        dst[:, :w - SAMP_NEW_LANES] = rolled[:, :w - SAMP_NEW_LANES]
        dst[:, w - SAMP_NEW_LANES:] = jnp.where(is_new, new, rolled[:, w - SAMP_NEW_LANES:])

    s_c = jnp.dot(q, kt.astype(BF16), preferred_element_type=F32)
    s_n = jnp.dot(q, knt.astype(BF16), preferred_element_type=F32)
    es_c, es_n, dens, lses = [], [], [], []
    for n in range(len(DILATED_BRANCHES)):
        sc = s_c + bc_ref[n]
        sn = s_n + bn_ref[n]
        m = jnp.maximum(jnp.max(sc, axis=-1, keepdims=True), jnp.max(sn, axis=-1, keepdims=True))
        ec = jnp.exp(sc - m)
        en = jnp.exp(sn - m)
        den = jnp.sum(ec, axis=-1, keepdims=True) + jnp.sum(en, axis=-1, keepdims=True)
        es_c.append(ec)
        es_n.append(en)
        dens.append(den)
        lses.append(m + jnp.log(den))
    mx = functools.reduce(jnp.maximum, lses)
    ws = [jnp.exp(l - mx) for l in lses]
    tot = functools.reduce(lambda a, b: a + b, ws)
    p_c = None
    p_n = None
    for w, den, ec, en in zip(ws, dens, es_c, es_n):
        coef = w / (tot * den)
        p_c = coef * ec if p_c is None else p_c + coef * ec
        p_n = coef * en if p_n is None else p_n + coef * en
    nt = (((1,), (1,)), ((), ()))
    o = (lax.dot_general(p_c.astype(BF16), vt.astype(BF16), nt, preferred_element_type=F32)
         + lax.dot_general(p_n.astype(BF16), vnt.astype(BF16), nt, preferred_element_type=F32))
    o = o * hm_ref[...]
    o_ref[...] = jnp.sum(o.reshape(SUBLANES, N_HEADS_A, WIDTH_A), axis=1)


def _samp_bias_tables(rel_bias, w_buf, ds):
    tabs_c, tabs_n = [], []
    s = np.arange(SUBLANES)[:, None]
    live_s = s < ds
    first_new = SAMP_NEW_LANES - ds
    for window, dil in DILATED_BRANCHES:
        n_keys = window // dil
        for keys, live_k, tabs in ((np.arange(w_buf)[None, :], True, tabs_c),
                                   (w_buf - first_new + np.arange(SAMP_NEW_LANES)[None, :],
                                    np.arange(SAMP_NEW_LANES)[None, :] >= first_new, tabs_n)):
            n = keys.shape[1]
            dist = w_buf + s - keys
            valid = (dist >= 0) & (dist % dil == 0) & (dist // dil <= n_keys) & live_k
            bias = _bias_by_bucket(rel_bias, _t5_bucket(np.clip(dist, 0, None))).transpose(1, 0, 2)
            tab = jnp.where(jnp.asarray(valid)[:, None, :], bias, NEG_INF)
            pad = jnp.where(jnp.asarray(np.broadcast_to(live_k, dist.shape))[:, None, :], 0.0, NEG_INF)
            tab = jnp.where(jnp.asarray(live_s)[:, :, None], tab, pad)
            tabs.append(tab.reshape(SAMP_Q_ROWS, n))
    return jnp.stack(tabs_c), jnp.stack(tabs_n)


def _samp_attn(qa, cache_kt, cache_vt, knt, vnt, rel_bias, ds):
    DB, DS, _ = qa.shape
    W = cache_kt.shape[2]
    head_of_lane = np.arange(WIDTH_A) // HEAD_DIM_A
    hmask = (np.arange(SAMP_Q_ROWS)[:, None] % N_HEADS_A == head_of_lane[None, :])
    q8 = jnp.pad(qa, ((0, 0), (0, SUBLANES - DS), (0, 0)))
    q_rows = jnp.where(jnp.asarray(hmask)[None], jnp.repeat(q8, N_HEADS_A, axis=1), jnp.zeros((), BF16))
    bias_c, bias_n = _samp_bias_tables(rel_bias, W, DS)

    def per_b(rows, w):
        return pl.BlockSpec((None, rows, w), lambda b: (b, 0, 0))

    def const(a):
        return pl.BlockSpec(a.shape, lambda b: (0,) * a.ndim)

    hm = jnp.asarray(hmask, F32)
    return pl.pallas_call(
        functools.partial(_samp_attn_body, ds=ds),
        grid=(DB,),
        in_specs=[per_b(SAMP_Q_ROWS, WIDTH_A), per_b(WIDTH_A, W), per_b(WIDTH_A, W),
                  per_b(WIDTH_A, SAMP_NEW_LANES), per_b(WIDTH_A, SAMP_NEW_LANES),
                  const(bias_c), const(bias_n), const(hm)],
        out_specs=(per_b(SUBLANES, WIDTH_A), per_b(WIDTH_A, W), per_b(WIDTH_A, W)),
        out_shape=(jax.ShapeDtypeStruct((DB, SUBLANES, WIDTH_A), F32),
                   jax.ShapeDtypeStruct((DB, WIDTH_A, W), F32), jax.ShapeDtypeStruct((DB, WIDTH_A, W), F32)),
        compiler_params=_params(1),
        name="samp_attn",
    )(q_rows, cache_kt, cache_vt, knt, vnt, bias_c, bias_n, hm)


ROUTE_ROWS = 16


def _route_body(cat_ref, x_ref, wout_ref, g_ref, b_ref, rwh_ref, rwl_ref, rb_ref, triu_ref, base_ref, hp_in_ref,
                h_ref, hp_ref, route_ref, route_t_ref, cnt_ref, *, alpha, n_exp):
    del hp_in_ref

    @pl.when(pl.program_id(0) == 0)
    def _():
        cnt_ref[...] = base_ref[...]

    mix = jnp.dot(cat_ref[...], wout_ref[...], preferred_element_type=F32)
    y = alpha * x_ref[...] + mix
    mu = jnp.mean(y, axis=-1, keepdims=True)
    var = jnp.mean(jnp.square(y - mu), axis=-1, keepdims=True)
    h = (y - mu) * lax.rsqrt(var + LN_EPS) * g_ref[...] + b_ref[...]
    h_ref[...] = h
    hb = h.astype(BF16)
    hp_ref[...] = _pack_bf16_pairs(h)
    hl = (h - hb.astype(F32)).astype(BF16)
    nt = (((1,), (1,)), ((), ()))
    logits = (lax.dot_general(rwh_ref[...], hb, nt, preferred_element_type=F32)
              + lax.dot_general(rwl_ref[...], hb, nt, preferred_element_type=F32)
              + lax.dot_general(rwh_ref[...], hl, nt, preferred_element_type=F32))[:n_exp] + rb_ref[...]

    tm = logits.shape[1]
    sub = lax.broadcasted_iota(jnp.int32, (n_exp, tm), 0)
    work = logits
    vals, idxs = [], []
    for _ in range(TOP_K):
        m = jnp.max(work, axis=0, keepdims=True)
        idx = jnp.min(jnp.where(work == m, sub, n_exp), axis=0, keepdims=True)
        vals.append(m)
        idxs.append(idx)
        work = jnp.where(sub == idx, -jnp.inf, work)
    es = [jnp.exp(v - vals[0]) for v in vals]
    tot = functools.reduce(lambda a, b: a + b, es)
    onehot = jnp.zeros((n_exp, tm), F32)
    for idx in idxs:
        onehot = onehot + (sub == idx).astype(F32)
    before = jnp.dot(onehot.astype(BF16), triu_ref[...], preferred_element_type=F32) + cnt_ref[:, 0:1]
    rows = ([idx.astype(F32) for idx in idxs] + [e / tot for e in es]
            + [jnp.sum(jnp.where(sub == idx, before, 0.0), axis=0, keepdims=True) for idx in idxs])
    rows.append(jnp.zeros((ROUTE_ROWS - len(rows), tm), F32))
    record = jnp.concatenate(rows, axis=0)
    route_t_ref[...] = record
    route_ref[...] = jnp.concatenate([record, jnp.zeros((LANES - ROUTE_ROWS, tm), F32)], axis=0).T
    cnt_ref[...] = cnt_ref[...] + jnp.sum(onehot, axis=1, keepdims=True)


def _out_route(cat, x2d, w_out_bf, ln_g, ln_b, rw_hi_t, rw_lo_t, rb_col, base, hp_buf, tile0, tm, alpha):
    T = x2d.shape[0]
    n_exp = rb_col.shape[0]
    triu = jnp.asarray(np.triu(np.ones((tm, tm), np.float32), 1), BF16)

    def tok(i):
        return (i, 0)

    def const(a):
        return pl.BlockSpec(a.shape, lambda i: (0,) * a.ndim)

    ins = (cat, x2d, w_out_bf, ln_g, ln_b, rw_hi_t, rw_lo_t, rb_col, triu, base, hp_buf)
    in_specs = ([pl.BlockSpec((tm, cat.shape[1]), tok), pl.BlockSpec((tm, D_MODEL), tok)]
                + [const(a) for a in ins[2:-1]] + [pl.BlockSpec(memory_space=pl.ANY)])
    return pl.pallas_call(
        functools.partial(_route_body, alpha=alpha, n_exp=n_exp),
        grid=(T // tm,),
        in_specs=in_specs,
        out_specs=(pl.BlockSpec((tm, D_MODEL), tok), pl.BlockSpec((tm, D_MODEL // 2), lambda i: (tile0 + i, 0)),
                   pl.BlockSpec((tm, LANES), tok), pl.BlockSpec((ROUTE_ROWS, tm), lambda i: (0, i)),
                   pl.BlockSpec(base.shape, lambda i: (0, 0))),
        out_shape=(jax.ShapeDtypeStruct((T, D_MODEL), F32), jax.ShapeDtypeStruct(hp_buf.shape, jnp.int32),
                   jax.ShapeDtypeStruct((T, LANES), F32), jax.ShapeDtypeStruct((ROUTE_ROWS, T), F32),
                   jax.ShapeDtypeStruct(base.shape, F32)),
        input_output_aliases={len(ins) - 1: 1},
        compiler_params=_params(1),
        name="out_route",
    )(*ins)


MOE_CAST_ROWS = 128


def _moe_body(be_ref, first_ref, next_ref, nused_ref, x_ref, bgu_ref, bdn_ref, wgu_hbm, wdn_hbm, y_ref,
              gu_stage, dn_stage, wgu_bf, wdn_bf, sem):
    b = pl.program_id(0)
    d_exp = wdn_bf.shape[0]

    def fetch(e):
        return (pltpu.make_async_copy(wgu_hbm.at[e], gu_stage, sem.at[0]),
                pltpu.make_async_copy(wdn_hbm.at[e], dn_stage, sem.at[1]))

    @pl.when(b == 0)
    def _():
        for c in fetch(be_ref[0]):
            c.start()

    @pl.when(first_ref[b] == 1)
    def _():
        for c in fetch(be_ref[b]):
            c.wait()

        def cast_gu(i, c):
            r = pl.ds(pl.multiple_of(i * MOE_CAST_ROWS, MOE_CAST_ROWS), MOE_CAST_ROWS)
            wgu_bf[r, :] = gu_stage[r, :].astype(BF16)
            return c

        def cast_dn(i, c):
            r = pl.ds(pl.multiple_of(i * MOE_CAST_ROWS, MOE_CAST_ROWS), MOE_CAST_ROWS)
            wdn_bf[r, :] = dn_stage[r, :].astype(BF16)
            return c

        lax.fori_loop(0, gu_stage.shape[0] // MOE_CAST_ROWS, cast_gu, 0)
        lax.fori_loop(0, d_exp // MOE_CAST_ROWS, cast_dn, 0)

        @pl.when(next_ref[b] >= 0)
        def _():
            for c in fetch(next_ref[b]):
                c.start()

    @pl.when(b < nused_ref[0])
    def _():
        x_lo, x_hi = _unpack_bf16_pairs(x_ref[...])
        x_lo = x_lo.astype(BF16)
        x_hi = x_hi.astype(BF16)
        dh = x_lo.shape[1]

        def xw(cols):
            return (jnp.dot(x_lo, wgu_bf[:dh, cols], preferred_element_type=F32)
                    + jnp.dot(x_hi, wgu_bf[dh:, cols], preferred_element_type=F32) + bgu_ref[:, cols])

        half = d_exp // 2
        y = None
        for c in range(2):
            lo = c * half
            gate = jnp.minimum(xw(slice(lo, lo + half)), SWIGLU_LIMIT)
            up = jnp.clip(xw(slice(d_exp + lo, d_exp + lo + half)), -SWIGLU_LIMIT, SWIGLU_LIMIT)
            act = (up + 1.0) * gate * (1.0 / (1.0 + jnp.exp(-SWIGLU_ALPHA * gate)))
            part = jnp.dot(act.astype(BF16), wdn_bf[lo:lo + half, :], preferred_element_type=F32)
            y = part if y is None else y + part
        y_ref[...] = _pack_bf16_pairs(y + bdn_ref[...])

    @pl.when(b >= nused_ref[0])
    def _():
        y_ref[...] = jnp.zeros_like(y_ref)


def _moe_ffn(xs, rows, block_e, n_used, has_rows, w_gu, b_gu, w_dn, b_dn):
    E, D, two_de = w_gu.shape
    d_exp = two_de // 2
    nb = rows // MOE_BLOCK
    idx = jnp.arange(nb, dtype=jnp.int32)
    first = ((idx < n_used[0]) & ((idx == 0) | (block_e != jnp.roll(block_e, 1)))).astype(jnp.int32)
    ids = jnp.arange(E, dtype=jnp.int32)
    later = jnp.where((ids[None, :] > ids[:, None]) & has_rows[None, :], ids[None, :], E).min(axis=1)
    next_of = jnp.where(later == E, -1, later).astype(jnp.int32)
    next_e = jnp.sum(jnp.where(block_e[:, None] == ids[None, :], next_of[None, :], 0), axis=1).astype(jnp.int32)
    grid_spec = pltpu.PrefetchScalarGridSpec(
        num_scalar_prefetch=4,
        grid=(nb,),
        in_specs=[
            pl.BlockSpec((MOE_BLOCK, D // 2), lambda b, be, fi, nx, nu: (b, 0)),
            pl.BlockSpec((None, 1, two_de), lambda b, be, fi, nx, nu: (be[b], 0, 0)),
            pl.BlockSpec((None, 1, D), lambda b, be, fi, nx, nu: (be[b], 0, 0)),
            pl.BlockSpec(memory_space=pl.ANY),
            pl.BlockSpec(memory_space=pl.ANY),
        ],
        out_specs=pl.BlockSpec((MOE_BLOCK, D // 2), lambda b, be, fi, nx, nu: (b, 0)),
        scratch_shapes=[pltpu.VMEM((D, two_de), F32), pltpu.VMEM((d_exp, D), F32),
                        pltpu.VMEM((D, two_de), BF16), pltpu.VMEM((d_exp, D), BF16),
                        pltpu.SemaphoreType.DMA((2,))],
    )
    return pl.pallas_call(
        _moe_body,
        grid_spec=grid_spec,
        out_shape=jax.ShapeDtypeStruct((rows, D // 2), jnp.int32),
        compiler_params=_params(1),
        name="moe_ffn",
    )(block_e, first, next_e, n_used, xs, b_gu.reshape(E, 1, two_de), b_dn.reshape(E, 1, D), w_gu, w_dn)


def _ple_body(h_ref, ys_ref, route_ref, p_ref, g_ref, b_ref, wpg_ref, bpg_ref, wpp_ref, o_ref, *, alpha):
    route = route_ref[...]
    f_lo = None
    f_hi = None
    for k in range(TOP_K):
        lo, hi = _unpack_bf16_pairs(ys_ref[k])
        g = route[:, TOP_K + k:TOP_K + k + 1]
        f_lo = g * lo if f_lo is None else f_lo + g * lo
        f_hi = g * hi if f_hi is None else f_hi + g * hi
    y = alpha * h_ref[...] + jnp.concatenate([f_lo, f_hi], axis=1)
    mu = jnp.mean(y, axis=-1, keepdims=True)
    var = jnp.mean(jnp.square(y - mu), axis=-1, keepdims=True)
    h2 = (y - mu) * lax.rsqrt(var + LN_EPS) * g_ref[...] + b_ref[...]
    z = jnp.dot(h2.astype(BF16), wpg_ref[...], preferred_element_type=F32) + bpg_ref[...]
    gate = 1.0 / (1.0 + jnp.exp(-z))
    proj = jnp.dot(p_ref[...].astype(BF16), wpp_ref[...], preferred_element_type=F32)
    o_ref[...] = h2 + gate * proj


def _ffn_ple(h, y_slots, route, p, ln_g, ln_b, w_pg_bf, b_pg, w_pp_bf, tm, tile0, alpha):
    T = h.shape[0]

    def tok(i):
        return (i, 0)

    def const(a):
        return pl.BlockSpec(a.shape, lambda i: (0,) * a.ndim)

    consts = (ln_g, ln_b, w_pg_bf, b_pg, w_pp_bf)
    return pl.pallas_call(
        functools.partial(_ple_body, alpha=alpha),
        grid=(T // tm,),
        in_specs=[pl.BlockSpec((tm, D_MODEL), tok),
                  pl.BlockSpec((TOP_K, tm, D_MODEL // 2), lambda i: (0, tile0 + i, 0)),
                  pl.BlockSpec((tm, LANES), tok),
                  pl.BlockSpec((tm, p.shape[1]), tok)] + [const(a) for a in consts],
        out_specs=pl.BlockSpec((tm, D_MODEL), tok),
        out_shape=jax.ShapeDtypeStruct((T, D_MODEL), F32),
        compiler_params=_params(1),
        name="ffn_ple",
    )(h, y_slots, route, p, *consts)


def _row(v):
    return v.reshape(1, -1).astype(F32)


def kernel(x_prompt, x_sample, cache_win_k, cache_win_v, state_ret, p_prompt, p_sample, rel_bias, w_in, w_out,
           ln1_g, ln1_b, router_w, router_b, w_gate_up, b_gate_up, w_down, b_down, ln2_g, ln2_b,
           w_ple_gate, b_ple_gate, w_ple_proj):
    B, S, D = x_prompt.shape
    DB, DS, _ = x_sample.shape
    depth = w_in.shape[0]
    w_buf = cache_win_k.shape[2]
    n_exp = router_w.shape[-1]
    alpha = (2.0 * depth) ** 0.25
    assert depth == 1 and D == D_MODEL
    assert S % (BLK * MAX_DIL) == 0 and S >= WINDOW_MAX and w_buf == WINDOW_MAX and DS <= SUBLANES
    tm_p = 512
    Tp, Ts = B * S, DB * DS
    assert Tp % tm_p == 0 and Ts % SUBLANES == 0

    i = 0
    w_in_bf = w_in[i].astype(BF16)
    w_out_bf = w_out[i].astype(BF16)
    w_pg_bf = w_ple_gate[i].astype(BF16)
    w_pp_bf = w_ple_proj[i].astype(BF16)
    rw_t = jnp.pad(router_w[i].T, ((0, LANES - n_exp), (0, 0)))
    rw_hi = rw_t.astype(BF16)
    rw_lo = (rw_t - rw_hi.astype(F32)).astype(BF16)
    rb = router_b[i].astype(F32).reshape(n_exp, 1)
    T = Tp + Ts
    t_align = SC_WORKERS * SC_ALIGN
    T_pad = -(-T // t_align) * t_align
    assert Tp % Ts == 0

    cos_p, sin_p = _rotary_tables(jnp.arange(S, dtype=jnp.int32))
    dils = tuple(d for _, d in DILATED_BRANCHES)
    extra = tuple(d for d in dils if d > 1)
    outs = _in_proj(x_prompt.reshape(Tp, D), w_in_bf, cos_p, sin_p, tm_p, S // tm_p, WINDOW_MAX // tm_p, extra)
    qa, ka, va, qb, kb, vb, gb, kf, vf = outs[:9]
    qkv = {1: (qa, ka, va)}
    for t, d in enumerate(extra):
        qkv[d] = tuple(outs[9 + j * len(extra) + t] for j in range(3))

    def seq(t):
        return t.reshape(B, S, t.shape[-1])

    outs_a, lses = [], []
    for window, dil in DILATED_BRANCHES:
        q_d, k_d, v_d = (t.reshape(B, S // dil, dil * WIDTH_A) for t in qkv[dil])
        o_n, l_n = _dilated_branch(q_d, k_d, v_d, _attn_bias_tables(rel_bias, window, dil), dil)
        outs_a.append(o_n)
        lses.append(l_n)
    st_zero = jnp.zeros((B, N_HEADS_B, KEY_DIM_B, VAL_DIM_B), F32)
    cat_p, rst_p = _ret_mix(seq(qb), seq(kb), seq(vb), seq(gb), (outs_a, lses, dils), st_zero, RET_CHUNK)
    base0 = jnp.zeros((n_exp, LANES), F32)
    hp0 = jnp.zeros((T_pad, D // 2), jnp.int32)
    h_p, hp_buf, route_p, rt_p, cnt_p = _out_route(cat_p.reshape(Tp, -1), x_prompt.reshape(Tp, D), w_out_bf,
                                                   _row(ln1_g[i]), _row(ln1_b[i]), rw_hi, rw_lo, rb, base0, hp0, 0,
                                                   tm_p, alpha)

    pos_s = jnp.tile(PAST_LEN + jnp.arange(DS, dtype=jnp.int32), DB)
    cos_s, sin_s = _rotary_tables(pos_s)
    qa_s, _, _, qb_s, kb_s, vb_s, gb_s, kf_s, vf_s = _in_proj(x_sample.reshape(Ts, D), w_in_bf, cos_s, sin_s,
                                                             Ts, 1, 1)

    def positions_minor(t):
        return jnp.transpose(t, (0, 2, 3, 1)).reshape(DB, WIDTH_A, t.shape[1])

    def positions_major(t):
        return jnp.transpose(t.reshape(t.shape[0], N_HEADS_A, HEAD_DIM_A, t.shape[2]), (0, 3, 1, 2))[None]

    def new_columns(t):
        t = jnp.transpose(t.reshape(WIDTH_A, DB, DS), (1, 0, 2))
        return jnp.pad(t, ((0, 0), (0, 0), (SAMP_NEW_LANES - DS, 0)))

    oa_s, kt_out, vt_out = _samp_attn(qa_s.reshape(DB, DS, WIDTH_A), positions_minor(cache_win_k[i]),
                                      positions_minor(cache_win_v[i]), new_columns(kf_s), new_columns(vf_s),
                                      rel_bias, DS)

    def pad_rows(t, rows):
        t = t.reshape(DB, -1, t.shape[-1])
        return jnp.pad(t, ((0, 0), (0, rows - t.shape[1]), (0, 0)))

    cat_s, rst_s = _ret_mix(pad_rows(qb_s, RET_CHUNK), pad_rows(kb_s, RET_CHUNK), pad_rows(vb_s, RET_CHUNK),
                            pad_rows(gb_s, RET_CHUNK), pad_rows(oa_s, RET_CHUNK), state_ret[i].astype(F32), DS)
    cat_s = cat_s[:, :DS].reshape(Ts, -1)
    h_s, hp_all, route_s, rt_s, cnt = _out_route(cat_s, x_sample.reshape(Ts, D), w_out_bf, _row(ln1_g[i]),
                                                 _row(ln1_b[i]), rw_hi, rw_lo, rb, cnt_p, hp_buf, Tp // Ts, Ts, alpha)

    n_fill = T_pad - T
    record = jnp.concatenate([rt_p, rt_s, jnp.zeros((ROUTE_ROWS, n_fill), F32)], axis=1)
    top_idx = record[:TOP_K].astype(jnp.int32)
    rank = record[2 * TOP_K:3 * TOP_K].astype(jnp.int32)
    counts = cnt[:, 0].astype(jnp.int32)
    padded = (counts + MOE_BLOCK - 1) // MOE_BLOCK * MOE_BLOCK
    pad_end = jnp.cumsum(padded)
    pad_start = pad_end - padded
    ids = jnp.arange(n_exp, dtype=jnp.int32)[:, None, None]
    dest = jnp.sum(jnp.where(top_idx[None] == ids, pad_start[:, None, None], 0), axis=0) + rank
    n_blocks = -(-T * TOP_K // MOE_BLOCK) + n_exp
    rows = n_blocks * MOE_BLOCK
    block_start = jnp.arange(n_blocks, dtype=jnp.int32) * MOE_BLOCK
    block_e = jnp.minimum(jnp.sum(pad_end[None, :] <= block_start[:, None], axis=1), n_exp - 1).astype(jnp.int32)
    n_used = (pad_end[-1:] // MOE_BLOCK).astype(jnp.int32)
    tok_id = jnp.arange(T_pad, dtype=jnp.int32)[None, :]
    spare = rows + (tok_id - T) * TOP_K + jnp.arange(TOP_K, dtype=jnp.int32)[:, None]
    dest_sc = jnp.where(tok_id >= T, spare, dest).reshape(-1)
    dest_ga = jnp.where(tok_id >= T, 0, dest).reshape(-1)
    xs = _sc_scatter_rows(hp_all, dest_sc, rows + n_fill * TOP_K)
    ys = _moe_ffn(xs, rows, block_e, n_used, padded > 0, w_gate_up[i], b_gate_up[i], w_down[i], b_down[i])
    y_slots = _sc_gather_rows(ys, dest_ga).reshape(TOP_K, T_pad, D // 2)

    ple_args = (_row(ln2_g[i]), _row(ln2_b[i]), w_pg_bf, _row(b_ple_gate[i]), w_pp_bf)
    y_p = _ffn_ple(h_p, y_slots, route_p, p_prompt[i].reshape(Tp, D_PLE), *ple_args, tm_p, 0, alpha)
    y_s = _ffn_ple(h_s, y_slots, route_s, p_sample[i].reshape(Ts, D_PLE), *ple_args, Ts, Tp // Ts, alpha)

    return (y_p.reshape(B, S, D), y_s.reshape(DB, DS, D), positions_major(kf), positions_major(vf),
            rst_p[None], positions_major(kt_out), positions_major(vt_out), rst_s[None])
```

```python
import functools

import numpy as np
import jax
import jax.numpy as jnp
from jax import lax
from jax.experimental import pallas as pl
from jax.experimental.pallas import tpu as pltpu
from jax.experimental.pallas import tpu_sc as plsc

F32 = jnp.float32
BF16 = jnp.bfloat16

D_MODEL = 1024
D_PLE = 256
N_HEADS_A = 8
HEAD_DIM_A = 64
WIDTH_A = N_HEADS_A * HEAD_DIM_A
DILATED_BRANCHES = ((128, 1), (512, 4), (2048, 16))
BLK = 128
WINDOW_MAX = 2048
MAX_DIL = 16
NUM_BUCKETS = 32
MAX_DISTANCE = 2048
N_HEADS_B = 4
KEY_DIM_B = 64
VAL_DIM_B = 128
QK_WIDTH_B = N_HEADS_B * KEY_DIM_B
WIDTH_B = N_HEADS_B * VAL_DIM_B
RET_CHUNK = 128
GN_EPS = 1e-6
TOP_K = 4
SWIGLU_LIMIT = 7.0
SWIGLU_ALPHA = 1.702
LN_EPS = 1e-5
NEG_INF = -1e30
PAST_LEN = 16384
MOE_BLOCK = 512
LANES = 128
SUBLANES = 8
VMEM_LIMIT = 52 * 1024 * 1024


def _params(n_axes, vmem=VMEM_LIMIT):
    return pltpu.CompilerParams(dimension_semantics=("arbitrary",) * n_axes, vmem_limit_bytes=vmem)


def _t5_bucket(dist):
    dist = np.asarray(dist, dtype=np.int32)
    max_exact = NUM_BUCKETS // 2
    d = np.maximum(dist, 1).astype(np.float32)
    large = max_exact + (np.log(d / max_exact) / np.log(MAX_DISTANCE / max_exact)
                         * (NUM_BUCKETS - max_exact)).astype(np.int32)
    large = np.minimum(large, NUM_BUCKETS - 1)
    return np.where(dist < max_exact, dist, large).astype(np.int32)


def _bias_by_bucket(rel_bias, buckets):
    b = jnp.asarray(buckets, jnp.int32)
    ids = jnp.arange(NUM_BUCKETS, dtype=jnp.int32).reshape((NUM_BUCKETS, 1) + (1,) * b.ndim)
    vals = rel_bias.astype(F32).reshape((NUM_BUCKETS, rel_bias.shape[1]) + (1,) * b.ndim)
    return jnp.sum(jnp.where(b[None, None] == ids, vals, 0.0), axis=0)


def _pack_bf16_pairs(v):
    w = v.shape[1] // 2
    lo = lax.bitcast_convert_type(v[:, :w].astype(BF16).astype(F32), jnp.uint32) >> 16
    hi = lax.bitcast_convert_type(v[:, w:].astype(BF16).astype(F32), jnp.uint32) & jnp.uint32(0xFFFF0000)
    return lax.bitcast_convert_type(lo | hi, jnp.int32)


def _unpack_bf16_pairs(p):
    u = lax.bitcast_convert_type(p, jnp.uint32)
    lo = lax.bitcast_convert_type(u << 16, F32)
    hi = lax.bitcast_convert_type(u & jnp.uint32(0xFFFF0000), F32)
    return lo, hi


SC_CORES = 2
SC_SUBCORES = 16
SC_WORKERS = SC_CORES * SC_SUBCORES
SC_ALIGN = 8
SC_CHUNK_ROWS = 80


def _sc_mesh():
    return plsc.VectorSubcoreMesh(core_axis_name="c", subcore_axis_name="s")


def _sc_chunk(per_worker):
    c = max(d for d in range(SC_ALIGN, SC_CHUNK_ROWS + 1, SC_ALIGN) if per_worker % d == 0)
    return c


def _sc_scatter_rows(src, dest_flat, n_out):
    T, W = src.shape
    K = dest_flat.shape[0] // T
    per_w = T // SC_WORKERS
    assert per_w * SC_WORKERS == T and per_w % SC_ALIGN == 0
    chunk = _sc_chunk(per_w)
    n_chunks = per_w // chunk

    @functools.partial(
        pl.kernel, mesh=_sc_mesh(), out_type=jax.ShapeDtypeStruct((n_out, W), src.dtype),
        scratch_types=[pltpu.VMEM((chunk, W), src.dtype)] * 2 + [pltpu.VMEM((chunk,), jnp.int32)] * (2 * K)
        + [pltpu.SemaphoreType.DMA] * 2,
        name="sc_scatter_rows")
    def k(src_hbm, dest_hbm, out_hbm, rows_a, rows_b, *rest):
        bufs = ((rows_a, rest[:K], rest[2 * K]), (rows_b, rest[K:2 * K], rest[2 * K + 1]))
        base = (lax.axis_index("s") * SC_CORES + lax.axis_index("c")) * per_w

        def scatters(buf):
            rows_v, idx_vs, sem = buf
            return [pltpu.make_async_copy(rows_v, out_hbm.at[idx_vs[kk]], sem) for kk in range(K)]

        def step(j, buf):
            rows_v, idx_vs, _ = buf

            @pl.when(j >= 2)
            def _():
                for c in scatters(buf):
                    c.wait()

            off = pl.multiple_of(base + j * chunk, SC_ALIGN)
            pltpu.sync_copy(src_hbm.at[pl.ds(off, chunk)], rows_v)
            for kk in range(K):
                pltpu.sync_copy(dest_hbm.at[pl.ds(kk * T + off, chunk)], idx_vs[kk])
            for c in scatters(buf):
                c.start()

        @pl.loop(0, n_chunks)
        def _(j):
            for parity in range(2):
                @pl.when(j % 2 == parity)
                def _():
                    step(j, bufs[parity])

        for j in range(max(n_chunks - 2, 0), n_chunks):
            for c in scatters(bufs[j % 2]):
                c.wait()

    return k(src, dest_flat)


def _sc_gather_rows(table, idx):
    B = idx.shape[0]
    W = table.shape[1]
    per_w = B // SC_WORKERS
    assert per_w * SC_WORKERS == B and per_w % SC_ALIGN == 0
    chunk = _sc_chunk(per_w)
    n_chunks = per_w // chunk

    @functools.partial(
        pl.kernel, mesh=_sc_mesh(), out_type=jax.ShapeDtypeStruct((B, W), table.dtype),
        scratch_types=[pltpu.VMEM((chunk,), jnp.int32)] * 2 + [pltpu.VMEM((chunk, W), table.dtype)] * 2
        + [pltpu.SemaphoreType.DMA] * 2,
        name="sc_gather_rows")
    def k(table_hbm, idx_hbm, out_hbm, idx_a, idx_b, rows_a, rows_b, sem_a, sem_b):
        bufs = ((idx_a, rows_a, sem_a), (idx_b, rows_b, sem_b))
        base = (lax.axis_index("s") * SC_CORES + lax.axis_index("c")) * per_w

        def gather(buf):
            idx_v, rows_v, sem = buf
            return pltpu.make_async_copy(table_hbm.at[idx_v], rows_v, sem)

        def finish(j, buf):
            gather(buf).wait()
            off = pl.multiple_of(base + j * chunk, SC_ALIGN)
            pltpu.sync_copy(buf[1], out_hbm.at[pl.ds(off, chunk)])

        def step(j, buf, other):
            off = pl.multiple_of(base + j * chunk, SC_ALIGN)
            pltpu.sync_copy(idx_hbm.at[pl.ds(off, chunk)], buf[0])
            gather(buf).start()

            @pl.when(j >= 1)
            def _():
                finish(j - 1, other)

        @pl.loop(0, n_chunks)
        def _(j):
            for parity in range(2):
                @pl.when(j % 2 == parity)
                def _():
                    step(j, bufs[parity], bufs[1 - parity])

        finish(n_chunks - 1, bufs[(n_chunks - 1) % 2])

    return k(table, idx)


def _in_proj_body(x_ref, w_ref, cos_ref, sin_ref, *refs, seq_tiles, first_win, dilations):
    qa_ref, ka_ref, va_ref, qb_ref, kb_ref, vb_ref, gb_ref, kt_ref, vt_ref = refs[:9]
    n_d = len(dilations)
    dil_refs = [refs[9 + t * n_d:9 + (t + 1) * n_d] for t in range(3)]
    zs_ref = refs[9 + 3 * n_d] if n_d else None
    x = x_ref[...].astype(BF16)
    in_window = pl.program_id(0) % seq_tiles >= first_win

    def emit(z, token_ref, class_refs):
        token_ref[...] = z.astype(BF16)
        if not class_refs:
            return
        for c in range(WIDTH_A // LANES):
            zs_ref[c] = z[:, c * LANES:(c + 1) * LANES]
        for d, ref in zip(dilations, class_refs):
            n = z.shape[0] // d
            for r in range(d):
                for c in range(WIDTH_A // LANES):
                    lo = r * WIDTH_A + c * LANES
                    ref[:, lo:lo + LANES] = zs_ref[c, pl.ds(r, n, stride=d), :].astype(BF16)

    def proj(lo, hi):
        return jnp.dot(x, w_ref[:, lo:hi], preferred_element_type=F32)

    o = 0
    emit(proj(o, o + WIDTH_A) * (HEAD_DIM_A ** -0.5), qa_ref, dil_refs[0])
    o += WIDTH_A
    ka = proj(o, o + WIDTH_A)
    emit(ka, ka_ref, dil_refs[1])

    @pl.when(in_window)
    def _():
        kt_ref[...] = ka.T
    o += WIDTH_A
    va = proj(o, o + WIDTH_A)
    emit(va, va_ref, dil_refs[2])

    @pl.when(in_window)
    def _():
        vt_ref[...] = va.T
    o += WIDTH_A

    cos = cos_ref[...]
    sin = sin_ref[...]
    lane = lax.broadcasted_iota(jnp.int32, cos.shape, 1)
    first_half = (lane % KEY_DIM_B) < (KEY_DIM_B // 2)

    def rot(z):
        sw = jnp.where(first_half, pltpu.roll(z, QK_WIDTH_B - KEY_DIM_B // 2, 1), pltpu.roll(z, KEY_DIM_B // 2, 1))
        return z * cos + sw * sin

    qb_ref[...] = rot(proj(o, o + QK_WIDTH_B)).astype(BF16)
    o += QK_WIDTH_B
    kb_ref[...] = (rot(proj(o, o + QK_WIDTH_B)) * (KEY_DIM_B ** -0.5)).astype(BF16)
    o += QK_WIDTH_B
    vb_ref[...] = proj(o, o + WIDTH_B).astype(BF16)
    o += WIDTH_B
    gb_ref[...] = proj(o, o + WIDTH_B).astype(BF16)


def _in_proj(x2d, w_bf, cos_t, sin_t, tm, seq_tiles, win_tiles, dilations=()):
    T = x2d.shape[0]
    nt = T // tm
    n_seq = nt // seq_tiles
    j0 = seq_tiles - win_tiles

    def tok(i):
        return (i, 0)

    def tab(i):
        return (i % seq_tiles, 0)

    def win(i):
        return (i // seq_tiles, 0, jnp.maximum(i % seq_tiles - j0, 0))

    def tspec(w):
        return pl.BlockSpec((tm, w), tok)

    out_shape = (
        jax.ShapeDtypeStruct((T, WIDTH_A), BF16), jax.ShapeDtypeStruct((T, WIDTH_A), BF16),
        jax.ShapeDtypeStruct((T, WIDTH_A), BF16),
        jax.ShapeDtypeStruct((T, QK_WIDTH_B), BF16), jax.ShapeDtypeStruct((T, QK_WIDTH_B), BF16),
        jax.ShapeDtypeStruct((T, WIDTH_B), BF16), jax.ShapeDtypeStruct((T, WIDTH_B), BF16),
        jax.ShapeDtypeStruct((n_seq, WIDTH_A, win_tiles * tm), F32),
        jax.ShapeDtypeStruct((n_seq, WIDTH_A, win_tiles * tm), F32),
    ) + tuple(jax.ShapeDtypeStruct((T // d, d * WIDTH_A), BF16) for _ in range(3) for d in dilations)
    class_specs = tuple(pl.BlockSpec((tm // d, d * WIDTH_A), tok) for _ in range(3) for d in dilations)
    return pl.pallas_call(
        functools.partial(_in_proj_body, seq_tiles=seq_tiles, first_win=j0, dilations=tuple(dilations)),
        grid=(nt,),
        in_specs=[tspec(D_MODEL), pl.BlockSpec(w_bf.shape, lambda i: (0, 0)),
                  pl.BlockSpec((tm, QK_WIDTH_B), tab), pl.BlockSpec((tm, QK_WIDTH_B), tab)],
        out_specs=(tspec(WIDTH_A), tspec(WIDTH_A), tspec(WIDTH_A), tspec(QK_WIDTH_B), tspec(QK_WIDTH_B),
                   tspec(WIDTH_B), tspec(WIDTH_B),
                   pl.BlockSpec((None, WIDTH_A, tm), win), pl.BlockSpec((None, WIDTH_A, tm), win)) + class_specs,
        out_shape=out_shape,
        scratch_shapes=[pltpu.VMEM((WIDTH_A // LANES, tm, LANES), F32)] if dilations else [],
        compiler_params=_params(1),
        name="in_proj",
    )(x2d, w_bf, cos_t, sin_t)


def _rotary_tables(pos):
    half = KEY_DIM_B // 2
    inv_freq = 1.0 / (10000.0 ** jnp.linspace(0.0, 1.0, half, dtype=F32))
    ang = pos.astype(F32)[:, None] * inv_freq[None, :]
    cos = jnp.cos(ang)
    sin = jnp.sin(ang)
    cos_h = jnp.concatenate([cos, cos], axis=-1)
    sin_h = jnp.concatenate([-sin, sin], axis=-1)
    return jnp.tile(cos_h, (1, N_HEADS_B)), jnp.tile(sin_h, (1, N_HEADS_B))


ATTN_BLOCKS_PER_STEP = 4
RET_CHUNKS_PER_STEP = 2
RET_STEP_ROWS = RET_CHUNK * RET_CHUNKS_PER_STEP


def _attn_body(q_ref, k_ref, v_ref, bias_ref, o_ref, lse_ref, kb_ref, vb_ref):
    n = pl.program_id(2)
    rows = q_ref.shape[0]

    @pl.when(n == 0)
    def _():
        kb_ref[:BLK, :] = jnp.zeros((BLK, WIDTH_A), BF16)
        vb_ref[:BLK, :] = jnp.zeros((BLK, WIDTH_A), BF16)

    kb_ref[BLK:, :] = k_ref[...]
    vb_ref[BLK:, :] = v_ref[...]
    lane = lax.broadcasted_iota(jnp.int32, (BLK, LANES), 1)
    low = lane < HEAD_DIM_A
    nt = (((1,), (1,)), ((), ()))

    def sub_block(j, carry):
        r0 = pl.multiple_of(j * BLK, BLK)
        table = jnp.where((n == 0) & (j == 0), 0, 1)
        lse_tile = jnp.zeros((BLK, LANES), F32)
        for p in range(N_HEADS_A // 2):
            cs = slice(p * LANES, (p + 1) * LANES)
            qp = q_ref[pl.ds(r0, BLK), cs]
            kp = kb_ref[pl.ds(r0, 2 * BLK), cs]
            vp = vb_ref[pl.ds(r0, 2 * BLK), cs]
            zero = jnp.zeros_like(qp)
            outs = []
            for half, qh in enumerate((jnp.where(low, qp, zero), jnp.where(low, zero, qp))):
                h = 2 * p + half
                s = lax.dot_general(qh, kp, nt, preferred_element_type=F32) + bias_ref[table, h]
                m = jnp.max(s, axis=-1, keepdims=True)
                e = jnp.exp(s - m)
                den = jnp.sum(e, axis=-1, keepdims=True)
                outs.append(jnp.dot(e.astype(BF16), vp, preferred_element_type=F32) / den)
                lse_tile = jnp.where(lane == h, m + jnp.log(den), lse_tile)
            o_ref[pl.ds(r0, BLK), cs] = jnp.where(low, outs[0], outs[1]).astype(o_ref.dtype)
        lse_ref[pl.ds(r0, BLK), :] = lse_tile
        return carry

    lax.fori_loop(0, rows // BLK, sub_block, 0, unroll=2)
    kb_ref[:BLK, :] = k_ref[rows - BLK:, :]
    vb_ref[:BLK, :] = v_ref[rows - BLK:, :]


def _attn_bias_tables(rel_bias, window, dil):
    n_keys = window // dil
    i = np.arange(BLK)[:, None]
    j = np.arange(2 * BLK)[None, :]
    rel = BLK + i - j
    in_band = (rel >= 0) & (rel <= n_keys)
    bias = _bias_by_bucket(rel_bias, _t5_bucket(np.clip(rel, 0, None) * dil))
    later = jnp.where(jnp.asarray(in_band)[None], bias, NEG_INF)
    first = jnp.where(jnp.asarray(in_band & (j >= BLK))[None], bias, NEG_INF)
    return jnp.stack([first, later])


def _dilated_branch(q, k, v, bias_tab, dil):
    B, L, _ = q.shape
    rows = min(ATTN_BLOCKS_PER_STEP, L // BLK) * BLK
    assert L % rows == 0
    o_dtype = BF16 if RET_STEP_ROWS // dil >= 16 else F32

    def cls(b, r, n):
        return (b, n, r)

    qkv_spec = pl.BlockSpec((None, rows, WIDTH_A), cls)
    o, lse = pl.pallas_call(
        _attn_body,
        grid=(B, dil, L // rows),
        in_specs=[qkv_spec, qkv_spec, qkv_spec,
                  pl.BlockSpec(bias_tab.shape, lambda b, r, n: (0, 0, 0, 0))],
        out_specs=(qkv_spec, pl.BlockSpec((None, rows, LANES), cls)),
        out_shape=(jax.ShapeDtypeStruct((B, L, dil * WIDTH_A), o_dtype),
                   jax.ShapeDtypeStruct((B, L, dil * LANES), F32)),
        scratch_shapes=[pltpu.VMEM((BLK + rows, WIDTH_A), BF16), pltpu.VMEM((BLK + rows, WIDTH_A), BF16)],
        compiler_params=_params(3),
        name=f"dil_attn_d{dil}",
    )(q, k, v, bias_tab)
    return o, lse


def _ret_body(*refs, dilations):
    n_branch = len(dilations)
    qb_ref, kb_ref, vb_ref, gb_ref = refs[:4]
    p = 4
    if n_branch:
        o_refs = refs[p:p + n_branch]
        l_refs = refs[p + n_branch:p + 2 * n_branch]
        exp_ref = refs[p + 2 * n_branch]
        p += 2 * n_branch + 1
    else:
        oa_ref = refs[p]
        p += 1
    st0_ref, dmat_ref, qdec_ref, kdec_ref, cdec_ref, cat_ref, sto_ref, st_ref = refs[p:p + 8]
    if n_branch:
        us_ref, ls_ref = refs[p + 8:]

    def lse_token_order(l_ref, d):
        if d == 1:
            return l_ref[...]
        n = l_ref.shape[0]
        for r in range(d):
            ls_ref[pl.ds(r, n, stride=d), :] = l_ref[:, r * LANES:(r + 1) * LANES]
        return ls_ref[...]

    def out_token_order(o_ref, d):
        if d == 1:
            return o_ref[...].astype(F32)
        n = o_ref.shape[0]
        for r in range(d):
            for c in range(WIDTH_A // LANES):
                lo = r * WIDTH_A + c * LANES
                us_ref[c, pl.ds(r, n, stride=d), :] = o_ref[:, lo:lo + LANES].astype(F32)
        return jnp.concatenate([us_ref[c] for c in range(WIDTH_A // LANES)], axis=1)

    @pl.when(pl.program_id(1) == 0)
    def _():
        st_ref[...] = st0_ref[...]

    if n_branch:
        ls = [lse_token_order(l_ref, d) for l_ref, d in zip(l_refs, dilations)]
        mx = functools.reduce(jnp.maximum, ls)
        ws = [jnp.exp(l - mx) for l in ls]
        tot = functools.reduce(lambda a, b: a + b, ws)
        oa = None
        for w, o_ref, d in zip(ws, o_refs, dilations):
            o_tok = out_token_order(o_ref, d)
            w = w / tot
            w_hi = w.astype(BF16)
            w_lo = (w - w_hi.astype(F32)).astype(BF16)
            w_full = (jnp.dot(w_hi, exp_ref[...], preferred_element_type=F32)
                      + jnp.dot(w_lo, exp_ref[...], preferred_element_type=F32))
            term = w_full * o_tok
            oa = term if oa is None else oa + term
        cat_ref[:, :WIDTH_A] = oa.astype(BF16)
    else:
        cat_ref[:, :WIDTH_A] = oa_ref[...].astype(BF16)

    for h in range(N_HEADS_B):
        ks = slice(h * KEY_DIM_B, (h + 1) * KEY_DIM_B)
        vs = slice(h * VAL_DIM_B, (h + 1) * VAL_DIM_B)
        st = st_ref[h]
        for c in range(qb_ref.shape[0] // RET_CHUNK):
            rc = slice(c * RET_CHUNK, (c + 1) * RET_CHUNK)
            q = qb_ref[rc, ks]
            k = kb_ref[rc, ks]
            v = vb_ref[rc, vs]
            a = lax.dot_general(q, k, (((1,), (1,)), ((), ())), preferred_element_type=F32) * dmat_ref[h]
            o = (jnp.dot(a.astype(BF16), v, preferred_element_type=F32)
                 + jnp.dot(q, st.astype(BF16), preferred_element_type=F32) * qdec_ref[h])
            kd = (k.astype(F32) * kdec_ref[h]).astype(BF16)
            st = st * cdec_ref[h] + lax.dot_general(kd, v, (((0,), (0,)), ((), ())), preferred_element_type=F32)
            mu = jnp.mean(o, axis=-1, keepdims=True)
            var = jnp.mean(jnp.square(o - mu), axis=-1, keepdims=True)
            obn = (o - mu) * lax.rsqrt(var + GN_EPS)
            g = gb_ref[rc, vs].astype(F32)
            gated = g * (1.0 / (1.0 + jnp.exp(-g))) * obn
            cat_ref[rc, WIDTH_A + h * VAL_DIM_B:WIDTH_A + (h + 1) * VAL_DIM_B] = gated.astype(BF16)
        st_ref[h] = st
        sto_ref[h] = st


def _decay_tables(chunk, rows):
    H = N_HEADS_B
    log_g = jnp.log(1.0 - 2.0 ** (-5.0 - jnp.arange(H, dtype=F32)))
    i = jnp.arange(rows, dtype=F32)
    live = np.arange(rows) < chunk
    diff = i[:, None] - i[None, :]
    causal = (diff >= 0) & jnp.asarray(live[:, None] & live[None, :])
    dmat = jnp.where(causal[None], jnp.exp(jnp.where(causal, diff, 0.0)[None] * log_g[:, None, None]), 0.0)
    q_decay = jnp.where(jnp.asarray(live)[None], jnp.exp((i[None, :] + 1.0) * log_g[:, None]), 0.0)
    k_decay = jnp.where(jnp.asarray(live)[None], jnp.exp((chunk - 1.0 - i)[None, :] * log_g[:, None]), 0.0)
    c_decay = jnp.exp(chunk * log_g)
    qdec = jnp.broadcast_to(q_decay[:, :, None], (H, rows, VAL_DIM_B))
    kdec = jnp.broadcast_to(k_decay[:, :, None], (H, rows, KEY_DIM_B))
    cdec = jnp.broadcast_to(c_decay[:, None, None], (H, KEY_DIM_B, VAL_DIM_B))
    return dmat.astype(F32), qdec.astype(F32), kdec.astype(F32), cdec.astype(F32)


def _ret_mix(qb, kb, vb, gb, attn, state0, chunk, chunks_per_step=1):
    B, S, _ = qb.shape
    rows = RET_CHUNK * chunks_per_step
    nc = S // rows
    assert nc * rows == S
    tables = _decay_tables(chunk, RET_CHUNK)

    def tok(b, c):
        return (b, c, 0)

    def tspec(w):
        return pl.BlockSpec((None, rows, w), tok)

    def const(shape):
        return pl.BlockSpec(shape, lambda b, c: (0,) * len(shape))

    ins = [qb, kb, vb, gb]
    in_specs = [tspec(QK_WIDTH_B), tspec(QK_WIDTH_B), tspec(WIDTH_B), tspec(WIDTH_B)]
    scratch = [pltpu.VMEM((N_HEADS_B, KEY_DIM_B, VAL_DIM_B), F32)]
    if isinstance(attn, tuple):
        outs_a, lses, dilations = attn
        expand = np.zeros((LANES, WIDTH_A), np.float32)
        for h in range(N_HEADS_A):
            expand[h, h * HEAD_DIM_A:(h + 1) * HEAD_DIM_A] = 1.0
        ins += list(outs_a) + list(lses) + [jnp.asarray(expand, BF16)]
        in_specs += ([pl.BlockSpec((None, rows // d, d * WIDTH_A), tok) for d in dilations]
                     + [pl.BlockSpec((None, rows // d, d * LANES), tok) for d in dilations] + [const((LANES, WIDTH_A))])
        scratch += [pltpu.VMEM((WIDTH_A // LANES, rows, LANES), F32), pltpu.VMEM((rows, LANES), F32)]
    else:
        dilations = ()
        ins.append(attn)
        in_specs.append(tspec(WIDTH_A))
    st_shape = (N_HEADS_B, KEY_DIM_B, VAL_DIM_B)
    st_spec = pl.BlockSpec((None,) + st_shape, lambda b, c: (b, 0, 0, 0))
    ins += [state0] + list(tables)
    in_specs += [st_spec] + [const(t.shape) for t in tables]
    return pl.pallas_call(
        functools.partial(_ret_body, dilations=tuple(dilations)),
        grid=(B, nc),
        in_specs=in_specs,
        out_specs=(tspec(WIDTH_A + WIDTH_B), st_spec),
        out_shape=(jax.ShapeDtypeStruct((B, S, WIDTH_A + WIDTH_B), BF16),
                   jax.ShapeDtypeStruct((B,) + st_shape, F32)),
        scratch_shapes=scratch,
        compiler_params=_params(2),
        name=f"ret_mix_{len(dilations)}",
    )(*ins)


SAMP_Q_ROWS = 64
SAMP_NEW_LANES = 128


def _samp_attn_body(q_ref, kt_ref, vt_ref, knt_ref, vnt_ref, bc_ref, bn_ref, hm_ref, o_ref, ko_ref, vo_ref, *, ds):
    q = q_ref[...]
    kt = kt_ref[...]
    vt = vt_ref[...]
    knt = knt_ref[...]
    vnt = vnt_ref[...]
    w = kt.shape[1]
    is_new = lax.broadcasted_iota(jnp.int32, knt.shape, 1) >= SAMP_NEW_LANES - ds
    for src, new, dst in ((kt, knt, ko_ref), (vt, vnt, vo_ref)):
        rolled = pltpu.roll(src, w - ds, 1)
        dst[:, :w - SAMP_NEW_LANES] = rolled[:, :w - SAMP_NEW_LANES]
        dst[:, w - SAMP_NEW_LANES:] = jnp.where(is_new, new, rolled[:, w - SAMP_NEW_LANES:])

    s_c = jnp.dot(q, kt.astype(BF16), preferred_element_type=F32)
    s_n = jnp.dot(q, knt.astype(BF16), preferred_element_type=F32)
    es_c, es_n, dens, lses = [], [], [], []
    for n in range(len(DILATED_BRANCHES)):
        sc = s_c + bc_ref[n]
        sn = s_n + bn_ref[n]
        m = jnp.maximum(jnp.max(sc, axis=-1, keepdims=True), jnp.max(sn, axis=-1, keepdims=True))
        ec = jnp.exp(sc - m)
        en = jnp.exp(sn - m)
        den = jnp.sum(ec, axis=-1, keepdims=True) + jnp.sum(en, axis=-1, keepdims=True)
        es_c.append(ec)
        es_n.append(en)
        dens.append(den)
        lses.append(m + jnp.log(den))
    mx = functools.reduce(jnp.maximum, lses)
    ws = [jnp.exp(l - mx) for l in lses]
    tot = functools.reduce(lambda a, b: a + b, ws)
    p_c = None
    p_n = None
    for w, den, ec, en in zip(ws, dens, es_c, es_n):
        coef = w / (tot * den)
        p_c = coef * ec if p_c is None else p_c + coef * ec
        p_n = coef * en if p_n is None else p_n + coef * en
    nt = (((1,), (1,)), ((), ()))
    o = (lax.dot_general(p_c.astype(BF16), vt.astype(BF16), nt, preferred_element_type=F32)
         + lax.dot_general(p_n.astype(BF16), vnt.astype(BF16), nt, preferred_element_type=F32))
    o = o * hm_ref[...]
    o_ref[...] = jnp.sum(o.reshape(SUBLANES, N_HEADS_A, WIDTH_A), axis=1)


def _samp_bias_tables(rel_bias, w_buf, ds):
    tabs_c, tabs_n = [], []
    s = np.arange(SUBLANES)[:, None]
    live_s = s < ds
    first_new = SAMP_NEW_LANES - ds
    for window, dil in DILATED_BRANCHES:
        n_keys = window // dil
        for keys, live_k, tabs in ((np.arange(w_buf)[None, :], True, tabs_c),
                                   (w_buf - first_new + np.arange(SAMP_NEW_LANES)[None, :],
                                    np.arange(SAMP_NEW_LANES)[None, :] >= first_new, tabs_n)):
            n = keys.shape[1]
            dist = w_buf + s - keys
            valid = (dist >= 0) & (dist % dil == 0) & (dist // dil <= n_keys) & live_k
            bias = _bias_by_bucket(rel_bias, _t5_bucket(np.clip(dist, 0, None))).transpose(1, 0, 2)
            tab = jnp.where(jnp.asarray(valid)[:, None, :], bias, NEG_INF)
            pad = jnp.where(jnp.asarray(np.broadcast_to(live_k, dist.shape))[:, None, :], 0.0, NEG_INF)
            tab = jnp.where(jnp.asarray(live_s)[:, :, None], tab, pad)
            tabs.append(tab.reshape(SAMP_Q_ROWS, n))
    return jnp.stack(tabs_c), jnp.stack(tabs_n)


def _samp_attn(qa, cache_kt, cache_vt, knt, vnt, rel_bias, ds):
    DB, DS, _ = qa.shape
    W = cache_kt.shape[2]
    head_of_lane = np.arange(WIDTH_A) // HEAD_DIM_A
    hmask = (np.arange(SAMP_Q_ROWS)[:, None] % N_HEADS_A == head_of_lane[None, :])
    q8 = jnp.pad(qa, ((0, 0), (0, SUBLANES - DS), (0, 0)))
    q_rows = jnp.where(jnp.asarray(hmask)[None], jnp.repeat(q8, N_HEADS_A, axis=1), jnp.zeros((), BF16))
    bias_c, bias_n = _samp_bias_tables(rel_bias, W, DS)

    def per_b(rows, w):
        return pl.BlockSpec((None, rows, w), lambda b: (b, 0, 0))

    def const(a):
        return pl.BlockSpec(a.shape, lambda b: (0,) * a.ndim)

    hm = jnp.asarray(hmask, F32)
    return pl.pallas_call(
        functools.partial(_samp_attn_body, ds=ds),
        grid=(DB,),
        in_specs=[per_b(SAMP_Q_ROWS, WIDTH_A), per_b(WIDTH_A, W), per_b(WIDTH_A, W),
                  per_b(WIDTH_A, SAMP_NEW_LANES), per_b(WIDTH_A, SAMP_NEW_LANES),
                  const(bias_c), const(bias_n), const(hm)],
        out_specs=(per_b(SUBLANES, WIDTH_A), per_b(WIDTH_A, W), per_b(WIDTH_A, W)),
        out_shape=(jax.ShapeDtypeStruct((DB, SUBLANES, WIDTH_A), F32),
                   jax.ShapeDtypeStruct((DB, WIDTH_A, W), F32), jax.ShapeDtypeStruct((DB, WIDTH_A, W), F32)),
        compiler_params=_params(1),
        name="samp_attn",
    )(q_rows, cache_kt, cache_vt, knt, vnt, bias_c, bias_n, hm)


ROUTE_ROWS = 16


def _route_body(cat_ref, x_ref, wout_ref, g_ref, b_ref, rwh_ref, rwl_ref, rb_ref, triu_ref, base_ref, *refs,
                alpha, n_exp):
    h_ref, hp_ref, route_ref, route_t_ref, cnt_ref = refs[1:]

    @pl.when(pl.program_id(0) == 0)
    def _():
        cnt_ref[...] = base_ref[...]

    mix = jnp.dot(cat_ref[...], wout_ref[...], preferred_element_type=F32)
    y = alpha * x_ref[...] + mix
    mu = jnp.mean(y, axis=-1, keepdims=True)
    var = jnp.mean(jnp.square(y - mu), axis=-1, keepdims=True)
    h = (y - mu) * lax.rsqrt(var + LN_EPS) * g_ref[...] + b_ref[...]
    h_ref[...] = h
    hb = h.astype(BF16)
    hp_ref[...] = _pack_bf16_pairs(h)
    hl = (h - hb.astype(F32)).astype(BF16)
    nt = (((1,), (1,)), ((), ()))
    logits = (lax.dot_general(rwh_ref[...], hb, nt, preferred_element_type=F32)
              + lax.dot_general(rwl_ref[...], hb, nt, preferred_element_type=F32)
              + lax.dot_general(rwh_ref[...], hl, nt, preferred_element_type=F32))[:n_exp] + rb_ref[...]

    tm = logits.shape[1]
    sub = lax.broadcasted_iota(jnp.int32, (n_exp, tm), 0)
    work = logits
    vals, idxs = [], []
    for _ in range(TOP_K):
        m = jnp.max(work, axis=0, keepdims=True)
        idx = jnp.min(jnp.where(work == m, sub, n_exp), axis=0, keepdims=True)
        vals.append(m)
        idxs.append(idx)
        work = jnp.where(sub == idx, -jnp.inf, work)
    es = [jnp.exp(v - vals[0]) for v in vals]
    tot = functools.reduce(lambda a, b: a + b, es)
    onehot = jnp.zeros((n_exp, tm), F32)
    for idx in idxs:
        onehot = onehot + (sub == idx).astype(F32)
    before = jnp.dot(onehot.astype(BF16), triu_ref[...], preferred_element_type=F32) + cnt_ref[:, 0:1]
    rows = ([idx.astype(F32) for idx in idxs] + [e / tot for e in es]
            + [jnp.sum(jnp.where(sub == idx, before, 0.0), axis=0, keepdims=True) for idx in idxs])
    rows.append(jnp.zeros((ROUTE_ROWS - len(rows), tm), F32))
    record = jnp.concatenate(rows, axis=0)
    route_t_ref[...] = record
    route_ref[...] = jnp.concatenate([record, jnp.zeros((LANES - ROUTE_ROWS, tm), F32)], axis=0).T
    cnt_ref[...] = cnt_ref[...] + jnp.sum(onehot, axis=1, keepdims=True)


def _out_route(cat, x2d, w_out_bf, ln_g, ln_b, rw_hi_t, rw_lo_t, rb_col, base, hp_buf, tile0, tm, alpha):
    T = x2d.shape[0]
    n_exp = rb_col.shape[0]
    triu = jnp.asarray(np.triu(np.ones((tm, tm), np.float32), 1), BF16)

    def tok(i):
        return (i, 0)

    def const(a):
        return pl.BlockSpec(a.shape, lambda i: (0,) * a.ndim)

    ins = (cat, x2d, w_out_bf, ln_g, ln_b, rw_hi_t, rw_lo_t, rb_col, triu, base, hp_buf)
    in_specs = ([pl.BlockSpec((tm, cat.shape[1]), tok), pl.BlockSpec((tm, D_MODEL), tok)]
                + [const(a) for a in ins[2:-1]] + [pl.BlockSpec(memory_space=pl.ANY)])
    return pl.pallas_call(
        functools.partial(_route_body, alpha=alpha, n_exp=n_exp),
        grid=(T // tm,),
        in_specs=in_specs,
        out_specs=(pl.BlockSpec((tm, D_MODEL), tok), pl.BlockSpec((tm, D_MODEL // 2), lambda i: (tile0 + i, 0)),
                   pl.BlockSpec((tm, LANES), tok), pl.BlockSpec((ROUTE_ROWS, tm), lambda i: (0, i)),
                   pl.BlockSpec(base.shape, lambda i: (0, 0))),
        out_shape=(jax.ShapeDtypeStruct((T, D_MODEL), F32), jax.ShapeDtypeStruct(hp_buf.shape, jnp.int32),
                   jax.ShapeDtypeStruct((T, LANES), F32), jax.ShapeDtypeStruct((ROUTE_ROWS, T), F32),
                   jax.ShapeDtypeStruct(base.shape, F32)),
        input_output_aliases={len(ins) - 1: 1},
        compiler_params=_params(1),
        name="out_route",
    )(*ins)


MOE_CAST_ROWS = 128


def _moe_body(be_ref, first_ref, next_ref, nused_ref, x_ref, bgu_ref, bdn_ref, wgu_hbm, wdn_hbm, y_ref,
              gu_stage, dn_stage, wgu_bf, wdn_bf, sem):
    b = pl.program_id(0)
    d_exp = wdn_bf.shape[0]

    def fetch(e):
        return (pltpu.make_async_copy(wgu_hbm.at[e], gu_stage, sem.at[0]),
                pltpu.make_async_copy(wdn_hbm.at[e], dn_stage, sem.at[1]))

    @pl.when(b == 0)
    def _():
        for c in fetch(be_ref[0]):
            c.start()

    @pl.when(first_ref[b] == 1)
    def _():
        for c in fetch(be_ref[b]):
            c.wait()

        def cast_gu(i, c):
            r = pl.ds(pl.multiple_of(i * MOE_CAST_ROWS, MOE_CAST_ROWS), MOE_CAST_ROWS)
            wgu_bf[r, :] = gu_stage[r, :].astype(BF16)
            return c

        def cast_dn(i, c):
            r = pl.ds(pl.multiple_of(i * MOE_CAST_ROWS, MOE_CAST_ROWS), MOE_CAST_ROWS)
            wdn_bf[r, :] = dn_stage[r, :].astype(BF16)
            return c

        lax.fori_loop(0, gu_stage.shape[0] // MOE_CAST_ROWS, cast_gu, 0)
        lax.fori_loop(0, d_exp // MOE_CAST_ROWS, cast_dn, 0)

        @pl.when(next_ref[b] >= 0)
        def _():
            for c in fetch(next_ref[b]):
                c.start()

    @pl.when(b < nused_ref[0])
    def _():
        x_lo, x_hi = _unpack_bf16_pairs(x_ref[...])
        x_lo = x_lo.astype(BF16)
        x_hi = x_hi.astype(BF16)
        dh = x_lo.shape[1]

        def xw(cols):
            return (jnp.dot(x_lo, wgu_bf[:dh, cols], preferred_element_type=F32)
                    + jnp.dot(x_hi, wgu_bf[dh:, cols], preferred_element_type=F32) + bgu_ref[:, cols])

        half = d_exp // 2
        y = None
        for c in range(2):
            lo = c * half
            gate = jnp.minimum(xw(slice(lo, lo + half)), SWIGLU_LIMIT)
            up = jnp.clip(xw(slice(d_exp + lo, d_exp + lo + half)), -SWIGLU_LIMIT, SWIGLU_LIMIT)
            act = (up + 1.0) * gate * (1.0 / (1.0 + jnp.exp(-SWIGLU_ALPHA * gate)))
            part = jnp.dot(act.astype(BF16), wdn_bf[lo:lo + half, :], preferred_element_type=F32)
            y = part if y is None else y + part
        y_ref[...] = _pack_bf16_pairs(y + bdn_ref[...])

    @pl.when(b >= nused_ref[0])
    def _():
        y_ref[...] = jnp.zeros_like(y_ref)


def _moe_ffn(xs, rows, block_e, n_used, has_rows, w_gu, b_gu, w_dn, b_dn):
    E, D, two_de = w_gu.shape
    d_exp = two_de // 2
    nb = rows // MOE_BLOCK
    idx = jnp.arange(nb, dtype=jnp.int32)
    first = ((idx < n_used[0]) & ((idx == 0) | (block_e != jnp.roll(block_e, 1)))).astype(jnp.int32)
    ids = jnp.arange(E, dtype=jnp.int32)
    later = jnp.where((ids[None, :] > ids[:, None]) & has_rows[None, :], ids[None, :], E).min(axis=1)
    next_of = jnp.where(later == E, -1, later).astype(jnp.int32)
    next_e = jnp.sum(jnp.where(block_e[:, None] == ids[None, :], next_of[None, :], 0), axis=1).astype(jnp.int32)
    grid_spec = pltpu.PrefetchScalarGridSpec(
        num_scalar_prefetch=4,
        grid=(nb,),
        in_specs=[
            pl.BlockSpec((MOE_BLOCK, D // 2), lambda b, be, fi, nx, nu: (b, 0)),
            pl.BlockSpec((None, 1, two_de), lambda b, be, fi, nx, nu: (be[b], 0, 0)),
            pl.BlockSpec((None, 1, D), lambda b, be, fi, nx, nu: (be[b], 0, 0)),
            pl.BlockSpec(memory_space=pl.ANY),
            pl.BlockSpec(memory_space=pl.ANY),
        ],
        out_specs=pl.BlockSpec((MOE_BLOCK, D // 2), lambda b, be, fi, nx, nu: (b, 0)),
        scratch_shapes=[pltpu.VMEM((D, two_de), F32), pltpu.VMEM((d_exp, D), F32),
                        pltpu.VMEM((D, two_de), BF16), pltpu.VMEM((d_exp, D), BF16),
                        pltpu.SemaphoreType.DMA((2,))],
    )
    return pl.pallas_call(
        _moe_body,
        grid_spec=grid_spec,
        out_shape=jax.ShapeDtypeStruct((rows, D // 2), jnp.int32),
        compiler_params=_params(1),
        name="moe_ffn",
    )(block_e, first, next_e, n_used, xs, b_gu.reshape(E, 1, two_de), b_dn.reshape(E, 1, D), w_gu, w_dn)


def _ple_body(h_ref, ys_ref, route_ref, p_ref, g_ref, b_ref, wpg_ref, bpg_ref, wpp_ref, o_ref, *, alpha):
    route = route_ref[...]
    f_lo = None
    f_hi = None
    for k in range(TOP_K):
        lo, hi = _unpack_bf16_pairs(ys_ref[k])
        g = route[:, TOP_K + k:TOP_K + k + 1]
        f_lo = g * lo if f_lo is None else f_lo + g * lo
        f_hi = g * hi if f_hi is None else f_hi + g * hi
    y = alpha * h_ref[...] + jnp.concatenate([f_lo, f_hi], axis=1)
    mu = jnp.mean(y, axis=-1, keepdims=True)
    var = jnp.mean(jnp.square(y - mu), axis=-1, keepdims=True)
    h2 = (y - mu) * lax.rsqrt(var + LN_EPS) * g_ref[...] + b_ref[...]
    z = jnp.dot(h2.astype(BF16), wpg_ref[...], preferred_element_type=F32) + bpg_ref[...]
    gate = 1.0 / (1.0 + jnp.exp(-z))
    proj = jnp.dot(p_ref[...].astype(BF16), wpp_ref[...], preferred_element_type=F32)
    o_ref[...] = h2 + gate * proj


def _ffn_ple(h, y_slots, route, p, ln_g, ln_b, w_pg_bf, b_pg, w_pp_bf, tm, tile0, alpha):
    T = h.shape[0]

    def tok(i):
        return (i, 0)

    def const(a):
        return pl.BlockSpec(a.shape, lambda i: (0,) * a.ndim)

    consts = (ln_g, ln_b, w_pg_bf, b_pg, w_pp_bf)
    return pl.pallas_call(
        functools.partial(_ple_body, alpha=alpha),
        grid=(T // tm,),
        in_specs=[pl.BlockSpec((tm, D_MODEL), tok),
                  pl.BlockSpec((TOP_K, tm, D_MODEL // 2), lambda i: (0, tile0 + i, 0)),
                  pl.BlockSpec((tm, LANES), tok),
                  pl.BlockSpec((tm, p.shape[1]), tok)] + [const(a) for a in consts],
        out_specs=pl.BlockSpec((tm, D_MODEL), tok),
        out_shape=jax.ShapeDtypeStruct((T, D_MODEL), F32),
        compiler_params=_params(1),
        name="ffn_ple",
    )(h, y_slots, route, p, *consts)


def _row(v):
    return v.reshape(1, -1).astype(F32)


def kernel(x_prompt, x_sample, cache_win_k, cache_win_v, state_ret, p_prompt, p_sample, rel_bias, w_in, w_out,
           ln1_g, ln1_b, router_w, router_b, w_gate_up, b_gate_up, w_down, b_down, ln2_g, ln2_b,
           w_ple_gate, b_ple_gate, w_ple_proj):
    B, S, D = x_prompt.shape
    DB, DS, _ = x_sample.shape
    depth = w_in.shape[0]
    w_buf = cache_win_k.shape[2]
    n_exp = router_w.shape[-1]
    alpha = (2.0 * depth) ** 0.25
    assert depth == 1 and D == D_MODEL
    assert S % (BLK * MAX_DIL) == 0 and S >= WINDOW_MAX and w_buf == WINDOW_MAX and DS <= SUBLANES
    tm_p = 512
    Tp, Ts = B * S, DB * DS
    assert Tp % tm_p == 0 and Ts % SUBLANES == 0

    i = 0
    w_in_bf = w_in[i].astype(BF16)
    w_out_bf = w_out[i].astype(BF16)
    w_pg_bf = w_ple_gate[i].astype(BF16)
    w_pp_bf = w_ple_proj[i].astype(BF16)
    rw_t = jnp.pad(router_w[i].T, ((0, LANES - n_exp), (0, 0)))
    rw_hi = rw_t.astype(BF16)
    rw_lo = (rw_t - rw_hi.astype(F32)).astype(BF16)
    rb = router_b[i].astype(F32).reshape(n_exp, 1)
    T = Tp + Ts
    t_align = SC_WORKERS * SC_ALIGN
    T_pad = -(-T // t_align) * t_align
    assert Tp % Ts == 0

    cos_p, sin_p = _rotary_tables(jnp.arange(S, dtype=jnp.int32))
    dils = tuple(d for _, d in DILATED_BRANCHES)
    extra = tuple(d for d in dils if d > 1)
    outs = _in_proj(x_prompt.reshape(Tp, D), w_in_bf, cos_p, sin_p, tm_p, S // tm_p, WINDOW_MAX // tm_p, extra)
    qa, ka, va, qb, kb, vb, gb, kf, vf = outs[:9]
    qkv = {1: (qa, ka, va)}
    for t, d in enumerate(extra):
        qkv[d] = tuple(outs[9 + j * len(extra) + t] for j in range(3))

    def seq(t):
        return t.reshape(B, S, t.shape[-1])

    outs_a, lses = [], []
    for window, dil in DILATED_BRANCHES:
        q_d, k_d, v_d = (t.reshape(B, S // dil, dil * WIDTH_A) for t in qkv[dil])
        o_n, l_n = _dilated_branch(q_d, k_d, v_d, _attn_bias_tables(rel_bias, window, dil), dil)
        outs_a.append(o_n)
        lses.append(l_n)
    st_zero = jnp.zeros((B, N_HEADS_B, KEY_DIM_B, VAL_DIM_B), F32)
    cat_p, rst_p = _ret_mix(seq(qb), seq(kb), seq(vb), seq(gb), (outs_a, lses, dils), st_zero, RET_CHUNK,
                            RET_CHUNKS_PER_STEP)
    base0 = jnp.zeros((n_exp, LANES), F32)
    hp0 = jnp.zeros((T_pad, D // 2), jnp.int32)
    h_p, hp_buf, route_p, rt_p, cnt_p = _out_route(cat_p.reshape(Tp, -1), x_prompt.reshape(Tp, D), w_out_bf,
                                                   _row(ln1_g[i]), _row(ln1_b[i]), rw_hi, rw_lo, rb, base0, hp0, 0,
                                                   tm_p, alpha)

    pos_s = jnp.tile(PAST_LEN + jnp.arange(DS, dtype=jnp.int32), DB)
    cos_s, sin_s = _rotary_tables(pos_s)
    qa_s, _, _, qb_s, kb_s, vb_s, gb_s, kf_s, vf_s = _in_proj(x_sample.reshape(Ts, D), w_in_bf, cos_s, sin_s,
                                                             Ts, 1, 1)

    def positions_minor(t):
        return jnp.transpose(t, (0, 2, 3, 1)).reshape(DB, WIDTH_A, t.shape[1])

    def positions_major(t):
        return jnp.transpose(t.reshape(t.shape[0], N_HEADS_A, HEAD_DIM_A, t.shape[2]), (0, 3, 1, 2))[None]

    def new_columns(t):
        t = jnp.transpose(t.reshape(WIDTH_A, DB, DS), (1, 0, 2))
        return jnp.pad(t, ((0, 0), (0, 0), (SAMP_NEW_LANES - DS, 0)))

    oa_s, kt_out, vt_out = _samp_attn(qa_s.reshape(DB, DS, WIDTH_A), positions_minor(cache_win_k[i]),
                                      positions_minor(cache_win_v[i]), new_columns(kf_s), new_columns(vf_s),
                                      rel_bias, DS)

    def pad_rows(t, rows):
        t = t.reshape(DB, -1, t.shape[-1])
        return jnp.pad(t, ((0, 0), (0, rows - t.shape[1]), (0, 0)))

    cat_s, rst_s = _ret_mix(pad_rows(qb_s, RET_CHUNK), pad_rows(kb_s, RET_CHUNK), pad_rows(vb_s, RET_CHUNK),
                            pad_rows(gb_s, RET_CHUNK), pad_rows(oa_s, RET_CHUNK), state_ret[i].astype(F32), DS)
    cat_s = cat_s[:, :DS].reshape(Ts, -1)
    h_s, hp_all, route_s, rt_s, cnt = _out_route(cat_s, x_sample.reshape(Ts, D), w_out_bf, _row(ln1_g[i]),
                                                 _row(ln1_b[i]), rw_hi, rw_lo, rb, cnt_p, hp_buf, Tp // Ts, Ts, alpha)

    n_fill = T_pad - T
    record = jnp.concatenate([rt_p, rt_s, jnp.zeros((ROUTE_ROWS, n_fill), F32)], axis=1)
    top_idx = record[:TOP_K].astype(jnp.int32)
    rank = record[2 * TOP_K:3 * TOP_K].astype(jnp.int32)
    counts = cnt[:, 0].astype(jnp.int32)
    padded = (counts + MOE_BLOCK - 1) // MOE_BLOCK * MOE_BLOCK
    pad_end = jnp.cumsum(padded)
    pad_start = pad_end - padded
    ids = jnp.arange(n_exp, dtype=jnp.int32)[:, None, None]
    dest = jnp.sum(jnp.where(top_idx[None] == ids, pad_start[:, None, None], 0), axis=0) + rank
    n_blocks = -(-T * TOP_K // MOE_BLOCK) + n_exp
    rows = n_blocks * MOE_BLOCK
    block_start = jnp.arange(n_blocks, dtype=jnp.int32) * MOE_BLOCK
    block_e = jnp.minimum(jnp.sum(pad_end[None, :] <= block_start[:, None], axis=1), n_exp - 1).astype(jnp.int32)
    n_used = (pad_end[-1:] // MOE_BLOCK).astype(jnp.int32)
    tok_id = jnp.arange(T_pad, dtype=jnp.int32)[None, :]
    spare = rows + (tok_id - T) * TOP_K + jnp.arange(TOP_K, dtype=jnp.int32)[:, None]
    dest_sc = jnp.where(tok_id >= T, spare, dest).reshape(-1)
    dest_ga = jnp.where(tok_id >= T, 0, dest).reshape(-1)
    xs = _sc_scatter_rows(hp_all, dest_sc, rows + n_fill * TOP_K)
    ys = _moe_ffn(xs, rows, block_e, n_used, padded > 0, w_gate_up[i], b_gate_up[i], w_down[i], b_down[i])
    y_slots = _sc_gather_rows(ys, dest_ga).reshape(TOP_K, T_pad, D // 2)

    ple_args = (_row(ln2_g[i]), _row(ln2_b[i]), w_pg_bf, _row(b_ple_gate[i]), w_pp_bf)
    y_p = _ffn_ple(h_p, y_slots, route_p, p_prompt[i].reshape(Tp, D_PLE), *ple_args, tm_p, 0, alpha)
    y_s = _ffn_ple(h_s, y_slots, route_s, p_sample[i].reshape(Ts, D_PLE), *ple_args, Ts, Tp // Ts, alpha)

    return (y_p.reshape(B, S, D), y_s.reshape(DB, DS, D), positions_major(kf), positions_major(vf),
            rst_p[None], positions_major(kt_out), positions_major(vt_out), rst_s[None])
```

```python
import functools

import numpy as np
import jax
import jax.numpy as jnp
from jax import lax
from jax.experimental import pallas as pl
from jax.experimental.pallas import tpu as pltpu
from jax.experimental.pallas import tpu_sc as plsc

F32 = jnp.float32
BF16 = jnp.bfloat16

D_MODEL = 1024
D_PLE = 256
N_HEADS_A = 8
HEAD_DIM_A = 64
WIDTH_A = N_HEADS_A * HEAD_DIM_A
DILATED_BRANCHES = ((128, 1), (512, 4), (2048, 16))
BLK = 128
WINDOW_MAX = 2048
MAX_DIL = 16
NUM_BUCKETS = 32
MAX_DISTANCE = 2048
N_HEADS_B = 4
KEY_DIM_B = 64
VAL_DIM_B = 128
QK_WIDTH_B = N_HEADS_B * KEY_DIM_B
WIDTH_B = N_HEADS_B * VAL_DIM_B
RET_CHUNK = 128
GN_EPS = 1e-6
TOP_K = 4
SWIGLU_LIMIT = 7.0
SWIGLU_ALPHA = 1.702
LN_EPS = 1e-5
NEG_INF = -1e30
PAST_LEN = 16384
MOE_BLOCK = 512
LANES = 128
SUBLANES = 8
VMEM_LIMIT = 52 * 1024 * 1024


def _params(n_axes, vmem=VMEM_LIMIT):
    return pltpu.CompilerParams(dimension_semantics=("arbitrary",) * n_axes, vmem_limit_bytes=vmem)


def _t5_bucket(dist):
    dist = np.asarray(dist, dtype=np.int32)
    max_exact = NUM_BUCKETS // 2
    d = np.maximum(dist, 1).astype(np.float32)
    large = max_exact + (np.log(d / max_exact) / np.log(MAX_DISTANCE / max_exact)
                         * (NUM_BUCKETS - max_exact)).astype(np.int32)
    large = np.minimum(large, NUM_BUCKETS - 1)
    return np.where(dist < max_exact, dist, large).astype(np.int32)


def _bias_by_bucket(rel_bias, buckets):
    b = jnp.asarray(buckets, jnp.int32)
    ids = jnp.arange(NUM_BUCKETS, dtype=jnp.int32).reshape((NUM_BUCKETS, 1) + (1,) * b.ndim)
    vals = rel_bias.astype(F32).reshape((NUM_BUCKETS, rel_bias.shape[1]) + (1,) * b.ndim)
    return jnp.sum(jnp.where(b[None, None] == ids, vals, 0.0), axis=0)


def _pack_bf16_pairs(v):
    w = v.shape[1] // 2
    lo = lax.bitcast_convert_type(v[:, :w].astype(BF16).astype(F32), jnp.uint32) >> 16
    hi = lax.bitcast_convert_type(v[:, w:].astype(BF16).astype(F32), jnp.uint32) & jnp.uint32(0xFFFF0000)
    return lax.bitcast_convert_type(lo | hi, jnp.int32)


def _unpack_bf16_pairs(p):
    u = lax.bitcast_convert_type(p, jnp.uint32)
    lo = lax.bitcast_convert_type(u << 16, F32)
    hi = lax.bitcast_convert_type(u & jnp.uint32(0xFFFF0000), F32)
    return lo, hi


SC_CORES = 2
SC_SUBCORES = 16
SC_WORKERS = SC_CORES * SC_SUBCORES
SC_ALIGN = 8
SC_CHUNK_ROWS = 80


def _sc_mesh():
    return plsc.VectorSubcoreMesh(core_axis_name="c", subcore_axis_name="s")


def _sc_chunk(per_worker):
    c = max(d for d in range(SC_ALIGN, SC_CHUNK_ROWS + 1, SC_ALIGN) if per_worker % d == 0)
    return c


def _sc_scatter_rows(src, dest_flat, n_out):
    T, W = src.shape
    K = dest_flat.shape[0] // T
    per_w = T // SC_WORKERS
    assert per_w * SC_WORKERS == T and per_w % SC_ALIGN == 0
    chunk = _sc_chunk(per_w)
    n_chunks = per_w // chunk

    @functools.partial(
        pl.kernel, mesh=_sc_mesh(), out_type=jax.ShapeDtypeStruct((n_out, W), src.dtype),
        scratch_types=[pltpu.VMEM((chunk, W), src.dtype)] * 2 + [pltpu.VMEM((chunk,), jnp.int32)] * (2 * K)
        + [pltpu.SemaphoreType.DMA] * 2,
        name="sc_scatter_rows")
    def k(src_hbm, dest_hbm, out_hbm, rows_a, rows_b, *rest):
        bufs = ((rows_a, rest[:K], rest[2 * K]), (rows_b, rest[K:2 * K], rest[2 * K + 1]))
        base = (lax.axis_index("s") * SC_CORES + lax.axis_index("c")) * per_w

        def scatters(buf):
            rows_v, idx_vs, sem = buf
            return [pltpu.make_async_copy(rows_v, out_hbm.at[idx_vs[kk]], sem) for kk in range(K)]

        def step(j, buf):
            rows_v, idx_vs, _ = buf

            @pl.when(j >= 2)
            def _():
                for c in scatters(buf):
                    c.wait()

            off = pl.multiple_of(base + j * chunk, SC_ALIGN)
            pltpu.sync_copy(src_hbm.at[pl.ds(off, chunk)], rows_v)
            for kk in range(K):
                pltpu.sync_copy(dest_hbm.at[pl.ds(kk * T + off, chunk)], idx_vs[kk])
            for c in scatters(buf):
                c.start()

        @pl.loop(0, n_chunks)
        def _(j):
            for parity in range(2):
                @pl.when(j % 2 == parity)
                def _():
                    step(j, bufs[parity])

        for j in range(max(n_chunks - 2, 0), n_chunks):
            for c in scatters(bufs[j % 2]):
                c.wait()

    return k(src, dest_flat)


def _sc_gather_rows(table, idx):
    B = idx.shape[0]
    W = table.shape[1]
    per_w = B // SC_WORKERS
    assert per_w * SC_WORKERS == B and per_w % SC_ALIGN == 0
    chunk = _sc_chunk(per_w)
    n_chunks = per_w // chunk

    @functools.partial(
        pl.kernel, mesh=_sc_mesh(), out_type=jax.ShapeDtypeStruct((B, W), table.dtype),
        scratch_types=[pltpu.VMEM((chunk,), jnp.int32)] * 2 + [pltpu.VMEM((chunk, W), table.dtype)] * 2
        + [pltpu.SemaphoreType.DMA] * 2,
        name="sc_gather_rows")
    def k(table_hbm, idx_hbm, out_hbm, idx_a, idx_b, rows_a, rows_b, sem_a, sem_b):
        bufs = ((idx_a, rows_a, sem_a), (idx_b, rows_b, sem_b))
        base = (lax.axis_index("s") * SC_CORES + lax.axis_index("c")) * per_w

        def gather(buf):
            idx_v, rows_v, sem = buf
            return pltpu.make_async_copy(table_hbm.at[idx_v], rows_v, sem)

        def finish(j, buf):
            gather(buf).wait()
            off = pl.multiple_of(base + j * chunk, SC_ALIGN)
            pltpu.sync_copy(buf[1], out_hbm.at[pl.ds(off, chunk)])

        def step(j, buf, other):
            off = pl.multiple_of(base + j * chunk, SC_ALIGN)
            pltpu.sync_copy(idx_hbm.at[pl.ds(off, chunk)], buf[0])
            gather(buf).start()

            @pl.when(j >= 1)
            def _():
                finish(j - 1, other)

        @pl.loop(0, n_chunks)
        def _(j):
            for parity in range(2):
                @pl.when(j % 2 == parity)
                def _():
                    step(j, bufs[parity], bufs[1 - parity])

        finish(n_chunks - 1, bufs[(n_chunks - 1) % 2])

    return k(table, idx)


def _in_proj_body(x_ref, w_ref, cos_ref, sin_ref, *refs, seq_tiles, first_win, dilations):
    qa_ref, ka_ref, va_ref, qb_ref, kb_ref, vb_ref, gb_ref, kt_ref, vt_ref = refs[:9]
    n_d = len(dilations)
    dil_refs = [refs[9 + t * n_d:9 + (t + 1) * n_d] for t in range(3)]
    zs_ref = refs[9 + 3 * n_d] if n_d else None
    x = x_ref[...].astype(BF16)
    in_window = pl.program_id(0) % seq_tiles >= first_win

    def emit(z, token_ref, class_refs):
        token_ref[...] = z.astype(BF16)
        if not class_refs:
            return
        for c in range(WIDTH_A // LANES):
            zs_ref[c] = z[:, c * LANES:(c + 1) * LANES]
        for d, ref in zip(dilations, class_refs):
            n = z.shape[0] // d
            for r in range(d):
                for c in range(WIDTH_A // LANES):
                    lo = r * WIDTH_A + c * LANES
                    ref[:, lo:lo + LANES] = zs_ref[c, pl.ds(r, n, stride=d), :].astype(BF16)

    def proj(lo, hi):
        return jnp.dot(x, w_ref[:, lo:hi], preferred_element_type=F32)

    o = 0
    qa = proj(o, o + WIDTH_A) * (HEAD_DIM_A ** -0.5)
    o += WIDTH_A
    ka = proj(o, o + WIDTH_A)
    o += WIDTH_A
    emit(qa, qa_ref, dil_refs[0])
    va = proj(o, o + WIDTH_A)
    o += WIDTH_A
    emit(ka, ka_ref, dil_refs[1])

    @pl.when(in_window)
    def _():
        kt_ref[...] = ka.T
        vt_ref[...] = va.T

    cos = cos_ref[...]
    sin = sin_ref[...]
    lane = lax.broadcasted_iota(jnp.int32, cos.shape, 1)
    first_half = (lane % KEY_DIM_B) < (KEY_DIM_B // 2)

    def rot(z):
        sw = jnp.where(first_half, pltpu.roll(z, QK_WIDTH_B - KEY_DIM_B // 2, 1), pltpu.roll(z, KEY_DIM_B // 2, 1))
        return z * cos + sw * sin

    qb_ref[...] = rot(proj(o, o + QK_WIDTH_B)).astype(BF16)
    o += QK_WIDTH_B
    emit(va, va_ref, dil_refs[2])
    kb_ref[...] = (rot(proj(o, o + QK_WIDTH_B)) * (KEY_DIM_B ** -0.5)).astype(BF16)
    o += QK_WIDTH_B
    vb_ref[...] = proj(o, o + WIDTH_B).astype(BF16)
    o += WIDTH_B
    gb_ref[...] = proj(o, o + WIDTH_B).astype(BF16)


def _in_proj(x2d, w_bf, cos_t, sin_t, tm, seq_tiles, win_tiles, dilations=()):
    T = x2d.shape[0]
    nt = T // tm
    n_seq = nt // seq_tiles
    j0 = seq_tiles - win_tiles

    def tok(i):
        return (i, 0)

    def tab(i):
        return (i % seq_tiles, 0)

    def win(i):
        return (i // seq_tiles, 0, jnp.maximum(i % seq_tiles - j0, 0))

    def tspec(w):
        return pl.BlockSpec((tm, w), tok)

    out_shape = (
        jax.ShapeDtypeStruct((T, WIDTH_A), BF16), jax.ShapeDtypeStruct((T, WIDTH_A), BF16),
        jax.ShapeDtypeStruct((T, WIDTH_A), BF16),
        jax.ShapeDtypeStruct((T, QK_WIDTH_B), BF16), jax.ShapeDtypeStruct((T, QK_WIDTH_B), BF16),
        jax.ShapeDtypeStruct((T, WIDTH_B), BF16), jax.ShapeDtypeStruct((T, WIDTH_B), BF16),
        jax.ShapeDtypeStruct((n_seq, WIDTH_A, win_tiles * tm), F32),
        jax.ShapeDtypeStruct((n_seq, WIDTH_A, win_tiles * tm), F32),
    ) + tuple(jax.ShapeDtypeStruct((T // d, d * WIDTH_A), BF16) for _ in range(3) for d in dilations)
    class_specs = tuple(pl.BlockSpec((tm // d, d * WIDTH_A), tok) for _ in range(3) for d in dilations)
    return pl.pallas_call(
        functools.partial(_in_proj_body, seq_tiles=seq_tiles, first_win=j0, dilations=tuple(dilations)),
        grid=(nt,),
        in_specs=[tspec(D_MODEL), pl.BlockSpec(w_bf.shape, lambda i: (0, 0)),
                  pl.BlockSpec((tm, QK_WIDTH_B), tab), pl.BlockSpec((tm, QK_WIDTH_B), tab)],
        out_specs=(tspec(WIDTH_A), tspec(WIDTH_A), tspec(WIDTH_A), tspec(QK_WIDTH_B), tspec(QK_WIDTH_B),
                   tspec(WIDTH_B), tspec(WIDTH_B),
                   pl.BlockSpec((None, WIDTH_A, tm), win), pl.BlockSpec((None, WIDTH_A, tm), win)) + class_specs,
        out_shape=out_shape,
        scratch_shapes=[pltpu.VMEM((WIDTH_A // LANES, tm, LANES), F32)] if dilations else [],
        compiler_params=_params(1),
        name="in_proj",
    )(x2d, w_bf, cos_t, sin_t)


def _rotary_tables(pos):
    half = KEY_DIM_B // 2
    inv_freq = 1.0 / (10000.0 ** jnp.linspace(0.0, 1.0, half, dtype=F32))
    ang = pos.astype(F32)[:, None] * inv_freq[None, :]
    cos = jnp.cos(ang)
    sin = jnp.sin(ang)
    cos_h = jnp.concatenate([cos, cos], axis=-1)
    sin_h = jnp.concatenate([-sin, sin], axis=-1)
    return jnp.tile(cos_h, (1, N_HEADS_B)), jnp.tile(sin_h, (1, N_HEADS_B))


ATTN_BLOCKS_PER_STEP = 8
RET_CHUNKS_PER_STEP = 4
RET_STEP_ROWS = RET_CHUNK * RET_CHUNKS_PER_STEP


def _attn_body(q_ref, k_ref, v_ref, bias_ref, o_ref, lse_ref, kb_ref, vb_ref):
    n = pl.program_id(2)
    rows = q_ref.shape[0]

    @pl.when(n == 0)
    def _():
        kb_ref[:BLK, :] = jnp.zeros((BLK, WIDTH_A), BF16)
        vb_ref[:BLK, :] = jnp.zeros((BLK, WIDTH_A), BF16)

    kb_ref[BLK:, :] = k_ref[...]
    vb_ref[BLK:, :] = v_ref[...]
    lane = lax.broadcasted_iota(jnp.int32, (BLK, LANES), 1)
    low = lane < HEAD_DIM_A
    nt = (((1,), (1,)), ((), ()))

    def sub_block(j, carry):
        r0 = pl.multiple_of(j * BLK, BLK)
        table = jnp.where((n == 0) & (j == 0), 0, 1)
        lse_tile = jnp.zeros((BLK, LANES), F32)
        for p in range(N_HEADS_A // 2):
            cs = slice(p * LANES, (p + 1) * LANES)
            qp = q_ref[pl.ds(r0, BLK), cs]
            kp = kb_ref[pl.ds(r0, 2 * BLK), cs]
            vp = vb_ref[pl.ds(r0, 2 * BLK), cs]
            zero = jnp.zeros_like(qp)
            outs = []
            for half, qh in enumerate((jnp.where(low, qp, zero), jnp.where(low, zero, qp))):
                h = 2 * p + half
                s = lax.dot_general(qh, kp, nt, preferred_element_type=F32) + bias_ref[table, h]
                m = jnp.max(s, axis=-1, keepdims=True)
                e = jnp.exp(s - m)
                den = jnp.sum(e, axis=-1, keepdims=True)
                outs.append(jnp.dot(e.astype(BF16), vp, preferred_element_type=F32) / den)
                lse_tile = jnp.where(lane == h, m + jnp.log(den), lse_tile)
            o_ref[pl.ds(r0, BLK), cs] = jnp.where(low, outs[0], outs[1]).astype(o_ref.dtype)
        lse_ref[pl.ds(r0, BLK), :] = lse_tile
        return carry

    lax.fori_loop(0, rows // BLK, sub_block, 0, unroll=2)
    kb_ref[:BLK, :] = k_ref[rows - BLK:, :]
    vb_ref[:BLK, :] = v_ref[rows - BLK:, :]


def _attn_bias_tables(rel_bias, window, dil):
    n_keys = window // dil
    i = np.arange(BLK)[:, None]
    j = np.arange(2 * BLK)[None, :]
    rel = BLK + i - j
    in_band = (rel >= 0) & (rel <= n_keys)
    bias = _bias_by_bucket(rel_bias, _t5_bucket(np.clip(rel, 0, None) * dil))
    later = jnp.where(jnp.asarray(in_band)[None], bias, NEG_INF)
    first = jnp.where(jnp.asarray(in_band & (j >= BLK))[None], bias, NEG_INF)
    return jnp.stack([first, later])


def _dilated_branch(q, k, v, bias_tab, dil):
    B, L, _ = q.shape
    rows = min(ATTN_BLOCKS_PER_STEP, L // BLK) * BLK
    assert L % rows == 0
    o_dtype = BF16 if RET_STEP_ROWS // dil >= 16 else F32

    def cls(b, r, n):
        return (b, n, r)

    qkv_spec = pl.BlockSpec((None, rows, WIDTH_A), cls)
    o, lse = pl.pallas_call(
        _attn_body,
        grid=(B, dil, L // rows),
        in_specs=[qkv_spec, qkv_spec, qkv_spec,
                  pl.BlockSpec(bias_tab.shape, lambda b, r, n: (0, 0, 0, 0))],
        out_specs=(qkv_spec, pl.BlockSpec((None, rows, LANES), cls)),
        out_shape=(jax.ShapeDtypeStruct((B, L, dil * WIDTH_A), o_dtype),
                   jax.ShapeDtypeStruct((B, L, dil * LANES), F32)),
        scratch_shapes=[pltpu.VMEM((BLK + rows, WIDTH_A), BF16), pltpu.VMEM((BLK + rows, WIDTH_A), BF16)],
        compiler_params=_params(3),
        name=f"dil_attn_d{dil}",
    )(q, k, v, bias_tab)
    return o, lse


def _ret_body(*refs, dilations):
    n_branch = len(dilations)
    qb_ref, kb_ref, vb_ref, gb_ref = refs[:4]
    p = 4
    if n_branch:
        o_refs = refs[p:p + n_branch]
        l_refs = refs[p + n_branch:p + 2 * n_branch]
        exp_ref = refs[p + 2 * n_branch]
        p += 2 * n_branch + 1
    else:
        oa_ref = refs[p]
        p += 1
    st0_ref, dmat_ref, qdec_ref, kdec_ref, cdec_ref, cat_ref, sto_ref, st_ref = refs[p:p + 8]
    if n_branch:
        us_ref, ls_ref = refs[p + 8:]

    def lse_token_order(l_ref, d):
        if d == 1:
            return l_ref[...]
        n = l_ref.shape[0]
        for r in range(d):
            ls_ref[pl.ds(r, n, stride=d), :] = l_ref[:, r * LANES:(r + 1) * LANES]
        return ls_ref[...]

    def out_token_order(o_ref, d):
        if d == 1:
            return o_ref[...].astype(F32)
        n = o_ref.shape[0]
        for r in range(d):
            for c in range(WIDTH_A // LANES):
                lo = r * WIDTH_A + c * LANES
                us_ref[c, pl.ds(r, n, stride=d), :] = o_ref[:, lo:lo + LANES].astype(F32)
        return jnp.concatenate([us_ref[c] for c in range(WIDTH_A // LANES)], axis=1)

    @pl.when(pl.program_id(1) == 0)
    def _():
        st_ref[...] = st0_ref[...]

    if n_branch:
        ls = [lse_token_order(l_ref, d) for l_ref, d in zip(l_refs, dilations)]
        mx = functools.reduce(jnp.maximum, ls)
        ws = [jnp.exp(l - mx) for l in ls]
        tot = functools.reduce(lambda a, b: a + b, ws)
        oa = None
        for w, o_ref, d in zip(ws, o_refs, dilations):
            o_tok = out_token_order(o_ref, d)
            w = w / tot
            w_hi = w.astype(BF16)
            w_lo = (w - w_hi.astype(F32)).astype(BF16)
            w_full = (jnp.dot(w_hi, exp_ref[...], preferred_element_type=F32)
                      + jnp.dot(w_lo, exp_ref[...], preferred_element_type=F32))
            term = w_full * o_tok
            oa = term if oa is None else oa + term
        cat_ref[:, :WIDTH_A] = oa.astype(BF16)
    else:
        cat_ref[:, :WIDTH_A] = oa_ref[...].astype(BF16)

    for h in range(N_HEADS_B):
        ks = slice(h * KEY_DIM_B, (h + 1) * KEY_DIM_B)
        vs = slice(h * VAL_DIM_B, (h + 1) * VAL_DIM_B)
        st = st_ref[h]
        for c in range(qb_ref.shape[0] // RET_CHUNK):
            rc = slice(c * RET_CHUNK, (c + 1) * RET_CHUNK)
            q = qb_ref[rc, ks]
            k = kb_ref[rc, ks]
            v = vb_ref[rc, vs]
            a = lax.dot_general(q, k, (((1,), (1,)), ((), ())), preferred_element_type=F32) * dmat_ref[h]
            o = (jnp.dot(a.astype(BF16), v, preferred_element_type=F32)
                 + jnp.dot(q, st.astype(BF16), preferred_element_type=F32) * qdec_ref[h])
            kd = (k.astype(F32) * kdec_ref[h]).astype(BF16)
            st = st * cdec_ref[h] + lax.dot_general(kd, v, (((0,), (0,)), ((), ())), preferred_element_type=F32)
            mu = jnp.mean(o, axis=-1, keepdims=True)
            var = jnp.mean(jnp.square(o - mu), axis=-1, keepdims=True)
            obn = (o - mu) * lax.rsqrt(var + GN_EPS)
            g = gb_ref[rc, vs].astype(F32)
            gated = g * (1.0 / (1.0 + jnp.exp(-g))) * obn
            cat_ref[rc, WIDTH_A + h * VAL_DIM_B:WIDTH_A + (h + 1) * VAL_DIM_B] = gated.astype(BF16)
        st_ref[h] = st
        sto_ref[h] = st


def _decay_tables(chunk, rows):
    H = N_HEADS_B
    log_g = jnp.log(1.0 - 2.0 ** (-5.0 - jnp.arange(H, dtype=F32)))
    i = jnp.arange(rows, dtype=F32)
    live = np.arange(rows) < chunk
    diff = i[:, None] - i[None, :]
    causal = (diff >= 0) & jnp.asarray(live[:, None] & live[None, :])
    dmat = jnp.where(causal[None], jnp.exp(jnp.where(causal, diff, 0.0)[None] * log_g[:, None, None]), 0.0)
    q_decay = jnp.where(jnp.asarray(live)[None], jnp.exp((i[None, :] + 1.0) * log_g[:, None]), 0.0)
    k_decay = jnp.where(jnp.asarray(live)[None], jnp.exp((chunk - 1.0 - i)[None, :] * log_g[:, None]), 0.0)
    c_decay = jnp.exp(chunk * log_g)
    qdec = jnp.broadcast_to(q_decay[:, :, None], (H, rows, VAL_DIM_B))
    kdec = jnp.broadcast_to(k_decay[:, :, None], (H, rows, KEY_DIM_B))
    cdec = jnp.broadcast_to(c_decay[:, None, None], (H, KEY_DIM_B, VAL_DIM_B))
    return dmat.astype(F32), qdec.astype(F32), kdec.astype(F32), cdec.astype(F32)


def _ret_mix(qb, kb, vb, gb, attn, state0, chunk, chunks_per_step=1):
    B, S, _ = qb.shape
    rows = RET_CHUNK * chunks_per_step
    nc = S // rows
    assert nc * rows == S
    tables = _decay_tables(chunk, RET_CHUNK)

    def tok(b, c):
        return (b, c, 0)

    def tspec(w):
        return pl.BlockSpec((None, rows, w), tok)

    def const(shape):
        return pl.BlockSpec(shape, lambda b, c: (0,) * len(shape))

    ins = [qb, kb, vb, gb]
    in_specs = [tspec(QK_WIDTH_B), tspec(QK_WIDTH_B), tspec(WIDTH_B), tspec(WIDTH_B)]
    scratch = [pltpu.VMEM((N_HEADS_B, KEY_DIM_B, VAL_DIM_B), F32)]
    if isinstance(attn, tuple):
        outs_a, lses, dilations = attn
        expand = np.zeros((LANES, WIDTH_A), np.float32)
        for h in range(N_HEADS_A):
            expand[h, h * HEAD_DIM_A:(h + 1) * HEAD_DIM_A] = 1.0
        ins += list(outs_a) + list(lses) + [jnp.asarray(expand, BF16)]
        in_specs += ([pl.BlockSpec((None, rows // d, d * WIDTH_A), tok) for d in dilations]
                     + [pl.BlockSpec((None, rows // d, d * LANES), tok) for d in dilations] + [const((LANES, WIDTH_A))])
        scratch += [pltpu.VMEM((WIDTH_A // LANES, rows, LANES), F32), pltpu.VMEM((rows, LANES), F32)]
    else:
        dilations = ()
        ins.append(attn)
        in_specs.append(tspec(WIDTH_A))
    st_shape = (N_HEADS_B, KEY_DIM_B, VAL_DIM_B)
    st_spec = pl.BlockSpec((None,) + st_shape, lambda b, c: (b, 0, 0, 0))
    ins += [state0] + list(tables)
    in_specs += [st_spec] + [const(t.shape) for t in tables]
    return pl.pallas_call(
        functools.partial(_ret_body, dilations=tuple(dilations)),
        grid=(B, nc),
        in_specs=in_specs,
        out_specs=(tspec(WIDTH_A + WIDTH_B), st_spec),
        out_shape=(jax.ShapeDtypeStruct((B, S, WIDTH_A + WIDTH_B), BF16),
                   jax.ShapeDtypeStruct((B,) + st_shape, F32)),
        scratch_shapes=scratch,
        compiler_params=_params(2),
        name=f"ret_mix_{len(dilations)}",
    )(*ins)


SAMP_Q_ROWS = 64
SAMP_NEW_LANES = 128


def _samp_attn_body(q_ref, kt_ref, vt_ref, knt_ref, vnt_ref, bc_ref, bn_ref, hm_ref, o_ref, ko_ref, vo_ref, *, ds):
    q = q_ref[...]
    kt = kt_ref[...]
    vt = vt_ref[...]
    knt = knt_ref[...]
    vnt = vnt_ref[...]
    w = kt.shape[1]
    is_new = lax.broadcasted_iota(jnp.int32, knt.shape, 1) >= SAMP_NEW_LANES - ds
    for src, new, dst in ((kt, knt, ko_ref), (vt, vnt, vo_ref)):
        rolled = pltpu.roll(src, w - ds, 1)
        dst[:, :w - SAMP_NEW_LANES] = rolled[:, :w - SAMP_NEW_LANES]
        dst[:, w - SAMP_NEW_LANES:] = jnp.where(is_new, new, rolled[:, w - SAMP_NEW_LANES:])

    s_c = jnp.dot(q, kt.astype(BF16), preferred_element_type=F32)
    s_n = jnp.dot(q, knt.astype(BF16), preferred_element_type=F32)
    es_c, es_n, dens, lses = [], [], [], []
    for n in range(len(DILATED_BRANCHES)):
        sc = s_c + bc_ref[n]
        sn = s_n + bn_ref[n]
        m = jnp.maximum(jnp.max(sc, axis=-1, keepdims=True), jnp.max(sn, axis=-1, keepdims=True))
        ec = jnp.exp(sc - m)
        en = jnp.exp(sn - m)
        den = jnp.sum(ec, axis=-1, keepdims=True) + jnp.sum(en, axis=-1, keepdims=True)
        es_c.append(ec)
        es_n.append(en)
        dens.append(den)
        lses.append(m + jnp.log(den))
    mx = functools.reduce(jnp.maximum, lses)
    ws = [jnp.exp(l - mx) for l in lses]
    tot = functools.reduce(lambda a, b: a + b, ws)
    p_c = None
    p_n = None
    for w, den, ec, en in zip(ws, dens, es_c, es_n):
        coef = w / (tot * den)
        p_c = coef * ec if p_c is None else p_c + coef * ec
        p_n = coef * en if p_n is None else p_n + coef * en
    nt = (((1,), (1,)), ((), ()))
    o = (lax.dot_general(p_c.astype(BF16), vt.astype(BF16), nt, preferred_element_type=F32)
         + lax.dot_general(p_n.astype(BF16), vnt.astype(BF16), nt, preferred_element_type=F32))
    o = o * hm_ref[...]
    o_ref[...] = jnp.sum(o.reshape(SUBLANES, N_HEADS_A, WIDTH_A), axis=1)


def _samp_bias_tables(rel_bias, w_buf, ds):
    tabs_c, tabs_n = [], []
    s = np.arange(SUBLANES)[:, None]
    live_s = s < ds
    first_new = SAMP_NEW_LANES - ds
    for window, dil in DILATED_BRANCHES:
        n_keys = window // dil
        for keys, live_k, tabs in ((np.arange(w_buf)[None, :], True, tabs_c),
                                   (w_buf - first_new + np.arange(SAMP_NEW_LANES)[None, :],
                                    np.arange(SAMP_NEW_LANES)[None, :] >= first_new, tabs_n)):
            n = keys.shape[1]
            dist = w_buf + s - keys
            valid = (dist >= 0) & (dist % dil == 0) & (dist // dil <= n_keys) & live_k
            bias = _bias_by_bucket(rel_bias, _t5_bucket(np.clip(dist, 0, None))).transpose(1, 0, 2)
            tab = jnp.where(jnp.asarray(valid)[:, None, :], bias, NEG_INF)
            pad = jnp.where(jnp.asarray(np.broadcast_to(live_k, dist.shape))[:, None, :], 0.0, NEG_INF)
            tab = jnp.where(jnp.asarray(live_s)[:, :, None], tab, pad)
            tabs.append(tab.reshape(SAMP_Q_ROWS, n))
    return jnp.stack(tabs_c), jnp.stack(tabs_n)


def _samp_attn(qa, cache_kt, cache_vt, knt, vnt, rel_bias, ds):
    DB, DS, _ = qa.shape
    W = cache_kt.shape[2]
    head_of_lane = np.arange(WIDTH_A) // HEAD_DIM_A
    hmask = (np.arange(SAMP_Q_ROWS)[:, None] % N_HEADS_A == head_of_lane[None, :])
    q8 = jnp.pad(qa, ((0, 0), (0, SUBLANES - DS), (0, 0)))
    q_rows = jnp.where(jnp.asarray(hmask)[None], jnp.repeat(q8, N_HEADS_A, axis=1), jnp.zeros((), BF16))
    bias_c, bias_n = _samp_bias_tables(rel_bias, W, DS)

    def per_b(rows, w):
        return pl.BlockSpec((None, rows, w), lambda b: (b, 0, 0))

    def const(a):
        return pl.BlockSpec(a.shape, lambda b: (0,) * a.ndim)

    hm = jnp.asarray(hmask, F32)
    return pl.pallas_call(
        functools.partial(_samp_attn_body, ds=ds),
        grid=(DB,),
        in_specs=[per_b(SAMP_Q_ROWS, WIDTH_A), per_b(WIDTH_A, W), per_b(WIDTH_A, W),
                  per_b(WIDTH_A, SAMP_NEW_LANES), per_b(WIDTH_A, SAMP_NEW_LANES),
                  const(bias_c), const(bias_n), const(hm)],
        out_specs=(per_b(SUBLANES, WIDTH_A), per_b(WIDTH_A, W), per_b(WIDTH_A, W)),
        out_shape=(jax.ShapeDtypeStruct((DB, SUBLANES, WIDTH_A), F32),
                   jax.ShapeDtypeStruct((DB, WIDTH_A, W), F32), jax.ShapeDtypeStruct((DB, WIDTH_A, W), F32)),
        compiler_params=_params(1),
        name="samp_attn",
    )(q_rows, cache_kt, cache_vt, knt, vnt, bias_c, bias_n, hm)


ROUTE_ROWS = 16


def _route_body(cat_ref, x_ref, wout_ref, g_ref, b_ref, rwh_ref, rwl_ref, rb_ref, triu_ref, base_ref, *refs,
                alpha, n_exp):
    h_ref, hp_ref, route_ref, route_t_ref, cnt_ref = refs[1:]

    @pl.when(pl.program_id(0) == 0)
    def _():
        cnt_ref[...] = base_ref[...]

    mix = jnp.dot(cat_ref[...], wout_ref[...], preferred_element_type=F32)
    y = alpha * x_ref[...] + mix
    mu = jnp.mean(y, axis=-1, keepdims=True)
    var = jnp.mean(jnp.square(y - mu), axis=-1, keepdims=True)
    h = (y - mu) * lax.rsqrt(var + LN_EPS) * g_ref[...] + b_ref[...]
    h_ref[...] = h
    hb = h.astype(BF16)
    hp_ref[...] = _pack_bf16_pairs(h)
    hl = (h - hb.astype(F32)).astype(BF16)
    nt = (((1,), (1,)), ((), ()))
    logits = (lax.dot_general(rwh_ref[...], hb, nt, preferred_element_type=F32)
              + lax.dot_general(rwl_ref[...], hb, nt, preferred_element_type=F32)
              + lax.dot_general(rwh_ref[...], hl, nt, preferred_element_type=F32))[:n_exp] + rb_ref[...]

    tm = logits.shape[1]
    sub = lax.broadcasted_iota(jnp.int32, (n_exp, tm), 0)
    work = logits
    vals, idxs = [], []
    for _ in range(TOP_K):
        m = jnp.max(work, axis=0, keepdims=True)
        idx = jnp.min(jnp.where(work == m, sub, n_exp), axis=0, keepdims=True)
        vals.append(m)
        idxs.append(idx)
        work = jnp.where(sub == idx, -jnp.inf, work)
    es = [jnp.exp(v - vals[0]) for v in vals]
    tot = functools.reduce(lambda a, b: a + b, es)
    onehot = jnp.zeros((n_exp, tm), F32)
    for idx in idxs:
        onehot = onehot + (sub == idx).astype(F32)
    before = jnp.dot(onehot.astype(BF16), triu_ref[...], preferred_element_type=F32) + cnt_ref[:, 0:1]
    rows = ([idx.astype(F32) for idx in idxs] + [e / tot for e in es]
            + [jnp.sum(jnp.where(sub == idx, before, 0.0), axis=0, keepdims=True) for idx in idxs])
    rows.append(jnp.zeros((ROUTE_ROWS - len(rows), tm), F32))
    record = jnp.concatenate(rows, axis=0)
    route_t_ref[...] = record
    route_ref[...] = jnp.concatenate([record, jnp.zeros((LANES - ROUTE_ROWS, tm), F32)], axis=0).T
    cnt_ref[...] = cnt_ref[...] + jnp.sum(onehot, axis=1, keepdims=True)


def _out_route(cat, x2d, w_out_bf, ln_g, ln_b, rw_hi_t, rw_lo_t, rb_col, base, hp_buf, tile0, tm, alpha):
    T = x2d.shape[0]
    n_exp = rb_col.shape[0]
    triu = jnp.asarray(np.triu(np.ones((tm, tm), np.float32), 1), BF16)

    def tok(i):
        return (i, 0)

    def const(a):
        return pl.BlockSpec(a.shape, lambda i: (0,) * a.ndim)

    ins = (cat, x2d, w_out_bf, ln_g, ln_b, rw_hi_t, rw_lo_t, rb_col, triu, base, hp_buf)
    in_specs = ([pl.BlockSpec((tm, cat.shape[1]), tok), pl.BlockSpec((tm, D_MODEL), tok)]
                + [const(a) for a in ins[2:-1]] + [pl.BlockSpec(memory_space=pl.ANY)])
    return pl.pallas_call(
        functools.partial(_route_body, alpha=alpha, n_exp=n_exp),
        grid=(T // tm,),
        in_specs=in_specs,
        out_specs=(pl.BlockSpec((tm, D_MODEL), tok), pl.BlockSpec((tm, D_MODEL // 2), lambda i: (tile0 + i, 0)),
                   pl.BlockSpec((tm, LANES), tok), pl.BlockSpec((ROUTE_ROWS, tm), lambda i: (0, i)),
                   pl.BlockSpec(base.shape, lambda i: (0, 0))),
        out_shape=(jax.ShapeDtypeStruct((T, D_MODEL), F32), jax.ShapeDtypeStruct(hp_buf.shape, jnp.int32),
                   jax.ShapeDtypeStruct((T, LANES), F32), jax.ShapeDtypeStruct((ROUTE_ROWS, T), F32),
                   jax.ShapeDtypeStruct(base.shape, F32)),
        input_output_aliases={len(ins) - 1: 1},
        compiler_params=_params(1),
        name="out_route",
    )(*ins)


MOE_CAST_ROWS = 128


def _moe_body(be_ref, first_ref, next_ref, nused_ref, x_ref, bgu_ref, bdn_ref, wgu_hbm, wdn_hbm, y_ref,
              gu_stage, dn_stage, wgu_bf, wdn_bf, sem):
    b = pl.program_id(0)
    d_exp = wdn_bf.shape[0]

    def fetch(e):
        return (pltpu.make_async_copy(wgu_hbm.at[e], gu_stage, sem.at[0]),
                pltpu.make_async_copy(wdn_hbm.at[e], dn_stage, sem.at[1]))

    @pl.when(b == 0)
    def _():
        for c in fetch(be_ref[0]):
            c.start()

    @pl.when(first_ref[b] == 1)
    def _():
        for c in fetch(be_ref[b]):
            c.wait()

        def cast_gu(i, c):
            r = pl.ds(pl.multiple_of(i * MOE_CAST_ROWS, MOE_CAST_ROWS), MOE_CAST_ROWS)
            wgu_bf[r, :] = gu_stage[r, :].astype(BF16)
            return c

        def cast_dn(i, c):
            r = pl.ds(pl.multiple_of(i * MOE_CAST_ROWS, MOE_CAST_ROWS), MOE_CAST_ROWS)
            wdn_bf[r, :] = dn_stage[r, :].astype(BF16)
            return c

        lax.fori_loop(0, gu_stage.shape[0] // MOE_CAST_ROWS, cast_gu, 0)
        lax.fori_loop(0, d_exp // MOE_CAST_ROWS, cast_dn, 0)

        @pl.when(next_ref[b] >= 0)
        def _():
            for c in fetch(next_ref[b]):
                c.start()

    @pl.when(b < nused_ref[0])
    def _():
        x_lo, x_hi = _unpack_bf16_pairs(x_ref[...])
        x_lo = x_lo.astype(BF16)
        x_hi = x_hi.astype(BF16)
        dh = x_lo.shape[1]

        def xw(cols):
            return (jnp.dot(x_lo, wgu_bf[:dh, cols], preferred_element_type=F32)
                    + jnp.dot(x_hi, wgu_bf[dh:, cols], preferred_element_type=F32) + bgu_ref[:, cols])

        half = d_exp // 2
        y = None
        for c in range(2):
            lo = c * half
            gate = jnp.minimum(xw(slice(lo, lo + half)), SWIGLU_LIMIT)
            up = jnp.clip(xw(slice(d_exp + lo, d_exp + lo + half)), -SWIGLU_LIMIT, SWIGLU_LIMIT)
            act = (up + 1.0) * gate * (1.0 / (1.0 + jnp.exp(-SWIGLU_ALPHA * gate)))
            part = jnp.dot(act.astype(BF16), wdn_bf[lo:lo + half, :], preferred_element_type=F32)
            y = part if y is None else y + part
        y_ref[...] = _pack_bf16_pairs(y + bdn_ref[...])

    @pl.when(b >= nused_ref[0])
    def _():
        y_ref[...] = jnp.zeros_like(y_ref)


def _moe_ffn(xs, rows, block_e, n_used, has_rows, w_gu, b_gu, w_dn, b_dn):
    E, D, two_de = w_gu.shape
    d_exp = two_de // 2
    nb = rows // MOE_BLOCK
    idx = jnp.arange(nb, dtype=jnp.int32)
    first = ((idx < n_used[0]) & ((idx == 0) | (block_e != jnp.roll(block_e, 1)))).astype(jnp.int32)
    ids = jnp.arange(E, dtype=jnp.int32)
    later = jnp.where((ids[None, :] > ids[:, None]) & has_rows[None, :], ids[None, :], E).min(axis=1)
    next_of = jnp.where(later == E, -1, later).astype(jnp.int32)
    next_e = jnp.sum(jnp.where(block_e[:, None] == ids[None, :], next_of[None, :], 0), axis=1).astype(jnp.int32)
    grid_spec = pltpu.PrefetchScalarGridSpec(
        num_scalar_prefetch=4,
        grid=(nb,),
        in_specs=[
            pl.BlockSpec((MOE_BLOCK, D // 2), lambda b, be, fi, nx, nu: (b, 0)),
            pl.BlockSpec((None, 1, two_de), lambda b, be, fi, nx, nu: (be[b], 0, 0)),
            pl.BlockSpec((None, 1, D), lambda b, be, fi, nx, nu: (be[b], 0, 0)),
            pl.BlockSpec(memory_space=pl.ANY),
            pl.BlockSpec(memory_space=pl.ANY),
        ],
        out_specs=pl.BlockSpec((MOE_BLOCK, D // 2), lambda b, be, fi, nx, nu: (b, 0)),
        scratch_shapes=[pltpu.VMEM((D, two_de), F32), pltpu.VMEM((d_exp, D), F32),
                        pltpu.VMEM((D, two_de), BF16), pltpu.VMEM((d_exp, D), BF16),
                        pltpu.SemaphoreType.DMA((2,))],
    )
    return pl.pallas_call(
        _moe_body,
        grid_spec=grid_spec,
        out_shape=jax.ShapeDtypeStruct((rows, D // 2), jnp.int32),
        compiler_params=_params(1),
        name="moe_ffn",
    )(block_e, first, next_e, n_used, xs, b_gu.reshape(E, 1, two_de), b_dn.reshape(E, 1, D), w_gu, w_dn)


def _ple_body(h_ref, ys_ref, route_ref, p_ref, g_ref, b_ref, wpg_ref, bpg_ref, wpp_ref, o_ref, *, alpha):
    route = route_ref[...]
    f_lo = None
    f_hi = None
    for k in range(TOP_K):
        lo, hi = _unpack_bf16_pairs(ys_ref[k])
        g = route[:, TOP_K + k:TOP_K + k + 1]
        f_lo = g * lo if f_lo is None else f_lo + g * lo
        f_hi = g * hi if f_hi is None else f_hi + g * hi
    y = alpha * h_ref[...] + jnp.concatenate([f_lo, f_hi], axis=1)
    mu = jnp.mean(y, axis=-1, keepdims=True)
    var = jnp.mean(jnp.square(y - mu), axis=-1, keepdims=True)
    h2 = (y - mu) * lax.rsqrt(var + LN_EPS) * g_ref[...] + b_ref[...]
    z = jnp.dot(h2.astype(BF16), wpg_ref[...], preferred_element_type=F32) + bpg_ref[...]
    gate = 1.0 / (1.0 + jnp.exp(-z))
    proj = jnp.dot(p_ref[...].astype(BF16), wpp_ref[...], preferred_element_type=F32)
    o_ref[...] = h2 + gate * proj


def _ffn_ple(h, y_slots, route, p, ln_g, ln_b, w_pg_bf, b_pg, w_pp_bf, tm, tile0, alpha):
    T = h.shape[0]

    def tok(i):
        return (i, 0)

    def const(a):
        return pl.BlockSpec(a.shape, lambda i: (0,) * a.ndim)

    consts = (ln_g, ln_b, w_pg_bf, b_pg, w_pp_bf)
    return pl.pallas_call(
        functools.partial(_ple_body, alpha=alpha),
        grid=(T // tm,),
        in_specs=[pl.BlockSpec((tm, D_MODEL), tok),
                  pl.BlockSpec((TOP_K, tm, D_MODEL // 2), lambda i: (0, tile0 + i, 0)),
                  pl.BlockSpec((tm, LANES), tok),
                  pl.BlockSpec((tm, p.shape[1]), tok)] + [const(a) for a in consts],
        out_specs=pl.BlockSpec((tm, D_MODEL), tok),
        out_shape=jax.ShapeDtypeStruct((T, D_MODEL), F32),
        compiler_params=_params(1),
        name="ffn_ple",
    )(h, y_slots, route, p, *consts)


def _row(v):
    return v.reshape(1, -1).astype(F32)


def kernel(x_prompt, x_sample, cache_win_k, cache_win_v, state_ret, p_prompt, p_sample, rel_bias, w_in, w_out,
           ln1_g, ln1_b, router_w, router_b, w_gate_up, b_gate_up, w_down, b_down, ln2_g, ln2_b,
           w_ple_gate, b_ple_gate, w_ple_proj):
    B, S, D = x_prompt.shape
    DB, DS, _ = x_sample.shape
    depth = w_in.shape[0]
    w_buf = cache_win_k.shape[2]
    n_exp = router_w.shape[-1]
    alpha = (2.0 * depth) ** 0.25
    assert depth == 1 and D == D_MODEL
    assert S % (BLK * MAX_DIL) == 0 and S >= WINDOW_MAX and w_buf == WINDOW_MAX and DS <= SUBLANES
    tm_p = 512
    Tp, Ts = B * S, DB * DS
    assert Tp % tm_p == 0 and Ts % SUBLANES == 0

    i = 0
    w_in_bf = w_in[i].astype(BF16)
    w_out_bf = w_out[i].astype(BF16)
    w_pg_bf = w_ple_gate[i].astype(BF16)
    w_pp_bf = w_ple_proj[i].astype(BF16)
    rw_t = jnp.pad(router_w[i].T, ((0, LANES - n_exp), (0, 0)))
    rw_hi = rw_t.astype(BF16)
    rw_lo = (rw_t - rw_hi.astype(F32)).astype(BF16)
    rb = router_b[i].astype(F32).reshape(n_exp, 1)
    T = Tp + Ts
    t_align = SC_WORKERS * SC_ALIGN
    T_pad = -(-T // t_align) * t_align
    assert Tp % Ts == 0

    cos_p, sin_p = _rotary_tables(jnp.arange(S, dtype=jnp.int32))
    dils = tuple(d for _, d in DILATED_BRANCHES)
    extra = tuple(d for d in dils if d > 1)
    outs = _in_proj(x_prompt.reshape(Tp, D), w_in_bf, cos_p, sin_p, tm_p, S // tm_p, WINDOW_MAX // tm_p, extra)
    qa, ka, va, qb, kb, vb, gb, kf, vf = outs[:9]
    qkv = {1: (qa, ka, va)}
    for t, d in enumerate(extra):
        qkv[d] = tuple(outs[9 + j * len(extra) + t] for j in range(3))

    def seq(t):
        return t.reshape(B, S, t.shape[-1])

    outs_a, lses = [], []
    for window, dil in DILATED_BRANCHES:
        q_d, k_d, v_d = (t.reshape(B, S // dil, dil * WIDTH_A) for t in qkv[dil])
        o_n, l_n = _dilated_branch(q_d, k_d, v_d, _attn_bias_tables(rel_bias, window, dil), dil)
        outs_a.append(o_n)
        lses.append(l_n)
    st_zero = jnp.zeros((B, N_HEADS_B, KEY_DIM_B, VAL_DIM_B), F32)
    cat_p, rst_p = _ret_mix(seq(qb), seq(kb), seq(vb), seq(gb), (outs_a, lses, dils), st_zero, RET_CHUNK,
                            RET_CHUNKS_PER_STEP)
    base0 = jnp.zeros((n_exp, LANES), F32)
    hp0 = jnp.zeros((T_pad, D // 2), jnp.int32)
    h_p, hp_buf, route_p, rt_p, cnt_p = _out_route(cat_p.reshape(Tp, -1), x_prompt.reshape(Tp, D), w_out_bf,
                                                   _row(ln1_g[i]), _row(ln1_b[i]), rw_hi, rw_lo, rb, base0, hp0, 0,
                                                   tm_p, alpha)

    pos_s = jnp.tile(PAST_LEN + jnp.arange(DS, dtype=jnp.int32), DB)
    cos_s, sin_s = _rotary_tables(pos_s)
    qa_s, _, _, qb_s, kb_s, vb_s, gb_s, kf_s, vf_s = _in_proj(x_sample.reshape(Ts, D), w_in_bf, cos_s, sin_s,
                                                             Ts, 1, 1)

    def positions_minor(t):
        return jnp.transpose(t, (0, 2, 3, 1)).reshape(DB, WIDTH_A, t.shape[1])

    def positions_major(t):
        return jnp.transpose(t.reshape(t.shape[0], N_HEADS_A, HEAD_DIM_A, t.shape[2]), (0, 3, 1, 2))[None]

    def new_columns(t):
        t = jnp.transpose(t.reshape(WIDTH_A, DB, DS), (1, 0, 2))
        return jnp.pad(t, ((0, 0), (0, 0), (SAMP_NEW_LANES - DS, 0)))

    oa_s, kt_out, vt_out = _samp_attn(qa_s.reshape(DB, DS, WIDTH_A), positions_minor(cache_win_k[i]),
                                      positions_minor(cache_win_v[i]), new_columns(kf_s), new_columns(vf_s),
                                      rel_bias, DS)

    def pad_rows(t, rows):
        t = t.reshape(DB, -1, t.shape[-1])
        return jnp.pad(t, ((0, 0), (0, rows - t.shape[1]), (0, 0)))

    cat_s, rst_s = _ret_mix(pad_rows(qb_s, RET_CHUNK), pad_rows(kb_s, RET_CHUNK), pad_rows(vb_s, RET_CHUNK),
                            pad_rows(gb_s, RET_CHUNK), pad_rows(oa_s, RET_CHUNK), state_ret[i].astype(F32), DS)
    cat_s = cat_s[:, :DS].reshape(Ts, -1)
    h_s, hp_all, route_s, rt_s, cnt = _out_route(cat_s, x_sample.reshape(Ts, D), w_out_bf, _row(ln1_g[i]),
                                                 _row(ln1_b[i]), rw_hi, rw_lo, rb, cnt_p, hp_buf, Tp // Ts, Ts, alpha)

    n_fill = T_pad - T
    record = jnp.concatenate([rt_p, rt_s, jnp.zeros((ROUTE_ROWS, n_fill), F32)], axis=1)
    top_idx = record[:TOP_K].astype(jnp.int32)
    rank = record[2 * TOP_K:3 * TOP_K].astype(jnp.int32)
    counts = cnt[:, 0].astype(jnp.int32)
    padded = (counts + MOE_BLOCK - 1) // MOE_BLOCK * MOE_BLOCK
    pad_end = jnp.cumsum(padded)
    pad_start = pad_end - padded
    ids = jnp.arange(n_exp, dtype=jnp.int32)[:, None, None]
    dest = jnp.sum(jnp.where(top_idx[None] == ids, pad_start[:, None, None], 0), axis=0) + rank
    n_blocks = -(-T * TOP_K // MOE_BLOCK) + n_exp
    rows = n_blocks * MOE_BLOCK
    block_start = jnp.arange(n_blocks, dtype=jnp.int32) * MOE_BLOCK
    block_e = jnp.minimum(jnp.sum(pad_end[None, :] <= block_start[:, None], axis=1), n_exp - 1).astype(jnp.int32)
    n_used = (pad_end[-1:] // MOE_BLOCK).astype(jnp.int32)
    tok_id = jnp.arange(T_pad, dtype=jnp.int32)[None, :]
    spare = rows + (tok_id - T) * TOP_K + jnp.arange(TOP_K, dtype=jnp.int32)[:, None]
    dest_sc = jnp.where(tok_id >= T, spare, dest).reshape(-1)
    dest_ga = jnp.where(tok_id >= T, 0, dest).reshape(-1)
    xs = _sc_scatter_rows(hp_all, dest_sc, rows + n_fill * TOP_K)
    ys = _moe_ffn(xs, rows, block_e, n_used, padded > 0, w_gate_up[i], b_gate_up[i], w_down[i], b_down[i])
    y_slots = _sc_gather_rows(ys, dest_ga).reshape(TOP_K, T_pad, D // 2)

    ple_args = (_row(ln2_g[i]), _row(ln2_b[i]), w_pg_bf, _row(b_ple_gate[i]), w_pp_bf)
    y_p = _ffn_ple(h_p, y_slots, route_p, p_prompt[i].reshape(Tp, D_PLE), *ple_args, tm_p, 0, alpha)
    y_s = _ffn_ple(h_s, y_slots, route_s, p_sample[i].reshape(Ts, D_PLE), *ple_args, Ts, Tp // Ts, alpha)

    return (y_p.reshape(B, S, D), y_s.reshape(DB, DS, D), positions_major(kf), positions_major(vf),
            rst_p[None], positions_major(kt_out), positions_major(vt_out), rst_s[None])
```

```python
import functools

import numpy as np
import jax
import jax.numpy as jnp
from jax import lax
from jax.experimental import pallas as pl
from jax.experimental.pallas import tpu as pltpu
from jax.experimental.pallas import tpu_sc as plsc

F32 = jnp.float32
BF16 = jnp.bfloat16

D_MODEL = 1024
D_PLE = 256
N_HEADS_A = 8
HEAD_DIM_A = 64
WIDTH_A = N_HEADS_A * HEAD_DIM_A
DILATED_BRANCHES = ((128, 1), (512, 4), (2048, 16))
BLK = 128
WINDOW_MAX = 2048
MAX_DIL = 16
NUM_BUCKETS = 32
MAX_DISTANCE = 2048
N_HEADS_B = 4
KEY_DIM_B = 64
VAL_DIM_B = 128
QK_WIDTH_B = N_HEADS_B * KEY_DIM_B
WIDTH_B = N_HEADS_B * VAL_DIM_B
RET_CHUNK = 128
GN_EPS = 1e-6
TOP_K = 4
SWIGLU_LIMIT = 7.0
SWIGLU_ALPHA = 1.702
LN_EPS = 1e-5
NEG_INF = -1e30
PAST_LEN = 16384
MOE_BLOCK = 512
LANES = 128
SUBLANES = 8
VMEM_LIMIT = 52 * 1024 * 1024


def _params(n_axes, vmem=VMEM_LIMIT):
    return pltpu.CompilerParams(dimension_semantics=("arbitrary",) * n_axes, vmem_limit_bytes=vmem)


def _t5_bucket(dist):
    dist = np.asarray(dist, dtype=np.int32)
    max_exact = NUM_BUCKETS // 2
    d = np.maximum(dist, 1).astype(np.float32)
    large = max_exact + (np.log(d / max_exact) / np.log(MAX_DISTANCE / max_exact)
                         * (NUM_BUCKETS - max_exact)).astype(np.int32)
    large = np.minimum(large, NUM_BUCKETS - 1)
    return np.where(dist < max_exact, dist, large).astype(np.int32)


def _bias_by_bucket(rel_bias, buckets):
    b = jnp.asarray(buckets, jnp.int32)
    ids = jnp.arange(NUM_BUCKETS, dtype=jnp.int32).reshape((NUM_BUCKETS, 1) + (1,) * b.ndim)
    vals = rel_bias.astype(F32).reshape((NUM_BUCKETS, rel_bias.shape[1]) + (1,) * b.ndim)
    return jnp.sum(jnp.where(b[None, None] == ids, vals, 0.0), axis=0)


def _pack_bf16_pairs(v):
    w = v.shape[1] // 2
    lo = lax.bitcast_convert_type(v[:, :w].astype(BF16).astype(F32), jnp.uint32) >> 16
    hi = lax.bitcast_convert_type(v[:, w:].astype(BF16).astype(F32), jnp.uint32) & jnp.uint32(0xFFFF0000)
    return lax.bitcast_convert_type(lo | hi, jnp.int32)


def _unpack_bf16_pairs(p):
    u = lax.bitcast_convert_type(p, jnp.uint32)
    lo = lax.bitcast_convert_type(u << 16, F32)
    hi = lax.bitcast_convert_type(u & jnp.uint32(0xFFFF0000), F32)
    return lo, hi


SC_CORES = 2
SC_SUBCORES = 16
SC_WORKERS = SC_CORES * SC_SUBCORES
SC_ALIGN = 8
SC_CHUNK_ROWS = 80


def _sc_mesh():
    return plsc.VectorSubcoreMesh(core_axis_name="c", subcore_axis_name="s")


def _sc_chunk(per_worker):
    c = max(d for d in range(SC_ALIGN, SC_CHUNK_ROWS + 1, SC_ALIGN) if per_worker % d == 0)
    return c


def _sc_scatter_rows(src, dest_flat, n_out):
    T, W = src.shape
    K = dest_flat.shape[0] // T
    per_w = T // SC_WORKERS
    assert per_w * SC_WORKERS == T and per_w % SC_ALIGN == 0
    chunk = _sc_chunk(per_w)
    n_chunks = per_w // chunk

    @functools.partial(
        pl.kernel, mesh=_sc_mesh(), out_type=jax.ShapeDtypeStruct((n_out, W), src.dtype),
        scratch_types=[pltpu.VMEM((chunk, W), src.dtype)] * 2 + [pltpu.VMEM((chunk,), jnp.int32)] * (2 * K)
        + [pltpu.SemaphoreType.DMA] * 2,
        name="sc_scatter_rows")
    def k(src_hbm, dest_hbm, out_hbm, rows_a, rows_b, *rest):
        bufs = ((rows_a, rest[:K], rest[2 * K]), (rows_b, rest[K:2 * K], rest[2 * K + 1]))
        base = (lax.axis_index("s") * SC_CORES + lax.axis_index("c")) * per_w

        def scatters(buf):
            rows_v, idx_vs, sem = buf
            return [pltpu.make_async_copy(rows_v, out_hbm.at[idx_vs[kk]], sem) for kk in range(K)]

        def step(j, buf):
            rows_v, idx_vs, _ = buf

            @pl.when(j >= 2)
            def _():
                for c in scatters(buf):
                    c.wait()

            off = pl.multiple_of(base + j * chunk, SC_ALIGN)
            pltpu.sync_copy(src_hbm.at[pl.ds(off, chunk)], rows_v)
            for kk in range(K):
                pltpu.sync_copy(dest_hbm.at[pl.ds(kk * T + off, chunk)], idx_vs[kk])
            for c in scatters(buf):
                c.start()

        @pl.loop(0, n_chunks)
        def _(j):
            for parity in range(2):
                @pl.when(j % 2 == parity)
                def _():
                    step(j, bufs[parity])

        for j in range(max(n_chunks - 2, 0), n_chunks):
            for c in scatters(bufs[j % 2]):
                c.wait()

    return k(src, dest_flat)


def _sc_gather_rows(table, idx):
    B = idx.shape[0]
    W = table.shape[1]
    per_w = B // SC_WORKERS
    assert per_w * SC_WORKERS == B and per_w % SC_ALIGN == 0
    chunk = _sc_chunk(per_w)
    n_chunks = per_w // chunk

    @functools.partial(
        pl.kernel, mesh=_sc_mesh(), out_type=jax.ShapeDtypeStruct((B, W), table.dtype),
        scratch_types=[pltpu.VMEM((chunk,), jnp.int32)] * 2 + [pltpu.VMEM((chunk, W), table.dtype)] * 2
        + [pltpu.SemaphoreType.DMA] * 2,
        name="sc_gather_rows")
    def k(table_hbm, idx_hbm, out_hbm, idx_a, idx_b, rows_a, rows_b, sem_a, sem_b):
        bufs = ((idx_a, rows_a, sem_a), (idx_b, rows_b, sem_b))
        base = (lax.axis_index("s") * SC_CORES + lax.axis_index("c")) * per_w

        def gather(buf):
            idx_v, rows_v, sem = buf
            return pltpu.make_async_copy(table_hbm.at[idx_v], rows_v, sem)

        def finish(j, buf):
            gather(buf).wait()
            off = pl.multiple_of(base + j * chunk, SC_ALIGN)
            pltpu.sync_copy(buf[1], out_hbm.at[pl.ds(off, chunk)])

        def step(j, buf, other):
            off = pl.multiple_of(base + j * chunk, SC_ALIGN)
            pltpu.sync_copy(idx_hbm.at[pl.ds(off, chunk)], buf[0])
            gather(buf).start()

            @pl.when(j >= 1)
            def _():
                finish(j - 1, other)

        @pl.loop(0, n_chunks)
        def _(j):
            for parity in range(2):
                @pl.when(j % 2 == parity)
                def _():
                    step(j, bufs[parity], bufs[1 - parity])

        finish(n_chunks - 1, bufs[(n_chunks - 1) % 2])

    return k(table, idx)


def _in_proj_body(x_ref, w_ref, cos_ref, sin_ref, *refs, seq_tiles, first_win, dilations):
    qa_ref, ka_ref, va_ref, qb_ref, kb_ref, vb_ref, gb_ref, kt_ref, vt_ref = refs[:9]
    n_d = len(dilations)
    dil_refs = [refs[9 + t * n_d:9 + (t + 1) * n_d] for t in range(3)]
    zs_ref = refs[9 + 3 * n_d] if n_d else None
    x = x_ref[...].astype(BF16)
    in_window = pl.program_id(0) % seq_tiles >= first_win

    def emit(z, token_ref, class_refs):
        token_ref[...] = z.astype(BF16)
        if not class_refs:
            return
        for c in range(WIDTH_A // LANES):
            zs_ref[c] = z[:, c * LANES:(c + 1) * LANES]
        for d, ref in zip(dilations, class_refs):
            n = z.shape[0] // d
            for r in range(d):
                for c in range(WIDTH_A // LANES):
                    lo = r * WIDTH_A + c * LANES
                    ref[:, lo:lo + LANES] = zs_ref[c, pl.ds(r, n, stride=d), :].astype(BF16)

    def proj(lo, hi):
        return jnp.dot(x, w_ref[:, lo:hi], preferred_element_type=F32)

    o = 0
    qa = proj(o, o + WIDTH_A) * (HEAD_DIM_A ** -0.5)
    o += WIDTH_A
    ka = proj(o, o + WIDTH_A)
    o += WIDTH_A
    emit(qa, qa_ref, dil_refs[0])
    va = proj(o, o + WIDTH_A)
    o += WIDTH_A
    emit(ka, ka_ref, dil_refs[1])

    @pl.when(in_window)
    def _():
        kt_ref[...] = ka.T
        vt_ref[...] = va.T

    cos = cos_ref[...]
    sin = sin_ref[...]
    lane = lax.broadcasted_iota(jnp.int32, cos.shape, 1)
    first_half = (lane % KEY_DIM_B) < (KEY_DIM_B // 2)

    def rot(z):
        sw = jnp.where(first_half, pltpu.roll(z, QK_WIDTH_B - KEY_DIM_B // 2, 1), pltpu.roll(z, KEY_DIM_B // 2, 1))
        return z * cos + sw * sin

    qb_ref[...] = rot(proj(o, o + QK_WIDTH_B)).astype(BF16)
    o += QK_WIDTH_B
    emit(va, va_ref, dil_refs[2])
    kb_ref[...] = (rot(proj(o, o + QK_WIDTH_B)) * (KEY_DIM_B ** -0.5)).astype(BF16)
    o += QK_WIDTH_B
    vb_ref[...] = proj(o, o + WIDTH_B).astype(BF16)
    o += WIDTH_B
    gb_ref[...] = proj(o, o + WIDTH_B).astype(BF16)


def _in_proj(x2d, w_bf, cos_t, sin_t, tm, seq_tiles, win_tiles, dilations=()):
    T = x2d.shape[0]
    nt = T // tm
    n_seq = nt // seq_tiles
    j0 = seq_tiles - win_tiles

    def tok(i):
        return (i, 0)

    def tab(i):
        return (i % seq_tiles, 0)

    def win(i):
        return (i // seq_tiles, 0, jnp.maximum(i % seq_tiles - j0, 0))

    def tspec(w):
        return pl.BlockSpec((tm, w), tok)

    out_shape = (
        jax.ShapeDtypeStruct((T, WIDTH_A), BF16), jax.ShapeDtypeStruct((T, WIDTH_A), BF16),
        jax.ShapeDtypeStruct((T, WIDTH_A), BF16),
        jax.ShapeDtypeStruct((T, QK_WIDTH_B), BF16), jax.ShapeDtypeStruct((T, QK_WIDTH_B), BF16),
        jax.ShapeDtypeStruct((T, WIDTH_B), BF16), jax.ShapeDtypeStruct((T, WIDTH_B), BF16),
        jax.ShapeDtypeStruct((n_seq, WIDTH_A, win_tiles * tm), F32),
        jax.ShapeDtypeStruct((n_seq, WIDTH_A, win_tiles * tm), F32),
    ) + tuple(jax.ShapeDtypeStruct((T // d, d * WIDTH_A), BF16) for _ in range(3) for d in dilations)
    class_specs = tuple(pl.BlockSpec((tm // d, d * WIDTH_A), tok) for _ in range(3) for d in dilations)
    return pl.pallas_call(
        functools.partial(_in_proj_body, seq_tiles=seq_tiles, first_win=j0, dilations=tuple(dilations)),
        grid=(nt,),
        in_specs=[tspec(D_MODEL), pl.BlockSpec(w_bf.shape, lambda i: (0, 0)),
                  pl.BlockSpec((tm, QK_WIDTH_B), tab), pl.BlockSpec((tm, QK_WIDTH_B), tab)],
        out_specs=(tspec(WIDTH_A), tspec(WIDTH_A), tspec(WIDTH_A), tspec(QK_WIDTH_B), tspec(QK_WIDTH_B),
                   tspec(WIDTH_B), tspec(WIDTH_B),
                   pl.BlockSpec((None, WIDTH_A, tm), win), pl.BlockSpec((None, WIDTH_A, tm), win)) + class_specs,
        out_shape=out_shape,
        scratch_shapes=[pltpu.VMEM((WIDTH_A // LANES, tm, LANES), F32)] if dilations else [],
        compiler_params=_params(1),
        name="in_proj",
    )(x2d, w_bf, cos_t, sin_t)


def _rotary_tables(pos):
    half = KEY_DIM_B // 2
    inv_freq = 1.0 / (10000.0 ** jnp.linspace(0.0, 1.0, half, dtype=F32))
    ang = pos.astype(F32)[:, None] * inv_freq[None, :]
    cos = jnp.cos(ang)
    sin = jnp.sin(ang)
    cos_h = jnp.concatenate([cos, cos], axis=-1)
    sin_h = jnp.concatenate([-sin, sin], axis=-1)
    return jnp.tile(cos_h, (1, N_HEADS_B)), jnp.tile(sin_h, (1, N_HEADS_B))


ATTN_BLOCKS_PER_STEP = 8
RET_CHUNKS_PER_STEP = 4
RET_STEP_ROWS = RET_CHUNK * RET_CHUNKS_PER_STEP


def _attn_body(q_ref, k_ref, v_ref, bias_ref, o_ref, lse_ref, kb_ref, vb_ref):
    n = pl.program_id(2)
    rows = q_ref.shape[0]

    @pl.when(n == 0)
    def _():
        kb_ref[:BLK, :] = jnp.zeros((BLK, WIDTH_A), BF16)
        vb_ref[:BLK, :] = jnp.zeros((BLK, WIDTH_A), BF16)

    kb_ref[BLK:, :] = k_ref[...]
    vb_ref[BLK:, :] = v_ref[...]
    lane = lax.broadcasted_iota(jnp.int32, (BLK, LANES), 1)
    low = lane < HEAD_DIM_A
    nt = (((1,), (1,)), ((), ()))

    def sub_block(j, carry):
        r0 = pl.multiple_of(j * BLK, BLK)
        table = jnp.where((n == 0) & (j == 0), 0, 1)
        lse_tile = jnp.zeros((BLK, LANES), F32)
        for p in range(N_HEADS_A // 2):
            cs = slice(p * LANES, (p + 1) * LANES)
            qp = q_ref[pl.ds(r0, BLK), cs]
            kp = kb_ref[pl.ds(r0, 2 * BLK), cs]
            vp = vb_ref[pl.ds(r0, 2 * BLK), cs]
            zero = jnp.zeros_like(qp)
            outs = []
            for half, qh in enumerate((jnp.where(low, qp, zero), jnp.where(low, zero, qp))):
                h = 2 * p + half
                s = lax.dot_general(qh, kp, nt, preferred_element_type=F32) + bias_ref[table, h]
                m = jnp.max(s, axis=-1, keepdims=True)
                e = jnp.exp(s - m)
                den = jnp.sum(e, axis=-1, keepdims=True)
                outs.append(jnp.dot(e.astype(BF16), vp, preferred_element_type=F32) / den)
                lse_tile = jnp.where(lane == h, m + jnp.log(den), lse_tile)
            o_ref[pl.ds(r0, BLK), cs] = jnp.where(low, outs[0], outs[1]).astype(o_ref.dtype)
        lse_ref[pl.ds(r0, BLK), :] = lse_tile
        return carry

    lax.fori_loop(0, rows // BLK, sub_block, 0, unroll=4)
    kb_ref[:BLK, :] = k_ref[rows - BLK:, :]
    vb_ref[:BLK, :] = v_ref[rows - BLK:, :]


def _attn_bias_tables(rel_bias, window, dil):
    n_keys = window // dil
    i = np.arange(BLK)[:, None]
    j = np.arange(2 * BLK)[None, :]
    rel = BLK + i - j
    in_band = (rel >= 0) & (rel <= n_keys)
    bias = _bias_by_bucket(rel_bias, _t5_bucket(np.clip(rel, 0, None) * dil))
    later = jnp.where(jnp.asarray(in_band)[None], bias, NEG_INF)
    first = jnp.where(jnp.asarray(in_band & (j >= BLK))[None], bias, NEG_INF)
    return jnp.stack([first, later])


def _dilated_branch(q, k, v, bias_tab, dil):
    B, L, _ = q.shape
    rows = min(ATTN_BLOCKS_PER_STEP, L // BLK) * BLK
    assert L % rows == 0
    o_dtype = BF16 if RET_STEP_ROWS // dil >= 16 else F32

    def cls(b, r, n):
        return (b, n, r)

    qkv_spec = pl.BlockSpec((None, rows, WIDTH_A), cls)
    o, lse = pl.pallas_call(
        _attn_body,
        grid=(B, dil, L // rows),
        in_specs=[qkv_spec, qkv_spec, qkv_spec,
                  pl.BlockSpec(bias_tab.shape, lambda b, r, n: (0, 0, 0, 0))],
        out_specs=(qkv_spec, pl.BlockSpec((None, rows, LANES), cls)),
        out_shape=(jax.ShapeDtypeStruct((B, L, dil * WIDTH_A), o_dtype),
                   jax.ShapeDtypeStruct((B, L, dil * LANES), F32)),
        scratch_shapes=[pltpu.VMEM((BLK + rows, WIDTH_A), BF16), pltpu.VMEM((BLK + rows, WIDTH_A), BF16)],
        compiler_params=_params(3),
        name=f"dil_attn_d{dil}",
    )(q, k, v, bias_tab)
    return o, lse


def _ret_body(*refs, dilations):
    n_branch = len(dilations)
    qb_ref, kb_ref, vb_ref, gb_ref = refs[:4]
    p = 4
    if n_branch:
        o_refs = refs[p:p + n_branch]
        l_refs = refs[p + n_branch:p + 2 * n_branch]
        exp_ref = refs[p + 2 * n_branch]
        p += 2 * n_branch + 1
    else:
        oa_ref = refs[p]
        p += 1
    st0_ref, dmat_ref, qdec_ref, kdec_ref, cdec_ref, cat_ref, sto_ref, st_ref = refs[p:p + 8]
    if n_branch:
        us_ref, ls_ref = refs[p + 8:]

    def lse_token_order(l_ref, d):
        if d == 1:
            return l_ref[...]
        n = l_ref.shape[0]
        for r in range(d):
            ls_ref[pl.ds(r, n, stride=d), :] = l_ref[:, r * LANES:(r + 1) * LANES]
        return ls_ref[...]

    def out_token_order(o_ref, d):
        if d == 1:
            return o_ref[...].astype(F32)
        n = o_ref.shape[0]
        for r in range(d):
            for c in range(WIDTH_A // LANES):
                lo = r * WIDTH_A + c * LANES
                us_ref[c, pl.ds(r, n, stride=d), :] = o_ref[:, lo:lo + LANES].astype(F32)
        return jnp.concatenate([us_ref[c] for c in range(WIDTH_A // LANES)], axis=1)

    @pl.when(pl.program_id(1) == 0)
    def _():
        st_ref[...] = st0_ref[...]

    if n_branch:
        ls = [lse_token_order(l_ref, d) for l_ref, d in zip(l_refs, dilations)]
        mx = functools.reduce(jnp.maximum, ls)
        ws = [jnp.exp(l - mx) for l in ls]
        tot = functools.reduce(lambda a, b: a + b, ws)
        oa = None
        for w, o_ref, d in zip(ws, o_refs, dilations):
            o_tok = out_token_order(o_ref, d)
            w = w / tot
            w_hi = w.astype(BF16)
            w_lo = (w - w_hi.astype(F32)).astype(BF16)
            w_full = (jnp.dot(w_hi, exp_ref[...], preferred_element_type=F32)
                      + jnp.dot(w_lo, exp_ref[...], preferred_element_type=F32))
            term = w_full * o_tok
            oa = term if oa is None else oa + term
        cat_ref[:, :WIDTH_A] = oa.astype(BF16)
    else:
        cat_ref[:, :WIDTH_A] = oa_ref[...].astype(BF16)

    for h in range(N_HEADS_B):
        ks = slice(h * KEY_DIM_B, (h + 1) * KEY_DIM_B)
        vs = slice(h * VAL_DIM_B, (h + 1) * VAL_DIM_B)
        st = st_ref[h]
        for c in range(qb_ref.shape[0] // RET_CHUNK):
            rc = slice(c * RET_CHUNK, (c + 1) * RET_CHUNK)
            q = qb_ref[rc, ks]
            k = kb_ref[rc, ks]
            v = vb_ref[rc, vs]
            a = lax.dot_general(q, k, (((1,), (1,)), ((), ())), preferred_element_type=F32) * dmat_ref[h]
            o = (jnp.dot(a.astype(BF16), v, preferred_element_type=F32)
                 + jnp.dot(q, st.astype(BF16), preferred_element_type=F32) * qdec_ref[h])
            kd = (k.astype(F32) * kdec_ref[h]).astype(BF16)
            st = st * cdec_ref[h] + lax.dot_general(kd, v, (((0,), (0,)), ((), ())), preferred_element_type=F32)
            mu = jnp.mean(o, axis=-1, keepdims=True)
            var = jnp.mean(jnp.square(o - mu), axis=-1, keepdims=True)
            obn = (o - mu) * lax.rsqrt(var + GN_EPS)
            g = gb_ref[rc, vs].astype(F32)
            gated = g * (1.0 / (1.0 + jnp.exp(-g))) * obn
            cat_ref[rc, WIDTH_A + h * VAL_DIM_B:WIDTH_A + (h + 1) * VAL_DIM_B] = gated.astype(BF16)
        st_ref[h] = st
        sto_ref[h] = st


def _decay_tables(chunk, rows):
    H = N_HEADS_B
    log_g = jnp.log(1.0 - 2.0 ** (-5.0 - jnp.arange(H, dtype=F32)))
    i = jnp.arange(rows, dtype=F32)
    live = np.arange(rows) < chunk
    diff = i[:, None] - i[None, :]
    causal = (diff >= 0) & jnp.asarray(live[:, None] & live[None, :])
    dmat = jnp.where(causal[None], jnp.exp(jnp.where(causal, diff, 0.0)[None] * log_g[:, None, None]), 0.0)
    q_decay = jnp.where(jnp.asarray(live)[None], jnp.exp((i[None, :] + 1.0) * log_g[:, None]), 0.0)
    k_decay = jnp.where(jnp.asarray(live)[None], jnp.exp((chunk - 1.0 - i)[None, :] * log_g[:, None]), 0.0)
    c_decay = jnp.exp(chunk * log_g)
    qdec = jnp.broadcast_to(q_decay[:, :, None], (H, rows, VAL_DIM_B))
    kdec = jnp.broadcast_to(k_decay[:, :, None], (H, rows, KEY_DIM_B))
    cdec = jnp.broadcast_to(c_decay[:, None, None], (H, KEY_DIM_B, VAL_DIM_B))
    return dmat.astype(F32), qdec.astype(F32), kdec.astype(F32), cdec.astype(F32)


def _ret_mix(qb, kb, vb, gb, attn, state0, chunk, chunks_per_step=1):
    B, S, _ = qb.shape
    rows = RET_CHUNK * chunks_per_step
    nc = S // rows
    assert nc * rows == S
    tables = _decay_tables(chunk, RET_CHUNK)

    def tok(b, c):
        return (b, c, 0)

    def tspec(w):
        return pl.BlockSpec((None, rows, w), tok)

    def const(shape):
        return pl.BlockSpec(shape, lambda b, c: (0,) * len(shape))

    ins = [qb, kb, vb, gb]
    in_specs = [tspec(QK_WIDTH_B), tspec(QK_WIDTH_B), tspec(WIDTH_B), tspec(WIDTH_B)]
    scratch = [pltpu.VMEM((N_HEADS_B, KEY_DIM_B, VAL_DIM_B), F32)]
    if isinstance(attn, tuple):
        outs_a, lses, dilations = attn
        expand = np.zeros((LANES, WIDTH_A), np.float32)
        for h in range(N_HEADS_A):
            expand[h, h * HEAD_DIM_A:(h + 1) * HEAD_DIM_A] = 1.0
        ins += list(outs_a) + list(lses) + [jnp.asarray(expand, BF16)]
        in_specs += ([pl.BlockSpec((None, rows // d, d * WIDTH_A), tok) for d in dilations]
                     + [pl.BlockSpec((None, rows // d, d * LANES), tok) for d in dilations] + [const((LANES, WIDTH_A))])
        scratch += [pltpu.VMEM((WIDTH_A // LANES, rows, LANES), F32), pltpu.VMEM((rows, LANES), F32)]
    else:
        dilations = ()
        ins.append(attn)
        in_specs.append(tspec(WIDTH_A))
    st_shape = (N_HEADS_B, KEY_DIM_B, VAL_DIM_B)
    st_spec = pl.BlockSpec((None,) + st_shape, lambda b, c: (b, 0, 0, 0))
    ins += [state0] + list(tables)
    in_specs += [st_spec] + [const(t.shape) for t in tables]
    return pl.pallas_call(
        functools.partial(_ret_body, dilations=tuple(dilations)),
        grid=(B, nc),
        in_specs=in_specs,
        out_specs=(tspec(WIDTH_A + WIDTH_B), st_spec),
        out_shape=(jax.ShapeDtypeStruct((B, S, WIDTH_A + WIDTH_B), BF16),
                   jax.ShapeDtypeStruct((B,) + st_shape, F32)),
        scratch_shapes=scratch,
        compiler_params=_params(2),
        name=f"ret_mix_{len(dilations)}",
    )(*ins)


SAMP_Q_ROWS = 64
SAMP_NEW_LANES = 128


def _samp_attn_body(q_ref, kt_ref, vt_ref, knt_ref, vnt_ref, bc_ref, bn_ref, hm_ref, o_ref, ko_ref, vo_ref, *, ds):
    q = q_ref[...]
    kt = kt_ref[...]
    vt = vt_ref[...]
    knt = knt_ref[...]
    vnt = vnt_ref[...]
    w = kt.shape[1]
    is_new = lax.broadcasted_iota(jnp.int32, knt.shape, 1) >= SAMP_NEW_LANES - ds
    for src, new, dst in ((kt, knt, ko_ref), (vt, vnt, vo_ref)):
        rolled = pltpu.roll(src, w - ds, 1)
        dst[:, :w - SAMP_NEW_LANES] = rolled[:, :w - SAMP_NEW_LANES]
        dst[:, w - SAMP_NEW_LANES:] = jnp.where(is_new, new, rolled[:, w - SAMP_NEW_LANES:])

    s_c = jnp.dot(q, kt.astype(BF16), preferred_element_type=F32)
    s_n = jnp.dot(q, knt.astype(BF16), preferred_element_type=F32)
    es_c, es_n, dens, lses = [], [], [], []
    for n in range(len(DILATED_BRANCHES)):
        sc = s_c + bc_ref[n]
        sn = s_n + bn_ref[n]
        m = jnp.maximum(jnp.max(sc, axis=-1, keepdims=True), jnp.max(sn, axis=-1, keepdims=True))
        ec = jnp.exp(sc - m)
        en = jnp.exp(sn - m)
        den = jnp.sum(ec, axis=-1, keepdims=True) + jnp.sum(en, axis=-1, keepdims=True)
        es_c.append(ec)
        es_n.append(en)
        dens.append(den)
        lses.append(m + jnp.log(den))
    mx = functools.reduce(jnp.maximum, lses)
    ws = [jnp.exp(l - mx) for l in lses]
    tot = functools.reduce(lambda a, b: a + b, ws)
    p_c = None
    p_n = None
    for w, den, ec, en in zip(ws, dens, es_c, es_n):
        coef = w / (tot * den)
        p_c = coef * ec if p_c is None else p_c + coef * ec
        p_n = coef * en if p_n is None else p_n + coef * en
    nt = (((1,), (1,)), ((), ()))
    o = (lax.dot_general(p_c.astype(BF16), vt.astype(BF16), nt, preferred_element_type=F32)
         + lax.dot_general(p_n.astype(BF16), vnt.astype(BF16), nt, preferred_element_type=F32))
    o = o * hm_ref[...]
    o_ref[...] = jnp.sum(o.reshape(SUBLANES, N_HEADS_A, WIDTH_A), axis=1)


def _samp_bias_tables(rel_bias, w_buf, ds):
    tabs_c, tabs_n = [], []
    s = np.arange(SUBLANES)[:, None]
    live_s = s < ds
    first_new = SAMP_NEW_LANES - ds
    for window, dil in DILATED_BRANCHES:
        n_keys = window // dil
        for keys, live_k, tabs in ((np.arange(w_buf)[None, :], True, tabs_c),
                                   (w_buf - first_new + np.arange(SAMP_NEW_LANES)[None, :],
                                    np.arange(SAMP_NEW_LANES)[None, :] >= first_new, tabs_n)):
            n = keys.shape[1]
            dist = w_buf + s - keys
            valid = (dist >= 0) & (dist % dil == 0) & (dist // dil <= n_keys) & live_k
            bias = _bias_by_bucket(rel_bias, _t5_bucket(np.clip(dist, 0, None))).transpose(1, 0, 2)
            tab = jnp.where(jnp.asarray(valid)[:, None, :], bias, NEG_INF)
            pad = jnp.where(jnp.asarray(np.broadcast_to(live_k, dist.shape))[:, None, :], 0.0, NEG_INF)
            tab = jnp.where(jnp.asarray(live_s)[:, :, None], tab, pad)
            tabs.append(tab.reshape(SAMP_Q_ROWS, n))
    return jnp.stack(tabs_c), jnp.stack(tabs_n)


def _samp_attn(qa, cache_kt, cache_vt, knt, vnt, rel_bias, ds):
    DB, DS, _ = qa.shape
    W = cache_kt.shape[2]
    head_of_lane = np.arange(WIDTH_A) // HEAD_DIM_A
    hmask = (np.arange(SAMP_Q_ROWS)[:, None] % N_HEADS_A == head_of_lane[None, :])
    q8 = jnp.pad(qa, ((0, 0), (0, SUBLANES - DS), (0, 0)))
    q_rows = jnp.where(jnp.asarray(hmask)[None], jnp.repeat(q8, N_HEADS_A, axis=1), jnp.zeros((), BF16))
    bias_c, bias_n = _samp_bias_tables(rel_bias, W, DS)

    def per_b(rows, w):
        return pl.BlockSpec((None, rows, w), lambda b: (b, 0, 0))

    def const(a):
        return pl.BlockSpec(a.shape, lambda b: (0,) * a.ndim)

    hm = jnp.asarray(hmask, F32)
    return pl.pallas_call(
        functools.partial(_samp_attn_body, ds=ds),
        grid=(DB,),
        in_specs=[per_b(SAMP_Q_ROWS, WIDTH_A), per_b(WIDTH_A, W), per_b(WIDTH_A, W),
                  per_b(WIDTH_A, SAMP_NEW_LANES), per_b(WIDTH_A, SAMP_NEW_LANES),
                  const(bias_c), const(bias_n), const(hm)],
        out_specs=(per_b(SUBLANES, WIDTH_A), per_b(WIDTH_A, W), per_b(WIDTH_A, W)),
        out_shape=(jax.ShapeDtypeStruct((DB, SUBLANES, WIDTH_A), F32),
                   jax.ShapeDtypeStruct((DB, WIDTH_A, W), F32), jax.ShapeDtypeStruct((DB, WIDTH_A, W), F32)),
        compiler_params=_params(1),
        name="samp_attn",
    )(q_rows, cache_kt, cache_vt, knt, vnt, bias_c, bias_n, hm)


ROUTE_ROWS = 16


def _route_body(cat_ref, x_ref, wout_ref, g_ref, b_ref, rwh_ref, rwl_ref, rb_ref, triu_ref, base_ref, *refs,
                alpha, n_exp):
    h_ref, hp_ref, route_ref, route_t_ref, cnt_ref = refs[1:]

    @pl.when(pl.program_id(0) == 0)
    def _():
        cnt_ref[...] = base_ref[...]

    mix = jnp.dot(cat_ref[...], wout_ref[...], preferred_element_type=F32)
    y = alpha * x_ref[...] + mix
    mu = jnp.mean(y, axis=-1, keepdims=True)
    var = jnp.mean(jnp.square(y - mu), axis=-1, keepdims=True)
    h = (y - mu) * lax.rsqrt(var + LN_EPS) * g_ref[...] + b_ref[...]
    h_ref[...] = h
    hb = h.astype(BF16)
    hp_ref[...] = _pack_bf16_pairs(h)
    hl = (h - hb.astype(F32)).astype(BF16)
    nt = (((1,), (1,)), ((), ()))
    logits = (lax.dot_general(rwh_ref[...], hb, nt, preferred_element_type=F32)
              + lax.dot_general(rwl_ref[...], hb, nt, preferred_element_type=F32)
              + lax.dot_general(rwh_ref[...], hl, nt, preferred_element_type=F32))[:n_exp] + rb_ref[...]

    tm = logits.shape[1]
    sub = lax.broadcasted_iota(jnp.int32, (n_exp, tm), 0)
    work = logits
    vals, idxs = [], []
    for _ in range(TOP_K):
        m = jnp.max(work, axis=0, keepdims=True)
        idx = jnp.min(jnp.where(work == m, sub, n_exp), axis=0, keepdims=True)
        vals.append(m)
        idxs.append(idx)
        work = jnp.where(sub == idx, -jnp.inf, work)
    es = [jnp.exp(v - vals[0]) for v in vals]
    tot = functools.reduce(lambda a, b: a + b, es)
    onehot = jnp.zeros((n_exp, tm), F32)
    for idx in idxs:
        onehot = onehot + (sub == idx).astype(F32)
    before = jnp.dot(onehot.astype(BF16), triu_ref[...], preferred_element_type=F32) + cnt_ref[:, 0:1]
    rows = ([idx.astype(F32) for idx in idxs] + [e / tot for e in es]
            + [jnp.sum(jnp.where(sub == idx, before, 0.0), axis=0, keepdims=True) for idx in idxs])
    rows.append(jnp.zeros((ROUTE_ROWS - len(rows), tm), F32))
    record = jnp.concatenate(rows, axis=0)
    route_t_ref[...] = record
    route_ref[...] = jnp.concatenate([record, jnp.zeros((LANES - ROUTE_ROWS, tm), F32)], axis=0).T
    cnt_ref[...] = cnt_ref[...] + jnp.sum(onehot, axis=1, keepdims=True)


def _out_route(cat, x2d, w_out_bf, ln_g, ln_b, rw_hi_t, rw_lo_t, rb_col, base, hp_buf, tile0, tm, alpha):
    T = x2d.shape[0]
    n_exp = rb_col.shape[0]
    triu = jnp.asarray(np.triu(np.ones((tm, tm), np.float32), 1), BF16)

    def tok(i):
        return (i, 0)

    def const(a):
        return pl.BlockSpec(a.shape, lambda i: (0,) * a.ndim)

    ins = (cat, x2d, w_out_bf, ln_g, ln_b, rw_hi_t, rw_lo_t, rb_col, triu, base, hp_buf)
    in_specs = ([pl.BlockSpec((tm, cat.shape[1]), tok), pl.BlockSpec((tm, D_MODEL), tok)]
                + [const(a) for a in ins[2:-1]] + [pl.BlockSpec(memory_space=pl.ANY)])
    return pl.pallas_call(
        functools.partial(_route_body, alpha=alpha, n_exp=n_exp),
        grid=(T // tm,),
        in_specs=in_specs,
        out_specs=(pl.BlockSpec((tm, D_MODEL), tok), pl.BlockSpec((tm, D_MODEL // 2), lambda i: (tile0 + i, 0)),
                   pl.BlockSpec((tm, LANES), tok), pl.BlockSpec((ROUTE_ROWS, tm), lambda i: (0, i)),
                   pl.BlockSpec(base.shape, lambda i: (0, 0))),
        out_shape=(jax.ShapeDtypeStruct((T, D_MODEL), F32), jax.ShapeDtypeStruct(hp_buf.shape, jnp.int32),
                   jax.ShapeDtypeStruct((T, LANES), F32), jax.ShapeDtypeStruct((ROUTE_ROWS, T), F32),
                   jax.ShapeDtypeStruct(base.shape, F32)),
        input_output_aliases={len(ins) - 1: 1},
        compiler_params=_params(1),
        name="out_route",
    )(*ins)


MOE_CAST_ROWS = 128


def _moe_body(be_ref, first_ref, next_ref, nused_ref, x_ref, bgu_ref, bdn_ref, wgu_hbm, wdn_hbm, y_ref,
              gu_stage, dn_stage, wgu_bf, wdn_bf, sem):
    b = pl.program_id(0)
    d_exp = wdn_bf.shape[0]

    def fetch(e):
        return (pltpu.make_async_copy(wgu_hbm.at[e], gu_stage, sem.at[0]),
                pltpu.make_async_copy(wdn_hbm.at[e], dn_stage, sem.at[1]))

    @pl.when(b == 0)
    def _():
        for c in fetch(be_ref[0]):
            c.start()

    @pl.when(first_ref[b] == 1)
    def _():
        for c in fetch(be_ref[b]):
            c.wait()

        def cast_gu(i, c):
            r = pl.ds(pl.multiple_of(i * MOE_CAST_ROWS, MOE_CAST_ROWS), MOE_CAST_ROWS)
            wgu_bf[r, :] = gu_stage[r, :].astype(BF16)
            return c

        def cast_dn(i, c):
            r = pl.ds(pl.multiple_of(i * MOE_CAST_ROWS, MOE_CAST_ROWS), MOE_CAST_ROWS)
            wdn_bf[r, :] = dn_stage[r, :].astype(BF16)
            return c

        lax.fori_loop(0, gu_stage.shape[0] // MOE_CAST_ROWS, cast_gu, 0)
        lax.fori_loop(0, d_exp // MOE_CAST_ROWS, cast_dn, 0)

        @pl.when(next_ref[b] >= 0)
        def _():
            for c in fetch(next_ref[b]):
                c.start()

    @pl.when(b < nused_ref[0])
    def _():
        x_lo, x_hi = _unpack_bf16_pairs(x_ref[...])
        x_lo = x_lo.astype(BF16)
        x_hi = x_hi.astype(BF16)
        dh = x_lo.shape[1]

        def xw(cols):
            return (jnp.dot(x_lo, wgu_bf[:dh, cols], preferred_element_type=F32)
                    + jnp.dot(x_hi, wgu_bf[dh:, cols], preferred_element_type=F32) + bgu_ref[:, cols])

        half = d_exp // 2
        y = None
        for c in range(2):
            lo = c * half
            gate = jnp.minimum(xw(slice(lo, lo + half)), SWIGLU_LIMIT)
            up = jnp.clip(xw(slice(d_exp + lo, d_exp + lo + half)), -SWIGLU_LIMIT, SWIGLU_LIMIT)
            act = (up + 1.0) * gate * (1.0 / (1.0 + jnp.exp(-SWIGLU_ALPHA * gate)))
            part = jnp.dot(act.astype(BF16), wdn_bf[lo:lo + half, :], preferred_element_type=F32)
            y = part if y is None else y + part
        y_ref[...] = _pack_bf16_pairs(y + bdn_ref[...])

    @pl.when(b >= nused_ref[0])
    def _():
        y_ref[...] = jnp.zeros_like(y_ref)


def _moe_ffn(xs, rows, block_e, n_used, has_rows, w_gu, b_gu, w_dn, b_dn):
    E, D, two_de = w_gu.shape
    d_exp = two_de // 2
    nb = rows // MOE_BLOCK
    idx = jnp.arange(nb, dtype=jnp.int32)
    first = ((idx < n_used[0]) & ((idx == 0) | (block_e != jnp.roll(block_e, 1)))).astype(jnp.int32)
    ids = jnp.arange(E, dtype=jnp.int32)
    later = jnp.where((ids[None, :] > ids[:, None]) & has_rows[None, :], ids[None, :], E).min(axis=1)
    next_of = jnp.where(later == E, -1, later).astype(jnp.int32)
    next_e = jnp.sum(jnp.where(block_e[:, None] == ids[None, :], next_of[None, :], 0), axis=1).astype(jnp.int32)
    grid_spec = pltpu.PrefetchScalarGridSpec(
        num_scalar_prefetch=4,
        grid=(nb,),
        in_specs=[
            pl.BlockSpec((MOE_BLOCK, D // 2), lambda b, be, fi, nx, nu: (b, 0)),
            pl.BlockSpec((None, 1, two_de), lambda b, be, fi, nx, nu: (be[b], 0, 0)),
            pl.BlockSpec((None, 1, D), lambda b, be, fi, nx, nu: (be[b], 0, 0)),
            pl.BlockSpec(memory_space=pl.ANY),
            pl.BlockSpec(memory_space=pl.ANY),
        ],
        out_specs=pl.BlockSpec((MOE_BLOCK, D // 2), lambda b, be, fi, nx, nu: (b, 0)),
        scratch_shapes=[pltpu.VMEM((D, two_de), F32), pltpu.VMEM((d_exp, D), F32),
                        pltpu.VMEM((D, two_de), BF16), pltpu.VMEM((d_exp, D), BF16),
                        pltpu.SemaphoreType.DMA((2,))],
    )
    return pl.pallas_call(
        _moe_body,
        grid_spec=grid_spec,
        out_shape=jax.ShapeDtypeStruct((rows, D // 2), jnp.int32),
        compiler_params=_params(1),
        name="moe_ffn",
    )(block_e, first, next_e, n_used, xs, b_gu.reshape(E, 1, two_de), b_dn.reshape(E, 1, D), w_gu, w_dn)


def _ple_body(h_ref, ys_ref, route_ref, p_ref, g_ref, b_ref, wpg_ref, bpg_ref, wpp_ref, o_ref, *, alpha):
    route = route_ref[...]
    f_lo = None
    f_hi = None
    for k in range(TOP_K):
        lo, hi = _unpack_bf16_pairs(ys_ref[k])
        g = route[:, TOP_K + k:TOP_K + k + 1]
        f_lo = g * lo if f_lo is None else f_lo + g * lo
        f_hi = g * hi if f_hi is None else f_hi + g * hi
    y = alpha * h_ref[...] + jnp.concatenate([f_lo, f_hi], axis=1)
    mu = jnp.mean(y, axis=-1, keepdims=True)
    var = jnp.mean(jnp.square(y - mu), axis=-1, keepdims=True)
    h2 = (y - mu) * lax.rsqrt(var + LN_EPS) * g_ref[...] + b_ref[...]
    z = jnp.dot(h2.astype(BF16), wpg_ref[...], preferred_element_type=F32) + bpg_ref[...]
    gate = 1.0 / (1.0 + jnp.exp(-z))
    proj = jnp.dot(p_ref[...].astype(BF16), wpp_ref[...], preferred_element_type=F32)
    o_ref[...] = h2 + gate * proj


def _ffn_ple(h, y_slots, route, p, ln_g, ln_b, w_pg_bf, b_pg, w_pp_bf, tm, tile0, alpha):
    T = h.shape[0]

    def tok(i):
        return (i, 0)

    def const(a):
        return pl.BlockSpec(a.shape, lambda i: (0,) * a.ndim)

    consts = (ln_g, ln_b, w_pg_bf, b_pg, w_pp_bf)
    return pl.pallas_call(
        functools.partial(_ple_body, alpha=alpha),
        grid=(T // tm,),
        in_specs=[pl.BlockSpec((tm, D_MODEL), tok),
                  pl.BlockSpec((TOP_K, tm, D_MODEL // 2), lambda i: (0, tile0 + i, 0)),
                  pl.BlockSpec((tm, LANES), tok),
                  pl.BlockSpec((tm, p.shape[1]), tok)] + [const(a) for a in consts],
        out_specs=pl.BlockSpec((tm, D_MODEL), tok),
        out_shape=jax.ShapeDtypeStruct((T, D_MODEL), F32),
        compiler_params=_params(1),
        name="ffn_ple",
    )(h, y_slots, route, p, *consts)


def _row(v):
    return v.reshape(1, -1).astype(F32)


def kernel(x_prompt, x_sample, cache_win_k, cache_win_v, state_ret, p_prompt, p_sample, rel_bias, w_in, w_out,
           ln1_g, ln1_b, router_w, router_b, w_gate_up, b_gate_up, w_down, b_down, ln2_g, ln2_b,
           w_ple_gate, b_ple_gate, w_ple_proj):
    B, S, D = x_prompt.shape
    DB, DS, _ = x_sample.shape
    depth = w_in.shape[0]
    w_buf = cache_win_k.shape[2]
    n_exp = router_w.shape[-1]
    alpha = (2.0 * depth) ** 0.25
    assert depth == 1 and D == D_MODEL
    assert S % (BLK * MAX_DIL) == 0 and S >= WINDOW_MAX and w_buf == WINDOW_MAX and DS <= SUBLANES
    tm_p = 512
    Tp, Ts = B * S, DB * DS
    assert Tp % tm_p == 0 and Ts % SUBLANES == 0

    i = 0
    w_in_bf = w_in[i].astype(BF16)
    w_out_bf = w_out[i].astype(BF16)
    w_pg_bf = w_ple_gate[i].astype(BF16)
    w_pp_bf = w_ple_proj[i].astype(BF16)
    rw_t = jnp.pad(router_w[i].T, ((0, LANES - n_exp), (0, 0)))
    rw_hi = rw_t.astype(BF16)
    rw_lo = (rw_t - rw_hi.astype(F32)).astype(BF16)
    rb = router_b[i].astype(F32).reshape(n_exp, 1)
    T = Tp + Ts
    t_align = SC_WORKERS * SC_ALIGN
    T_pad = -(-T // t_align) * t_align
    assert Tp % Ts == 0

    cos_p, sin_p = _rotary_tables(jnp.arange(S, dtype=jnp.int32))
    dils = tuple(d for _, d in DILATED_BRANCHES)
    extra = tuple(d for d in dils if d > 1)
    outs = _in_proj(x_prompt.reshape(Tp, D), w_in_bf, cos_p, sin_p, tm_p, S // tm_p, WINDOW_MAX // tm_p, extra)
    qa, ka, va, qb, kb, vb, gb, kf, vf = outs[:9]
    qkv = {1: (qa, ka, va)}
    for t, d in enumerate(extra):
        qkv[d] = tuple(outs[9 + j * len(extra) + t] for j in range(3))

    def seq(t):
        return t.reshape(B, S, t.shape[-1])

    outs_a, lses = [], []
    for window, dil in DILATED_BRANCHES:
        q_d, k_d, v_d = (t.reshape(B, S // dil, dil * WIDTH_A) for t in qkv[dil])
        o_n, l_n = _dilated_branch(q_d, k_d, v_d, _attn_bias_tables(rel_bias, window, dil), dil)
        outs_a.append(o_n)
        lses.append(l_n)
    st_zero = jnp.zeros((B, N_HEADS_B, KEY_DIM_B, VAL_DIM_B), F32)
    cat_p, rst_p = _ret_mix(seq(qb), seq(kb), seq(vb), seq(gb), (outs_a, lses, dils), st_zero, RET_CHUNK,
                            RET_CHUNKS_PER_STEP)
    base0 = jnp.zeros((n_exp, LANES), F32)
    hp0 = jnp.zeros((T_pad, D // 2), jnp.int32)
    h_p, hp_buf, route_p, rt_p, cnt_p = _out_route(cat_p.reshape(Tp, -1), x_prompt.reshape(Tp, D), w_out_bf,
                                                   _row(ln1_g[i]), _row(ln1_b[i]), rw_hi, rw_lo, rb, base0, hp0, 0,
                                                   tm_p, alpha)

    pos_s = jnp.tile(PAST_LEN + jnp.arange(DS, dtype=jnp.int32), DB)
    cos_s, sin_s = _rotary_tables(pos_s)
    qa_s, _, _, qb_s, kb_s, vb_s, gb_s, kf_s, vf_s = _in_proj(x_sample.reshape(Ts, D), w_in_bf, cos_s, sin_s,
                                                             Ts, 1, 1)

    def positions_minor(t):
        return jnp.transpose(t, (0, 2, 3, 1)).reshape(DB, WIDTH_A, t.shape[1])

    def positions_major(t):
        return jnp.transpose(t.reshape(t.shape[0], N_HEADS_A, HEAD_DIM_A, t.shape[2]), (0, 3, 1, 2))[None]

    def new_columns(t):
        t = jnp.transpose(t.reshape(WIDTH_A, DB, DS), (1, 0, 2))
        return jnp.pad(t, ((0, 0), (0, 0), (SAMP_NEW_LANES - DS, 0)))

    oa_s, kt_out, vt_out = _samp_attn(qa_s.reshape(DB, DS, WIDTH_A), positions_minor(cache_win_k[i]),
                                      positions_minor(cache_win_v[i]), new_columns(kf_s), new_columns(vf_s),
                                      rel_bias, DS)

    def pad_rows(t, rows):
        t = t.reshape(DB, -1, t.shape[-1])
        return jnp.pad(t, ((0, 0), (0, rows - t.shape[1]), (0, 0)))

    cat_s, rst_s = _ret_mix(pad_rows(qb_s, RET_CHUNK), pad_rows(kb_s, RET_CHUNK), pad_rows(vb_s, RET_CHUNK),
                            pad_rows(gb_s, RET_CHUNK), pad_rows(oa_s, RET_CHUNK), state_ret[i].astype(F32), DS)
    cat_s = cat_s[:, :DS].reshape(Ts, -1)
    h_s, hp_all, route_s, rt_s, cnt = _out_route(cat_s, x_sample.reshape(Ts, D), w_out_bf, _row(ln1_g[i]),
                                                 _row(ln1_b[i]), rw_hi, rw_lo, rb, cnt_p, hp_buf, Tp // Ts, Ts, alpha)

    n_fill = T_pad - T
    record = jnp.concatenate([rt_p, rt_s, jnp.zeros((ROUTE_ROWS, n_fill), F32)], axis=1)
    top_idx = record[:TOP_K].astype(jnp.int32)
    rank = record[2 * TOP_K:3 * TOP_K].astype(jnp.int32)
    counts = cnt[:, 0].astype(jnp.int32)
    padded = (counts + MOE_BLOCK - 1) // MOE_BLOCK * MOE_BLOCK
    pad_end = jnp.cumsum(padded)
    pad_start = pad_end - padded
    ids = jnp.arange(n_exp, dtype=jnp.int32)[:, None, None]
    dest = jnp.sum(jnp.where(top_idx[None] == ids, pad_start[:, None, None], 0), axis=0) + rank
    n_blocks = -(-T * TOP_K // MOE_BLOCK) + n_exp
    rows = n_blocks * MOE_BLOCK
    block_start = jnp.arange(n_blocks, dtype=jnp.int32) * MOE_BLOCK
    block_e = jnp.minimum(jnp.sum(pad_end[None, :] <= block_start[:, None], axis=1), n_exp - 1).astype(jnp.int32)
    n_used = (pad_end[-1:] // MOE_BLOCK).astype(jnp.int32)
    tok_id = jnp.arange(T_pad, dtype=jnp.int32)[None, :]
    spare = rows + (tok_id - T) * TOP_K + jnp.arange(TOP_K, dtype=jnp.int32)[:, None]
    dest_sc = jnp.where(tok_id >= T, spare, dest).reshape(-1)
    dest_ga = jnp.where(tok_id >= T, 0, dest).reshape(-1)
    xs = _sc_scatter_rows(hp_all, dest_sc, rows + n_fill * TOP_K)
    ys = _moe_ffn(xs, rows, block_e, n_used, padded > 0, w_gate_up[i], b_gate_up[i], w_down[i], b_down[i])
    y_slots = _sc_gather_rows(ys, dest_ga).reshape(TOP_K, T_pad, D // 2)

    ple_args = (_row(ln2_g[i]), _row(ln2_b[i]), w_pg_bf, _row(b_ple_gate[i]), w_pp_bf)
    y_p = _ffn_ple(h_p, y_slots, route_p, p_prompt[i].reshape(Tp, D_PLE), *ple_args, tm_p, 0, alpha)
    y_s = _ffn_ple(h_s, y_slots, route_s, p_sample[i].reshape(Ts, D_PLE), *ple_args, Ts, Tp // Ts, alpha)

    return (y_p.reshape(B, S, D), y_s.reshape(DB, DS, D), positions_major(kf), positions_major(vf),
            rst_p[None], positions_major(kt_out), positions_major(vt_out), rst_s[None])
```

```python
import functools

import numpy as np
import jax
import jax.numpy as jnp
from jax import lax
from jax.experimental import pallas as pl
from jax.experimental.pallas import tpu as pltpu
from jax.experimental.pallas import tpu_sc as plsc

F32 = jnp.float32
BF16 = jnp.bfloat16

D_MODEL = 1024
D_PLE = 256
N_HEADS_A = 8
HEAD_DIM_A = 64
WIDTH_A = N_HEADS_A * HEAD_DIM_A
DILATED_BRANCHES = ((128, 1), (512, 4), (2048, 16))
BLK = 128
WINDOW_MAX = 2048
MAX_DIL = 16
NUM_BUCKETS = 32
MAX_DISTANCE = 2048
N_HEADS_B = 4
KEY_DIM_B = 64
VAL_DIM_B = 128
QK_WIDTH_B = N_HEADS_B * KEY_DIM_B
WIDTH_B = N_HEADS_B * VAL_DIM_B
RET_CHUNK = 128
GN_EPS = 1e-6
TOP_K = 4
SWIGLU_LIMIT = 7.0
SWIGLU_ALPHA = 1.702
LN_EPS = 1e-5
NEG_INF = -1e30
PAST_LEN = 16384
MOE_BLOCK = 512
LANES = 128
SUBLANES = 8
VMEM_LIMIT = 52 * 1024 * 1024


def _params(n_axes, vmem=VMEM_LIMIT):
    return pltpu.CompilerParams(dimension_semantics=("arbitrary",) * n_axes, vmem_limit_bytes=vmem)


def _t5_bucket(dist):
    dist = np.asarray(dist, dtype=np.int32)
    max_exact = NUM_BUCKETS // 2
    d = np.maximum(dist, 1).astype(np.float32)
    large = max_exact + (np.log(d / max_exact) / np.log(MAX_DISTANCE / max_exact)
                         * (NUM_BUCKETS - max_exact)).astype(np.int32)
    large = np.minimum(large, NUM_BUCKETS - 1)
    return np.where(dist < max_exact, dist, large).astype(np.int32)


def _bias_by_bucket(rel_bias, buckets):
    b = jnp.asarray(buckets, jnp.int32)
    ids = jnp.arange(NUM_BUCKETS, dtype=jnp.int32).reshape((NUM_BUCKETS, 1) + (1,) * b.ndim)
    vals = rel_bias.astype(F32).reshape((NUM_BUCKETS, rel_bias.shape[1]) + (1,) * b.ndim)
    return jnp.sum(jnp.where(b[None, None] == ids, vals, 0.0), axis=0)


def _pack_bf16_pairs(v):
    w = v.shape[1] // 2
    lo = lax.bitcast_convert_type(v[:, :w].astype(BF16).astype(F32), jnp.uint32) >> 16
    hi = lax.bitcast_convert_type(v[:, w:].astype(BF16).astype(F32), jnp.uint32) & jnp.uint32(0xFFFF0000)
    return lax.bitcast_convert_type(lo | hi, jnp.int32)


def _unpack_bf16_pairs(p):
    u = lax.bitcast_convert_type(p, jnp.uint32)
    lo = lax.bitcast_convert_type(u << 16, F32)
    hi = lax.bitcast_convert_type(u & jnp.uint32(0xFFFF0000), F32)
    return lo, hi


SC_CORES = 2
SC_SUBCORES = 16
SC_WORKERS = SC_CORES * SC_SUBCORES
SC_ALIGN = 8
SC_CHUNK_ROWS = 80


def _sc_mesh():
    return plsc.VectorSubcoreMesh(core_axis_name="c", subcore_axis_name="s")


def _sc_chunk(per_worker):
    c = max(d for d in range(SC_ALIGN, SC_CHUNK_ROWS + 1, SC_ALIGN) if per_worker % d == 0)
    return c


def _sc_scatter_rows(src, dest_flat, n_out):
    T, W = src.shape
    K = dest_flat.shape[0] // T
    per_w = T // SC_WORKERS
    assert per_w * SC_WORKERS == T and per_w % SC_ALIGN == 0
    chunk = _sc_chunk(per_w)
    n_chunks = per_w // chunk

    @functools.partial(
        pl.kernel, mesh=_sc_mesh(), out_type=jax.ShapeDtypeStruct((n_out, W), src.dtype),
        scratch_types=[pltpu.VMEM((chunk, W), src.dtype)] * 2 + [pltpu.VMEM((chunk,), jnp.int32)] * (2 * K)
        + [pltpu.SemaphoreType.DMA] * 2,
        name="sc_scatter_rows")
    def k(src_hbm, dest_hbm, out_hbm, rows_a, rows_b, *rest):
        bufs = ((rows_a, rest[:K], rest[2 * K]), (rows_b, rest[K:2 * K], rest[2 * K + 1]))
        base = (lax.axis_index("s") * SC_CORES + lax.axis_index("c")) * per_w

        def scatters(buf):
            rows_v, idx_vs, sem = buf
            return [pltpu.make_async_copy(rows_v, out_hbm.at[idx_vs[kk]], sem) for kk in range(K)]

        def step(j, buf):
            rows_v, idx_vs, _ = buf

            @pl.when(j >= 2)
            def _():
                for c in scatters(buf):
                    c.wait()

            off = pl.multiple_of(base + j * chunk, SC_ALIGN)
            pltpu.sync_copy(src_hbm.at[pl.ds(off, chunk)], rows_v)
            for kk in range(K):
                pltpu.sync_copy(dest_hbm.at[pl.ds(kk * T + off, chunk)], idx_vs[kk])
            for c in scatters(buf):
                c.start()

        @pl.loop(0, n_chunks)
        def _(j):
            for parity in range(2):
                @pl.when(j % 2 == parity)
                def _():
                    step(j, bufs[parity])

        for j in range(max(n_chunks - 2, 0), n_chunks):
            for c in scatters(bufs[j % 2]):
                c.wait()

    return k(src, dest_flat)


def _sc_gather_rows(table, idx):
    B = idx.shape[0]
    W = table.shape[1]
    per_w = B // SC_WORKERS
    assert per_w * SC_WORKERS == B and per_w % SC_ALIGN == 0
    chunk = _sc_chunk(per_w)
    n_chunks = per_w // chunk

    @functools.partial(
        pl.kernel, mesh=_sc_mesh(), out_type=jax.ShapeDtypeStruct((B, W), table.dtype),
        scratch_types=[pltpu.VMEM((chunk,), jnp.int32)] * 2 + [pltpu.VMEM((chunk, W), table.dtype)] * 2
        + [pltpu.SemaphoreType.DMA] * 2,
        name="sc_gather_rows")
    def k(table_hbm, idx_hbm, out_hbm, idx_a, idx_b, rows_a, rows_b, sem_a, sem_b):
        bufs = ((idx_a, rows_a, sem_a), (idx_b, rows_b, sem_b))
        base = (lax.axis_index("s") * SC_CORES + lax.axis_index("c")) * per_w

        def gather(buf):
            idx_v, rows_v, sem = buf
            return pltpu.make_async_copy(table_hbm.at[idx_v], rows_v, sem)

        def finish(j, buf):
            gather(buf).wait()
            off = pl.multiple_of(base + j * chunk, SC_ALIGN)
            pltpu.sync_copy(buf[1], out_hbm.at[pl.ds(off, chunk)])

        def step(j, buf, other):
            off = pl.multiple_of(base + j * chunk, SC_ALIGN)
            pltpu.sync_copy(idx_hbm.at[pl.ds(off, chunk)], buf[0])
            gather(buf).start()

            @pl.when(j >= 1)
            def _():
                finish(j - 1, other)

        @pl.loop(0, n_chunks)
        def _(j):
            for parity in range(2):
                @pl.when(j % 2 == parity)
                def _():
                    step(j, bufs[parity], bufs[1 - parity])

        finish(n_chunks - 1, bufs[(n_chunks - 1) % 2])

    return k(table, idx)


def _in_proj_body(x_ref, w_ref, cos_ref, sin_ref, *refs, seq_tiles, first_win, dilations):
    qa_ref, ka_ref, va_ref, qb_ref, kb_ref, vb_ref, gb_ref, kt_ref, vt_ref = refs[:9]
    n_d = len(dilations)
    dil_refs = [refs[9 + t * n_d:9 + (t + 1) * n_d] for t in range(3)]
    zs_ref = refs[9 + 3 * n_d] if n_d else None
    x = x_ref[...].astype(BF16)
    in_window = pl.program_id(0) % seq_tiles >= first_win

    def emit(z, token_ref, class_refs):
        token_ref[...] = z.astype(BF16)
        if not class_refs:
            return
        for c in range(WIDTH_A // LANES):
            zs_ref[c] = z[:, c * LANES:(c + 1) * LANES]
        for d, ref in zip(dilations, class_refs):
            n = z.shape[0] // d
            for r in range(d):
                for c in range(WIDTH_A // LANES):
                    lo = r * WIDTH_A + c * LANES
                    ref[:, lo:lo + LANES] = zs_ref[c, pl.ds(r, n, stride=d), :].astype(BF16)

    def proj(lo, hi):
        return jnp.dot(x, w_ref[:, lo:hi], preferred_element_type=F32)

    o = 0
    qa = proj(o, o + WIDTH_A) * (HEAD_DIM_A ** -0.5)
    o += WIDTH_A
    ka = proj(o, o + WIDTH_A)
    o += WIDTH_A
    emit(qa, qa_ref, dil_refs[0])
    va = proj(o, o + WIDTH_A)
    o += WIDTH_A
    emit(ka, ka_ref, dil_refs[1])

    @pl.when(in_window)
    def _():
        kt_ref[...] = ka.T
        vt_ref[...] = va.T

    cos = cos_ref[...]
    sin = sin_ref[...]
    lane = lax.broadcasted_iota(jnp.int32, cos.shape, 1)
    first_half = (lane % KEY_DIM_B) < (KEY_DIM_B // 2)

    def rot(z):
        sw = jnp.where(first_half, pltpu.roll(z, QK_WIDTH_B - KEY_DIM_B // 2, 1), pltpu.roll(z, KEY_DIM_B // 2, 1))
        return z * cos + sw * sin

    qb_ref[...] = rot(proj(o, o + QK_WIDTH_B)).astype(BF16)
    o += QK_WIDTH_B
    emit(va, va_ref, dil_refs[2])
    kb_ref[...] = (rot(proj(o, o + QK_WIDTH_B)) * (KEY_DIM_B ** -0.5)).astype(BF16)
    o += QK_WIDTH_B
    vb_ref[...] = proj(o, o + WIDTH_B).astype(BF16)
    o += WIDTH_B
    gb_ref[...] = proj(o, o + WIDTH_B).astype(BF16)


def _in_proj(x2d, w_bf, cos_t, sin_t, tm, seq_tiles, win_tiles, dilations=()):
    T = x2d.shape[0]
    nt = T // tm
    n_seq = nt // seq_tiles
    j0 = seq_tiles - win_tiles

    def tok(i):
        return (i, 0)

    def tab(i):
        return (i % seq_tiles, 0)

    def win(i):
        return (i // seq_tiles, 0, jnp.maximum(i % seq_tiles - j0, 0))

    def tspec(w):
        return pl.BlockSpec((tm, w), tok)

    out_shape = (
        jax.ShapeDtypeStruct((T, WIDTH_A), BF16), jax.ShapeDtypeStruct((T, WIDTH_A), BF16),
        jax.ShapeDtypeStruct((T, WIDTH_A), BF16),
        jax.ShapeDtypeStruct((T, QK_WIDTH_B), BF16), jax.ShapeDtypeStruct((T, QK_WIDTH_B), BF16),
        jax.ShapeDtypeStruct((T, WIDTH_B), BF16), jax.ShapeDtypeStruct((T, WIDTH_B), BF16),
        jax.ShapeDtypeStruct((n_seq, WIDTH_A, win_tiles * tm), F32),
        jax.ShapeDtypeStruct((n_seq, WIDTH_A, win_tiles * tm), F32),
    ) + tuple(jax.ShapeDtypeStruct((T // d, d * WIDTH_A), BF16) for _ in range(3) for d in dilations)
    class_specs = tuple(pl.BlockSpec((tm // d, d * WIDTH_A), tok) for _ in range(3) for d in dilations)
    return pl.pallas_call(
        functools.partial(_in_proj_body, seq_tiles=seq_tiles, first_win=j0, dilations=tuple(dilations)),
        grid=(nt,),
        in_specs=[tspec(D_MODEL), pl.BlockSpec(w_bf.shape, lambda i: (0, 0)),
                  pl.BlockSpec((tm, QK_WIDTH_B), tab), pl.BlockSpec((tm, QK_WIDTH_B), tab)],
        out_specs=(tspec(WIDTH_A), tspec(WIDTH_A), tspec(WIDTH_A), tspec(QK_WIDTH_B), tspec(QK_WIDTH_B),
                   tspec(WIDTH_B), tspec(WIDTH_B),
                   pl.BlockSpec((None, WIDTH_A, tm), win), pl.BlockSpec((None, WIDTH_A, tm), win)) + class_specs,
        out_shape=out_shape,
        scratch_shapes=[pltpu.VMEM((WIDTH_A // LANES, tm, LANES), F32)] if dilations else [],
        compiler_params=_params(1),
        name="in_proj",
    )(x2d, w_bf, cos_t, sin_t)


def _rotary_tables(pos):
    half = KEY_DIM_B // 2
    inv_freq = 1.0 / (10000.0 ** jnp.linspace(0.0, 1.0, half, dtype=F32))
    ang = pos.astype(F32)[:, None] * inv_freq[None, :]
    cos = jnp.cos(ang)
    sin = jnp.sin(ang)
    cos_h = jnp.concatenate([cos, cos], axis=-1)
    sin_h = jnp.concatenate([-sin, sin], axis=-1)
    return jnp.tile(cos_h, (1, N_HEADS_B)), jnp.tile(sin_h, (1, N_HEADS_B))


ATTN_BLOCKS_PER_STEP = 8
RET_CHUNKS_PER_STEP = 4
RET_STEP_ROWS = RET_CHUNK * RET_CHUNKS_PER_STEP


def _attn_body(q_ref, k_ref, v_ref, bias_ref, o_ref, lse_ref, kb_ref, vb_ref):
    n = pl.program_id(2)
    rows = q_ref.shape[0]

    @pl.when(n == 0)
    def _():
        kb_ref[:BLK, :] = jnp.zeros((BLK, WIDTH_A), BF16)
        vb_ref[:BLK, :] = jnp.zeros((BLK, WIDTH_A), BF16)

    kb_ref[BLK:, :] = k_ref[...]
    vb_ref[BLK:, :] = v_ref[...]
    lane = lax.broadcasted_iota(jnp.int32, (BLK, LANES), 1)
    low = lane < HEAD_DIM_A
    nt = (((1,), (1,)), ((), ()))

    def sub_block(j, carry):
        r0 = pl.multiple_of(j * BLK, BLK)
        table = jnp.where((n == 0) & (j == 0), 0, 1)
        lse_tile = jnp.zeros((BLK, LANES), F32)
        for p in range(N_HEADS_A // 2):
            cs = slice(p * LANES, (p + 1) * LANES)
            qp = q_ref[pl.ds(r0, BLK), cs]
            kp = kb_ref[pl.ds(r0, 2 * BLK), cs]
            vp = vb_ref[pl.ds(r0, 2 * BLK), cs]
            zero = jnp.zeros_like(qp)
            outs = []
            for half, qh in enumerate((jnp.where(low, qp, zero), jnp.where(low, zero, qp))):
                h = 2 * p + half
                s = lax.dot_general(qh, kp, nt, preferred_element_type=F32) + bias_ref[table, h]
                m = jnp.max(s, axis=-1, keepdims=True)
                e = jnp.exp(s - m)
                den = jnp.sum(e, axis=-1, keepdims=True)
                outs.append(jnp.dot(e.astype(BF16), vp, preferred_element_type=F32) / den)
                lse_tile = jnp.where(lane == h, m + jnp.log(den), lse_tile)
            o_ref[pl.ds(r0, BLK), cs] = jnp.where(low, outs[0], outs[1]).astype(o_ref.dtype)
        lse_ref[pl.ds(r0, BLK), :] = lse_tile
        return carry

    lax.fori_loop(0, rows // BLK, sub_block, 0, unroll=True)
    kb_ref[:BLK, :] = k_ref[rows - BLK:, :]
    vb_ref[:BLK, :] = v_ref[rows - BLK:, :]


def _attn_bias_tables(rel_bias, window, dil):
    n_keys = window // dil
    i = np.arange(BLK)[:, None]
    j = np.arange(2 * BLK)[None, :]
    rel = BLK + i - j
    in_band = (rel >= 0) & (rel <= n_keys)
    bias = _bias_by_bucket(rel_bias, _t5_bucket(np.clip(rel, 0, None) * dil))
    later = jnp.where(jnp.asarray(in_band)[None], bias, NEG_INF)
    first = jnp.where(jnp.asarray(in_band & (j >= BLK))[None], bias, NEG_INF)
    return jnp.stack([first, later])


def _dilated_branch(q, k, v, bias_tab, dil):
    B, L, _ = q.shape
    rows = min(ATTN_BLOCKS_PER_STEP, L // BLK) * BLK
    assert L % rows == 0
    o_dtype = BF16 if RET_STEP_ROWS // dil >= 16 else F32

    def cls(b, r, n):
        return (b, n, r)

    qkv_spec = pl.BlockSpec((None, rows, WIDTH_A), cls)
    o, lse = pl.pallas_call(
        _attn_body,
        grid=(B, dil, L // rows),
        in_specs=[qkv_spec, qkv_spec, qkv_spec,
                  pl.BlockSpec(bias_tab.shape, lambda b, r, n: (0, 0, 0, 0))],
        out_specs=(qkv_spec, pl.BlockSpec((None, rows, LANES), cls)),
        out_shape=(jax.ShapeDtypeStruct((B, L, dil * WIDTH_A), o_dtype),
                   jax.ShapeDtypeStruct((B, L, dil * LANES), F32)),
        scratch_shapes=[pltpu.VMEM((BLK + rows, WIDTH_A), BF16), pltpu.VMEM((BLK + rows, WIDTH_A), BF16)],
        compiler_params=_params(3),
        name=f"dil_attn_d{dil}",
    )(q, k, v, bias_tab)
    return o, lse


def _ret_body(*refs, dilations):
    n_branch = len(dilations)
    qb_ref, kb_ref, vb_ref, gb_ref = refs[:4]
    p = 4
    if n_branch:
        o_refs = refs[p:p + n_branch]
        l_refs = refs[p + n_branch:p + 2 * n_branch]
        exp_ref = refs[p + 2 * n_branch]
        p += 2 * n_branch + 1
    else:
        oa_ref = refs[p]
        p += 1
    st0_ref, dmat_ref, qdec_ref, kdec_ref, cdec_ref, cat_ref, sto_ref, st_ref = refs[p:p + 8]
    if n_branch:
        us_ref, ls_ref = refs[p + 8:]

    def lse_token_order(l_ref, d):
        if d == 1:
            return l_ref[...]
        n = l_ref.shape[0]
        for r in range(d):
            ls_ref[pl.ds(r, n, stride=d), :] = l_ref[:, r * LANES:(r + 1) * LANES]
        return ls_ref[...]

    def out_token_order(o_ref, d):
        if d == 1:
            return o_ref[...].astype(F32)
        n = o_ref.shape[0]
        for r in range(d):
            for c in range(WIDTH_A // LANES):
                lo = r * WIDTH_A + c * LANES
                us_ref[c, pl.ds(r, n, stride=d), :] = o_ref[:, lo:lo + LANES].astype(F32)
        return jnp.concatenate([us_ref[c] for c in range(WIDTH_A // LANES)], axis=1)

    @pl.when(pl.program_id(1) == 0)
    def _():
        st_ref[...] = st0_ref[...]

    if n_branch:
        ls = [lse_token_order(l_ref, d) for l_ref, d in zip(l_refs, dilations)]
        mx = functools.reduce(jnp.maximum, ls)
        ws = [jnp.exp(l - mx) for l in ls]
        tot = functools.reduce(lambda a, b: a + b, ws)
        oa = None
        for w, o_ref, d in zip(ws, o_refs, dilations):
            o_tok = out_token_order(o_ref, d)
            w = w / tot
            w_hi = w.astype(BF16)
            w_lo = (w - w_hi.astype(F32)).astype(BF16)
            w_full = (jnp.dot(w_hi, exp_ref[...], preferred_element_type=F32)
                      + jnp.dot(w_lo, exp_ref[...], preferred_element_type=F32))
            term = w_full * o_tok
            oa = term if oa is None else oa + term
        cat_ref[:, :WIDTH_A] = oa.astype(BF16)
    else:
        cat_ref[:, :WIDTH_A] = oa_ref[...].astype(BF16)

    for h in range(N_HEADS_B):
        ks = slice(h * KEY_DIM_B, (h + 1) * KEY_DIM_B)
        vs = slice(h * VAL_DIM_B, (h + 1) * VAL_DIM_B)
        st = st_ref[h]
        for c in range(qb_ref.shape[0] // RET_CHUNK):
            rc = slice(c * RET_CHUNK, (c + 1) * RET_CHUNK)
            q = qb_ref[rc, ks]
            k = kb_ref[rc, ks]
            v = vb_ref[rc, vs]
            a = lax.dot_general(q, k, (((1,), (1,)), ((), ())), preferred_element_type=F32) * dmat_ref[h]
            o = (jnp.dot(a.astype(BF16), v, preferred_element_type=F32)
                 + jnp.dot(q, st.astype(BF16), preferred_element_type=F32) * qdec_ref[h])
            kd = (k.astype(F32) * kdec_ref[h]).astype(BF16)
            st = st * cdec_ref[h] + lax.dot_general(kd, v, (((0,), (0,)), ((), ())), preferred_element_type=F32)
            mu = jnp.mean(o, axis=-1, keepdims=True)
            var = jnp.mean(jnp.square(o - mu), axis=-1, keepdims=True)
            obn = (o - mu) * lax.rsqrt(var + GN_EPS)
            g = gb_ref[rc, vs].astype(F32)
            gated = g * (1.0 / (1.0 + jnp.exp(-g))) * obn
            cat_ref[rc, WIDTH_A + h * VAL_DIM_B:WIDTH_A + (h + 1) * VAL_DIM_B] = gated.astype(BF16)
        st_ref[h] = st
        sto_ref[h] = st


def _decay_tables(chunk, rows):
    H = N_HEADS_B
    log_g = jnp.log(1.0 - 2.0 ** (-5.0 - jnp.arange(H, dtype=F32)))
    i = jnp.arange(rows, dtype=F32)
    live = np.arange(rows) < chunk
    diff = i[:, None] - i[None, :]
    causal = (diff >= 0) & jnp.asarray(live[:, None] & live[None, :])
    dmat = jnp.where(causal[None], jnp.exp(jnp.where(causal, diff, 0.0)[None] * log_g[:, None, None]), 0.0)
    q_decay = jnp.where(jnp.asarray(live)[None], jnp.exp((i[None, :] + 1.0) * log_g[:, None]), 0.0)
    k_decay = jnp.where(jnp.asarray(live)[None], jnp.exp((chunk - 1.0 - i)[None, :] * log_g[:, None]), 0.0)
    c_decay = jnp.exp(chunk * log_g)
    qdec = jnp.broadcast_to(q_decay[:, :, None], (H, rows, VAL_DIM_B))
    kdec = jnp.broadcast_to(k_decay[:, :, None], (H, rows, KEY_DIM_B))
    cdec = jnp.broadcast_to(c_decay[:, None, None], (H, KEY_DIM_B, VAL_DIM_B))
    return dmat.astype(F32), qdec.astype(F32), kdec.astype(F32), cdec.astype(F32)


def _ret_mix(qb, kb, vb, gb, attn, state0, chunk, chunks_per_step=1):
    B, S, _ = qb.shape
    rows = RET_CHUNK * chunks_per_step
    nc = S // rows
    assert nc * rows == S
    tables = _decay_tables(chunk, RET_CHUNK)

    def tok(b, c):
        return (b, c, 0)

    def tspec(w):
        return pl.BlockSpec((None, rows, w), tok)

    def const(shape):
        return pl.BlockSpec(shape, lambda b, c: (0,) * len(shape))

    ins = [qb, kb, vb, gb]
    in_specs = [tspec(QK_WIDTH_B), tspec(QK_WIDTH_B), tspec(WIDTH_B), tspec(WIDTH_B)]
    scratch = [pltpu.VMEM((N_HEADS_B, KEY_DIM_B, VAL_DIM_B), F32)]
    if isinstance(attn, tuple):
        outs_a, lses, dilations = attn
        expand = np.zeros((LANES, WIDTH_A), np.float32)
        for h in range(N_HEADS_A):
            expand[h, h * HEAD_DIM_A:(h + 1) * HEAD_DIM_A] = 1.0
        ins += list(outs_a) + list(lses) + [jnp.asarray(expand, BF16)]
        in_specs += ([pl.BlockSpec((None, rows // d, d * WIDTH_A), tok) for d in dilations]
                     + [pl.BlockSpec((None, rows // d, d * LANES), tok) for d in dilations] + [const((LANES, WIDTH_A))])
        scratch += [pltpu.VMEM((WIDTH_A // LANES, rows, LANES), F32), pltpu.VMEM((rows, LANES), F32)]
    else:
        dilations = ()
        ins.append(attn)
        in_specs.append(tspec(WIDTH_A))
    st_shape = (N_HEADS_B, KEY_DIM_B, VAL_DIM_B)
    st_spec = pl.BlockSpec((None,) + st_shape, lambda b, c: (b, 0, 0, 0))
    ins += [state0] + list(tables)
    in_specs += [st_spec] + [const(t.shape) for t in tables]
    return pl.pallas_call(
        functools.partial(_ret_body, dilations=tuple(dilations)),
        grid=(B, nc),
        in_specs=in_specs,
        out_specs=(tspec(WIDTH_A + WIDTH_B), st_spec),
        out_shape=(jax.ShapeDtypeStruct((B, S, WIDTH_A + WIDTH_B), BF16),
                   jax.ShapeDtypeStruct((B,) + st_shape, F32)),
        scratch_shapes=scratch,
        compiler_params=_params(2),
        name=f"ret_mix_{len(dilations)}",
    )(*ins)


SAMP_Q_ROWS = 64
SAMP_NEW_LANES = 128


def _samp_attn_body(q_ref, kt_ref, vt_ref, knt_ref, vnt_ref, bc_ref, bn_ref, hm_ref, o_ref, ko_ref, vo_ref, *, ds):
    q = q_ref[...]
    kt = kt_ref[...]
    vt = vt_ref[...]
    knt = knt_ref[...]
    vnt = vnt_ref[...]
    w = kt.shape[1]
    is_new = lax.broadcasted_iota(jnp.int32, knt.shape, 1) >= SAMP_NEW_LANES - ds
    for src, new, dst in ((kt, knt, ko_ref), (vt, vnt, vo_ref)):
        rolled = pltpu.roll(src, w - ds, 1)
        dst[:, :w - SAMP_NEW_LANES] = rolled[:, :w - SAMP_NEW_LANES]
        dst[:, w - SAMP_NEW_LANES:] = jnp.where(is_new, new, rolled[:, w - SAMP_NEW_LANES:])

    s_c = jnp.dot(q, kt.astype(BF16), preferred_element_type=F32)
    s_n = jnp.dot(q, knt.astype(BF16), preferred_element_type=F32)
    es_c, es_n, dens, lses = [], [], [], []
    for n in range(len(DILATED_BRANCHES)):
        sc = s_c + bc_ref[n]
        sn = s_n + bn_ref[n]
        m = jnp.maximum(jnp.max(sc, axis=-1, keepdims=True), jnp.max(sn, axis=-1, keepdims=True))
        ec = jnp.exp(sc - m)
        en = jnp.exp(sn - m)
        den = jnp.sum(ec, axis=-1, keepdims=True) + jnp.sum(en, axis=-1, keepdims=True)
        es_c.append(ec)
        es_n.append(en)
        dens.append(den)
        lses.append(m + jnp.log(den))
    mx = functools.reduce(jnp.maximum, lses)
    ws = [jnp.exp(l - mx) for l in lses]
    tot = functools.reduce(lambda a, b: a + b, ws)
    p_c = None
    p_n = None
    for w, den, ec, en in zip(ws, dens, es_c, es_n):
        coef = w / (tot * den)
        p_c = coef * ec if p_c is None else p_c + coef * ec
        p_n = coef * en if p_n is None else p_n + coef * en
    nt = (((1,), (1,)), ((), ()))
    o = (lax.dot_general(p_c.astype(BF16), vt.astype(BF16), nt, preferred_element_type=F32)
         + lax.dot_general(p_n.astype(BF16), vnt.astype(BF16), nt, preferred_element_type=F32))
    o = o * hm_ref[...]
    o_ref[...] = jnp.sum(o.reshape(SUBLANES, N_HEADS_A, WIDTH_A), axis=1)


def _samp_bias_tables(rel_bias, w_buf, ds):
    tabs_c, tabs_n = [], []
    s = np.arange(SUBLANES)[:, None]
    live_s = s < ds
    first_new = SAMP_NEW_LANES - ds
    for window, dil in DILATED_BRANCHES:
        n_keys = window // dil
        for keys, live_k, tabs in ((np.arange(w_buf)[None, :], True, tabs_c),
                                   (w_buf - first_new + np.arange(SAMP_NEW_LANES)[None, :],
                                    np.arange(SAMP_NEW_LANES)[None, :] >= first_new, tabs_n)):
            n = keys.shape[1]
            dist = w_buf + s - keys
            valid = (dist >= 0) & (dist % dil == 0) & (dist // dil <= n_keys) & live_k
            bias = _bias_by_bucket(rel_bias, _t5_bucket(np.clip(dist, 0, None))).transpose(1, 0, 2)
            tab = jnp.where(jnp.asarray(valid)[:, None, :], bias, NEG_INF)
            pad = jnp.where(jnp.asarray(np.broadcast_to(live_k, dist.shape))[:, None, :], 0.0, NEG_INF)
            tab = jnp.where(jnp.asarray(live_s)[:, :, None], tab, pad)
            tabs.append(tab.reshape(SAMP_Q_ROWS, n))
    return jnp.stack(tabs_c), jnp.stack(tabs_n)


def _samp_attn(qa, cache_kt, cache_vt, knt, vnt, rel_bias, ds):
    DB, DS, _ = qa.shape
    W = cache_kt.shape[2]
    head_of_lane = np.arange(WIDTH_A) // HEAD_DIM_A
    hmask = (np.arange(SAMP_Q_ROWS)[:, None] % N_HEADS_A == head_of_lane[None, :])
    q8 = jnp.pad(qa, ((0, 0), (0, SUBLANES - DS), (0, 0)))
    q_rows = jnp.where(jnp.asarray(hmask)[None], jnp.repeat(q8, N_HEADS_A, axis=1), jnp.zeros((), BF16))
    bias_c, bias_n = _samp_bias_tables(rel_bias, W, DS)

    def per_b(rows, w):
        return pl.BlockSpec((None, rows, w), lambda b: (b, 0, 0))

    def const(a):
        return pl.BlockSpec(a.shape, lambda b: (0,) * a.ndim)

    hm = jnp.asarray(hmask, F32)
    return pl.pallas_call(
        functools.partial(_samp_attn_body, ds=ds),
        grid=(DB,),
        in_specs=[per_b(SAMP_Q_ROWS, WIDTH_A), per_b(WIDTH_A, W), per_b(WIDTH_A, W),
                  per_b(WIDTH_A, SAMP_NEW_LANES), per_b(WIDTH_A, SAMP_NEW_LANES),
                  const(bias_c), const(bias_n), const(hm)],
        out_specs=(per_b(SUBLANES, WIDTH_A), per_b(WIDTH_A, W), per_b(WIDTH_A, W)),
        out_shape=(jax.ShapeDtypeStruct((DB, SUBLANES, WIDTH_A), F32),
                   jax.ShapeDtypeStruct((DB, WIDTH_A, W), F32), jax.ShapeDtypeStruct((DB, WIDTH_A, W), F32)),
        compiler_params=_params(1),
        name="samp_attn",
    )(q_rows, cache_kt, cache_vt, knt, vnt, bias_c, bias_n, hm)


ROUTE_ROWS = 16


def _route_body(cat_ref, x_ref, wout_ref, g_ref, b_ref, rwh_ref, rwl_ref, rb_ref, triu_ref, base_ref, *refs,
                alpha, n_exp):
    h_ref, hp_ref, route_ref, route_t_ref, cnt_ref = refs[1:]

    @pl.when(pl.program_id(0) == 0)
    def _():
        cnt_ref[...] = base_ref[...]

    mix = jnp.dot(cat_ref[...], wout_ref[...], preferred_element_type=F32)
    y = alpha * x_ref[...] + mix
    mu = jnp.mean(y, axis=-1, keepdims=True)
    var = jnp.mean(jnp.square(y - mu), axis=-1, keepdims=True)
    h = (y - mu) * lax.rsqrt(var + LN_EPS) * g_ref[...] + b_ref[...]
    h_ref[...] = h
    hb = h.astype(BF16)
    hp_ref[...] = _pack_bf16_pairs(h)
    hl = (h - hb.astype(F32)).astype(BF16)
    nt = (((1,), (1,)), ((), ()))
    logits = (lax.dot_general(rwh_ref[...], hb, nt, preferred_element_type=F32)
              + lax.dot_general(rwl_ref[...], hb, nt, preferred_element_type=F32)
              + lax.dot_general(rwh_ref[...], hl, nt, preferred_element_type=F32))[:n_exp] + rb_ref[...]

    tm = logits.shape[1]
    sub = lax.broadcasted_iota(jnp.int32, (n_exp, tm), 0)
    work = logits
    vals, idxs = [], []
    for _ in range(TOP_K):
        m = jnp.max(work, axis=0, keepdims=True)
        idx = jnp.min(jnp.where(work == m, sub, n_exp), axis=0, keepdims=True)
        vals.append(m)
        idxs.append(idx)
        work = jnp.where(sub == idx, -jnp.inf, work)
    es = [jnp.exp(v - vals[0]) for v in vals]
    tot = functools.reduce(lambda a, b: a + b, es)
    onehot = jnp.zeros((n_exp, tm), F32)
    for idx in idxs:
        onehot = onehot + (sub == idx).astype(F32)
    before = jnp.dot(onehot.astype(BF16), triu_ref[...], preferred_element_type=F32) + cnt_ref[:, 0:1]
    rows = ([idx.astype(F32) for idx in idxs] + [e / tot for e in es]
            + [jnp.sum(jnp.where(sub == idx, before, 0.0), axis=0, keepdims=True) for idx in idxs])
    rows.append(jnp.zeros((ROUTE_ROWS - len(rows), tm), F32))
    record = jnp.concatenate(rows, axis=0)
    route_t_ref[...] = record
    route_ref[...] = jnp.concatenate([record, jnp.zeros((LANES - ROUTE_ROWS, tm), F32)], axis=0).T
    cnt_ref[...] = cnt_ref[...] + jnp.sum(onehot, axis=1, keepdims=True)


def _out_route(cat, x2d, w_out_bf, ln_g, ln_b, rw_hi_t, rw_lo_t, rb_col, base, hp_buf, tile0, tm, alpha):
    T = x2d.shape[0]
    n_exp = rb_col.shape[0]
    triu = jnp.asarray(np.triu(np.ones((tm, tm), np.float32), 1), BF16)

    def tok(i):
        return (i, 0)

    def const(a):
        return pl.BlockSpec(a.shape, lambda i: (0,) * a.ndim)

    ins = (cat, x2d, w_out_bf, ln_g, ln_b, rw_hi_t, rw_lo_t, rb_col, triu, base, hp_buf)
    in_specs = ([pl.BlockSpec((tm, cat.shape[1]), tok), pl.BlockSpec((tm, D_MODEL), tok)]
                + [const(a) for a in ins[2:-1]] + [pl.BlockSpec(memory_space=pl.ANY)])
    return pl.pallas_call(
        functools.partial(_route_body, alpha=alpha, n_exp=n_exp),
        grid=(T // tm,),
        in_specs=in_specs,
        out_specs=(pl.BlockSpec((tm, D_MODEL), tok), pl.BlockSpec((tm, D_MODEL // 2), lambda i: (tile0 + i, 0)),
                   pl.BlockSpec((tm, LANES), tok), pl.BlockSpec((ROUTE_ROWS, tm), lambda i: (0, i)),
                   pl.BlockSpec(base.shape, lambda i: (0, 0))),
        out_shape=(jax.ShapeDtypeStruct((T, D_MODEL), F32), jax.ShapeDtypeStruct(hp_buf.shape, jnp.int32),
                   jax.ShapeDtypeStruct((T, LANES), F32), jax.ShapeDtypeStruct((ROUTE_ROWS, T), F32),
                   jax.ShapeDtypeStruct(base.shape, F32)),
        input_output_aliases={len(ins) - 1: 1},
        compiler_params=_params(1),
        name="out_route",
    )(*ins)


MOE_CAST_ROWS = 128


def _moe_body(be_ref, first_ref, next_ref, nused_ref, x_ref, bgu_ref, bdn_ref, wgu_hbm, wdn_hbm, y_ref,
              gu_stage, dn_stage, wgu_bf, wdn_bf, sem):
    b = pl.program_id(0)
    d_exp = wdn_bf.shape[0]

    def fetch(e):
        return (pltpu.make_async_copy(wgu_hbm.at[e], gu_stage, sem.at[0]),
                pltpu.make_async_copy(wdn_hbm.at[e], dn_stage, sem.at[1]))

    @pl.when(b == 0)
    def _():
        for c in fetch(be_ref[0]):
            c.start()

    @pl.when(first_ref[b] == 1)
    def _():
        for c in fetch(be_ref[b]):
            c.wait()

        def cast_gu(i, c):
            r = pl.ds(pl.multiple_of(i * MOE_CAST_ROWS, MOE_CAST_ROWS), MOE_CAST_ROWS)
            wgu_bf[r, :] = gu_stage[r, :].astype(BF16)
            return c

        def cast_dn(i, c):
            r = pl.ds(pl.multiple_of(i * MOE_CAST_ROWS, MOE_CAST_ROWS), MOE_CAST_ROWS)
            wdn_bf[r, :] = dn_stage[r, :].astype(BF16)
            return c

        lax.fori_loop(0, gu_stage.shape[0] // MOE_CAST_ROWS, cast_gu, 0)
        lax.fori_loop(0, d_exp // MOE_CAST_ROWS, cast_dn, 0)

        @pl.when(next_ref[b] >= 0)
        def _():
            for c in fetch(next_ref[b]):
                c.start()

    @pl.when(b < nused_ref[0])
    def _():
        x_lo, x_hi = _unpack_bf16_pairs(x_ref[...])
        x_lo = x_lo.astype(BF16)
        x_hi = x_hi.astype(BF16)
        dh = x_lo.shape[1]

        def xw(cols):
            return (jnp.dot(x_lo, wgu_bf[:dh, cols], preferred_element_type=F32)
                    + jnp.dot(x_hi, wgu_bf[dh:, cols], preferred_element_type=F32) + bgu_ref[:, cols])

        half = d_exp // 2
        y = None
        for c in range(2):
            lo = c * half
            gate = jnp.minimum(xw(slice(lo, lo + half)), SWIGLU_LIMIT)
            up = jnp.clip(xw(slice(d_exp + lo, d_exp + lo + half)), -SWIGLU_LIMIT, SWIGLU_LIMIT)
            act = (up + 1.0) * gate * (1.0 / (1.0 + jnp.exp(-SWIGLU_ALPHA * gate)))
            part = jnp.dot(act.astype(BF16), wdn_bf[lo:lo + half, :], preferred_element_type=F32)
            y = part if y is None else y + part
        y_ref[...] = _pack_bf16_pairs(y + bdn_ref[...])

    @pl.when(b >= nused_ref[0])
    def _():
        y_ref[...] = jnp.zeros_like(y_ref)


def _moe_ffn(xs, rows, block_e, n_used, has_rows, w_gu, b_gu, w_dn, b_dn):
    E, D, two_de = w_gu.shape
    d_exp = two_de // 2
    nb = rows // MOE_BLOCK
    idx = jnp.arange(nb, dtype=jnp.int32)
    first = ((idx < n_used[0]) & ((idx == 0) | (block_e != jnp.roll(block_e, 1)))).astype(jnp.int32)
    ids = jnp.arange(E, dtype=jnp.int32)
    later = jnp.where((ids[None, :] > ids[:, None]) & has_rows[None, :], ids[None, :], E).min(axis=1)
    next_of = jnp.where(later == E, -1, later).astype(jnp.int32)
    next_e = jnp.sum(jnp.where(block_e[:, None] == ids[None, :], next_of[None, :], 0), axis=1).astype(jnp.int32)
    grid_spec = pltpu.PrefetchScalarGridSpec(
        num_scalar_prefetch=4,
        grid=(nb,),
        in_specs=[
            pl.BlockSpec((MOE_BLOCK, D // 2), lambda b, be, fi, nx, nu: (b, 0)),
            pl.BlockSpec((None, 1, two_de), lambda b, be, fi, nx, nu: (be[b], 0, 0)),
            pl.BlockSpec((None, 1, D), lambda b, be, fi, nx, nu: (be[b], 0, 0)),
            pl.BlockSpec(memory_space=pl.ANY),
            pl.BlockSpec(memory_space=pl.ANY),
        ],
        out_specs=pl.BlockSpec((MOE_BLOCK, D // 2), lambda b, be, fi, nx, nu: (b, 0)),
        scratch_shapes=[pltpu.VMEM((D, two_de), F32), pltpu.VMEM((d_exp, D), F32),
                        pltpu.VMEM((D, two_de), BF16), pltpu.VMEM((d_exp, D), BF16),
                        pltpu.SemaphoreType.DMA((2,))],
    )
    return pl.pallas_call(
        _moe_body,
        grid_spec=grid_spec,
        out_shape=jax.ShapeDtypeStruct((rows, D // 2), jnp.int32),
        compiler_params=_params(1),
        name="moe_ffn",
    )(block_e, first, next_e, n_used, xs, b_gu.reshape(E, 1, two_de), b_dn.reshape(E, 1, D), w_gu, w_dn)


def _ple_body(h_ref, ys_ref, route_ref, p_ref, g_ref, b_ref, wpg_ref, bpg_ref, wpp_ref, o_ref, *, alpha):
    route = route_ref[...]
    f_lo = None
    f_hi = None
    for k in range(TOP_K):
        lo, hi = _unpack_bf16_pairs(ys_ref[k])
        g = route[:, TOP_K + k:TOP_K + k + 1]
        f_lo = g * lo if f_lo is None else f_lo + g * lo
        f_hi = g * hi if f_hi is None else f_hi + g * hi
    y = alpha * h_ref[...] + jnp.concatenate([f_lo, f_hi], axis=1)
    mu = jnp.mean(y, axis=-1, keepdims=True)
    var = jnp.mean(jnp.square(y - mu), axis=-1, keepdims=True)
    h2 = (y - mu) * lax.rsqrt(var + LN_EPS) * g_ref[...] + b_ref[...]
    z = jnp.dot(h2.astype(BF16), wpg_ref[...], preferred_element_type=F32) + bpg_ref[...]
    gate = 1.0 / (1.0 + jnp.exp(-z))
    proj = jnp.dot(p_ref[...].astype(BF16), wpp_ref[...], preferred_element_type=F32)
    o_ref[...] = h2 + gate * proj


def _ffn_ple(h, y_slots, route, p, ln_g, ln_b, w_pg_bf, b_pg, w_pp_bf, tm, tile0, alpha):
    T = h.shape[0]

    def tok(i):
        return (i, 0)

    def const(a):
        return pl.BlockSpec(a.shape, lambda i: (0,) * a.ndim)

    consts = (ln_g, ln_b, w_pg_bf, b_pg, w_pp_bf)
    return pl.pallas_call(
        functools.partial(_ple_body, alpha=alpha),
        grid=(T // tm,),
        in_specs=[pl.BlockSpec((tm, D_MODEL), tok),
                  pl.BlockSpec((TOP_K, tm, D_MODEL // 2), lambda i: (0, tile0 + i, 0)),
                  pl.BlockSpec((tm, LANES), tok),
                  pl.BlockSpec((tm, p.shape[1]), tok)] + [const(a) for a in consts],
        out_specs=pl.BlockSpec((tm, D_MODEL), tok),
        out_shape=jax.ShapeDtypeStruct((T, D_MODEL), F32),
        compiler_params=_params(1),
        name="ffn_ple",
    )(h, y_slots, route, p, *consts)


def _row(v):
    return v.reshape(1, -1).astype(F32)


def kernel(x_prompt, x_sample, cache_win_k, cache_win_v, state_ret, p_prompt, p_sample, rel_bias, w_in, w_out,
           ln1_g, ln1_b, router_w, router_b, w_gate_up, b_gate_up, w_down, b_down, ln2_g, ln2_b,
           w_ple_gate, b_ple_gate, w_ple_proj):
    B, S, D = x_prompt.shape
    DB, DS, _ = x_sample.shape
    depth = w_in.shape[0]
    w_buf = cache_win_k.shape[2]
    n_exp = router_w.shape[-1]
    alpha = (2.0 * depth) ** 0.25
    assert depth == 1 and D == D_MODEL
    assert S % (BLK * MAX_DIL) == 0 and S >= WINDOW_MAX and w_buf == WINDOW_MAX and DS <= SUBLANES
    tm_p = 512
    Tp, Ts = B * S, DB * DS
    assert Tp % tm_p == 0 and Ts % SUBLANES == 0

    i = 0
    w_in_bf = w_in[i].astype(BF16)
    w_out_bf = w_out[i].astype(BF16)
    w_pg_bf = w_ple_gate[i].astype(BF16)
    w_pp_bf = w_ple_proj[i].astype(BF16)
    rw_t = jnp.pad(router_w[i].T, ((0, LANES - n_exp), (0, 0)))
    rw_hi = rw_t.astype(BF16)
    rw_lo = (rw_t - rw_hi.astype(F32)).astype(BF16)
    rb = router_b[i].astype(F32).reshape(n_exp, 1)
    T = Tp + Ts
    t_align = SC_WORKERS * SC_ALIGN
    T_pad = -(-T // t_align) * t_align
    assert Tp % Ts == 0

    cos_p, sin_p = _rotary_tables(jnp.arange(S, dtype=jnp.int32))
    dils = tuple(d for _, d in DILATED_BRANCHES)
    extra = tuple(d for d in dils if d > 1)
    outs = _in_proj(x_prompt.reshape(Tp, D), w_in_bf, cos_p, sin_p, tm_p, S // tm_p, WINDOW_MAX // tm_p, extra)
    qa, ka, va, qb, kb, vb, gb, kf, vf = outs[:9]
    qkv = {1: (qa, ka, va)}
    for t, d in enumerate(extra):
        qkv[d] = tuple(outs[9 + j * len(extra) + t] for j in range(3))

    def seq(t):
        return t.reshape(B, S, t.shape[-1])

    outs_a, lses = [], []
    for window, dil in DILATED_BRANCHES:
        q_d, k_d, v_d = (t.reshape(B, S // dil, dil * WIDTH_A) for t in qkv[dil])
        o_n, l_n = _dilated_branch(q_d, k_d, v_d, _attn_bias_tables(rel_bias, window, dil), dil)
        outs_a.append(o_n)
        lses.append(l_n)
    st_zero = jnp.zeros((B, N_HEADS_B, KEY_DIM_B, VAL_DIM_B), F32)
    cat_p, rst_p = _ret_mix(seq(qb), seq(kb), seq(vb), seq(gb), (outs_a, lses, dils), st_zero, RET_CHUNK,
                            RET_CHUNKS_PER_STEP)
    base0 = jnp.zeros((n_exp, LANES), F32)
    hp0 = jnp.zeros((T_pad, D // 2), jnp.int32)
    h_p, hp_buf, route_p, rt_p, cnt_p = _out_route(cat_p.reshape(Tp, -1), x_prompt.reshape(Tp, D), w_out_bf,
                                                   _row(ln1_g[i]), _row(ln1_b[i]), rw_hi, rw_lo, rb, base0, hp0, 0,
                                                   tm_p, alpha)

    pos_s = jnp.tile(PAST_LEN + jnp.arange(DS, dtype=jnp.int32), DB)
    cos_s, sin_s = _rotary_tables(pos_s)
    qa_s, _, _, qb_s, kb_s, vb_s, gb_s, kf_s, vf_s = _in_proj(x_sample.reshape(Ts, D), w_in_bf, cos_s, sin_s,
                                                             Ts, 1, 1)

    def positions_minor(t):
        return jnp.transpose(t, (0, 2, 3, 1)).reshape(DB, WIDTH_A, t.shape[1])

    def positions_major(t):
        return jnp.transpose(t.reshape(t.shape[0], N_HEADS_A, HEAD_DIM_A, t.shape[2]), (0, 3, 1, 2))[None]

    def new_columns(t):
        t = jnp.transpose(t.reshape(WIDTH_A, DB, DS), (1, 0, 2))
        return jnp.pad(t, ((0, 0), (0, 0), (SAMP_NEW_LANES - DS, 0)))

    oa_s, kt_out, vt_out = _samp_attn(qa_s.reshape(DB, DS, WIDTH_A), positions_minor(cache_win_k[i]),
                                      positions_minor(cache_win_v[i]), new_columns(kf_s), new_columns(vf_s),
                                      rel_bias, DS)

    def pad_rows(t, rows):
        t = t.reshape(DB, -1, t.shape[-1])
        return jnp.pad(t, ((0, 0), (0, rows - t.shape[1]), (0, 0)))

    cat_s, rst_s = _ret_mix(pad_rows(qb_s, RET_CHUNK), pad_rows(kb_s, RET_CHUNK), pad_rows(vb_s, RET_CHUNK),
                            pad_rows(gb_s, RET_CHUNK), pad_rows(oa_s, RET_CHUNK), state_ret[i].astype(F32), DS)
    cat_s = cat_s[:, :DS].reshape(Ts, -1)
    h_s, hp_all, route_s, rt_s, cnt = _out_route(cat_s, x_sample.reshape(Ts, D), w_out_bf, _row(ln1_g[i]),
                                                 _row(ln1_b[i]), rw_hi, rw_lo, rb, cnt_p, hp_buf, Tp // Ts, Ts, alpha)

    n_fill = T_pad - T
    record = jnp.concatenate([rt_p, rt_s, jnp.zeros((ROUTE_ROWS, n_fill), F32)], axis=1)
    top_idx = record[:TOP_K].astype(jnp.int32)
    rank = record[2 * TOP_K:3 * TOP_K].astype(jnp.int32)
    counts = cnt[:, 0].astype(jnp.int32)
    padded = (counts + MOE_BLOCK - 1) // MOE_BLOCK * MOE_BLOCK
    pad_end = jnp.cumsum(padded)
    pad_start = pad_end - padded
    ids = jnp.arange(n_exp, dtype=jnp.int32)[:, None, None]
    dest = jnp.sum(jnp.where(top_idx[None] == ids, pad_start[:, None, None], 0), axis=0) + rank
    n_blocks = -(-T * TOP_K // MOE_BLOCK) + n_exp
    rows = n_blocks * MOE_BLOCK
    block_start = jnp.arange(n_blocks, dtype=jnp.int32) * MOE_BLOCK
    block_e = jnp.minimum(jnp.sum(pad_end[None, :] <= block_start[:, None], axis=1), n_exp - 1).astype(jnp.int32)
    n_used = (pad_end[-1:] // MOE_BLOCK).astype(jnp.int32)
    tok_id = jnp.arange(T_pad, dtype=jnp.int32)[None, :]
    spare = rows + (tok_id - T) * TOP_K + jnp.arange(TOP_K, dtype=jnp.int32)[:, None]
    dest_sc = jnp.where(tok_id >= T, spare, dest).reshape(-1)
    dest_ga = jnp.where(tok_id >= T, 0, dest).reshape(-1)
    xs = _sc_scatter_rows(hp_all, dest_sc, rows + n_fill * TOP_K)
    ys = _moe_ffn(xs, rows, block_e, n_used, padded > 0, w_gate_up[i], b_gate_up[i], w_down[i], b_down[i])
    y_slots = _sc_gather_rows(ys, dest_ga).reshape(TOP_K, T_pad, D // 2)

    ple_args = (_row(ln2_g[i]), _row(ln2_b[i]), w_pg_bf, _row(b_ple_gate[i]), w_pp_bf)
    y_p = _ffn_ple(h_p, y_slots, route_p, p_prompt[i].reshape(Tp, D_PLE), *ple_args, tm_p, 0, alpha)
    y_s = _ffn_ple(h_s, y_slots, route_s, p_sample[i].reshape(Ts, D_PLE), *ple_args, Ts, Tp // Ts, alpha)

    return (y_p.reshape(B, S, D), y_s.reshape(DB, DS, D), positions_major(kf), positions_major(vf),
            rst_p[None], positions_major(kt_out), positions_major(vt_out), rst_s[None])
```
